```python
import math
import jax
import jax.numpy as jnp
from jax import lax
import numpy as np

D_MODEL = 1024
BATCH = 2
SEQ = 8192
DEPTH = 4
DEC_BATCH = 128
DEC_SEQ = 1
PAST_LEN = 2048
PAGE_SIZE = 128

N_MIXERS = 3
ALPHA = (2 * DEPTH) ** 0.25
BETA = (8 * DEPTH) ** -0.25
LN_EPS = 1e-5
RMS_EPS = 1e-6
N_A = (DEPTH + 2) // 3
N_B = (DEPTH + 1) // 3
N_C = DEPTH // 3
N_DENSE = (DEPTH + 1) // 2
N_MOE = DEPTH // 2

HG_HEADS = 8
HG_DK = D_MODEL // HG_HEADS
HG_DV = D_MODEL // HG_HEADS
HG_WIDTH = HG_HEADS * HG_DK
HG_CHUNK = 64
LB_FLOOR = 1e-30

RG_WIDTH = 1280
RG_BLOCKS = 10
RG_BS = RG_WIDTH // RG_BLOCKS
RG_CONV = 4
RG_C = 8.0

NSA_HEADS = 16
NSA_KV_HEADS = 4
NSA_GROUP = NSA_HEADS // NSA_KV_HEADS
NSA_DH = 64
NSA_KVW = NSA_KV_HEADS * NSA_DH
CMP_BLOCK = 32
CMP_HID = 128
SLC_BLOCK = 64
SLC_RATIO = SLC_BLOCK // CMP_BLOCK
SLC_TOPK = 16
WINDOW = 512
NSA_QBLOCK = 128
NSA_IN = NSA_HEADS * NSA_DH + 6 * NSA_KVW + 3 * NSA_HEADS
MASK_NEG = -1e30
SEL_BIG = 1e9

D_FF = 2816
N_EXPERTS = 8
MOE_TOP_K = 2

kernel_name = 'hybrid_hgrn2_rglru_nsa_deepnorm_step'


def layer_norm(x, g, b):
    xf = x.astype(jnp.float32)
    xc = xf - jnp.mean(xf, -1, keepdims=True)
    var = jnp.mean(xc * xc, -1, keepdims=True)
    return (xc * lax.rsqrt(var + LN_EPS) * g.astype(jnp.float32) + b.astype(jnp.float32)).astype(x.dtype)


def swiglu(x, w_gu, w_down):
    gate, up = jnp.split(x @ w_gu, 2, axis=-1)
    return (jax.nn.silu(gate) * up) @ w_down


def moe_swiglu(x, w_router, w_gu, w_down):
    shp = x.shape
    xt = x.reshape(-1, shp[-1])
    logits = (xt @ w_router).astype(jnp.float32)
    top_val, top_idx = lax.top_k(logits, MOE_TOP_K)
    top_w = jax.nn.softmax(top_val, axis=-1)
    gate = jnp.sum(jax.nn.one_hot(top_idx, N_EXPERTS, dtype=jnp.float32) * top_w[..., None], axis=1)
    y = jnp.zeros(xt.shape, jnp.float32)
    for e in range(N_EXPERTS):
        y = y + gate[:, e:e + 1] * swiglu(xt, w_gu[e], w_down[e]).astype(jnp.float32)
    return y.astype(x.dtype).reshape(shp)


def masked_softmax(s, mask):
    s = jnp.where(mask, s.astype(jnp.float32), MASK_NEG)
    m = jnp.max(s, axis=-1, keepdims=True)
    e = jnp.where(mask, jnp.exp(s - m), 0.0)
    return e / jnp.maximum(jnp.sum(e, axis=-1, keepdims=True), 1e-30)


def hgrn_lower_bounds(lb_param):
    p = jax.nn.softmax(lb_param.astype(jnp.float32), axis=0)
    return jnp.cumsum(p, axis=0) - p[0]


def gla_chunked(q, k, v, log_f, s0):
    b_, l_, h_, dk = q.shape
    dv = v.shape[-1]
    chunk = min(HG_CHUNK, l_)
    n = -(-l_ // chunk)
    pad = n * chunk - l_

    def blocks(a):
        a = jnp.pad(a, ((0, 0), (0, pad), (0, 0), (0, 0)))
        return a.reshape(b_, n, chunk, h_, a.shape[-1]).transpose(1, 0, 3, 2, 4)

    causal = jnp.tril(jnp.ones((chunk, chunk), bool))[:, :, None]

    def step(s, inp):
        qc, kc, vc, gc = inp
        cum = jnp.cumsum(gc, axis=2)
        o_inter = jnp.einsum('bhtk,bhkv->bhtv', qc * jnp.exp(cum), s)
        rel = cum[:, :, :, None, :] - cum[:, :, None, :, :]
        decay = jnp.where(causal, jnp.exp(jnp.where(causal, rel, 0.0)), 0.0)
        att = jnp.einsum('bhtk,bhsk,bhtsk->bhts', qc, kc, decay)
        o = o_inter + jnp.einsum('bhts,bhsv->bhtv', att, vc)
        last = cum[:, :, -1]
        s = s * jnp.exp(last)[..., None] + jnp.einsum('bhsk,bhsv->bhkv', kc * jnp.exp(last[:, :, None] - cum), vc)
        return s, o

    s_fin, o = lax.scan(step, s0, (blocks(q), blocks(k), blocks(v), blocks(log_f)))
    o = o.transpose(1, 0, 3, 2, 4).reshape(b_, n * chunk, h_, dv)[:, :l_]
    return o, s_fin


def hgrn2_mixer(x, s0, w_in, lb, norm_g, w_out):
    b_, l_, _ = x.shape
    q, f, i, g = jnp.split(x @ w_in, 4, axis=-1)
    heads = lambda a: a.reshape(b_, l_, HG_HEADS, -1).astype(jnp.float32)
    lb = lb.astype(jnp.float32)
    log_f = jnp.logaddexp(jnp.log(jnp.maximum(lb, LB_FLOOR)), jnp.log1p(-lb) + jax.nn.log_sigmoid(f.astype(jnp.float32)))
    k = -jnp.expm1(log_f)
    o, s = gla_chunked(heads(jax.nn.silu(q)), heads(k), heads(i), heads(log_f), s0.astype(jnp.float32))
    o = o * lax.rsqrt(jnp.mean(o * o, axis=-1, keepdims=True) + RMS_EPS)
    o = o.reshape(b_, l_, HG_WIDTH) * norm_g.astype(jnp.float32) * jax.nn.silu(g.astype(jnp.float32))
    return o.astype(x.dtype) @ w_out, s.astype(x.dtype)


def linear_recurrence(a, u, h0):
    def combine(c1, c2):
        a1, u1 = c1
        a2, u2 = c2
        return a1 * a2, a2 * u1 + u2
    a_cum, u_cum = lax.associative_scan(combine, (a, u), axis=1)
    return a_cum * h0[:, None] + u_cum


def rglru_mixer(x, conv_buf, h0, w_in, conv_w, conv_b, w_gate, b_gate, lam, w_out):
    b_, l_, _ = x.shape
    y_br, x_br = jnp.split(x @ w_in, 2, axis=-1)
    xpad = jnp.concatenate([conv_buf.astype(x.dtype), x_br], axis=1)
    xc = conv_b
    for j in range(RG_CONV):
        xc = xc + xpad[:, j:j + l_] * conv_w[j]
    gates = jnp.einsum('blnk,gnkm->gblnm', xc.reshape(b_, l_, RG_BLOCKS, RG_BS), w_gate)
    gates = gates.reshape(2, b_, l_, RG_WIDTH) + b_gate[:, None, None, :]
    r = jax.nn.sigmoid(gates[0].astype(jnp.float32))
    i = jax.nn.sigmoid(gates[1].astype(jnp.float32))
    log_a = -RG_C * r * jax.nn.softplus(-lam.astype(jnp.float32))
    u = jnp.sqrt(-jnp.expm1(2.0 * log_a)) * i * xc.astype(jnp.float32)
    h = linear_recurrence(jnp.exp(log_a), u, h0.astype(jnp.float32))
    out = (jax.nn.gelu(y_br.astype(jnp.float32)) * h).astype(x.dtype) @ w_out
    return out, xpad[:, l_:], h[:, -1].astype(x.dtype)


def nsa_project(x, w_in):
    b_, l_, _ = x.shape
    z = x @ w_in
    hq = NSA_HEADS * NSA_DH
    q = z[..., :hq].reshape(b_, l_, NSA_HEADS, NSA_DH)
    kv = z[..., hq:hq + 6 * NSA_KVW].reshape(b_, l_, 3, 2, NSA_KV_HEADS, NSA_DH)
    gates = jax.nn.sigmoid(z[..., hq + 6 * NSA_KVW:].astype(jnp.float32)).reshape(b_, l_, NSA_HEADS, 3)
    return q, kv[:, :, 0], kv[:, :, 1], kv[:, :, 2], gates


def nsa_compress(kv, pos, w1, w2):
    b_, l_ = kv.shape[:2]
    nc = l_ // CMP_BLOCK
    blk = kv[:, :nc * CMP_BLOCK].reshape(b_, nc, CMP_BLOCK, 2, NSA_KV_HEADS, NSA_DH)
    blk = blk + pos.transpose(1, 0, 2)[:, :, None, :]
    hid = jax.nn.silu(jnp.einsum('bnlchd,cldf->bnchf', blk, w1))
    return jnp.einsum('bnchf,cfd->bnchd', hid, w2), nc


def selection_blocks(kv):
    b_, l_ = kv.shape[:2]
    n_s = -(-l_ // SLC_BLOCK)
    kv = jnp.pad(kv, ((0, 0), (0, n_s * SLC_BLOCK - l_), (0, 0), (0, 0), (0, 0)))
    kv = kv.reshape(b_, n_s, SLC_BLOCK, 2, NSA_KV_HEADS, NSA_DH).transpose(3, 0, 4, 1, 2, 5)
    return kv[0], kv[1]


def nsa_attend(q, gates, t_pos, ckv, c_end, sk, sv, wkv, w_pos):
    b_, nq = q.shape[:2]
    qg = q.reshape(b_, nq, NSA_KV_HEADS, NSA_GROUP, NSA_DH) * (NSA_DH ** -0.5)
    ck, cv = ckv[:, :, 0], ckv[:, :, 1]
    p_c = masked_softmax(jnp.einsum('bqhgd,bnhd->bhgqn', qg, ck), c_end[None, :] <= t_pos[:, None])
    o_c = jnp.einsum('bhgqn,bnhd->bqhgd', p_c.astype(cv.dtype), cv)
    n_s = sk.shape[2]
    imp = jnp.sum(p_c, axis=2)
    imp = jnp.pad(imp, ((0, 0), (0, 0), (0, 0), (0, n_s * SLC_RATIO - imp.shape[-1])))
    imp = imp.reshape(b_, NSA_KV_HEADS, nq, n_s, SLC_RATIO).sum(-1)
    blk = jnp.arange(n_s)[None, :]
    cur = (t_pos // SLC_BLOCK)[:, None]
    forced = (blk == 0) | (blk == cur) | (blk == cur - 1)
    score = jnp.where(forced, SEL_BIG, jnp.where(blk <= cur, imp, -SEL_BIG))
    _, idx = lax.top_k(score, min(SLC_TOPK, n_s))
    bi = jnp.arange(b_)[:, None, None, None]
    hi = jnp.arange(NSA_KV_HEADS)[None, :, None, None]
    gk, gv = sk[bi, hi, idx], sv[bi, hi, idx]
    kk = idx.shape[-1]
    kpos = idx[..., None] * SLC_BLOCK + jnp.arange(SLC_BLOCK)
    s_s = jnp.einsum('bqhgd,bhqksd->bhgqks', qg, gk)
    m_s = (kpos <= t_pos[None, None, :, None, None]).reshape(b_, NSA_KV_HEADS, 1, nq, kk * SLC_BLOCK)
    p_s = masked_softmax(s_s.reshape(b_, NSA_KV_HEADS, NSA_GROUP, nq, kk * SLC_BLOCK), m_s)
    o_s = jnp.einsum('bhgqks,bhqksd->bqhgd', p_s.reshape(s_s.shape).astype(gv.dtype), gv)
    wk, wv = wkv[:, :, 0], wkv[:, :, 1]
    dist = t_pos[:, None] - w_pos[None, :]
    p_w = masked_softmax(jnp.einsum('bqhgd,bwhd->bhgqw', qg, wk), (dist >= 0) & (dist <= WINDOW) & (w_pos[None, :] >= 0))
    o_w = jnp.einsum('bhgqw,bwhd->bqhgd', p_w.astype(wv.dtype), wv)
    g = gates.reshape(b_, nq, NSA_KV_HEADS, NSA_GROUP, 3)
    o = g[..., 0:1] * o_c + g[..., 1:2] * o_s + g[..., 2:3] * o_w
    return o.reshape(b_, nq, NSA_HEADS * NSA_DH)


def nsa_prompt(x, w_in, pos, w1, w2, w_out):
    b_, l_, _ = x.shape
    q, kv_c, kv_s, kv_w, gates = nsa_project(x, w_in)
    ckv, nc = nsa_compress(kv_c, pos, w1, w2)
    c_end = jnp.arange(nc) * CMP_BLOCK + CMP_BLOCK - 1
    sk, sv = selection_blocks(kv_s)
    kv_w_pad = jnp.pad(kv_w, ((0, 0), (WINDOW, 0), (0, 0), (0, 0), (0, 0)))
    qb = min(NSA_QBLOCK, l_)

    def one_block(j):
        start = j * qb
        t_pos = start + jnp.arange(qb)
        q_b = lax.dynamic_slice_in_dim(q, start, qb, axis=1)
        g_b = lax.dynamic_slice_in_dim(gates, start, qb, axis=1)
        w_b = lax.dynamic_slice_in_dim(kv_w_pad, start, WINDOW + qb, axis=1)
        w_pos = start - WINDOW + jnp.arange(WINDOW + qb)
        return nsa_attend(q_b, g_b, t_pos, ckv, c_end, sk, sv, w_b, w_pos)

    o = lax.map(one_block, jnp.arange(l_ // qb))
    o = o.transpose(1, 0, 2, 3).reshape(b_, l_, NSA_HEADS * NSA_DH)
    win = min(WINDOW, l_)
    return o.astype(x.dtype) @ w_out, kv_c, kv_s, kv_w[:, l_ - win:]


def nsa_sample(x, cache_cmp, cache_slc, swa_buf, page_table, w_in, pos, w1, w2, w_out):
    b_, l_, _ = x.shape
    q, kv_c, kv_s, kv_w, gates = nsa_project(x, w_in)

    def paged(cache):
        rows = cache[page_table]
        return rows.reshape(b_, -1, *cache.shape[2:]).astype(x.dtype)

    full_c = jnp.concatenate([paged(cache_cmp), kv_c], axis=1)
    full_s = jnp.concatenate([paged(cache_slc), kv_s], axis=1)
    past = full_c.shape[1] - l_
    ckv, nc = nsa_compress(full_c, pos, w1, w2)
    c_end = jnp.arange(nc) * CMP_BLOCK + CMP_BLOCK - 1
    sk, sv = selection_blocks(full_s)
    win_buf = swa_buf.shape[1]
    wkv = jnp.concatenate([swa_buf.astype(x.dtype), kv_w], axis=1)
    w_pos = past - win_buf + jnp.arange(win_buf + l_)
    t_pos = past + jnp.arange(l_)
    o = nsa_attend(q, gates, t_pos, ckv, c_end, sk, sv, wkv, w_pos)
    return o.astype(x.dtype) @ w_out, kv_c, kv_s, wkv[:, l_:]


def setup_inputs(seed: int = 0) -> dict:
    key = jax.random.key(seed)
    ks = iter(jax.random.split(key, 48))
    nrm = lambda shape, scale: scale * jax.random.normal(next(ks), shape, jnp.float32)
    n_pages = PAST_LEN // PAGE_SIZE
    n_phys = (DEC_BATCH * n_pages * 5) // 4
    win_buf = min(WINDOW, PAST_LEN)
    page_table = jax.random.permutation(next(ks), n_phys)[:DEC_BATCH * n_pages].reshape(DEC_BATCH, n_pages).astype(jnp.int32)
    u = jax.random.uniform(next(ks), (N_B, RG_WIDTH), jnp.float32, 0.9, 0.999)
    a0 = u ** (1.0 / RG_C)
    kv_row = (2, NSA_KV_HEADS, NSA_DH)
    return {
        'x_prompt': nrm((BATCH, SEQ, D_MODEL), 1.0),
        'x_sample': nrm((DEC_BATCH, DEC_SEQ, D_MODEL), 1.0),
        'state_hgrn': nrm((N_A, DEC_BATCH, HG_HEADS, HG_DK, HG_DV), 0.5),
        'state_rglru_conv': nrm((N_B, DEC_BATCH, RG_CONV - 1, RG_WIDTH), 1.0),
        'state_rglru_h': nrm((N_B, DEC_BATCH, RG_WIDTH), 0.5),
        'cache_nsa_cmp': nrm((N_C, n_phys, PAGE_SIZE) + kv_row, 1.0),
        'cache_nsa_slc': nrm((N_C, n_phys, PAGE_SIZE) + kv_row, 1.0),
        'cache_nsa_swa': nrm((N_C, DEC_BATCH, win_buf) + kv_row, 1.0),
        'page_table': page_table,
        'ln_g': 1.0 + nrm((DEPTH, 2, D_MODEL), 0.02),
        'ln_b': nrm((DEPTH, 2, D_MODEL), 0.02),
        'hgrn_w_in': nrm((N_A, D_MODEL, 4 * HG_WIDTH), D_MODEL ** -0.5),
        'hgrn_lb': nrm((DEPTH, HG_WIDTH), 0.1),
        'hgrn_norm_g': 1.0 + nrm((N_A, HG_WIDTH), 0.02),
        'hgrn_w_out': nrm((N_A, HG_WIDTH, D_MODEL), BETA * HG_WIDTH ** -0.5),
        'rg_w_in': nrm((N_B, D_MODEL, 2 * RG_WIDTH), D_MODEL ** -0.5),
        'rg_conv_w': nrm((N_B, RG_CONV, RG_WIDTH), RG_CONV ** -0.5),
        'rg_conv_b': nrm((N_B, RG_WIDTH), 0.01),
        'rg_w_gate': nrm((N_B, 2, RG_BLOCKS, RG_BS, RG_BS), RG_BS ** -0.5),
        'rg_b_gate': nrm((N_B, 2, RG_WIDTH), 0.01),
        'rg_lambda': jnp.log(a0) - jnp.log1p(-a0),
        'rg_w_out': nrm((N_B, RG_WIDTH, D_MODEL), BETA * RG_WIDTH ** -0.5),
        'nsa_w_in': nrm((N_C, D_MODEL, NSA_IN), D_MODEL ** -0.5),
        'nsa_cmp_pos': nrm((N_C, 2, CMP_BLOCK, NSA_DH), 0.02),
        'nsa_cmp_w1': nrm((N_C, 2, CMP_BLOCK, NSA_DH, CMP_HID), (CMP_BLOCK * NSA_DH) ** -0.5),
        'nsa_cmp_w2': nrm((N_C, 2, CMP_HID, NSA_DH), CMP_HID ** -0.5),
        'nsa_w_out': nrm((N_C, NSA_HEADS * NSA_DH, D_MODEL), BETA * (NSA_HEADS * NSA_DH) ** -0.5),
        'ffn_w_gu': nrm((N_DENSE, D_MODEL, 2 * D_FF), D_MODEL ** -0.5),
        'ffn_w_down': nrm((N_DENSE, D_FF, D_MODEL), BETA * D_FF ** -0.5),
        'moe_router': nrm((N_MOE, D_MODEL, N_EXPERTS), D_MODEL ** -0.5),
        'moe_w_gu': nrm((N_MOE, N_EXPERTS, D_MODEL, 2 * D_FF), D_MODEL ** -0.5),
        'moe_w_down': nrm((N_MOE, N_EXPERTS, D_FF, D_MODEL), BETA * D_FF ** -0.5),
    }


def reference(x_prompt, x_sample, state_hgrn, state_rglru_conv, state_rglru_h,
              cache_nsa_cmp, cache_nsa_slc, cache_nsa_swa, page_table,
              ln_g, ln_b, hgrn_w_in, hgrn_lb, hgrn_norm_g, hgrn_w_out,
              rg_w_in, rg_conv_w, rg_conv_b, rg_w_gate, rg_b_gate, rg_lambda, rg_w_out,
              nsa_w_in, nsa_cmp_pos, nsa_cmp_w1, nsa_cmp_w2, nsa_w_out,
              ffn_w_gu, ffn_w_down, moe_router, moe_w_gu, moe_w_down):
    bp = x_prompt.shape[0]
    lower_bounds = hgrn_lower_bounds(hgrn_lb)
    yp, ys = x_prompt, x_sample
    hg_p, hg_s, rc_p, rc_s, rh_p, rh_s = [], [], [], [], [], []
    cm_p, cm_s, sl_p, sl_s, sw_p, sw_s = [], [], [], [], [], []
    for layer in range(DEPTH):
        kind, slot = layer % N_MIXERS, layer // N_MIXERS
        if kind == 0:
            w = (hgrn_w_in[slot], lower_bounds[layer], hgrn_norm_g[slot], hgrn_w_out[slot])
            s0 = jnp.zeros((bp, HG_HEADS, HG_DK, HG_DV), jnp.float32)
            mp, st_p = hgrn2_mixer(yp, s0, *w)
            ms, st_s = hgrn2_mixer(ys, state_hgrn[slot], *w)
            hg_p.append(st_p)
            hg_s.append(st_s)
        elif kind == 1:
            w = (rg_w_in[slot], rg_conv_w[slot], rg_conv_b[slot], rg_w_gate[slot], rg_b_gate[slot], rg_lambda[slot], rg_w_out[slot])
            buf0 = jnp.zeros((bp, RG_CONV - 1, RG_WIDTH), x_prompt.dtype)
            h0 = jnp.zeros((bp, RG_WIDTH), jnp.float32)
            mp, cb_p, h_p = rglru_mixer(yp, buf0, h0, *w)
            ms, cb_s, h_s = rglru_mixer(ys, state_rglru_conv[slot], state_rglru_h[slot], *w)
            rc_p.append(cb_p)
            rc_s.append(cb_s)
            rh_p.append(h_p)
            rh_s.append(h_s)
        else:
            w = (nsa_w_in[slot], nsa_cmp_pos[slot], nsa_cmp_w1[slot], nsa_cmp_w2[slot], nsa_w_out[slot])
            mp, c_p, s_p, w_p = nsa_prompt(yp, *w)
            ms, c_s, s_s, w_s = nsa_sample(ys, cache_nsa_cmp[slot], cache_nsa_slc[slot], cache_nsa_swa[slot], page_table, *w)
            cm_p.append(c_p)
            cm_s.append(c_s)
            sl_p.append(s_p)
            sl_s.append(s_s)
            sw_p.append(w_p)
            sw_s.append(w_s)
        yp = layer_norm(ALPHA * yp + mp, ln_g[layer, 0], ln_b[layer, 0])
        ys = layer_norm(ALPHA * ys + ms, ln_g[layer, 0], ln_b[layer, 0])
        if layer % 2 == 0:
            fw = (ffn_w_gu[layer // 2], ffn_w_down[layer // 2])
            fp, fs = swiglu(yp, *fw), swiglu(ys, *fw)
        else:
            fw = (moe_router[layer // 2], moe_w_gu[layer // 2], moe_w_down[layer // 2])
            fp, fs = moe_swiglu(yp, *fw), moe_swiglu(ys, *fw)
        yp = layer_norm(ALPHA * yp + fp, ln_g[layer, 1], ln_b[layer, 1])
        ys = layer_norm(ALPHA * ys + fs, ln_g[layer, 1], ln_b[layer, 1])
    return (yp, ys, jnp.stack(hg_p), jnp.stack(hg_s), jnp.stack(rc_p), jnp.stack(rc_s), jnp.stack(rh_p), jnp.stack(rh_s), jnp.stack(cm_p), jnp.stack(cm_s), jnp.stack(sl_p), jnp.stack(sl_s), jnp.stack(sw_p), jnp.stack(sw_s))
```

```python
import functools

import jax
import jax.numpy as jnp
from jax import lax
from jax.experimental import pallas as pl
from jax.experimental.pallas import tpu as pltpu

F32 = jnp.float32
BF16 = jnp.bfloat16
HIGHEST = lax.Precision.HIGHEST

D_MODEL = 1024
DEPTH = 4
ALPHA = (2 * DEPTH) ** 0.25
LN_EPS = 1e-5
RMS_EPS = 1e-6
HG_HEADS = 8
HG_DK = 128
HG_CHUNK = 64
HG_SUB = 16
LB_FLOOR = 1e-30
RG_WIDTH = 1280
RG_BLOCKS = 10
RG_BS = 128
RG_CONV = 4
RG_C = 8.0
D_FF = 2816
N_EXPERTS = 8

VMEM_LIMIT_BYTES = 56 * 1024 * 1024


def _params(*sem):
    return pltpu.CompilerParams(dimension_semantics=sem, vmem_limit_bytes=VMEM_LIMIT_BYTES)


def _layer_norm(v, g, b):
    mu = jnp.mean(v, axis=-1, keepdims=True)
    vc = v - mu
    var = jnp.mean(vc * vc, axis=-1, keepdims=True)
    return vc * lax.rsqrt(var + LN_EPS) * g + b


def _sigmoid(x):
    return 1.0 / (1.0 + jnp.exp(-x))


def _silu(x):
    return x * _sigmoid(x)


def _mm_kernel(x_ref, w_ref, o_ref, xb_ref):
    @pl.when(pl.program_id(1) == 0)
    def _():
        xb_ref[...] = x_ref[...].astype(BF16)

    o_ref[...] = jnp.dot(xb_ref[...], w_ref[...], preferred_element_type=F32).astype(o_ref.dtype)


def matmul(x, w, *, tm, tn, out_dtype=F32):
    m, k = x.shape
    n = w.shape[1]
    return pl.pallas_call(
        _mm_kernel,
        grid=(m // tm, n // tn),
        in_specs=[pl.BlockSpec((tm, k), lambda i, j: (i, 0)),
                  pl.BlockSpec((k, tn), lambda i, j: (0, j))],
        out_specs=pl.BlockSpec((tm, tn), lambda i, j: (i, j)),
        out_shape=jax.ShapeDtypeStruct((m, n), out_dtype),
        scratch_shapes=[pltpu.VMEM((tm, k), BF16)],
        compiler_params=_params("parallel", "arbitrary"),
        name="matmul",
    )(x, w)


def _mm_res_ln_kernel(a_ref, w_ref, r_ref, g_ref, b_ref, o_ref):
    m = jnp.dot(a_ref[...].astype(BF16), w_ref[...], preferred_element_type=F32)
    o_ref[...] = _layer_norm(ALPHA * r_ref[...] + m, g_ref[...], b_ref[...])


def matmul_residual_ln(a, w, res, g, b, *, tm):
    m, k = a.shape
    d = w.shape[1]
    return pl.pallas_call(
        _mm_res_ln_kernel,
        grid=(m // tm,),
        in_specs=[pl.BlockSpec((tm, k), lambda i: (i, 0)),
                  pl.BlockSpec((k, d), lambda i: (0, 0)),
                  pl.BlockSpec((tm, d), lambda i: (i, 0)),
                  pl.BlockSpec((1, d), lambda i: (0, 0)),
                  pl.BlockSpec((1, d), lambda i: (0, 0))],
        out_specs=pl.BlockSpec((tm, d), lambda i: (i, 0)),
        out_shape=jax.ShapeDtypeStruct((m, d), F32),
        compiler_params=_params("parallel"),
        name="matmul_residual_ln",
    )(a, w, res, g.reshape(1, d), b.reshape(1, d))


def _ffn_kernel(x_ref, gate_ref, wg_ref, wu_ref, wd_ref, g_ref, b_ref, o_ref, xb_ref, acc_ref, y_ref, *, n_e, n_f):
    e = pl.program_id(1)
    f = pl.program_id(2)

    @pl.when((e == 0) & (f == 0))
    def _():
        xb_ref[...] = x_ref[...].astype(BF16)
        y_ref[...] = jnp.zeros_like(y_ref)

    xb = xb_ref[...]
    h = jnp.dot(xb, wg_ref[0], preferred_element_type=F32)
    u = jnp.dot(xb, wu_ref[0], preferred_element_type=F32)
    part = jnp.dot((_silu(h) * u).astype(BF16), wd_ref[0], preferred_element_type=F32)

    @pl.when(f == 0)
    def _():
        acc_ref[...] = part

    @pl.when(f > 0)
    def _():
        acc_ref[...] += part

    @pl.when(f == n_f - 1)
    def _():
        y_ref[...] += gate_ref[0] * acc_ref[...]

    @pl.when((e == n_e - 1) & (f == n_f - 1))
    def _():
        o_ref[...] = _layer_norm(ALPHA * x_ref[...] + y_ref[...], g_ref[...], b_ref[...])


def gated_ffn_residual_ln(x, gate, w_gu, w_down, g, b, *, tm, tf):
    m, d = x.shape
    n_e, _, ff2 = w_gu.shape
    ff = ff2 // 2
    n_f = ff // tf
    return pl.pallas_call(
        functools.partial(_ffn_kernel, n_e=n_e, n_f=n_f),
        grid=(m // tm, n_e, n_f),
        in_specs=[pl.BlockSpec((tm, d), lambda i, e, f: (i, 0)),
                  pl.BlockSpec((1, tm, 1), lambda i, e, f: (e, i, 0)),
                  pl.BlockSpec((1, d, tf), lambda i, e, f: (e, 0, f)),
                  pl.BlockSpec((1, d, tf), lambda i, e, f: (e, 0, n_f + f)),
                  pl.BlockSpec((1, tf, d), lambda i, e, f: (e, f, 0)),
                  pl.BlockSpec((1, d), lambda i, e, f: (0, 0)),
                  pl.BlockSpec((1, d), lambda i, e, f: (0, 0))],
        out_specs=pl.BlockSpec((tm, d), lambda i, e, f: (i, 0)),
        out_shape=jax.ShapeDtypeStruct((m, d), F32),
        scratch_shapes=[pltpu.VMEM((tm, d), BF16), pltpu.VMEM((tm, d), F32), pltpu.VMEM((tm, d), F32)],
        compiler_params=_params("parallel", "arbitrary", "arbitrary"),
        name="gated_ffn_residual_ln",
    )(x, gate, w_gu, w_gu, w_down, g.reshape(1, d), b.reshape(1, d))


def _router_kernel(x_ref, w_ref, o_ref):
    logits = jnp.dot(x_ref[...], w_ref[...], preferred_element_type=F32, precision=HIGHEST)
    lane = lax.broadcasted_iota(jnp.int32, logits.shape, 1)
    big = jnp.int32(N_EXPERTS)
    m1 = jnp.max(logits, axis=-1, keepdims=True)
    i1 = jnp.min(jnp.where(logits == m1, lane, big), axis=-1, keepdims=True)
    rest = jnp.where(lane == i1, -jnp.inf, logits)
    m2 = jnp.max(rest, axis=-1, keepdims=True)
    i2 = jnp.min(jnp.where(rest == m2, lane, big), axis=-1, keepdims=True)
    e2 = jnp.exp(m2 - m1)
    denom = 1.0 + e2
    o_ref[...] = jnp.where(lane == i1, 1.0 / denom, jnp.where(lane == i2, e2 / denom, 0.0))


def moe_gates(x, w_router, *, tm):
    m, d = x.shape
    n_e = w_router.shape[1]
    return pl.pallas_call(
        _router_kernel,
        grid=(m // tm,),
        in_specs=[pl.BlockSpec((tm, d), lambda i: (i, 0)),
                  pl.BlockSpec((d, n_e), lambda i: (0, 0))],
        out_specs=pl.BlockSpec((tm, n_e), lambda i: (i, 0)),
        out_shape=jax.ShapeDtypeStruct((m, n_e), F32),
        compiler_params=_params("parallel"),
        name="moe_gates",
    )(x, w_router)


def _hgrn_lower_bound(lbp, layer):
    m = jnp.max(lbp, axis=0, keepdims=True)
    e = jnp.exp(lbp - m)
    p = e / jnp.sum(e, axis=0, keepdims=True)
    c = p[0:1]
    for i in range(1, layer + 1):
        c = c + p[i:i + 1]
    return c - p[0:1]


def _hgrn_gates(q, fx, lb):
    log_lb = jnp.log(jnp.maximum(lb, LB_FLOOR))
    log1m = jnp.log1p(-lb)
    log_sig = -(jnp.maximum(-fx, 0.0) + jnp.log1p(jnp.exp(-jnp.abs(fx))))
    b2 = log1m + log_sig
    log_f = jnp.maximum(log_lb, b2) + jnp.log1p(jnp.exp(-jnp.abs(log_lb - b2)))
    kk = (1.0 - lb) * _sigmoid(-fx) - (jnp.maximum(lb, LB_FLOOR) - lb)
    return _silu(q), log_f, kk


def _hgrn_out_norm(o, g, ng):
    o = o * lax.rsqrt(jnp.mean(o * o, axis=-1, keepdims=True) + RMS_EPS)
    return o * ng * _silu(g)


def _dot_nt(a, b):
    return lax.dot_general(a, b, (((1,), (1,)), ((), ())), preferred_element_type=F32)


def _dot_tn(a, b):
    return lax.dot_general(a, b, (((0,), (0,)), ((), ())), preferred_element_type=F32)


def _gla_kernel(q_ref, f_ref, i_ref, g_ref, lbp_ref, ng_ref, o_ref, s_out_ref, st_ref, *, layer, n_t, tl):
    t = pl.program_id(2)
    c_len, sub = HG_CHUNK, HG_SUB
    n_sub = c_len // sub
    n_stack = sub * (n_sub * (n_sub - 1) // 2)

    @pl.when(t == 0)
    def _():
        st_ref[...] = jnp.zeros_like(st_ref)

    lb = _hgrn_lower_bound(lbp_ref[...], layer)
    ng = ng_ref[...]
    row = lax.broadcasted_iota(jnp.int32, (c_len, c_len), 0)
    col = lax.broadcasted_iota(jnp.int32, (c_len, c_len), 1)
    tri = (col <= row).astype(F32)
    srow = lax.broadcasted_iota(jnp.int32, (c_len, n_stack), 0) // sub
    scol = lax.broadcasted_iota(jnp.int32, (c_len, n_stack), 1)
    part = jnp.zeros((c_len, n_stack), jnp.int32)
    for p in range(1, n_sub):
        part = part + (scol >= sub * (p * (p - 1) // 2)).astype(jnp.int32)
    inter_mask = srow == part
    row_s = lax.broadcasted_iota(jnp.int32, (sub, 1), 0)

    def chunk(c, carry):
        r = pl.ds(pl.multiple_of(c * c_len, c_len), c_len)
        qs, log_f, kk = _hgrn_gates(q_ref[r, :], f_ref[r, :], lb)
        v = i_ref[r, :]
        cum = jnp.dot(tri, log_f, preferred_element_type=F32, precision=HIGHEST)
        last = cum[c_len - 1:c_len, :]
        st = st_ref[...]
        o = _dot_nt((qs * jnp.exp(cum)).astype(BF16), st.astype(BF16))
        bounds = [cum[sub * a - 1:sub * a, :] for a in range(1, n_sub)]
        cq = jnp.concatenate([cum[0:sub]] + [jnp.broadcast_to(b, (sub, HG_DK)) for b in bounds], axis=0)
        q_rel = (qs * jnp.exp(cum - cq)).astype(BF16)
        k_st = jnp.concatenate([kk[0:sub * a] * jnp.exp(bounds[a - 1] - cum[0:sub * a]) for a in range(1, n_sub)], axis=0)
        v_st = jnp.concatenate([v[0:sub * a] for a in range(1, n_sub)], axis=0)
        att = jnp.where(inter_mask, _dot_nt(q_rel, k_st.astype(BF16)), 0.0)
        o = o + jnp.dot(att.astype(BF16), v_st.astype(BF16), preferred_element_type=F32)
        parts = []
        for a in range(n_sub):
            sl = slice(sub * a, sub * a + sub)
            qa, ka, va, ca = qs[sl], kk[sl], v[sl], cum[sl]
            od = o[sl]
            for s in range(sub):
                e = jnp.exp(jnp.minimum(ca - ca[s:s + 1], 0.0))
                w = jnp.sum(qa * (ka[s:s + 1] * e), axis=-1, keepdims=True)
                od = od + jnp.where(row_s >= s, w, 0.0) * va[s:s + 1]
            parts.append(od)
        o = jnp.concatenate(parts, axis=0)
        o_ref[r, :] = _hgrn_out_norm(o, g_ref[r, :], ng).astype(o_ref.dtype)
        k_end = (kk * jnp.exp(last - cum)).astype(BF16)
        st_ref[...] = st * jnp.exp(last) + _dot_tn(v.astype(BF16), k_end)
        return carry

    lax.fori_loop(0, tl // c_len, chunk, 0)

    @pl.when(t == n_t - 1)
    def _():
        s_out_ref[0, 0] = st_ref[...].T


def hgrn_prompt(z, lb_param, norm_g, *, layer, batch, tl):
    m = z.shape[0]
    n_t = m // batch // tl
    h, dk = HG_HEADS, HG_DK
    zspec = lambda off: pl.BlockSpec((tl, dk), lambda b, hh, t: (b * n_t + t, off + hh))
    return pl.pallas_call(
        functools.partial(_gla_kernel, layer=layer, n_t=n_t, tl=tl),
        grid=(batch, h, n_t),
        in_specs=[zspec(0), zspec(h), zspec(2 * h), zspec(3 * h),
                  pl.BlockSpec((DEPTH, dk), lambda b, hh, t: (0, hh)),
                  pl.BlockSpec((1, dk), lambda b, hh, t: (0, hh))],
        out_specs=[pl.BlockSpec((tl, dk), lambda b, hh, t: (b * n_t + t, hh)),
                   pl.BlockSpec((1, 1, dk, dk), lambda b, hh, t: (b, hh, 0, 0))],
        out_shape=[jax.ShapeDtypeStruct((m, h * dk), BF16),
                   jax.ShapeDtypeStruct((batch, h, dk, dk), F32)],
        scratch_shapes=[pltpu.VMEM((dk, dk), F32)],
        compiler_params=_params("parallel", "parallel", "arbitrary"),
        name="hgrn_prompt",
    )(z, z, z, z, lb_param, norm_g.reshape(1, h * dk))


def _hgrn_step_kernel(q_ref, f_ref, i_ref, g_ref, lbp_ref, ng_ref, s0_ref, o_ref, s_ref,
                      qt_ref, ft_ref, kt_ref, oacc_ref, *, layer, n_j, sb):
    j = pl.program_id(1)
    n_seq = q_ref.shape[0]

    @pl.when(j == 0)
    def _():
        lb = _hgrn_lower_bound(lbp_ref[...], layer)
        qs, log_f, kk = _hgrn_gates(q_ref[...], f_ref[...], lb)
        qt_ref[...] = qs.T
        ft_ref[...] = jnp.exp(log_f).T
        kt_ref[...] = kk.T

    seq = lax.broadcasted_iota(jnp.int32, (n_seq, HG_DK), 0)
    for i in range(sb):
        b = j * sb + i
        pick = (seq == b).astype(F32)
        col = lambda ref: jnp.dot(ref[...], pick, preferred_element_type=F32, precision=HIGHEST)
        s_new = s0_ref[i, 0] * col(ft_ref) + col(kt_ref) * i_ref[pl.ds(b, 1), :]
        s_ref[i, 0] = s_new
        oacc_ref[pl.ds(b, 1), :] = jnp.sum(col(qt_ref) * s_new, axis=0, keepdims=True)

    @pl.when(j == n_j - 1)
    def _():
        o_ref[...] = _hgrn_out_norm(oacc_ref[...], g_ref[...], ng_ref[...]).astype(o_ref.dtype)


def hgrn_step(z, s0, lb_param, norm_g, *, layer, sb):
    n_seq = z.shape[0]
    h, dk = HG_HEADS, HG_DK
    n_j = n_seq // sb
    zspec = lambda off: pl.BlockSpec((n_seq, dk), lambda hh, j: (0, off + hh))
    return pl.pallas_call(
        functools.partial(_hgrn_step_kernel, layer=layer, n_j=n_j, sb=sb),
        grid=(h, n_j),
        in_specs=[zspec(0), zspec(h), zspec(2 * h), zspec(3 * h),
                  pl.BlockSpec((DEPTH, dk), lambda hh, j: (0, hh)),
                  pl.BlockSpec((1, dk), lambda hh, j: (0, hh)),
                  pl.BlockSpec((sb, 1, dk, dk), lambda hh, j: (j, hh, 0, 0))],
        out_specs=[pl.BlockSpec((n_seq, dk), lambda hh, j: (0, hh)),
                   pl.BlockSpec((sb, 1, dk, dk), lambda hh, j: (j, hh, 0, 0))],
        out_shape=[jax.ShapeDtypeStruct((n_seq, h * dk), BF16),
                   jax.ShapeDtypeStruct(s0.shape, F32)],
        scratch_shapes=[pltpu.VMEM((dk, n_seq), F32)] * 3 + [pltpu.VMEM((n_seq, dk), F32)],
        compiler_params=_params("parallel", "arbitrary"),
        name="hgrn_step",
    )(z, z, z, z, lb_param, norm_g.reshape(1, h * dk), s0)


def _gelu_tanh(x):
    return x * (0.5 * (1.0 + jnp.tanh(0.7978845608028654 * (x + 0.044715 * (x * x * x)))))


def _rg_decay_and_input(xc, wg_ref, bg_ref, lam_ref):
    xb = xc.astype(BF16)
    gates = []
    for gi in range(2):
        blocks = [jnp.dot(xb[:, n * RG_BS:(n + 1) * RG_BS], wg_ref[gi, n], preferred_element_type=F32)
                  for n in range(RG_BLOCKS)]
        gates.append(jnp.concatenate(blocks, axis=1) + bg_ref[gi:gi + 1, :])
    r = _sigmoid(gates[0])
    i = _sigmoid(gates[1])
    neg_lam = -lam_ref[...]
    softplus = jnp.maximum(neg_lam, 0.0) + jnp.log1p(jnp.exp(-jnp.abs(neg_lam)))
    log_a = -RG_C * r * softplus
    one_minus_a2 = -jnp.tanh(log_a) * (jnp.exp(2.0 * log_a) + 1.0)
    return jnp.exp(log_a), jnp.sqrt(one_minus_a2) * i * xc


def _rg_prompt_kernel(y_ref, x_ref, cw_ref, cb_ref, wg_ref, bg_ref, lam_ref,
                      o_ref, conv_out_ref, h_out_ref, xpad_ref, h_ref, *, n_t, tl):
    t = pl.program_id(1)
    halo = 8

    @pl.when(t == 0)
    def _():
        xpad_ref[0:halo, :] = jnp.zeros((halo, RG_WIDTH), F32)
        h_ref[...] = jnp.zeros_like(h_ref)

    x = x_ref[...]
    xpad_ref[halo:halo + tl, :] = x
    xc = cb_ref[...]
    for j in range(RG_CONV):
        off = halo - (RG_CONV - 1) + j
        xc = xc + xpad_ref[off:off + tl, :] * cw_ref[j:j + 1, :]
    xpad_ref[0:halo, :] = x[tl - halo:tl, :]
    a, u = _rg_decay_and_input(xc, wg_ref, bg_ref, lam_ref)
    row = lax.broadcasted_iota(jnp.int32, (tl, 1), 0)
    s = 1
    while s < tl:
        keep = row >= s
        a_prev = jnp.where(keep, pltpu.roll(a, s, 0), 1.0)
        u_prev = jnp.where(keep, pltpu.roll(u, s, 0), 0.0)
        u = a * u_prev + u
        a = a * a_prev
        s *= 2
    h = a * h_ref[...] + u
    h_ref[...] = h[tl - 1:tl, :]
    o_ref[...] = (_gelu_tanh(y_ref[...]) * h).astype(o_ref.dtype)

    @pl.when(t == n_t - 1)
    def _():
        conv_out_ref[0] = x[tl - (RG_CONV - 1):tl, :]
        h_out_ref[0] = h[tl - 1:tl, :]


def rglru_prompt(z, conv_w, conv_b, w_gate, b_gate, lam, *, batch, tl):
    m = z.shape[0]
    n_t = m // batch // tl
    w = RG_WIDTH
    const = lambda shape: pl.BlockSpec(shape, lambda b, t: (0,) * len(shape))
    return pl.pallas_call(
        functools.partial(_rg_prompt_kernel, n_t=n_t, tl=tl),
        grid=(batch, n_t),
        in_specs=[pl.BlockSpec((tl, w), lambda b, t: (b * n_t + t, 0)),
                  pl.BlockSpec((tl, w), lambda b, t: (b * n_t + t, 1)),
                  const((RG_CONV, w)), const((1, w)), const((2, RG_BLOCKS, RG_BS, RG_BS)), const((2, w)), const((1, w))],
        out_specs=[pl.BlockSpec((tl, w), lambda b, t: (b * n_t + t, 0)),
                   pl.BlockSpec((1, RG_CONV - 1, w), lambda b, t: (b, 0, 0)),
                   pl.BlockSpec((1, 1, w), lambda b, t: (b, 0, 0))],
        out_shape=[jax.ShapeDtypeStruct((m, w), BF16),
                   jax.ShapeDtypeStruct((batch, RG_CONV - 1, w), F32),
                   jax.ShapeDtypeStruct((batch, 1, w), F32)],
        scratch_shapes=[pltpu.VMEM((tl + 8, w), F32), pltpu.VMEM((1, w), F32)],
        compiler_params=_params("parallel", "arbitrary"),
        name="rglru_prompt",
    )(z, z, conv_w, conv_b.reshape(1, w), w_gate, b_gate, lam.reshape(1, w))


def _rg_step_kernel(y_ref, x_ref, buf_ref, h0_ref, cw_ref, cb_ref, wg_ref, bg_ref, lam_ref,
                    o_ref, conv_out_ref, h_out_ref):
    x = x_ref[...]
    xc = cb_ref[...]
    for j in range(RG_CONV - 1):
        xc = xc + buf_ref[:, j, :] * cw_ref[j:j + 1, :]
    xc = xc + x * cw_ref[RG_CONV - 1:RG_CONV, :]
    a, u = _rg_decay_and_input(xc, wg_ref, bg_ref, lam_ref)
    h = a * h0_ref[...] + u
    o_ref[...] = (_gelu_tanh(y_ref[...]) * h).astype(o_ref.dtype)
    h_out_ref[...] = h
    for j in range(RG_CONV - 2):
        conv_out_ref[:, j, :] = buf_ref[:, j + 1, :]
    conv_out_ref[:, RG_CONV - 2, :] = x


def rglru_step(z, conv_buf, h0, conv_w, conv_b, w_gate, b_gate, lam):
    n_seq = z.shape[0]
    w = RG_WIDTH
    const = lambda shape: pl.BlockSpec(shape, lambda i: (0,) * len(shape))
    return pl.pallas_call(
        _rg_step_kernel,
        grid=(1,),
        in_specs=[pl.BlockSpec((n_seq, w), lambda i: (0, 0)),
                  pl.BlockSpec((n_seq, w), lambda i: (0, 1)),
                  const((n_seq, RG_CONV - 1, w)), const((n_seq, w)),
                  const((RG_CONV, w)), const((1, w)), const((2, RG_BLOCKS, RG_BS, RG_BS)), const((2, w)), const((1, w))],
        out_specs=[const((n_seq, w)), const((n_seq, RG_CONV - 1, w)), const((n_seq, w))],
        out_shape=[jax.ShapeDtypeStruct((n_seq, w), BF16),
                   jax.ShapeDtypeStruct((n_seq, RG_CONV - 1, w), F32),
                   jax.ShapeDtypeStruct((n_seq, w), F32)],
        compiler_params=_params("arbitrary"),
        name="rglru_step",
    )(z, z, conv_buf, h0, conv_w, conv_b.reshape(1, w), w_gate, b_gate, lam.reshape(1, w))


NSA_HEADS = 16
NSA_KVH = 4
NSA_GROUP = 4
NSA_DH = 64
NSA_KVW = NSA_KVH * NSA_DH
NSA_ROW = 2 * NSA_KVW
CMP_BLOCK = 32
CMP_HID = 128
SLC_BLOCK = 64
SLC_TOPK = 16
WINDOW = 512
NSA_QBLOCK = 128
MASK_NEG = -1e30
SEL_BIG = 1e9
LANES = 128
SEL_KEYS = 512


def _masked_softmax(s, mask):
    s = jnp.where(mask, s, MASK_NEG)
    m = jnp.max(s, axis=-1, keepdims=True)
    e = jnp.where(mask, jnp.exp(s - m), 0.0)
    return e / jnp.maximum(jnp.sum(e, axis=-1, keepdims=True), 1e-30)


def _top_blocks(score, k):
    lane = lax.broadcasted_iota(jnp.int32, score.shape, 1)
    n = score.shape[1]

    def pick(_, carry):
        sc, sel = carry
        mx = jnp.max(sc, axis=-1, keepdims=True)
        idx = jnp.min(jnp.where(sc == mx, lane, n), axis=-1, keepdims=True)
        hit = lane == idx
        return jnp.where(hit, -jnp.inf, sc), jnp.where(hit, 1.0, sel)

    return lax.fori_loop(0, k, pick, (score, jnp.zeros(score.shape, F32)))[1]


def _compress_rows(x_refs, pos_ref, w1_ref, w2_ref, n_blk):
    half = n_blk // 2
    tiles = NSA_KVW // LANES
    outs = []
    for c in range(2):
        def body(l, acc, c=c):
            pos = pos_ref[c, pl.ds(l, 1), :]
            heads = []
            for tl in range(tiles):
                ref = x_refs[c * tiles + tl]
                xe = ref[pl.ds(l, half, stride=2 * CMP_BLOCK), :]
                xo = ref[pl.ds(l + CMP_BLOCK, half, stride=2 * CMP_BLOCK), :]
                x = jnp.concatenate([xe, xo], axis=0) + pos[:, tl * LANES:(tl + 1) * LANES]
                heads += [x[:, :NSA_DH], x[:, NSA_DH:]]
            xs = jnp.concatenate(heads, axis=0)
            return acc + jnp.dot(xs.astype(BF16), w1_ref[c, l], preferred_element_type=F32)

        acc = lax.fori_loop(0, CMP_BLOCK, body, jnp.zeros((NSA_KVH * n_blk, CMP_HID), F32))
        out = jnp.dot(_silu(acc).astype(BF16), w2_ref[c], preferred_element_type=F32)
        outs.append(jnp.concatenate([out[h * n_blk:(h + 1) * n_blk] for h in range(NSA_KVH)], axis=1))
    return jnp.concatenate(outs, axis=1)


def _compress_kernel(x0_ref, x1_ref, x2_ref, x3_ref, pos_ref, w1_ref, w2_ref, even_ref, odd_ref, *, n_blk):
    full = _compress_rows((x0_ref, x1_ref, x2_ref, x3_ref), pos_ref, w1_ref, w2_ref, n_blk)
    even_ref[0] = full[:n_blk // 2]
    odd_ref[0] = full[n_blk // 2:]


def nsa_compress(kv, col_block, pos4, w1, w2, *, batch, n_blk):
    m = kv.shape[0]
    n_t = m // batch // (n_blk * CMP_BLOCK)
    half = n_blk // 2
    const = lambda shape: pl.BlockSpec(shape, lambda b, t: (0,) * len(shape))
    out = jax.ShapeDtypeStruct((batch, n_t * half, NSA_ROW), F32)
    tiles = NSA_ROW // LANES
    lane_tile = lambda i: pl.BlockSpec((n_blk * CMP_BLOCK, LANES), lambda b, t: (b * n_t + t, col_block * tiles + i))
    return pl.pallas_call(
        functools.partial(_compress_kernel, n_blk=n_blk),
        grid=(batch, n_t),
        in_specs=[lane_tile(i) for i in range(tiles)]
        + [const((2, CMP_BLOCK, NSA_KVW)), const((2, CMP_BLOCK, NSA_DH, CMP_HID)), const((2, CMP_HID, NSA_DH))],
        out_specs=[pl.BlockSpec((1, half, NSA_ROW), lambda b, t: (b, t, 0))] * 2,
        out_shape=[out, out],
        compiler_params=_params("parallel", "parallel"),
        name="nsa_compress",
    )(kv, kv, kv, kv, pos4, w1, w2)


def nsa_prepare_weights(w_in, pos, w1, w2):
    hq = NSA_HEADS * NSA_DH
    kv_end = hq + 3 * NSA_ROW
    w_qkv = w_in[:, :kv_end].astype(BF16)
    w_kvb = w_in[:, hq + NSA_ROW:kv_end].astype(BF16)
    w_g = w_in[:, kv_end:].reshape(-1, NSA_KVH, 3 * NSA_GROUP)
    w_g = jnp.pad(w_g, ((0, 0), (0, 0), (0, LANES - 3 * NSA_GROUP))).reshape(-1, NSA_KVH * LANES).astype(BF16)
    pos4 = jnp.tile(pos, (1, 1, NSA_KVH))
    return w_qkv, w_kvb, w_g, pos4, w1.astype(BF16), w2.astype(BF16)


def _stack_group_queries(q, par):
    half = lax.broadcasted_iota(jnp.int32, (q.shape[0], LANES), 1) // NSA_DH
    parts = []
    for g in range(NSA_GROUP):
        qg = q[:, g * NSA_DH:(g + 1) * NSA_DH]
        parts.append(jnp.where(half == par, jnp.concatenate([qg, qg], axis=1), 0.0))
    return jnp.concatenate(parts, axis=0).astype(BF16)


def _nsa_prompt_kernel(q_ref, gp_ref, ck_ref, cv_ref, ks_ref, vs_ref, kw_ref, vw_ref, o_ref):
    kvh = pl.program_id(1)
    j = pl.program_id(2)
    par = kvh % 2
    qb, grp = NSA_QBLOCK, NSA_GROUP
    rows = grp * qb
    qp = _stack_group_queries(q_ref[...] * (NSA_DH ** -0.5), par)
    t_pos = j * qb + lax.broadcasted_iota(jnp.int32, (qb, 1), 0)

    def grouped(fn, s, mask):
        n = s.shape[-1]
        return fn(s.reshape(grp, qb, n), mask[None]).reshape(rows, n)

    n_c = ck_ref.shape[1]
    n_s = n_c // 2
    ck = ck_ref[0].astype(BF16)
    cv = cv_ref[0].astype(BF16)
    nn = lax.broadcasted_iota(jnp.int32, (1, n_c), 1)
    c_blk = jnp.where(nn < n_s, 2 * nn, 2 * (nn - n_s) + 1)
    c_vis = (c_blk * CMP_BLOCK + CMP_BLOCK - 1) <= t_pos
    p_c = grouped(_masked_softmax, _dot_nt(qp, ck), c_vis)
    o_c = jnp.dot(p_c.astype(BF16), cv, preferred_element_type=F32)
    imp = p_c[0:qb]
    for g in range(1, grp):
        imp = imp + p_c[g * qb:(g + 1) * qb]
    imp = imp[:, :n_s] + imp[:, n_s:]
    blk = lax.broadcasted_iota(jnp.int32, (qb, n_s), 1)
    cur = t_pos // SLC_BLOCK
    forced = (blk == 0) | (blk == cur) | (blk == cur - 1)
    sel = _top_blocks(jnp.where(forced, SEL_BIG, jnp.where(blk <= cur, imp, -SEL_BIG)), SLC_TOPK)
    sel_b = sel.astype(BF16)

    per = SEL_KEYS // SLC_BLOCK
    e_row = lax.broadcasted_iota(jnp.int32, (n_s, SEL_KEYS), 0)
    e_col = lax.broadcasted_iota(jnp.int32, (n_s, SEL_KEYS), 1) // SLC_BLOCK
    k_off = lax.broadcasted_iota(jnp.int32, (1, SEL_KEYS), 1)

    def sel_step(kb, carry):
        m, l, acc = carry
        r = pl.ds(pl.multiple_of(kb * SEL_KEYS, SEL_KEYS), SEL_KEYS)
        expand = (e_row == kb * per + e_col).astype(BF16)
        chosen = jnp.dot(sel_b, expand, preferred_element_type=F32) > 0.5
        ok = (chosen & ((kb * SEL_KEYS + k_off) <= t_pos))[None]
        s = jnp.where(ok, _dot_nt(qp, ks_ref[r, :]).reshape(grp, qb, SEL_KEYS), MASK_NEG)
        m_new = jnp.maximum(m, jnp.max(s, axis=-1, keepdims=True))
        alpha = jnp.exp(m - m_new)
        e = jnp.where(ok, jnp.exp(s - m_new), 0.0)
        l = alpha * l + jnp.sum(e, axis=-1, keepdims=True)
        pv = jnp.dot(e.reshape(rows, SEL_KEYS).astype(BF16), vs_ref[r, :], preferred_element_type=F32)
        return m_new, l, alpha * acc + pv.reshape(grp, qb, LANES)

    init = (jnp.full((grp, qb, 1), MASK_NEG, F32), jnp.zeros((grp, qb, 1), F32), jnp.zeros((grp, qb, LANES), F32))
    n_kb = (j * qb + qb - 1) // SEL_KEYS + 1
    _, l, acc = lax.fori_loop(0, n_kb, sel_step, init)
    o_s = (acc / jnp.maximum(l, 1e-30)).reshape(rows, LANES)

    n_w = WINDOW + qb
    w0 = pl.multiple_of(jnp.maximum(j * qb - WINDOW, 0), qb)
    dist = t_pos - (w0 + lax.broadcasted_iota(jnp.int32, (1, n_w), 1))
    w_vis = (dist >= 0) & (dist <= WINDOW)
    p_w = grouped(_masked_softmax, _dot_nt(qp, kw_ref[pl.ds(w0, n_w), :]), w_vis)
    o_w = jnp.dot(p_w.astype(BF16), vw_ref[pl.ds(w0, n_w), :], preferred_element_type=F32)

    gates = _sigmoid(gp_ref[...])
    gate = lambda br: jnp.concatenate(
        [jnp.broadcast_to(gates[:, 3 * g + br:3 * g + br + 1], (qb, LANES)) for g in range(grp)], axis=0)
    o = gate(0) * o_c + gate(1) * o_s + gate(2) * o_w
    half = lax.broadcasted_iota(jnp.int32, (rows, LANES), 1) // NSA_DH
    o = jnp.where(half == par, o, 0.0)
    o = o[:, :NSA_DH] + o[:, NSA_DH:]
    o_ref[...] = jnp.concatenate([o[g * qb:(g + 1) * qb] for g in range(grp)], axis=1).astype(o_ref.dtype)


def nsa_prompt_attention(qkv, gates_pre, ckv, kvb, *, batch):
    m = qkv.shape[0]
    l_seq = m // batch
    nq = l_seq // NSA_QBLOCK
    n_c = ckv.shape[1]
    qw = NSA_GROUP * NSA_DH
    pair = NSA_KVW // LANES
    kv_spec = lambda part: pl.BlockSpec((l_seq, LANES), lambda b, h, j: (b, part * pair + h // 2))
    return pl.pallas_call(
        _nsa_prompt_kernel,
        grid=(batch, NSA_KVH, nq),
        in_specs=[pl.BlockSpec((NSA_QBLOCK, qw), lambda b, h, j: (b * nq + j, h)),
                  pl.BlockSpec((NSA_QBLOCK, LANES), lambda b, h, j: (b * nq + j, h)),
                  pl.BlockSpec((1, n_c, LANES), lambda b, h, j: (b, 0, h // 2)),
                  pl.BlockSpec((1, n_c, LANES), lambda b, h, j: (b, 0, pair + h // 2)),
                  kv_spec(0), kv_spec(1), kv_spec(2), kv_spec(3)],
        out_specs=pl.BlockSpec((NSA_QBLOCK, qw), lambda b, h, j: (b * nq + j, h)),
        out_shape=jax.ShapeDtypeStruct((m, NSA_HEADS * NSA_DH), BF16),
        compiler_params=_params("parallel", "parallel", "arbitrary"),
        name="nsa_prompt_attention",
    )(qkv, gates_pre, ckv, ckv, kvb, kvb, kvb, kvb)


def _paged_compress_kernel(pt_ref, *refs, n_pages, page):
    del pt_ref
    pages = refs[:n_pages]
    pos_ref, w1_ref, w2_ref, even_ref, odd_ref = refs[n_pages:n_pages + 5]
    rows = refs[n_pages + 5:]
    for p in range(n_pages):
        for i, r in enumerate(rows):
            r[p * page:(p + 1) * page, :] = pages[p][0, :, i * LANES:(i + 1) * LANES]
    n_blk = n_pages * page // CMP_BLOCK
    full = _compress_rows(rows, pos_ref, w1_ref, w2_ref, n_blk)
    even_ref[0] = full[:n_blk // 2]
    odd_ref[0] = full[n_blk // 2:]


def _page_specs(n_pages, page):
    return [pl.BlockSpec((1, page, NSA_ROW), lambda b, pt, p=p: (pt[b, p], 0, 0)) for p in range(n_pages)]


def nsa_compress_paged(cache, page_table, pos4, w1, w2):
    n_seq, n_pages = page_table.shape
    page = cache.shape[1]
    half = n_pages * page // CMP_BLOCK // 2
    const = lambda shape: pl.BlockSpec(shape, lambda b, pt: (0,) * len(shape))
    out = jax.ShapeDtypeStruct((n_seq, half, NSA_ROW), F32)
    return pl.pallas_call(
        functools.partial(_paged_compress_kernel, n_pages=n_pages, page=page),
        grid_spec=pltpu.PrefetchScalarGridSpec(
            num_scalar_prefetch=1,
            grid=(n_seq,),
            in_specs=_page_specs(n_pages, page)
            + [const((2, CMP_BLOCK, NSA_KVW)), const((2, CMP_BLOCK, NSA_DH, CMP_HID)), const((2, CMP_HID, NSA_DH))],
            out_specs=[pl.BlockSpec((1, half, NSA_ROW), lambda b, pt: (b, 0, 0))] * 2,
            scratch_shapes=[pltpu.VMEM((n_pages * page, LANES), F32)] * (NSA_ROW // LANES)),
        out_shape=[out, out],
        compiler_params=_params("arbitrary"),
        name="nsa_compress_paged",
    )(page_table, *([cache] * n_pages), pos4, w1, w2)


def _softmax_with_new_key(s, ok, s_new, new_ok):
    s = jnp.where(ok, s, MASK_NEG)
    s_new = jnp.where(new_ok, s_new, MASK_NEG)
    m = jnp.maximum(jnp.max(s, axis=-1, keepdims=True), s_new)
    e = jnp.where(ok, jnp.exp(s - m), 0.0)
    e_new = jnp.where(new_ok, jnp.exp(s_new - m), 0.0)
    total = jnp.sum(e, axis=-1, keepdims=True) + e_new
    return e, e_new, 1.0 / jnp.maximum(total, 1e-30)


def _nsa_step_kernel(pt_ref, *refs, n_pages, page, past):
    del pt_ref
    pages = refs[:n_pages]
    (q_ref, gp_ref, cke_ref, cko_ref, kvs_new_ref, swa_ref, kvw_new_ref,
     o_ref, swa_out_ref, kb_ref, vb_ref) = refs[n_pages:]
    nh, kw = NSA_HEADS, NSA_KVW
    t = past
    own = (lax.broadcasted_iota(jnp.int32, (nh, kw), 1) // NSA_DH
           == lax.broadcasted_iota(jnp.int32, (nh, kw), 0) // NSA_GROUP)
    q = q_ref[0] * (NSA_DH ** -0.5)
    qp = jnp.where(own, jnp.concatenate([q] * NSA_KVH, axis=1), 0.0)
    qpb = qp.astype(BF16)

    half = cke_ref.shape[1]
    n_c = 2 * half
    ce, co = cke_ref[0], cko_ref[0]
    ck = jnp.concatenate([ce[:, :kw], co[:, :kw]], axis=0).astype(BF16)
    cv = jnp.concatenate([ce[:, kw:], co[:, kw:]], axis=0).astype(BF16)
    nn = lax.broadcasted_iota(jnp.int32, (1, n_c), 1)
    c_blk = jnp.where(nn < half, 2 * nn, 2 * (nn - half) + 1)
    p_c = _masked_softmax(_dot_nt(qpb, ck), (c_blk * CMP_BLOCK + CMP_BLOCK - 1) <= t)
    o_c = jnp.dot(p_c.astype(BF16), cv, preferred_element_type=F32)
    same_group = (lax.broadcasted_iota(jnp.int32, (nh, nh), 0) // NSA_GROUP
                  == lax.broadcasted_iota(jnp.int32, (nh, nh), 1) // NSA_GROUP).astype(F32)
    imp = jnp.dot(same_group, p_c, preferred_element_type=F32, precision=HIGHEST)
    imp = imp[:, :half] + imp[:, half:]
    imp = jnp.concatenate([imp, jnp.zeros((nh, LANES - half), F32)], axis=1)
    blk = lax.broadcasted_iota(jnp.int32, (nh, LANES), 1)
    cur = t // SLC_BLOCK
    forced = (blk == 0) | (blk == cur) | (blk == cur - 1)
    sel = _top_blocks(jnp.where(forced, SEL_BIG, jnp.where(blk <= cur, imp, -SEL_BIG)), SLC_TOPK)

    for p in range(n_pages):
        kb_ref[p * page:(p + 1) * page, :] = pages[p][0, :, :kw].astype(BF16)
        vb_ref[p * page:(p + 1) * page, :] = pages[p][0, :, kw:].astype(BF16)
    n_keys = n_pages * page
    expand = (lax.broadcasted_iota(jnp.int32, (LANES, n_keys), 0)
              == lax.broadcasted_iota(jnp.int32, (LANES, n_keys), 1) // SLC_BLOCK).astype(BF16)
    chosen = jnp.dot(sel.astype(BF16), expand, preferred_element_type=F32) > 0.5
    ok = chosen & (lax.broadcasted_iota(jnp.int32, (1, n_keys), 1) <= t)
    kv_new = kvs_new_ref[0]
    s_new = jnp.sum(qp * kv_new[:, :kw], axis=-1, keepdims=True)
    e, e_new, inv = _softmax_with_new_key(_dot_nt(qpb, kb_ref[...]), ok, s_new, sel[:, cur:cur + 1] > 0.5)
    o_s = (jnp.dot(e.astype(BF16), vb_ref[...], preferred_element_type=F32) + e_new * kv_new[:, kw:]) * inv

    n_w = swa_ref.shape[1]
    swa = swa_ref[0]
    dist = t - (t - n_w + lax.broadcasted_iota(jnp.int32, (1, n_w), 1))
    w_ok = (dist >= 0) & (dist <= WINDOW) & (t - dist >= 0)
    kvw_new = kvw_new_ref[0]
    s_wn = jnp.sum(qp * kvw_new[:, :kw], axis=-1, keepdims=True)
    new_ok = lax.broadcasted_iota(jnp.int32, (nh, 1), 0) >= 0
    e, e_new, inv = _softmax_with_new_key(_dot_nt(qpb, swa[:, :kw].astype(BF16)), w_ok, s_wn, new_ok)
    o_w = (jnp.dot(e.astype(BF16), swa[:, kw:].astype(BF16), preferred_element_type=F32) + e_new * kvw_new[:, kw:]) * inv

    g = _sigmoid(gp_ref[0])
    o = jnp.where(own, g[:, 0:1] * o_c + g[:, 1:2] * o_s + g[:, 2:3] * o_w, 0.0)
    o_h = o[:, :NSA_DH]
    for h in range(1, NSA_KVH):
        o_h = o_h + o[:, h * NSA_DH:(h + 1) * NSA_DH]
    o_ref[0] = o_h
    swa_out_ref[0, 0:n_w - 1, :] = swa_ref[0, 1:n_w, :]
    swa_out_ref[0, n_w - 1:n_w, :] = kvw_new


def nsa_step_attention(q3, gates3, cke, cko, cache_slc, page_table, kvs_new, swa, kvw_new, *, past):
    n_seq, n_pages = page_table.shape
    page = cache_slc.shape[1]
    n_w = swa.shape[1]
    seq3 = lambda a: pl.BlockSpec((1,) + a.shape[1:], lambda b, pt: (b, 0, 0))
    return pl.pallas_call(
        functools.partial(_nsa_step_kernel, n_pages=n_pages, page=page, past=past),
        grid_spec=pltpu.PrefetchScalarGridSpec(
            num_scalar_prefetch=1,
            grid=(n_seq,),
            in_specs=_page_specs(n_pages, page) + [seq3(a) for a in (q3, gates3, cke, cko, kvs_new, swa, kvw_new)],
            out_specs=[seq3(q3), seq3(swa)],
            scratch_shapes=[pltpu.VMEM((n_pages * page, NSA_KVW), BF16)] * 2),
        out_shape=[jax.ShapeDtypeStruct(q3.shape, F32), jax.ShapeDtypeStruct(swa.shape, F32)],
        compiler_params=_params("arbitrary"),
        name="nsa_step_attention",
    )(page_table, *([cache_slc] * n_pages), q3, gates3, cke, cko, kvs_new, swa, kvw_new)


PROMPT_TM = 512
HG_TL = 512
RG_TL = 256
FFN_TF = 1408
CMP_TILE_BLOCKS = 64


def _ffn_layer(x, w_gu, w_down, g, b, *, tm):
    ones = jnp.ones((1, x.shape[0], 1), F32)
    return gated_ffn_residual_ln(x, ones, w_gu[None].astype(BF16), w_down[None].astype(BF16), g, b, tm=tm, tf=FFN_TF)


def _moe_layer(x, w_router, w_gu, w_down, g, b, *, tm):
    gates = moe_gates(x, w_router, tm=tm)
    return gated_ffn_residual_ln(x, gates.T[:, :, None], w_gu.astype(BF16), w_down.astype(BF16), g, b, tm=tm, tf=FFN_TF)


def kernel(x_prompt, x_sample, state_hgrn, state_rglru_conv, state_rglru_h, cache_nsa_cmp, cache_nsa_slc, cache_nsa_swa, page_table, ln_g, ln_b, hgrn_w_in, hgrn_lb, hgrn_norm_g, hgrn_w_out, rg_w_in, rg_conv_w, rg_conv_b, rg_w_gate, rg_b_gate, rg_lambda, rg_w_out, nsa_w_in, nsa_cmp_pos, nsa_cmp_w1, nsa_cmp_w2, nsa_w_out, ffn_w_gu, ffn_w_down, moe_router, moe_w_gu, moe_w_down):
    bp, l_seq, d = x_prompt.shape
    n_seq = x_sample.shape[0]
    past = page_table.shape[1] * cache_nsa_cmp.shape[2]
    xp = x_prompt.reshape(bp * l_seq, d)
    xs = x_sample.reshape(n_seq, d)
    tm_p, tm_s = PROMPT_TM, n_seq
    kv_shape = (2, NSA_KVH, NSA_DH)
    hg_p, hg_s, rc_p, rc_s, rh_p, rh_s = [], [], [], [], [], []
    cm_p, cm_s, sl_p, sl_s, sw_p, sw_s = [], [], [], [], [], []
    for layer in range(DEPTH):
        kind, slot = layer % 3, layer // 3
        g0, b0 = ln_g[layer, 0], ln_b[layer, 0]
        if kind == 0:
            w_in, w_out = hgrn_w_in[slot].astype(BF16), hgrn_w_out[slot].astype(BF16)
            op, st_p = hgrn_prompt(matmul(xp, w_in, tm=tm_p, tn=512), hgrn_lb, hgrn_norm_g[slot],
                                   layer=layer, batch=bp, tl=HG_TL)
            os_, st_s = hgrn_step(matmul(xs, w_in, tm=tm_s, tn=512), state_hgrn[slot], hgrn_lb, hgrn_norm_g[slot],
                                  layer=layer, sb=8)
            hg_p.append(st_p)
            hg_s.append(st_s)
        elif kind == 1:
            w_in, w_out = rg_w_in[slot].astype(BF16), rg_w_out[slot].astype(BF16)
            rg_w = (rg_conv_w[slot], rg_conv_b[slot], rg_w_gate[slot].astype(BF16), rg_b_gate[slot], rg_lambda[slot])
            op, cb_p, h_p = rglru_prompt(matmul(xp, w_in, tm=tm_p, tn=512), *rg_w, batch=bp, tl=RG_TL)
            os_, cb_s, h_s = rglru_step(matmul(xs, w_in, tm=tm_s, tn=512), state_rglru_conv[slot], state_rglru_h[slot], *rg_w)
            rc_p.append(cb_p)
            rc_s.append(cb_s)
            rh_p.append(h_p[:, 0])
            rh_s.append(h_s)
        else:
            w_qkv, w_kvb, w_g, pos4, w1, w2 = nsa_prepare_weights(
                nsa_w_in[slot], nsa_cmp_pos[slot], nsa_cmp_w1[slot], nsa_cmp_w2[slot])
            w_out = nsa_w_out[slot].astype(BF16)
            hq = NSA_HEADS * NSA_DH
            qkv = matmul(xp, w_qkv, tm=tm_p, tn=512)
            kvb = matmul(xp, w_kvb, tm=tm_p, tn=512, out_dtype=BF16)
            gates_pre = matmul(xp, w_g, tm=tm_p, tn=NSA_KVH * LANES)
            cke, cko = nsa_compress(qkv, hq // NSA_ROW, pos4, w1, w2, batch=bp, n_blk=CMP_TILE_BLOCKS)
            op = nsa_prompt_attention(qkv, gates_pre, jnp.concatenate([cke, cko], axis=1), kvb, batch=bp)
            kv = qkv[:, hq:].reshape(bp, l_seq, 3, *kv_shape)
            cm_p.append(kv[:, :, 0])
            sl_p.append(kv[:, :, 1])
            sw_p.append(kv[:, l_seq - min(WINDOW, l_seq):, 2])
            qkv = matmul(xs, w_qkv, tm=tm_s, tn=512)
            gates_pre = matmul(xs, w_g, tm=tm_s, tn=NSA_KVH * LANES)
            rows = lambda c: c.reshape(c.shape[0], c.shape[1], NSA_ROW)
            cke, cko = nsa_compress_paged(rows(cache_nsa_cmp[slot]), page_table, pos4, w1, w2)
            gates3 = gates_pre.reshape(n_seq, NSA_KVH, LANES)[:, :, :3 * NSA_GROUP].reshape(n_seq, NSA_HEADS, 3)
            kv = qkv[:, hq:].reshape(n_seq, 3, 1, NSA_ROW)
            o3, swa_new = nsa_step_attention(qkv[:, :hq].reshape(n_seq, NSA_HEADS, NSA_DH), gates3, cke, cko,
                                             rows(cache_nsa_slc[slot]), page_table, kv[:, 1], rows(cache_nsa_swa[slot]),
                                             kv[:, 2], past=past)
            os_ = o3.reshape(n_seq, hq)
            cm_s.append(kv[:, 0].reshape(n_seq, 1, *kv_shape))
            sl_s.append(kv[:, 1].reshape(n_seq, 1, *kv_shape))
            sw_s.append(swa_new.reshape(n_seq, -1, *kv_shape))
        xp = matmul_residual_ln(op, w_out, xp, g0, b0, tm=tm_p)
        xs = matmul_residual_ln(os_, w_out, xs, g0, b0, tm=tm_s)
        g1, b1 = ln_g[layer, 1], ln_b[layer, 1]
        if layer % 2 == 0:
            fw = (ffn_w_gu[layer // 2], ffn_w_down[layer // 2])
            xp = _ffn_layer(xp, *fw, g1, b1, tm=tm_p)
            xs = _ffn_layer(xs, *fw, g1, b1, tm=tm_s)
        else:
            fw = (moe_router[layer // 2], moe_w_gu[layer // 2], moe_w_down[layer // 2])
            xp = _moe_layer(xp, *fw, g1, b1, tm=tm_p)
            xs = _moe_layer(xs, *fw, g1, b1, tm=tm_s)
    return (xp.reshape(bp, l_seq, d), xs.reshape(n_seq, 1, d),
            jnp.stack(hg_p), jnp.stack(hg_s), jnp.stack(rc_p), jnp.stack(rc_s), jnp.stack(rh_p), jnp.stack(rh_s),
            jnp.stack(cm_p), jnp.stack(cm_s), jnp.stack(sl_p), jnp.stack(sl_s), jnp.stack(sw_p), jnp.stack(sw_s))
```

```python
import functools

import jax
import jax.numpy as jnp
from jax import lax
from jax.experimental import pallas as pl
from jax.experimental.pallas import tpu as pltpu

F32 = jnp.float32
BF16 = jnp.bfloat16
HIGHEST = lax.Precision.HIGHEST

D_MODEL = 1024
DEPTH = 4
ALPHA = (2 * DEPTH) ** 0.25
LN_EPS = 1e-5
RMS_EPS = 1e-6
HG_HEADS = 8
HG_DK = 128
HG_CHUNK = 64
HG_SUB = 16
HG_HEADS_PER_STEP = 4
LB_FLOOR = 1e-30
RG_WIDTH = 1280
RG_BLOCKS = 10
RG_BS = 128
RG_CONV = 4
RG_C = 8.0
D_FF = 2816
N_EXPERTS = 8

VMEM_LIMIT_BYTES = 56 * 1024 * 1024


def _params(*sem):
    return pltpu.CompilerParams(dimension_semantics=sem, vmem_limit_bytes=VMEM_LIMIT_BYTES)


def _layer_norm(v, g, b):
    mu = jnp.mean(v, axis=-1, keepdims=True)
    vc = v - mu
    var = jnp.mean(vc * vc, axis=-1, keepdims=True)
    return vc * lax.rsqrt(var + LN_EPS) * g + b


def _sigmoid(x):
    return 1.0 / (1.0 + jnp.exp(-x))


def _silu(x):
    return x * _sigmoid(x)


def _mm_kernel(x_ref, w_ref, o_ref, xb_ref):
    @pl.when(pl.program_id(1) == 0)
    def _():
        xb_ref[...] = x_ref[...].astype(BF16)

    o_ref[...] = jnp.dot(xb_ref[...], w_ref[...], preferred_element_type=F32).astype(o_ref.dtype)


def matmul(x, w, *, tm, tn, out_dtype=F32):
    m, k = x.shape
    n = w.shape[1]
    return pl.pallas_call(
        _mm_kernel,
        grid=(m // tm, n // tn),
        in_specs=[pl.BlockSpec((tm, k), lambda i, j: (i, 0)),
                  pl.BlockSpec((k, tn), lambda i, j: (0, j))],
        out_specs=pl.BlockSpec((tm, tn), lambda i, j: (i, j)),
        out_shape=jax.ShapeDtypeStruct((m, n), out_dtype),
        scratch_shapes=[pltpu.VMEM((tm, k), BF16)],
        compiler_params=_params("parallel", "arbitrary"),
        name="matmul",
    )(x, w)


def _mm_res_ln_kernel(a_ref, w_ref, r_ref, g_ref, b_ref, o_ref):
    m = jnp.dot(a_ref[...].astype(BF16), w_ref[...], preferred_element_type=F32)
    o_ref[...] = _layer_norm(ALPHA * r_ref[...] + m, g_ref[...], b_ref[...])


def matmul_residual_ln(a, w, res, g, b, *, tm):
    m, k = a.shape
    d = w.shape[1]
    return pl.pallas_call(
        _mm_res_ln_kernel,
        grid=(m // tm,),
        in_specs=[pl.BlockSpec((tm, k), lambda i: (i, 0)),
                  pl.BlockSpec((k, d), lambda i: (0, 0)),
                  pl.BlockSpec((tm, d), lambda i: (i, 0)),
                  pl.BlockSpec((1, d), lambda i: (0, 0)),
                  pl.BlockSpec((1, d), lambda i: (0, 0))],
        out_specs=pl.BlockSpec((tm, d), lambda i: (i, 0)),
        out_shape=jax.ShapeDtypeStruct((m, d), F32),
        compiler_params=_params("parallel"),
        name="matmul_residual_ln",
    )(a, w, res, g.reshape(1, d), b.reshape(1, d))


def _ffn_kernel(x_ref, wg_ref, wu_ref, wd_ref, g_ref, b_ref, o_ref, xb_ref, acc_ref, *, n_f):
    f = pl.program_id(1)

    @pl.when(f == 0)
    def _():
        xb_ref[...] = x_ref[...].astype(BF16)

    xb = xb_ref[...]
    h = jnp.dot(xb, wg_ref[...], preferred_element_type=F32)
    u = jnp.dot(xb, wu_ref[...], preferred_element_type=F32)
    part = jnp.dot((_silu(h) * u).astype(BF16), wd_ref[...], preferred_element_type=F32)

    @pl.when(f == 0)
    def _():
        acc_ref[...] = part

    @pl.when(f > 0)
    def _():
        acc_ref[...] += part

    @pl.when(f == n_f - 1)
    def _():
        o_ref[...] = _layer_norm(ALPHA * x_ref[...] + acc_ref[...], g_ref[...], b_ref[...])


def ffn_residual_ln(x, w_gu, w_down, g, b, *, tm, tf):
    m, d = x.shape
    ff = w_down.shape[0]
    n_f = ff // tf
    return pl.pallas_call(
        functools.partial(_ffn_kernel, n_f=n_f),
        grid=(m // tm, n_f),
        in_specs=[pl.BlockSpec((tm, d), lambda i, f: (i, 0)),
                  pl.BlockSpec((d, tf), lambda i, f: (0, f)),
                  pl.BlockSpec((d, tf), lambda i, f: (0, n_f + f)),
                  pl.BlockSpec((tf, d), lambda i, f: (f, 0)),
                  pl.BlockSpec((1, d), lambda i, f: (0, 0)),
                  pl.BlockSpec((1, d), lambda i, f: (0, 0))],
        out_specs=pl.BlockSpec((tm, d), lambda i, f: (i, 0)),
        out_shape=jax.ShapeDtypeStruct((m, d), F32),
        scratch_shapes=[pltpu.VMEM((tm, d), BF16), pltpu.VMEM((tm, d), F32)],
        compiler_params=_params("parallel", "arbitrary"),
        name="ffn_residual_ln",
    )(x, w_gu, w_gu, w_down, g.reshape(1, d), b.reshape(1, d))


def _router_kernel(x_ref, wt_ref, gate_ref, pos_ref):
    logits = lax.dot_general(wt_ref[...], x_ref[...], (((1,), (1,)), ((), ())),
                             preferred_element_type=F32, precision=HIGHEST)
    n_e, tm = logits.shape
    expert = lax.broadcasted_iota(jnp.int32, logits.shape, 0)
    m1 = jnp.max(logits, axis=0, keepdims=True)
    i1 = jnp.min(jnp.where(logits == m1, expert, n_e), axis=0, keepdims=True)
    rest = jnp.where(expert == i1, -jnp.inf, logits)
    m2 = jnp.max(rest, axis=0, keepdims=True)
    i2 = jnp.min(jnp.where(rest == m2, expert, n_e), axis=0, keepdims=True)
    e2 = jnp.exp(m2 - m1)
    denom = 1.0 + e2
    first = expert == i1
    routed = first | (expert == i2)
    gate_ref[...] = jnp.where(first, 1.0 / denom, jnp.where(routed, e2 / denom, 0.0))
    upper = (lax.broadcasted_iota(jnp.int32, (tm, tm), 0) <= lax.broadcasted_iota(jnp.int32, (tm, tm), 1)).astype(BF16)
    upto = jnp.dot(routed.astype(BF16), upper, preferred_element_type=F32)
    pos_ref[...] = jnp.where(routed, upto - 1.0, -1.0).astype(jnp.int32)


def moe_route(x, w_router_t, *, tb):
    m, d = x.shape
    n_e = w_router_t.shape[0]
    return pl.pallas_call(
        _router_kernel,
        grid=(m // tb,),
        in_specs=[pl.BlockSpec((tb, d), lambda i: (i, 0)),
                  pl.BlockSpec((n_e, d), lambda i: (0, 0))],
        out_specs=[pl.BlockSpec((n_e, tb), lambda i: (0, i))] * 2,
        out_shape=[jax.ShapeDtypeStruct((n_e, m), F32), jax.ShapeDtypeStruct((n_e, m), jnp.int32)],
        compiler_params=_params("parallel"),
        name="moe_route",
    )(x, w_router_t)


def _moe_kernel(cnt_ref, x_ref, gate_ref, slot_ref, wg_ref, wu_ref, wd_ref, g_ref, b_ref, o_ref,
                xt_ref, xc_ref, acc_ref, yt_ref, *, n_e, n_f, ts):
    i, e, f = pl.program_id(0), pl.program_id(1), pl.program_id(2)
    tb = x_ref.shape[0]
    n_sub = (cnt_ref[i * n_e + e] + ts - 1) // ts
    row = lax.broadcasted_iota(jnp.int32, (ts, tb), 0)

    def one_hot(sub):
        return (slot_ref[pl.ds(e, 1), :] == row + sub * ts).astype(BF16)

    @pl.when((e == 0) & (f == 0))
    def _():
        xt_ref[...] = x_ref[...].T.astype(BF16)
        yt_ref[...] = jnp.zeros_like(yt_ref)

    tf, d = wg_ref.shape[1], wd_ref.shape[1]
    halves = lambda n: (slice(0, n // 2), slice(n // 2, n))

    @pl.when(f == 0)
    def _():
        def gather(sub, c):
            pick = one_hot(sub)
            for rows in halves(d):
                xc_ref[sub, rows, :] = _dot_nt(xt_ref[rows, :], pick).astype(BF16)
            return c

        lax.fori_loop(0, n_sub, gather, 0)

    def expert_ffn(sub, c):
        xc = xc_ref[sub]
        act = []
        for rows in halves(tf):
            h = jnp.dot(wg_ref[0, rows, :], xc, preferred_element_type=F32)
            u = jnp.dot(wu_ref[0, rows, :], xc, preferred_element_type=F32)
            act.append((_silu(h) * u).astype(BF16))
        act = jnp.concatenate(act, axis=0)
        part = jnp.concatenate([jnp.dot(wd_ref[0, rows, :], act, preferred_element_type=F32) for rows in halves(d)], axis=0)

        @pl.when(f == 0)
        def _():
            acc_ref[sub] = part

        @pl.when(f > 0)
        def _():
            acc_ref[sub] += part

        return c

    lax.fori_loop(0, n_sub, expert_ffn, 0)

    @pl.when(f == n_f - 1)
    def _():
        def scatter(sub, c):
            pick = one_hot(sub)
            gate = gate_ref[pl.ds(e, 1), :]
            for rows in halves(d):
                back = jnp.dot(acc_ref[sub, rows, :].astype(BF16), pick, preferred_element_type=F32)
                yt_ref[rows, :] += gate * back
            return c

        lax.fori_loop(0, n_sub, scatter, 0)

    @pl.when((e == n_e - 1) & (f == n_f - 1))
    def _():
        o_ref[...] = _layer_norm(ALPHA * x_ref[...] + yt_ref[...].T, g_ref[...], b_ref[...])


def moe_residual_ln(x, gates, slots, counts, w_gu_t, w_down_t, g, b, *, tb, ts, tf):
    m, d = x.shape
    n_e, ff2, _ = w_gu_t.shape
    n_f = ff2 // 2 // tf
    n_sub = tb // ts
    return pl.pallas_call(
        functools.partial(_moe_kernel, n_e=n_e, n_f=n_f, ts=ts),
        grid_spec=pltpu.PrefetchScalarGridSpec(
            num_scalar_prefetch=1,
            grid=(m // tb, n_e, n_f),
            in_specs=[pl.BlockSpec((tb, d), lambda i, e, f, c: (i, 0)),
                      pl.BlockSpec((n_e, tb), lambda i, e, f, c: (0, i)),
                      pl.BlockSpec((n_e, tb), lambda i, e, f, c: (0, i)),
                      pl.BlockSpec((1, tf, d), lambda i, e, f, c: (e, f, 0)),
                      pl.BlockSpec((1, tf, d), lambda i, e, f, c: (e, n_f + f, 0)),
                      pl.BlockSpec((1, d, tf), lambda i, e, f, c: (e, 0, f)),
                      pl.BlockSpec((1, d), lambda i, e, f, c: (0, 0)),
                      pl.BlockSpec((1, d), lambda i, e, f, c: (0, 0))],
            out_specs=pl.BlockSpec((tb, d), lambda i, e, f, c: (i, 0)),
            scratch_shapes=[pltpu.VMEM((d, tb), BF16), pltpu.VMEM((n_sub, d, ts), BF16),
                            pltpu.VMEM((n_sub, d, ts), F32), pltpu.VMEM((d, tb), F32)]),
        out_shape=jax.ShapeDtypeStruct((m, d), F32),
        compiler_params=_params("parallel", "arbitrary", "arbitrary"),
        name="moe_residual_ln",
    )(counts, x, gates, slots, w_gu_t, w_gu_t, w_down_t, g.reshape(1, d), b.reshape(1, d))


def _hgrn_lower_bound(lbp, layer):
    m = jnp.max(lbp, axis=0, keepdims=True)
    e = jnp.exp(lbp - m)
    p = e / jnp.sum(e, axis=0, keepdims=True)
    c = p[0:1]
    for i in range(1, layer + 1):
        c = c + p[i:i + 1]
    return c - p[0:1]


def _hgrn_gates(q, fx, lb):
    log_lb = jnp.log(jnp.maximum(lb, LB_FLOOR))
    log1m = jnp.log1p(-lb)
    log_sig = -(jnp.maximum(-fx, 0.0) + jnp.log1p(jnp.exp(-jnp.abs(fx))))
    b2 = log1m + log_sig
    log_f = jnp.maximum(log_lb, b2) + jnp.log1p(jnp.exp(-jnp.abs(log_lb - b2)))
    kk = (1.0 - lb) * _sigmoid(-fx) - (jnp.maximum(lb, LB_FLOOR) - lb)
    return _silu(q), log_f, kk


def _hgrn_out_norm(o, g, ng):
    o = o * lax.rsqrt(jnp.mean(o * o, axis=-1, keepdims=True) + RMS_EPS)
    return o * ng * _silu(g)


def _dot_nt(a, b):
    return lax.dot_general(a, b, (((1,), (1,)), ((), ())), preferred_element_type=F32)


def _dot_tn(a, b):
    return lax.dot_general(a, b, (((0,), (0,)), ((), ())), preferred_element_type=F32)


def _gla_kernel(q_ref, f_ref, i_ref, g_ref, lbp_ref, ng_ref, o_ref, s_out_ref, st_ref, *, layer, n_t, tl, hp):
    t = pl.program_id(2)
    c_len, sub = HG_CHUNK, HG_SUB
    n_sub = c_len // sub
    n_stack = sub * (n_sub * (n_sub - 1) // 2)

    @pl.when(t == 0)
    def _():
        st_ref[...] = jnp.zeros_like(st_ref)

    lb = _hgrn_lower_bound(lbp_ref[...], layer)
    ng = ng_ref[...]
    row = lax.broadcasted_iota(jnp.int32, (c_len, c_len), 0)
    col = lax.broadcasted_iota(jnp.int32, (c_len, c_len), 1)
    tri = (col <= row).astype(F32)
    srow = lax.broadcasted_iota(jnp.int32, (c_len, n_stack), 0) // sub
    scol = lax.broadcasted_iota(jnp.int32, (c_len, n_stack), 1)
    part = jnp.zeros((c_len, n_stack), jnp.int32)
    for p in range(1, n_sub):
        part = part + (scol >= sub * (p * (p - 1) // 2)).astype(jnp.int32)
    inter_mask = srow == part
    row_s = lax.broadcasted_iota(jnp.int32, (sub, 1), 0)

    def one_head(hh, r):
        lanes = slice(hh * HG_DK, (hh + 1) * HG_DK)
        qs, log_f, kk = _hgrn_gates(q_ref[r, lanes], f_ref[r, lanes], lb[:, lanes])
        v = i_ref[r, lanes]
        cum = jnp.dot(tri, log_f, preferred_element_type=F32, precision=HIGHEST)
        last = cum[c_len - 1:c_len, :]
        st = st_ref[hh]
        o = _dot_nt((qs * jnp.exp(cum)).astype(BF16), st.astype(BF16))
        bounds = [cum[sub * a - 1:sub * a, :] for a in range(1, n_sub)]
        cq = jnp.concatenate([cum[0:sub]] + [jnp.broadcast_to(b, (sub, HG_DK)) for b in bounds], axis=0)
        q_rel = (qs * jnp.exp(cum - cq)).astype(BF16)
        k_st = jnp.concatenate([kk[0:sub * a] * jnp.exp(bounds[a - 1] - cum[0:sub * a]) for a in range(1, n_sub)], axis=0)
        v_st = jnp.concatenate([v[0:sub * a] for a in range(1, n_sub)], axis=0)
        att = jnp.where(inter_mask, _dot_nt(q_rel, k_st.astype(BF16)), 0.0)
        o = o + jnp.dot(att.astype(BF16), v_st.astype(BF16), preferred_element_type=F32)
        parts = []
        for a in range(n_sub):
            sl = slice(sub * a, sub * a + sub)
            qa, ka, va, ca = qs[sl], kk[sl], v[sl], cum[sl]
            od = o[sl]
            for s in range(sub):
                e = jnp.exp(jnp.minimum(ca - ca[s:s + 1], 0.0))
                w = jnp.sum(qa * (ka[s:s + 1] * e), axis=-1, keepdims=True)
                od = od + jnp.where(row_s >= s, w, 0.0) * va[s:s + 1]
            parts.append(od)
        o = jnp.concatenate(parts, axis=0)
        o_ref[r, lanes] = _hgrn_out_norm(o, g_ref[r, lanes], ng[:, lanes]).astype(o_ref.dtype)
        k_end = (kk * jnp.exp(last - cum)).astype(BF16)
        st_ref[hh] = st * jnp.exp(last) + _dot_tn(v.astype(BF16), k_end)

    def chunk(c, carry):
        r = pl.ds(pl.multiple_of(c * c_len, c_len), c_len)
        for hh in range(hp):
            one_head(hh, r)
        return carry

    lax.fori_loop(0, tl // c_len, chunk, 0)

    @pl.when(t == n_t - 1)
    def _():
        for hh in range(hp):
            s_out_ref[0, hh] = st_ref[hh].T


def hgrn_prompt(z, lb_param, norm_g, *, layer, batch, tl):
    m = z.shape[0]
    n_t = m // batch // tl
    h, dk = HG_HEADS, HG_DK
    hp = HG_HEADS_PER_STEP
    n_hg = h // hp
    zspec = lambda part: pl.BlockSpec((tl, hp * dk), lambda b, hh, t: (b * n_t + t, part * n_hg + hh))
    return pl.pallas_call(
        functools.partial(_gla_kernel, layer=layer, n_t=n_t, tl=tl, hp=hp),
        grid=(batch, n_hg, n_t),
        in_specs=[zspec(0), zspec(1), zspec(2), zspec(3),
                  pl.BlockSpec((DEPTH, hp * dk), lambda b, hh, t: (0, hh)),
                  pl.BlockSpec((1, hp * dk), lambda b, hh, t: (0, hh))],
        out_specs=[pl.BlockSpec((tl, hp * dk), lambda b, hh, t: (b * n_t + t, hh)),
                   pl.BlockSpec((1, hp, dk, dk), lambda b, hh, t: (b, hh, 0, 0))],
        out_shape=[jax.ShapeDtypeStruct((m, h * dk), BF16),
                   jax.ShapeDtypeStruct((batch, h, dk, dk), F32)],
        scratch_shapes=[pltpu.VMEM((hp, dk, dk), F32)],
        compiler_params=_params("parallel", "parallel", "arbitrary"),
        name="hgrn_prompt",
    )(z, z, z, z, lb_param, norm_g.reshape(1, h * dk))


def _hgrn_step_kernel(q_ref, f_ref, i_ref, g_ref, lbp_ref, ng_ref, s0_ref, o_ref, s_ref,
                      qt_ref, ft_ref, kt_ref, oacc_ref, *, layer, n_j, sb):
    j = pl.program_id(1)
    n_seq = q_ref.shape[0]

    @pl.when(j == 0)
    def _():
        lb = _hgrn_lower_bound(lbp_ref[...], layer)
        qs, log_f, kk = _hgrn_gates(q_ref[...], f_ref[...], lb)
        qt_ref[...] = qs.T
        ft_ref[...] = jnp.exp(log_f).T
        kt_ref[...] = kk.T

    seq = lax.broadcasted_iota(jnp.int32, (n_seq, HG_DK), 0)
    for i in range(sb):
        b = j * sb + i
        pick = (seq == b).astype(F32)
        col = lambda ref: jnp.dot(ref[...], pick, preferred_element_type=F32, precision=HIGHEST)
        s_new = s0_ref[i, 0] * col(ft_ref) + col(kt_ref) * i_ref[pl.ds(b, 1), :]
        s_ref[i, 0] = s_new
        oacc_ref[pl.ds(b, 1), :] = jnp.sum(col(qt_ref) * s_new, axis=0, keepdims=True)

    @pl.when(j == n_j - 1)
    def _():
        o_ref[...] = _hgrn_out_norm(oacc_ref[...], g_ref[...], ng_ref[...]).astype(o_ref.dtype)


def hgrn_step(z, s0, lb_param, norm_g, *, layer, sb):
    n_seq = z.shape[0]
    h, dk = HG_HEADS, HG_DK
    n_j = n_seq // sb
    zspec = lambda off: pl.BlockSpec((n_seq, dk), lambda hh, j: (0, off + hh))
    return pl.pallas_call(
        functools.partial(_hgrn_step_kernel, layer=layer, n_j=n_j, sb=sb),
        grid=(h, n_j),
        in_specs=[zspec(0), zspec(h), zspec(2 * h), zspec(3 * h),
                  pl.BlockSpec((DEPTH, dk), lambda hh, j: (0, hh)),
                  pl.BlockSpec((1, dk), lambda hh, j: (0, hh)),
                  pl.BlockSpec((sb, 1, dk, dk), lambda hh, j: (j, hh, 0, 0))],
        out_specs=[pl.BlockSpec((n_seq, dk), lambda hh, j: (0, hh)),
                   pl.BlockSpec((sb, 1, dk, dk), lambda hh, j: (j, hh, 0, 0))],
        out_shape=[jax.ShapeDtypeStruct((n_seq, h * dk), BF16),
                   jax.ShapeDtypeStruct(s0.shape, F32)],
        scratch_shapes=[pltpu.VMEM((dk, n_seq), F32)] * 3 + [pltpu.VMEM((n_seq, dk), F32)],
        compiler_params=_params("parallel", "arbitrary"),
        name="hgrn_step",
    )(z, z, z, z, lb_param, norm_g.reshape(1, h * dk), s0)


def _gelu_tanh(x):
    return x * (0.5 * (1.0 + jnp.tanh(0.7978845608028654 * (x + 0.044715 * (x * x * x)))))


def _rg_decay_and_input(xc, wg_ref, bg_ref, lam_ref):
    xb = xc.astype(BF16)
    gates = []
    for gi in range(2):
        blocks = [jnp.dot(xb[:, n * RG_BS:(n + 1) * RG_BS], wg_ref[gi, n], preferred_element_type=F32)
                  for n in range(RG_BLOCKS)]
        gates.append(jnp.concatenate(blocks, axis=1) + bg_ref[gi:gi + 1, :])
    r = _sigmoid(gates[0])
    i = _sigmoid(gates[1])
    neg_lam = -lam_ref[...]
    softplus = jnp.maximum(neg_lam, 0.0) + jnp.log1p(jnp.exp(-jnp.abs(neg_lam)))
    log_a = -RG_C * r * softplus
    one_minus_a2 = -jnp.tanh(log_a) * (jnp.exp(2.0 * log_a) + 1.0)
    return jnp.exp(log_a), jnp.sqrt(one_minus_a2) * i * xc


def _rg_prompt_kernel(y_ref, x_ref, cw_ref, cb_ref, wg_ref, bg_ref, lam_ref,
                      o_ref, conv_out_ref, h_out_ref, xpad_ref, h_ref, *, n_t, tl):
    t = pl.program_id(1)
    halo = 8

    @pl.when(t == 0)
    def _():
        xpad_ref[0:halo, :] = jnp.zeros((halo, RG_WIDTH), F32)
        h_ref[...] = jnp.zeros_like(h_ref)

    x = x_ref[...]
    xpad_ref[halo:halo + tl, :] = x
    xc = cb_ref[...]
    for j in range(RG_CONV):
        off = halo - (RG_CONV - 1) + j
        xc = xc + xpad_ref[off:off + tl, :] * cw_ref[j:j + 1, :]
    xpad_ref[0:halo, :] = x[tl - halo:tl, :]
    a, u = _rg_decay_and_input(xc, wg_ref, bg_ref, lam_ref)
    row = lax.broadcasted_iota(jnp.int32, (tl, 1), 0)
    s = 1
    while s < tl:
        keep = row >= s
        a_prev = jnp.where(keep, pltpu.roll(a, s, 0), 1.0)
        u_prev = jnp.where(keep, pltpu.roll(u, s, 0), 0.0)
        u = a * u_prev + u
        a = a * a_prev
        s *= 2
    h = a * h_ref[...] + u
    h_ref[...] = h[tl - 1:tl, :]
    o_ref[...] = (_gelu_tanh(y_ref[...]) * h).astype(o_ref.dtype)

    @pl.when(t == n_t - 1)
    def _():
        conv_out_ref[0] = x[tl - (RG_CONV - 1):tl, :]
        h_out_ref[0] = h[tl - 1:tl, :]


def rglru_prompt(z, conv_w, conv_b, w_gate, b_gate, lam, *, batch, tl):
    m = z.shape[0]
    n_t = m // batch // tl
    w = RG_WIDTH
    const = lambda shape: pl.BlockSpec(shape, lambda b, t: (0,) * len(shape))
    return pl.pallas_call(
        functools.partial(_rg_prompt_kernel, n_t=n_t, tl=tl),
        grid=(batch, n_t),
        in_specs=[pl.BlockSpec((tl, w), lambda b, t: (b * n_t + t, 0)),
                  pl.BlockSpec((tl, w), lambda b, t: (b * n_t + t, 1)),
                  const((RG_CONV, w)), const((1, w)), const((2, RG_BLOCKS, RG_BS, RG_BS)), const((2, w)), const((1, w))],
        out_specs=[pl.BlockSpec((tl, w), lambda b, t: (b * n_t + t, 0)),
                   pl.BlockSpec((1, RG_CONV - 1, w), lambda b, t: (b, 0, 0)),
                   pl.BlockSpec((1, 1, w), lambda b, t: (b, 0, 0))],
        out_shape=[jax.ShapeDtypeStruct((m, w), BF16),
                   jax.ShapeDtypeStruct((batch, RG_CONV - 1, w), F32),
                   jax.ShapeDtypeStruct((batch, 1, w), F32)],
        scratch_shapes=[pltpu.VMEM((tl + 8, w), F32), pltpu.VMEM((1, w), F32)],
        compiler_params=_params("parallel", "arbitrary"),
        name="rglru_prompt",
    )(z, z, conv_w, conv_b.reshape(1, w), w_gate, b_gate, lam.reshape(1, w))


def _rg_step_kernel(y_ref, x_ref, buf_ref, h0_ref, cw_ref, cb_ref, wg_ref, bg_ref, lam_ref,
                    o_ref, conv_out_ref, h_out_ref):
    x = x_ref[...]
    xc = cb_ref[...]
    for j in range(RG_CONV - 1):
        xc = xc + buf_ref[:, j, :] * cw_ref[j:j + 1, :]
    xc = xc + x * cw_ref[RG_CONV - 1:RG_CONV, :]
    a, u = _rg_decay_and_input(xc, wg_ref, bg_ref, lam_ref)
    h = a * h0_ref[...] + u
    o_ref[...] = (_gelu_tanh(y_ref[...]) * h).astype(o_ref.dtype)
    h_out_ref[...] = h
    for j in range(RG_CONV - 2):
        conv_out_ref[:, j, :] = buf_ref[:, j + 1, :]
    conv_out_ref[:, RG_CONV - 2, :] = x


def rglru_step(z, conv_buf, h0, conv_w, conv_b, w_gate, b_gate, lam):
    n_seq = z.shape[0]
    w = RG_WIDTH
    const = lambda shape: pl.BlockSpec(shape, lambda i: (0,) * len(shape))
    return pl.pallas_call(
        _rg_step_kernel,
        grid=(1,),
        in_specs=[pl.BlockSpec((n_seq, w), lambda i: (0, 0)),
                  pl.BlockSpec((n_seq, w), lambda i: (0, 1)),
                  const((n_seq, RG_CONV - 1, w)), const((n_seq, w)),
                  const((RG_CONV, w)), const((1, w)), const((2, RG_BLOCKS, RG_BS, RG_BS)), const((2, w)), const((1, w))],
        out_specs=[const((n_seq, w)), const((n_seq, RG_CONV - 1, w)), const((n_seq, w))],
        out_shape=[jax.ShapeDtypeStruct((n_seq, w), BF16),
                   jax.ShapeDtypeStruct((n_seq, RG_CONV - 1, w), F32),
                   jax.ShapeDtypeStruct((n_seq, w), F32)],
        compiler_params=_params("arbitrary"),
        name="rglru_step",
    )(z, z, conv_buf, h0, conv_w, conv_b.reshape(1, w), w_gate, b_gate, lam.reshape(1, w))


NSA_HEADS = 16
NSA_KVH = 4
NSA_GROUP = 4
NSA_DH = 64
NSA_KVW = NSA_KVH * NSA_DH
NSA_ROW = 2 * NSA_KVW
CMP_BLOCK = 32
CMP_HID = 128
SLC_BLOCK = 64
SLC_TOPK = 16
WINDOW = 512
NSA_QBLOCK = 128
MASK_NEG = -1e30
SEL_BIG = 1e9
LANES = 128
SEL_KEYS = 512


def _masked_softmax(s, mask):
    s = jnp.where(mask, s, MASK_NEG)
    m = jnp.max(s, axis=-1, keepdims=True)
    e = jnp.where(mask, jnp.exp(s - m), 0.0)
    return e / jnp.maximum(jnp.sum(e, axis=-1, keepdims=True), 1e-30)


def _top_blocks(score, k, axis):
    pos = lax.broadcasted_iota(jnp.int32, score.shape, axis)
    n = score.shape[axis]

    def pick(_, carry):
        sc, sel = carry
        mx = jnp.max(sc, axis=axis, keepdims=True)
        idx = jnp.min(jnp.where(sc == mx, pos, n), axis=axis, keepdims=True)
        hit = pos == idx
        return jnp.where(hit, -jnp.inf, sc), jnp.where(hit, 1.0, sel)

    return lax.fori_loop(0, k, pick, (score, jnp.zeros(score.shape, F32)))[1]


def _compress_rows(x_refs, pos_ref, w1_ref, w2_ref, n_blk):
    half = n_blk // 2
    tiles = NSA_KVW // LANES
    low = lax.broadcasted_iota(jnp.int32, (n_blk, LANES), 1) < NSA_DH
    outs = []
    for c in range(2):
        cols = [[] for _ in range(NSA_KVH)]
        for l in range(0, CMP_BLOCK, 2):
            for tl in range(tiles):
                ref = x_refs[c * tiles + tl]

                def rows(r, ref=ref, tl=tl):
                    xe = ref[pl.ds(r, half, stride=2 * CMP_BLOCK), :]
                    xo = ref[pl.ds(r + CMP_BLOCK, half, stride=2 * CMP_BLOCK), :]
                    return jnp.concatenate([xe, xo], axis=0) + pos_ref[c, r:r + 1, tl * LANES:(tl + 1) * LANES]

                x0, x1 = rows(l), rows(l + 1)
                cols[2 * tl].append(jnp.where(low, x0, pltpu.roll(x1, NSA_DH, 1)))
                cols[2 * tl + 1].append(jnp.where(low, pltpu.roll(x0, NSA_DH, 1), x1))
        xs = jnp.concatenate([jnp.concatenate(cols[h], axis=1) for h in range(NSA_KVH)], axis=0)
        acc = jnp.dot(xs.astype(BF16), w1_ref[c], preferred_element_type=F32)
        out = jnp.dot(_silu(acc).astype(BF16), w2_ref[c], preferred_element_type=F32)
        outs.append(jnp.concatenate([out[h * n_blk:(h + 1) * n_blk] for h in range(NSA_KVH)], axis=1))
    return jnp.concatenate(outs, axis=1)


def _compress_kernel(x0_ref, x1_ref, x2_ref, x3_ref, pos_ref, w1_ref, w2_ref, even_ref, odd_ref, *, n_blk):
    full = _compress_rows((x0_ref, x1_ref, x2_ref, x3_ref), pos_ref, w1_ref, w2_ref, n_blk)
    even_ref[0] = full[:n_blk // 2]
    odd_ref[0] = full[n_blk // 2:]


def nsa_compress(kv, col_block, pos4, w1, w2, *, batch, n_blk):
    m = kv.shape[0]
    n_t = m // batch // (n_blk * CMP_BLOCK)
    half = n_blk // 2
    const = lambda shape: pl.BlockSpec(shape, lambda b, t: (0,) * len(shape))
    out = jax.ShapeDtypeStruct((batch, n_t * half, NSA_ROW), F32)
    tiles = NSA_ROW // LANES
    lane_tile = lambda i: pl.BlockSpec((n_blk * CMP_BLOCK, LANES), lambda b, t: (b * n_t + t, col_block * tiles + i))
    return pl.pallas_call(
        functools.partial(_compress_kernel, n_blk=n_blk),
        grid=(batch, n_t),
        in_specs=[lane_tile(i) for i in range(tiles)]
        + [const((2, CMP_BLOCK, NSA_KVW)), const((2, CMP_BLOCK * NSA_DH, CMP_HID)), const((2, CMP_HID, NSA_DH))],
        out_specs=[pl.BlockSpec((1, half, NSA_ROW), lambda b, t: (b, t, 0))] * 2,
        out_shape=[out, out],
        compiler_params=_params("parallel", "parallel"),
        name="nsa_compress",
    )(kv, kv, kv, kv, pos4, w1, w2)


def nsa_prepare_weights(w_in, pos, w1, w2):
    hq = NSA_HEADS * NSA_DH
    kv_end = hq + 3 * NSA_ROW
    w_qkv = w_in[:, :kv_end].astype(BF16)
    w_kvb = w_in[:, hq + NSA_ROW:kv_end].astype(BF16)
    w_g = w_in[:, kv_end:].reshape(-1, NSA_KVH, 3 * NSA_GROUP)
    w_g = jnp.pad(w_g, ((0, 0), (0, 0), (0, LANES - 3 * NSA_GROUP))).reshape(-1, NSA_KVH * LANES).astype(BF16)
    pos4 = jnp.tile(pos, (1, 1, NSA_KVH))
    w1 = w1.reshape(2, CMP_BLOCK * NSA_DH, CMP_HID)
    return w_qkv, w_kvb, w_g, pos4, w1.astype(BF16), w2.astype(BF16)


def _stack_group_queries(q, par):
    half = lax.broadcasted_iota(jnp.int32, (q.shape[0], LANES), 1) // NSA_DH
    parts = []
    for g in range(NSA_GROUP):
        qg = q[:, g * NSA_DH:(g + 1) * NSA_DH]
        parts.append(jnp.where(half == par, jnp.concatenate([qg, qg], axis=1), 0.0))
    return jnp.concatenate(parts, axis=0).astype(BF16)


def _nsa_prompt_kernel(q_ref, gp_ref, ck_ref, cv_ref, ks_ref, vs_ref, kw_ref, vw_ref, o_ref):
    kvh = pl.program_id(1)
    j = pl.program_id(2)
    par = kvh % 2
    qb, grp = NSA_QBLOCK, NSA_GROUP
    rows = grp * qb
    qp = _stack_group_queries(q_ref[...] * (NSA_DH ** -0.5), par)
    t_pos = j * qb + lax.broadcasted_iota(jnp.int32, (qb, 1), 0)

    def grouped(fn, s, mask):
        n = s.shape[-1]
        return fn(s.reshape(grp, qb, n), mask[None]).reshape(rows, n)

    n_c = ck_ref.shape[1]
    n_s = n_c // 2
    ck = ck_ref[0].astype(BF16)
    cv = cv_ref[0].astype(BF16)
    nn = lax.broadcasted_iota(jnp.int32, (1, n_c), 1)
    c_blk = jnp.where(nn < n_s, 2 * nn, 2 * (nn - n_s) + 1)
    c_vis = (c_blk * CMP_BLOCK + CMP_BLOCK - 1) <= t_pos
    p_c = grouped(_masked_softmax, _dot_nt(qp, ck), c_vis)
    o_c = jnp.dot(p_c.astype(BF16), cv, preferred_element_type=F32)
    imp = p_c[0:qb]
    for g in range(1, grp):
        imp = imp + p_c[g * qb:(g + 1) * qb]
    imp = imp[:, :n_s] + imp[:, n_s:]
    blk = lax.broadcasted_iota(jnp.int32, (qb, n_s), 1)
    cur = t_pos // SLC_BLOCK
    forced = (blk == 0) | (blk == cur) | (blk == cur - 1)
    score = jnp.where(forced, SEL_BIG, jnp.where(blk <= cur, imp, -SEL_BIG))
    sel_b = _top_blocks(score.T, SLC_TOPK, 0).T.astype(BF16)

    per = SEL_KEYS // SLC_BLOCK
    e_row = lax.broadcasted_iota(jnp.int32, (n_s, SEL_KEYS), 0)
    e_col = lax.broadcasted_iota(jnp.int32, (n_s, SEL_KEYS), 1) // SLC_BLOCK
    k_off = lax.broadcasted_iota(jnp.int32, (1, SEL_KEYS), 1)

    def sel_step(kb, carry):
        m, l, acc = carry
        r = pl.ds(pl.multiple_of(kb * SEL_KEYS, SEL_KEYS), SEL_KEYS)
        expand = (e_row == kb * per + e_col).astype(BF16)
        chosen = jnp.dot(sel_b, expand, preferred_element_type=F32) > 0.5
        ok = chosen & ((kb * SEL_KEYS + k_off) <= t_pos)
        bias = jnp.where(ok, 0.0, MASK_NEG)[None]
        s = _dot_nt(qp, ks_ref[r, :]).reshape(grp, qb, SEL_KEYS) + bias
        m_new = jnp.maximum(m, jnp.max(s, axis=-1, keepdims=True))
        alpha = jnp.exp(m - m_new)
        e = jnp.exp(s - m_new)
        l = alpha * l + jnp.sum(e, axis=-1, keepdims=True)
        pv = jnp.dot(e.reshape(rows, SEL_KEYS).astype(BF16), vs_ref[r, :], preferred_element_type=F32)
        return m_new, l, alpha * acc + pv.reshape(grp, qb, LANES)

    init = (jnp.full((grp, qb, 1), MASK_NEG, F32), jnp.zeros((grp, qb, 1), F32), jnp.zeros((grp, qb, LANES), F32))
    n_kb = (j * qb + qb - 1) // SEL_KEYS + 1
    _, l, acc = lax.fori_loop(0, n_kb, sel_step, init)
    o_s = (acc / jnp.maximum(l, 1e-30)).reshape(rows, LANES)

    n_w = WINDOW + qb
    w0 = pl.multiple_of(jnp.maximum(j * qb - WINDOW, 0), qb)
    dist = t_pos - (w0 + lax.broadcasted_iota(jnp.int32, (1, n_w), 1))
    w_vis = (dist >= 0) & (dist <= WINDOW)
    p_w = grouped(_masked_softmax, _dot_nt(qp, kw_ref[pl.ds(w0, n_w), :]), w_vis)
    o_w = jnp.dot(p_w.astype(BF16), vw_ref[pl.ds(w0, n_w), :], preferred_element_type=F32)

    gates = _sigmoid(gp_ref[...])
    gate = lambda br: jnp.concatenate(
        [jnp.broadcast_to(gates[:, 3 * g + br:3 * g + br + 1], (qb, LANES)) for g in range(grp)], axis=0)
    o = gate(0) * o_c + gate(1) * o_s + gate(2) * o_w
    half = lax.broadcasted_iota(jnp.int32, (rows, LANES), 1) // NSA_DH
    o = jnp.where(half == par, o, 0.0)
    o = o[:, :NSA_DH] + o[:, NSA_DH:]
    o_ref[...] = jnp.concatenate([o[g * qb:(g + 1) * qb] for g in range(grp)], axis=1).astype(o_ref.dtype)


def nsa_prompt_attention(qkv, gates_pre, ckv, kvb, *, batch):
    m = qkv.shape[0]
    l_seq = m // batch
    nq = l_seq // NSA_QBLOCK
    n_c = ckv.shape[1]
    qw = NSA_GROUP * NSA_DH
    pair = NSA_KVW // LANES
    kv_spec = lambda part: pl.BlockSpec((l_seq, LANES), lambda b, h, j: (b, part * pair + h // 2))
    return pl.pallas_call(
        _nsa_prompt_kernel,
        grid=(batch, NSA_KVH, nq),
        in_specs=[pl.BlockSpec((NSA_QBLOCK, qw), lambda b, h, j: (b * nq + j, h)),
                  pl.BlockSpec((NSA_QBLOCK, LANES), lambda b, h, j: (b * nq + j, h)),
                  pl.BlockSpec((1, n_c, LANES), lambda b, h, j: (b, 0, h // 2)),
                  pl.BlockSpec((1, n_c, LANES), lambda b, h, j: (b, 0, pair + h // 2)),
                  kv_spec(0), kv_spec(1), kv_spec(2), kv_spec(3)],
        out_specs=pl.BlockSpec((NSA_QBLOCK, qw), lambda b, h, j: (b * nq + j, h)),
        out_shape=jax.ShapeDtypeStruct((m, NSA_HEADS * NSA_DH), BF16),
        compiler_params=_params("parallel", "parallel", "arbitrary"),
        name="nsa_prompt_attention",
    )(qkv, gates_pre, ckv, ckv, kvb, kvb, kvb, kvb)


def _paged_compress_kernel(pt_ref, *refs, n_pages, page):
    del pt_ref
    pages = refs[:n_pages]
    pos_ref, w1_ref, w2_ref, even_ref, odd_ref = refs[n_pages:n_pages + 5]
    rows = refs[n_pages + 5:]
    for p in range(n_pages):
        for i, r in enumerate(rows):
            r[p * page:(p + 1) * page, :] = pages[p][0, :, i * LANES:(i + 1) * LANES]
    n_blk = n_pages * page // CMP_BLOCK
    full = _compress_rows(rows, pos_ref, w1_ref, w2_ref, n_blk)
    even_ref[0] = full[:n_blk // 2]
    odd_ref[0] = full[n_blk // 2:]


def _page_specs(n_pages, page):
    return [pl.BlockSpec((1, page, NSA_ROW), lambda b, pt, p=p: (pt[b, p], 0, 0)) for p in range(n_pages)]


def nsa_compress_paged(cache, page_table, pos4, w1, w2):
    n_seq, n_pages = page_table.shape
    page = cache.shape[1]
    half = n_pages * page // CMP_BLOCK // 2
    const = lambda shape: pl.BlockSpec(shape, lambda b, pt: (0,) * len(shape))
    out = jax.ShapeDtypeStruct((n_seq, half, NSA_ROW), F32)
    return pl.pallas_call(
        functools.partial(_paged_compress_kernel, n_pages=n_pages, page=page),
        grid_spec=pltpu.PrefetchScalarGridSpec(
            num_scalar_prefetch=1,
            grid=(n_seq,),
            in_specs=_page_specs(n_pages, page)
            + [const((2, CMP_BLOCK, NSA_KVW)), const((2, CMP_BLOCK * NSA_DH, CMP_HID)), const((2, CMP_HID, NSA_DH))],
            out_specs=[pl.BlockSpec((1, half, NSA_ROW), lambda b, pt: (b, 0, 0))] * 2,
            scratch_shapes=[pltpu.VMEM((n_pages * page, LANES), F32)] * (NSA_ROW // LANES)),
        out_shape=[out, out],
        compiler_params=_params("arbitrary"),
        name="nsa_compress_paged",
    )(page_table, *([cache] * n_pages), pos4, w1, w2)


def _softmax_with_new_key(s, ok, s_new, new_ok):
    s = jnp.where(ok, s, MASK_NEG)
    s_new = jnp.where(new_ok, s_new, MASK_NEG)
    m = jnp.maximum(jnp.max(s, axis=-1, keepdims=True), s_new)
    e = jnp.where(ok, jnp.exp(s - m), 0.0)
    e_new = jnp.where(new_ok, jnp.exp(s_new - m), 0.0)
    total = jnp.sum(e, axis=-1, keepdims=True) + e_new
    return e, e_new, 1.0 / jnp.maximum(total, 1e-30)


def _nsa_step_kernel(pt_ref, *refs, n_pages, page, past):
    del pt_ref
    pages = refs[:n_pages]
    (q_ref, gp_ref, cke_ref, cko_ref, kvs_new_ref, swa_ref, kvw_new_ref,
     o_ref, swa_out_ref, kb_ref, vb_ref) = refs[n_pages:]
    nh, kw = NSA_HEADS, NSA_KVW
    t = past
    own = (lax.broadcasted_iota(jnp.int32, (nh, kw), 1) // NSA_DH
           == lax.broadcasted_iota(jnp.int32, (nh, kw), 0) // NSA_GROUP)
    q = q_ref[0] * (NSA_DH ** -0.5)
    qp = jnp.where(own, jnp.concatenate([q] * NSA_KVH, axis=1), 0.0)
    qpb = qp.astype(BF16)

    half = cke_ref.shape[1]
    n_c = 2 * half
    ce, co = cke_ref[0], cko_ref[0]
    ck = jnp.concatenate([ce[:, :kw], co[:, :kw]], axis=0).astype(BF16)
    cv = jnp.concatenate([ce[:, kw:], co[:, kw:]], axis=0).astype(BF16)
    nn = lax.broadcasted_iota(jnp.int32, (1, n_c), 1)
    c_blk = jnp.where(nn < half, 2 * nn, 2 * (nn - half) + 1)
    p_c = _masked_softmax(_dot_nt(qpb, ck), (c_blk * CMP_BLOCK + CMP_BLOCK - 1) <= t)
    o_c = jnp.dot(p_c.astype(BF16), cv, preferred_element_type=F32)
    same_group = (lax.broadcasted_iota(jnp.int32, (nh, nh), 0) // NSA_GROUP
                  == lax.broadcasted_iota(jnp.int32, (nh, nh), 1) // NSA_GROUP).astype(F32)
    imp = jnp.dot(same_group, p_c, preferred_element_type=F32, precision=HIGHEST)
    imp = imp[:, :half] + imp[:, half:]
    imp = jnp.concatenate([imp, jnp.zeros((nh, LANES - half), F32)], axis=1)
    blk = lax.broadcasted_iota(jnp.int32, (nh, LANES), 1)
    cur = t // SLC_BLOCK
    forced = (blk == 0) | (blk == cur) | (blk == cur - 1)
    sel = _top_blocks(jnp.where(forced, SEL_BIG, jnp.where(blk <= cur, imp, -SEL_BIG)), SLC_TOPK, 1)

    for p in range(n_pages):
        kb_ref[p * page:(p + 1) * page, :] = pages[p][0, :, :kw].astype(BF16)
        vb_ref[p * page:(p + 1) * page, :] = pages[p][0, :, kw:].astype(BF16)
    n_keys = n_pages * page
    expand = (lax.broadcasted_iota(jnp.int32, (LANES, n_keys), 0)
              == lax.broadcasted_iota(jnp.int32, (LANES, n_keys), 1) // SLC_BLOCK).astype(BF16)
    chosen = jnp.dot(sel.astype(BF16), expand, preferred_element_type=F32) > 0.5
    ok = chosen & (lax.broadcasted_iota(jnp.int32, (1, n_keys), 1) <= t)
    kv_new = kvs_new_ref[0]
    s_new = jnp.sum(qp * kv_new[:, :kw], axis=-1, keepdims=True)
    e, e_new, inv = _softmax_with_new_key(_dot_nt(qpb, kb_ref[...]), ok, s_new, sel[:, cur:cur + 1] > 0.5)
    o_s = (jnp.dot(e.astype(BF16), vb_ref[...], preferred_element_type=F32) + e_new * kv_new[:, kw:]) * inv

    n_w = swa_ref.shape[1]
    swa = swa_ref[0]
    dist = t - (t - n_w + lax.broadcasted_iota(jnp.int32, (1, n_w), 1))
    w_ok = (dist >= 0) & (dist <= WINDOW) & (t - dist >= 0)
    kvw_new = kvw_new_ref[0]
    s_wn = jnp.sum(qp * kvw_new[:, :kw], axis=-1, keepdims=True)
    new_ok = lax.broadcasted_iota(jnp.int32, (nh, 1), 0) >= 0
    e, e_new, inv = _softmax_with_new_key(_dot_nt(qpb, swa[:, :kw].astype(BF16)), w_ok, s_wn, new_ok)
    o_w = (jnp.dot(e.astype(BF16), swa[:, kw:].astype(BF16), preferred_element_type=F32) + e_new * kvw_new[:, kw:]) * inv

    g = _sigmoid(gp_ref[0])
    o = jnp.where(own, g[:, 0:1] * o_c + g[:, 1:2] * o_s + g[:, 2:3] * o_w, 0.0)
    o_h = o[:, :NSA_DH]
    for h in range(1, NSA_KVH):
        o_h = o_h + o[:, h * NSA_DH:(h + 1) * NSA_DH]
    o_ref[0] = o_h
    swa_out_ref[0, 0:n_w - 1, :] = swa_ref[0, 1:n_w, :]
    swa_out_ref[0, n_w - 1:n_w, :] = kvw_new


def nsa_step_attention(q3, gates3, cke, cko, cache_slc, page_table, kvs_new, swa, kvw_new, *, past):
    n_seq, n_pages = page_table.shape
    page = cache_slc.shape[1]
    n_w = swa.shape[1]
    seq3 = lambda a: pl.BlockSpec((1,) + a.shape[1:], lambda b, pt: (b, 0, 0))
    return pl.pallas_call(
        functools.partial(_nsa_step_kernel, n_pages=n_pages, page=page, past=past),
        grid_spec=pltpu.PrefetchScalarGridSpec(
            num_scalar_prefetch=1,
            grid=(n_seq,),
            in_specs=_page_specs(n_pages, page) + [seq3(a) for a in (q3, gates3, cke, cko, kvs_new, swa, kvw_new)],
            out_specs=[seq3(q3), seq3(swa)],
            scratch_shapes=[pltpu.VMEM((n_pages * page, NSA_KVW), BF16)] * 2),
        out_shape=[jax.ShapeDtypeStruct(q3.shape, F32), jax.ShapeDtypeStruct(swa.shape, F32)],
        compiler_params=_params("arbitrary"),
        name="nsa_step_attention",
    )(page_table, *([cache_slc] * n_pages), q3, gates3, cke, cko, kvs_new, swa, kvw_new)


PROMPT_TM = 512
HG_TL = 512
RG_TL = 256
FFN_TF = 1408
CMP_TILE_BLOCKS = 64
MOE_TB = 1024
MOE_TS = 256


def _ffn_layer(x, w_gu, w_down, g, b, *, tm):
    return ffn_residual_ln(x, w_gu.astype(BF16), w_down.astype(BF16), g, b, tm=tm, tf=FFN_TF)


def _moe_layer(x, w_router, w_gu, w_down, g, b):
    m = x.shape[0]
    tb = min(MOE_TB, m)
    gates, slots = moe_route(x, w_router.T, tb=tb)
    n_e = gates.shape[0]
    counts = (jnp.max(slots.reshape(n_e, m // tb, tb), axis=-1) + 1).T.reshape(-1)
    w_gu_t = jnp.swapaxes(w_gu, 1, 2).astype(BF16)
    w_down_t = jnp.swapaxes(w_down, 1, 2).astype(BF16)
    return moe_residual_ln(x, gates, slots, counts, w_gu_t, w_down_t, g, b, tb=tb, ts=min(MOE_TS, tb), tf=FFN_TF)


def kernel(x_prompt, x_sample, state_hgrn, state_rglru_conv, state_rglru_h, cache_nsa_cmp, cache_nsa_slc, cache_nsa_swa, page_table, ln_g, ln_b, hgrn_w_in, hgrn_lb, hgrn_norm_g, hgrn_w_out, rg_w_in, rg_conv_w, rg_conv_b, rg_w_gate, rg_b_gate, rg_lambda, rg_w_out, nsa_w_in, nsa_cmp_pos, nsa_cmp_w1, nsa_cmp_w2, nsa_w_out, ffn_w_gu, ffn_w_down, moe_router, moe_w_gu, moe_w_down):
    bp, l_seq, d = x_prompt.shape
    n_seq = x_sample.shape[0]
    past = page_table.shape[1] * cache_nsa_cmp.shape[2]
    xp = x_prompt.reshape(bp * l_seq, d)
    xs = x_sample.reshape(n_seq, d)
    tm_p, tm_s = PROMPT_TM, n_seq
    kv_shape = (2, NSA_KVH, NSA_DH)
    hg_p, hg_s, rc_p, rc_s, rh_p, rh_s = [], [], [], [], [], []
    cm_p, cm_s, sl_p, sl_s, sw_p, sw_s = [], [], [], [], [], []
    for layer in range(DEPTH):
        kind, slot = layer % 3, layer // 3
        g0, b0 = ln_g[layer, 0], ln_b[layer, 0]
        if kind == 0:
            w_in, w_out = hgrn_w_in[slot].astype(BF16), hgrn_w_out[slot].astype(BF16)
            op, st_p = hgrn_prompt(matmul(xp, w_in, tm=tm_p, tn=512), hgrn_lb, hgrn_norm_g[slot],
                                   layer=layer, batch=bp, tl=HG_TL)
            os_, st_s = hgrn_step(matmul(xs, w_in, tm=tm_s, tn=512), state_hgrn[slot], hgrn_lb, hgrn_norm_g[slot],
                                  layer=layer, sb=8)
            hg_p.append(st_p)
            hg_s.append(st_s)
        elif kind == 1:
            w_in, w_out = rg_w_in[slot].astype(BF16), rg_w_out[slot].astype(BF16)
            rg_w = (rg_conv_w[slot], rg_conv_b[slot], rg_w_gate[slot].astype(BF16), rg_b_gate[slot], rg_lambda[slot])
            op, cb_p, h_p = rglru_prompt(matmul(xp, w_in, tm=tm_p, tn=512), *rg_w, batch=bp, tl=RG_TL)
            os_, cb_s, h_s = rglru_step(matmul(xs, w_in, tm=tm_s, tn=512), state_rglru_conv[slot], state_rglru_h[slot], *rg_w)
            rc_p.append(cb_p)
            rc_s.append(cb_s)
            rh_p.append(h_p[:, 0])
            rh_s.append(h_s)
        else:
            w_qkv, w_kvb, w_g, pos4, w1, w2 = nsa_prepare_weights(
                nsa_w_in[slot], nsa_cmp_pos[slot], nsa_cmp_w1[slot], nsa_cmp_w2[slot])
            w_out = nsa_w_out[slot].astype(BF16)
            hq = NSA_HEADS * NSA_DH
            qkv = matmul(xp, w_qkv, tm=tm_p, tn=512)
            kvb = matmul(xp, w_kvb, tm=tm_p, tn=512, out_dtype=BF16)
            gates_pre = matmul(xp, w_g, tm=tm_p, tn=NSA_KVH * LANES)
            cke, cko = nsa_compress(qkv, hq // NSA_ROW, pos4, w1, w2, batch=bp, n_blk=CMP_TILE_BLOCKS)
            op = nsa_prompt_attention(qkv, gates_pre, jnp.concatenate([cke, cko], axis=1), kvb, batch=bp)
            kv = qkv[:, hq:].reshape(bp, l_seq, 3, *kv_shape)
            cm_p.append(kv[:, :, 0])
            sl_p.append(kv[:, :, 1])
            sw_p.append(kv[:, l_seq - min(WINDOW, l_seq):, 2])
            qkv = matmul(xs, w_qkv, tm=tm_s, tn=512)
            gates_pre = matmul(xs, w_g, tm=tm_s, tn=NSA_KVH * LANES)
            rows = lambda c: c.reshape(c.shape[0], c.shape[1], NSA_ROW)
            cke, cko = nsa_compress_paged(rows(cache_nsa_cmp[slot]), page_table, pos4, w1, w2)
            gates3 = gates_pre.reshape(n_seq, NSA_KVH, LANES)[:, :, :3 * NSA_GROUP].reshape(n_seq, NSA_HEADS, 3)
            kv = qkv[:, hq:].reshape(n_seq, 3, 1, NSA_ROW)
            o3, swa_new = nsa_step_attention(qkv[:, :hq].reshape(n_seq, NSA_HEADS, NSA_DH), gates3, cke, cko,
                                             rows(cache_nsa_slc[slot]), page_table, kv[:, 1], rows(cache_nsa_swa[slot]),
                                             kv[:, 2], past=past)
            os_ = o3.reshape(n_seq, hq)
            cm_s.append(kv[:, 0].reshape(n_seq, 1, *kv_shape))
            sl_s.append(kv[:, 1].reshape(n_seq, 1, *kv_shape))
            sw_s.append(swa_new.reshape(n_seq, -1, *kv_shape))
        xp = matmul_residual_ln(op, w_out, xp, g0, b0, tm=tm_p)
        xs = matmul_residual_ln(os_, w_out, xs, g0, b0, tm=tm_s)
        g1, b1 = ln_g[layer, 1], ln_b[layer, 1]
        if layer % 2 == 0:
            fw = (ffn_w_gu[layer // 2], ffn_w_down[layer // 2])
            xp = _ffn_layer(xp, *fw, g1, b1, tm=tm_p)
            xs = _ffn_layer(xs, *fw, g1, b1, tm=tm_s)
        else:
            fw = (moe_router[layer // 2], moe_w_gu[layer // 2], moe_w_down[layer // 2])
            xp = _moe_layer(xp, *fw, g1, b1)
            xs = _moe_layer(xs, *fw, g1, b1)
    return (xp.reshape(bp, l_seq, d), xs.reshape(n_seq, 1, d),
            jnp.stack(hg_p), jnp.stack(hg_s), jnp.stack(rc_p), jnp.stack(rc_s), jnp.stack(rh_p), jnp.stack(rh_s),
            jnp.stack(cm_p), jnp.stack(cm_s), jnp.stack(sl_p), jnp.stack(sl_s), jnp.stack(sw_p), jnp.stack(sw_s))
```

```python
import functools

import jax
import jax.numpy as jnp
from jax import lax
from jax.experimental import pallas as pl
from jax.experimental.pallas import tpu as pltpu

F32 = jnp.float32
BF16 = jnp.bfloat16
HIGHEST = lax.Precision.HIGHEST

D_MODEL = 1024
DEPTH = 4
ALPHA = (2 * DEPTH) ** 0.25
LN_EPS = 1e-5
RMS_EPS = 1e-6
HG_HEADS = 8
HG_DK = 128
HG_CHUNK = 64
HG_SUB = 16
HG_HEADS_PER_STEP = 4
LB_FLOOR = 1e-30
RG_WIDTH = 1280
RG_BLOCKS = 10
RG_BS = 128
RG_CONV = 4
RG_C = 8.0
D_FF = 2816
N_EXPERTS = 8

VMEM_LIMIT_BYTES = 56 * 1024 * 1024


def _params(*sem):
    return pltpu.CompilerParams(dimension_semantics=sem, vmem_limit_bytes=VMEM_LIMIT_BYTES)


def _layer_norm(v, g, b):
    mu = jnp.mean(v, axis=-1, keepdims=True)
    vc = v - mu
    var = jnp.mean(vc * vc, axis=-1, keepdims=True)
    return vc * lax.rsqrt(var + LN_EPS) * g + b


def _sigmoid(x):
    return 1.0 / (1.0 + jnp.exp(-x))


def _silu(x):
    return x * _sigmoid(x)


def _mm_kernel(x_ref, w_ref, o_ref, xb_ref):
    @pl.when(pl.program_id(1) == 0)
    def _():
        xb_ref[...] = x_ref[...].astype(BF16)

    o_ref[...] = jnp.dot(xb_ref[...], w_ref[...], preferred_element_type=F32).astype(o_ref.dtype)


MAX_COL_TILE = 1280


def _col_tile(n):
    return max(t for t in range(128, min(n, MAX_COL_TILE) + 1, 128) if n % t == 0)


def matmul(x, w, *, tm, out_dtype=F32):
    m, k = x.shape
    n = w.shape[1]
    tn = _col_tile(n)
    return pl.pallas_call(
        _mm_kernel,
        grid=(m // tm, n // tn),
        in_specs=[pl.BlockSpec((tm, k), lambda i, j: (i, 0)),
                  pl.BlockSpec((k, tn), lambda i, j: (0, j))],
        out_specs=pl.BlockSpec((tm, tn), lambda i, j: (i, j)),
        out_shape=jax.ShapeDtypeStruct((m, n), out_dtype),
        scratch_shapes=[pltpu.VMEM((tm, k), BF16)],
        compiler_params=_params("parallel", "arbitrary"),
        name="matmul",
    )(x, w)


def _mm_res_ln_kernel(a_ref, w_ref, r_ref, g_ref, b_ref, o_ref):
    m = jnp.dot(a_ref[...].astype(BF16), w_ref[...], preferred_element_type=F32)
    o_ref[...] = _layer_norm(ALPHA * r_ref[...] + m, g_ref[...], b_ref[...])


def matmul_residual_ln(a, w, res, g, b, *, tm):
    m, k = a.shape
    d = w.shape[1]
    return pl.pallas_call(
        _mm_res_ln_kernel,
        grid=(m // tm,),
        in_specs=[pl.BlockSpec((tm, k), lambda i: (i, 0)),
                  pl.BlockSpec((k, d), lambda i: (0, 0)),
                  pl.BlockSpec((tm, d), lambda i: (i, 0)),
                  pl.BlockSpec((1, d), lambda i: (0, 0)),
                  pl.BlockSpec((1, d), lambda i: (0, 0))],
        out_specs=pl.BlockSpec((tm, d), lambda i: (i, 0)),
        out_shape=jax.ShapeDtypeStruct((m, d), F32),
        compiler_params=_params("parallel"),
        name="matmul_residual_ln",
    )(a, w, res, g.reshape(1, d), b.reshape(1, d))


def _ffn_kernel(x_ref, wg_ref, wu_ref, wd_ref, g_ref, b_ref, o_ref, xb_ref, acc_ref, *, n_f):
    f = pl.program_id(1)

    @pl.when(f == 0)
    def _():
        xb_ref[...] = x_ref[...].astype(BF16)

    xb = xb_ref[...]
    h = jnp.dot(xb, wg_ref[...], preferred_element_type=F32)
    u = jnp.dot(xb, wu_ref[...], preferred_element_type=F32)
    part = jnp.dot((_silu(h) * u).astype(BF16), wd_ref[...], preferred_element_type=F32)

    @pl.when(f == 0)
    def _():
        acc_ref[...] = part

    @pl.when(f > 0)
    def _():
        acc_ref[...] += part

    @pl.when(f == n_f - 1)
    def _():
        o_ref[...] = _layer_norm(ALPHA * x_ref[...] + acc_ref[...], g_ref[...], b_ref[...])


def ffn_residual_ln(x, w_gu, w_down, g, b, *, tm, tf):
    m, d = x.shape
    ff = w_down.shape[0]
    n_f = ff // tf
    return pl.pallas_call(
        functools.partial(_ffn_kernel, n_f=n_f),
        grid=(m // tm, n_f),
        in_specs=[pl.BlockSpec((tm, d), lambda i, f: (i, 0)),
                  pl.BlockSpec((d, tf), lambda i, f: (0, f)),
                  pl.BlockSpec((d, tf), lambda i, f: (0, n_f + f)),
                  pl.BlockSpec((tf, d), lambda i, f: (f, 0)),
                  pl.BlockSpec((1, d), lambda i, f: (0, 0)),
                  pl.BlockSpec((1, d), lambda i, f: (0, 0))],
        out_specs=pl.BlockSpec((tm, d), lambda i, f: (i, 0)),
        out_shape=jax.ShapeDtypeStruct((m, d), F32),
        scratch_shapes=[pltpu.VMEM((tm, d), BF16), pltpu.VMEM((tm, d), F32)],
        compiler_params=_params("parallel", "arbitrary"),
        name="ffn_residual_ln",
    )(x, w_gu, w_gu, w_down, g.reshape(1, d), b.reshape(1, d))


def _router_kernel(x_ref, wt_ref, gate_ref, pos_ref):
    logits = lax.dot_general(wt_ref[...], x_ref[...], (((1,), (1,)), ((), ())),
                             preferred_element_type=F32, precision=HIGHEST)
    n_e, tm = logits.shape
    expert = lax.broadcasted_iota(jnp.int32, logits.shape, 0)
    m1 = jnp.max(logits, axis=0, keepdims=True)
    i1 = jnp.min(jnp.where(logits == m1, expert, n_e), axis=0, keepdims=True)
    rest = jnp.where(expert == i1, -jnp.inf, logits)
    m2 = jnp.max(rest, axis=0, keepdims=True)
    i2 = jnp.min(jnp.where(rest == m2, expert, n_e), axis=0, keepdims=True)
    e2 = jnp.exp(m2 - m1)
    denom = 1.0 + e2
    first = expert == i1
    routed = first | (expert == i2)
    gate_ref[...] = jnp.where(first, 1.0 / denom, jnp.where(routed, e2 / denom, 0.0))
    upper = (lax.broadcasted_iota(jnp.int32, (tm, tm), 0) <= lax.broadcasted_iota(jnp.int32, (tm, tm), 1)).astype(BF16)
    upto = jnp.dot(routed.astype(BF16), upper, preferred_element_type=F32)
    pos_ref[...] = jnp.where(routed, upto - 1.0, -1.0).astype(jnp.int32)


def moe_route(x, w_router_t, *, tb):
    m, d = x.shape
    n_e = w_router_t.shape[0]
    return pl.pallas_call(
        _router_kernel,
        grid=(m // tb,),
        in_specs=[pl.BlockSpec((tb, d), lambda i: (i, 0)),
                  pl.BlockSpec((n_e, d), lambda i: (0, 0))],
        out_specs=[pl.BlockSpec((n_e, tb), lambda i: (0, i))] * 2,
        out_shape=[jax.ShapeDtypeStruct((n_e, m), F32), jax.ShapeDtypeStruct((n_e, m), jnp.int32)],
        compiler_params=_params("parallel"),
        name="moe_route",
    )(x, w_router_t)


def _moe_kernel(cnt_ref, x_ref, gate_ref, slot_ref, wg_ref, wu_ref, wd_ref, g_ref, b_ref, o_ref,
                xt_ref, xc_ref, acc_ref, yt_ref, *, n_e, n_f, ts):
    i, e, f = pl.program_id(0), pl.program_id(1), pl.program_id(2)
    tb = x_ref.shape[0]
    n_sub = (cnt_ref[i * n_e + e] + ts - 1) // ts
    row = lax.broadcasted_iota(jnp.int32, (ts, tb), 0)

    def one_hot(sub):
        return (slot_ref[pl.ds(e, 1), :] == row + sub * ts).astype(BF16)

    @pl.when((e == 0) & (f == 0))
    def _():
        xt_ref[...] = x_ref[...].T.astype(BF16)
        yt_ref[...] = jnp.zeros_like(yt_ref)

    tf, d = wg_ref.shape[1], wd_ref.shape[1]
    halves = lambda n: (slice(0, n // 2), slice(n // 2, n))

    @pl.when(f == 0)
    def _():
        def gather(sub, c):
            pick = one_hot(sub)
            for rows in halves(d):
                xc_ref[sub, rows, :] = _dot_nt(xt_ref[rows, :], pick).astype(BF16)
            return c

        lax.fori_loop(0, n_sub, gather, 0)

    def expert_ffn(sub, c):
        xc = xc_ref[sub]
        act = []
        for rows in halves(tf):
            h = jnp.dot(wg_ref[0, rows, :], xc, preferred_element_type=F32)
            u = jnp.dot(wu_ref[0, rows, :], xc, preferred_element_type=F32)
            act.append((_silu(h) * u).astype(BF16))
        act = jnp.concatenate(act, axis=0)
        part = jnp.concatenate([jnp.dot(wd_ref[0, rows, :], act, preferred_element_type=F32) for rows in halves(d)], axis=0)

        @pl.when(f == 0)
        def _():
            acc_ref[sub] = part

        @pl.when(f > 0)
        def _():
            acc_ref[sub] += part

        return c

    lax.fori_loop(0, n_sub, expert_ffn, 0)

    @pl.when(f == n_f - 1)
    def _():
        def scatter(sub, c):
            pick = one_hot(sub)
            gate = gate_ref[pl.ds(e, 1), :]
            for rows in halves(d):
                back = jnp.dot(acc_ref[sub, rows, :].astype(BF16), pick, preferred_element_type=F32)
                yt_ref[rows, :] += gate * back
            return c

        lax.fori_loop(0, n_sub, scatter, 0)

    @pl.when((e == n_e - 1) & (f == n_f - 1))
    def _():
        o_ref[...] = _layer_norm(ALPHA * x_ref[...] + yt_ref[...].T, g_ref[...], b_ref[...])


def moe_residual_ln(x, gates, slots, counts, w_gu_t, w_down_t, g, b, *, tb, ts, tf):
    m, d = x.shape
    n_e, ff2, _ = w_gu_t.shape
    n_f = ff2 // 2 // tf
    n_sub = tb // ts
    return pl.pallas_call(
        functools.partial(_moe_kernel, n_e=n_e, n_f=n_f, ts=ts),
        grid_spec=pltpu.PrefetchScalarGridSpec(
            num_scalar_prefetch=1,
            grid=(m // tb, n_e, n_f),
            in_specs=[pl.BlockSpec((tb, d), lambda i, e, f, c: (i, 0)),
                      pl.BlockSpec((n_e, tb), lambda i, e, f, c: (0, i)),
                      pl.BlockSpec((n_e, tb), lambda i, e, f, c: (0, i)),
                      pl.BlockSpec((1, tf, d), lambda i, e, f, c: (e, f, 0)),
                      pl.BlockSpec((1, tf, d), lambda i, e, f, c: (e, n_f + f, 0)),
                      pl.BlockSpec((1, d, tf), lambda i, e, f, c: (e, 0, f)),
                      pl.BlockSpec((1, d), lambda i, e, f, c: (0, 0)),
                      pl.BlockSpec((1, d), lambda i, e, f, c: (0, 0))],
            out_specs=pl.BlockSpec((tb, d), lambda i, e, f, c: (i, 0)),
            scratch_shapes=[pltpu.VMEM((d, tb), BF16), pltpu.VMEM((n_sub, d, ts), BF16),
                            pltpu.VMEM((n_sub, d, ts), F32), pltpu.VMEM((d, tb), F32)]),
        out_shape=jax.ShapeDtypeStruct((m, d), F32),
        compiler_params=_params("parallel", "arbitrary", "arbitrary"),
        name="moe_residual_ln",
    )(counts, x, gates, slots, w_gu_t, w_gu_t, w_down_t, g.reshape(1, d), b.reshape(1, d))


def _hgrn_lower_bound(lbp, layer):
    m = jnp.max(lbp, axis=0, keepdims=True)
    e = jnp.exp(lbp - m)
    p = e / jnp.sum(e, axis=0, keepdims=True)
    c = p[0:1]
    for i in range(1, layer + 1):
        c = c + p[i:i + 1]
    return c - p[0:1]


def _hgrn_gates(q, fx, lb):
    log_lb = jnp.log(jnp.maximum(lb, LB_FLOOR))
    log1m = jnp.log1p(-lb)
    log_sig = -(jnp.maximum(-fx, 0.0) + jnp.log1p(jnp.exp(-jnp.abs(fx))))
    b2 = log1m + log_sig
    log_f = jnp.maximum(log_lb, b2) + jnp.log1p(jnp.exp(-jnp.abs(log_lb - b2)))
    kk = (1.0 - lb) * _sigmoid(-fx) - (jnp.maximum(lb, LB_FLOOR) - lb)
    return _silu(q), log_f, kk


def _hgrn_out_norm(o, g, ng):
    o = o * lax.rsqrt(jnp.mean(o * o, axis=-1, keepdims=True) + RMS_EPS)
    return o * ng * _silu(g)


def _dot_nt(a, b):
    return lax.dot_general(a, b, (((1,), (1,)), ((), ())), preferred_element_type=F32)


def _dot_tn(a, b):
    return lax.dot_general(a, b, (((0,), (0,)), ((), ())), preferred_element_type=F32)


def _gla_kernel(q_ref, f_ref, i_ref, g_ref, lbp_ref, ng_ref, o_ref, s_out_ref, st_ref, *, layer, n_t, tl, hp):
    t = pl.program_id(2)
    c_len, sub = HG_CHUNK, HG_SUB
    n_sub = c_len // sub
    n_stack = sub * (n_sub * (n_sub - 1) // 2)

    @pl.when(t == 0)
    def _():
        st_ref[...] = jnp.zeros_like(st_ref)

    lb = _hgrn_lower_bound(lbp_ref[...], layer)
    ng = ng_ref[...]
    row = lax.broadcasted_iota(jnp.int32, (c_len, c_len), 0)
    col = lax.broadcasted_iota(jnp.int32, (c_len, c_len), 1)
    tri = (col <= row).astype(F32)
    srow = lax.broadcasted_iota(jnp.int32, (c_len, n_stack), 0) // sub
    scol = lax.broadcasted_iota(jnp.int32, (c_len, n_stack), 1)
    part = jnp.zeros((c_len, n_stack), jnp.int32)
    for p in range(1, n_sub):
        part = part + (scol >= sub * (p * (p - 1) // 2)).astype(jnp.int32)
    inter_mask = srow == part
    row_s = lax.broadcasted_iota(jnp.int32, (sub, 1), 0)

    def one_head(hh, r):
        lanes = slice(hh * HG_DK, (hh + 1) * HG_DK)
        qs, log_f, kk = _hgrn_gates(q_ref[r, lanes], f_ref[r, lanes], lb[:, lanes])
        v = i_ref[r, lanes]
        cum = jnp.dot(tri, log_f, preferred_element_type=F32, precision=HIGHEST)
        last = cum[c_len - 1:c_len, :]
        st = st_ref[hh]
        o = _dot_nt((qs * jnp.exp(cum)).astype(BF16), st.astype(BF16))
        bounds = [cum[sub * a - 1:sub * a, :] for a in range(1, n_sub)]
        cq = jnp.concatenate([cum[0:sub]] + [jnp.broadcast_to(b, (sub, HG_DK)) for b in bounds], axis=0)
        q_rel = (qs * jnp.exp(cum - cq)).astype(BF16)
        k_st = jnp.concatenate([kk[0:sub * a] * jnp.exp(bounds[a - 1] - cum[0:sub * a]) for a in range(1, n_sub)], axis=0)
        v_st = jnp.concatenate([v[0:sub * a] for a in range(1, n_sub)], axis=0)
        att = jnp.where(inter_mask, _dot_nt(q_rel, k_st.astype(BF16)), 0.0)
        o = o + jnp.dot(att.astype(BF16), v_st.astype(BF16), preferred_element_type=F32)
        parts = []
        for a in range(n_sub):
            sl = slice(sub * a, sub * a + sub)
            qa, ka, va, ca = qs[sl], kk[sl], v[sl], cum[sl]
            od = o[sl]
            for s in range(sub):
                e = jnp.exp(jnp.minimum(ca - ca[s:s + 1], 0.0))
                w = jnp.sum(qa * (ka[s:s + 1] * e), axis=-1, keepdims=True)
                od = od + jnp.where(row_s >= s, w, 0.0) * va[s:s + 1]
            parts.append(od)
        o = jnp.concatenate(parts, axis=0)
        o_ref[r, lanes] = _hgrn_out_norm(o, g_ref[r, lanes], ng[:, lanes]).astype(o_ref.dtype)
        k_end = (kk * jnp.exp(last - cum)).astype(BF16)
        st_ref[hh] = st * jnp.exp(last) + _dot_tn(v.astype(BF16), k_end)

    def chunk(c, carry):
        r = pl.ds(pl.multiple_of(c * c_len, c_len), c_len)
        for hh in range(hp):
            one_head(hh, r)
        return carry

    lax.fori_loop(0, tl // c_len, chunk, 0)

    @pl.when(t == n_t - 1)
    def _():
        for hh in range(hp):
            s_out_ref[0, hh] = st_ref[hh].T


def hgrn_prompt(z, lb_param, norm_g, *, layer, batch, tl):
    m = z.shape[0]
    n_t = m // batch // tl
    h, dk = HG_HEADS, HG_DK
    hp = HG_HEADS_PER_STEP
    n_hg = h // hp
    zspec = lambda part: pl.BlockSpec((tl, hp * dk), lambda b, hh, t: (b * n_t + t, part * n_hg + hh))
    return pl.pallas_call(
        functools.partial(_gla_kernel, layer=layer, n_t=n_t, tl=tl, hp=hp),
        grid=(batch, n_hg, n_t),
        in_specs=[zspec(0), zspec(1), zspec(2), zspec(3),
                  pl.BlockSpec((DEPTH, hp * dk), lambda b, hh, t: (0, hh)),
                  pl.BlockSpec((1, hp * dk), lambda b, hh, t: (0, hh))],
        out_specs=[pl.BlockSpec((tl, hp * dk), lambda b, hh, t: (b * n_t + t, hh)),
                   pl.BlockSpec((1, hp, dk, dk), lambda b, hh, t: (b, hh, 0, 0))],
        out_shape=[jax.ShapeDtypeStruct((m, h * dk), BF16),
                   jax.ShapeDtypeStruct((batch, h, dk, dk), F32)],
        scratch_shapes=[pltpu.VMEM((hp, dk, dk), F32)],
        compiler_params=_params("parallel", "parallel", "arbitrary"),
        name="hgrn_prompt",
    )(z, z, z, z, lb_param, norm_g.reshape(1, h * dk))


def _hgrn_step_kernel(q_ref, f_ref, i_ref, g_ref, lbp_ref, ng_ref, s0_ref, o_ref, s_ref,
                      qt_ref, ft_ref, kt_ref, oacc_ref, *, layer, n_j, sb):
    j = pl.program_id(1)
    n_seq = q_ref.shape[0]

    def split3(x, ref):
        hi = x.astype(BF16)
        r1 = x - hi.astype(F32)
        mid = r1.astype(BF16)
        ref[0], ref[1], ref[2] = hi, mid, (r1 - mid.astype(F32)).astype(BF16)

    @pl.when(j == 0)
    def _():
        lb = _hgrn_lower_bound(lbp_ref[...], layer)
        qs, log_f, kk = _hgrn_gates(q_ref[...], f_ref[...], lb)
        split3(qs.T, qt_ref)
        split3(jnp.exp(log_f).T, ft_ref)
        split3(kk.T, kt_ref)

    seq = lax.broadcasted_iota(jnp.int32, (n_seq, HG_DK), 0)
    for i in range(sb):
        b = j * sb + i
        pick = (seq == b).astype(BF16)
        col = lambda ref: sum(jnp.dot(ref[p], pick, preferred_element_type=F32) for p in range(3))
        s_new = s0_ref[i, 0] * col(ft_ref) + col(kt_ref) * i_ref[pl.ds(b, 1), :]
        s_ref[i, 0] = s_new
        oacc_ref[pl.ds(b, 1), :] = jnp.sum(col(qt_ref) * s_new, axis=0, keepdims=True)

    @pl.when(j == n_j - 1)
    def _():
        o_ref[...] = _hgrn_out_norm(oacc_ref[...], g_ref[...], ng_ref[...]).astype(o_ref.dtype)


def hgrn_step(z, s0, lb_param, norm_g, *, layer, sb):
    n_seq = z.shape[0]
    h, dk = HG_HEADS, HG_DK
    n_j = n_seq // sb
    zspec = lambda off: pl.BlockSpec((n_seq, dk), lambda hh, j: (0, off + hh))
    return pl.pallas_call(
        functools.partial(_hgrn_step_kernel, layer=layer, n_j=n_j, sb=sb),
        grid=(h, n_j),
        in_specs=[zspec(0), zspec(h), zspec(2 * h), zspec(3 * h),
                  pl.BlockSpec((DEPTH, dk), lambda hh, j: (0, hh)),
                  pl.BlockSpec((1, dk), lambda hh, j: (0, hh)),
                  pl.BlockSpec((sb, 1, dk, dk), lambda hh, j: (j, hh, 0, 0))],
        out_specs=[pl.BlockSpec((n_seq, dk), lambda hh, j: (0, hh)),
                   pl.BlockSpec((sb, 1, dk, dk), lambda hh, j: (j, hh, 0, 0))],
        out_shape=[jax.ShapeDtypeStruct((n_seq, h * dk), BF16),
                   jax.ShapeDtypeStruct(s0.shape, F32)],
        scratch_shapes=[pltpu.VMEM((3, dk, n_seq), BF16)] * 3 + [pltpu.VMEM((n_seq, dk), F32)],
        compiler_params=_params("parallel", "arbitrary"),
        name="hgrn_step",
    )(z, z, z, z, lb_param, norm_g.reshape(1, h * dk), s0)


def _gelu_tanh(x):
    return x * (0.5 * (1.0 + jnp.tanh(0.7978845608028654 * (x + 0.044715 * (x * x * x)))))


def _rg_decay_and_input(xc, wg_ref, bg_ref, lam_ref):
    xb = xc.astype(BF16)
    gates = []
    for gi in range(2):
        blocks = [jnp.dot(xb[:, n * RG_BS:(n + 1) * RG_BS], wg_ref[gi, n], preferred_element_type=F32)
                  for n in range(RG_BLOCKS)]
        gates.append(jnp.concatenate(blocks, axis=1) + bg_ref[gi:gi + 1, :])
    r = _sigmoid(gates[0])
    i = _sigmoid(gates[1])
    neg_lam = -lam_ref[...]
    softplus = jnp.maximum(neg_lam, 0.0) + jnp.log1p(jnp.exp(-jnp.abs(neg_lam)))
    log_a = -RG_C * r * softplus
    one_minus_a2 = -jnp.tanh(log_a) * (jnp.exp(2.0 * log_a) + 1.0)
    return jnp.exp(log_a), jnp.sqrt(one_minus_a2) * i * xc


def _rg_prompt_kernel(y_ref, x_ref, cw_ref, cb_ref, wg_ref, bg_ref, lam_ref,
                      o_ref, conv_out_ref, h_out_ref, xpad_ref, h_ref, *, n_t, tl):
    t = pl.program_id(1)
    halo = 8

    @pl.when(t == 0)
    def _():
        xpad_ref[0:halo, :] = jnp.zeros((halo, RG_WIDTH), F32)
        h_ref[...] = jnp.zeros_like(h_ref)

    x = x_ref[...]
    xpad_ref[halo:halo + tl, :] = x
    xc = cb_ref[...]
    for j in range(RG_CONV):
        off = halo - (RG_CONV - 1) + j
        xc = xc + xpad_ref[off:off + tl, :] * cw_ref[j:j + 1, :]
    xpad_ref[0:halo, :] = x[tl - halo:tl, :]
    a, u = _rg_decay_and_input(xc, wg_ref, bg_ref, lam_ref)
    row = lax.broadcasted_iota(jnp.int32, (tl, 1), 0)
    s = 1
    while s < tl:
        keep = row >= s
        a_prev = jnp.where(keep, pltpu.roll(a, s, 0), 1.0)
        u_prev = jnp.where(keep, pltpu.roll(u, s, 0), 0.0)
        u = a * u_prev + u
        a = a * a_prev
        s *= 2
    h = a * h_ref[...] + u
    h_ref[...] = h[tl - 1:tl, :]
    o_ref[...] = (_gelu_tanh(y_ref[...]) * h).astype(o_ref.dtype)

    @pl.when(t == n_t - 1)
    def _():
        conv_out_ref[0] = x[tl - (RG_CONV - 1):tl, :]
        h_out_ref[0] = h[tl - 1:tl, :]


def rglru_prompt(z, conv_w, conv_b, w_gate, b_gate, lam, *, batch, tl):
    m = z.shape[0]
    n_t = m // batch // tl
    w = RG_WIDTH
    const = lambda shape: pl.BlockSpec(shape, lambda b, t: (0,) * len(shape))
    return pl.pallas_call(
        functools.partial(_rg_prompt_kernel, n_t=n_t, tl=tl),
        grid=(batch, n_t),
        in_specs=[pl.BlockSpec((tl, w), lambda b, t: (b * n_t + t, 0)),
                  pl.BlockSpec((tl, w), lambda b, t: (b * n_t + t, 1)),
                  const((RG_CONV, w)), const((1, w)), const((2, RG_BLOCKS, RG_BS, RG_BS)), const((2, w)), const((1, w))],
        out_specs=[pl.BlockSpec((tl, w), lambda b, t: (b * n_t + t, 0)),
                   pl.BlockSpec((1, RG_CONV - 1, w), lambda b, t: (b, 0, 0)),
                   pl.BlockSpec((1, 1, w), lambda b, t: (b, 0, 0))],
        out_shape=[jax.ShapeDtypeStruct((m, w), BF16),
                   jax.ShapeDtypeStruct((batch, RG_CONV - 1, w), F32),
                   jax.ShapeDtypeStruct((batch, 1, w), F32)],
        scratch_shapes=[pltpu.VMEM((tl + 8, w), F32), pltpu.VMEM((1, w), F32)],
        compiler_params=_params("parallel", "arbitrary"),
        name="rglru_prompt",
    )(z, z, conv_w, conv_b.reshape(1, w), w_gate, b_gate, lam.reshape(1, w))


def _rg_step_kernel(y_ref, x_ref, buf_ref, h0_ref, cw_ref, cb_ref, wg_ref, bg_ref, lam_ref,
                    o_ref, conv_out_ref, h_out_ref):
    x = x_ref[...]
    xc = cb_ref[...]
    for j in range(RG_CONV - 1):
        xc = xc + buf_ref[:, j, :] * cw_ref[j:j + 1, :]
    xc = xc + x * cw_ref[RG_CONV - 1:RG_CONV, :]
    a, u = _rg_decay_and_input(xc, wg_ref, bg_ref, lam_ref)
    h = a * h0_ref[...] + u
    o_ref[...] = (_gelu_tanh(y_ref[...]) * h).astype(o_ref.dtype)
    h_out_ref[...] = h
    for j in range(RG_CONV - 2):
        conv_out_ref[:, j, :] = buf_ref[:, j + 1, :]
    conv_out_ref[:, RG_CONV - 2, :] = x


def rglru_step(z, conv_buf, h0, conv_w, conv_b, w_gate, b_gate, lam):
    n_seq = z.shape[0]
    w = RG_WIDTH
    const = lambda shape: pl.BlockSpec(shape, lambda i: (0,) * len(shape))
    return pl.pallas_call(
        _rg_step_kernel,
        grid=(1,),
        in_specs=[pl.BlockSpec((n_seq, w), lambda i: (0, 0)),
                  pl.BlockSpec((n_seq, w), lambda i: (0, 1)),
                  const((n_seq, RG_CONV - 1, w)), const((n_seq, w)),
                  const((RG_CONV, w)), const((1, w)), const((2, RG_BLOCKS, RG_BS, RG_BS)), const((2, w)), const((1, w))],
        out_specs=[const((n_seq, w)), const((n_seq, RG_CONV - 1, w)), const((n_seq, w))],
        out_shape=[jax.ShapeDtypeStruct((n_seq, w), BF16),
                   jax.ShapeDtypeStruct((n_seq, RG_CONV - 1, w), F32),
                   jax.ShapeDtypeStruct((n_seq, w), F32)],
        compiler_params=_params("arbitrary"),
        name="rglru_step",
    )(z, z, conv_buf, h0, conv_w, conv_b.reshape(1, w), w_gate, b_gate, lam.reshape(1, w))


NSA_HEADS = 16
NSA_KVH = 4
NSA_GROUP = 4
NSA_DH = 64
NSA_KVW = NSA_KVH * NSA_DH
NSA_ROW = 2 * NSA_KVW
CMP_BLOCK = 32
CMP_HID = 128
SLC_BLOCK = 64
SLC_TOPK = 16
WINDOW = 512
NSA_QBLOCK = 128
MASK_NEG = -1e30
SEL_BIG = 1e9
LANES = 128
SEL_KEYS = 1024
LOG2_E = 1.4426950408889634


def _masked_softmax(s, mask, exp=jnp.exp):
    s = jnp.where(mask, s, MASK_NEG)
    m = jnp.max(s, axis=-1, keepdims=True)
    e = jnp.where(mask, exp(s - m), 0.0)
    return e / jnp.maximum(jnp.sum(e, axis=-1, keepdims=True), 1e-30)


def _top_blocks(score, k, axis):
    pos = lax.broadcasted_iota(jnp.int32, score.shape, axis)
    n = score.shape[axis]

    def pick(_, carry):
        sc, sel = carry
        mx = jnp.max(sc, axis=axis, keepdims=True)
        idx = jnp.min(jnp.where(sc == mx, pos, n), axis=axis, keepdims=True)
        hit = pos == idx
        return jnp.where(hit, -jnp.inf, sc), jnp.where(hit, 1.0, sel)

    return lax.fori_loop(0, k, pick, (score, jnp.zeros(score.shape, F32)))[1]


def _compress_rows(x_refs, pos_ref, w1_ref, w2_ref, n_blk):
    half = n_blk // 2
    tiles = NSA_KVW // LANES
    low = lax.broadcasted_iota(jnp.int32, (n_blk, LANES), 1) < NSA_DH
    outs = []
    for c in range(2):
        cols = [[] for _ in range(NSA_KVH)]
        for l in range(0, CMP_BLOCK, 2):
            for tl in range(tiles):
                ref = x_refs[c * tiles + tl]

                def rows(r, ref=ref, tl=tl):
                    xe = ref[pl.ds(r, half, stride=2 * CMP_BLOCK), :]
                    xo = ref[pl.ds(r + CMP_BLOCK, half, stride=2 * CMP_BLOCK), :]
                    return jnp.concatenate([xe, xo], axis=0) + pos_ref[c, r:r + 1, tl * LANES:(tl + 1) * LANES]

                x0, x1 = rows(l), rows(l + 1)
                cols[2 * tl].append(jnp.where(low, x0, pltpu.roll(x1, NSA_DH, 1)))
                cols[2 * tl + 1].append(jnp.where(low, pltpu.roll(x0, NSA_DH, 1), x1))
        xs = jnp.concatenate([jnp.concatenate(cols[h], axis=1) for h in range(NSA_KVH)], axis=0)
        acc = jnp.dot(xs.astype(BF16), w1_ref[c], preferred_element_type=F32)
        out = jnp.dot(_silu(acc).astype(BF16), w2_ref[c], preferred_element_type=F32)
        outs.append(jnp.concatenate([out[h * n_blk:(h + 1) * n_blk] for h in range(NSA_KVH)], axis=1))
    return jnp.concatenate(outs, axis=1)


def _compress_kernel(x0_ref, x1_ref, x2_ref, x3_ref, pos_ref, w1_ref, w2_ref, even_ref, odd_ref, *, n_blk):
    full = _compress_rows((x0_ref, x1_ref, x2_ref, x3_ref), pos_ref, w1_ref, w2_ref, n_blk)
    even_ref[0] = full[:n_blk // 2]
    odd_ref[0] = full[n_blk // 2:]


def nsa_compress(kv, col_block, pos4, w1, w2, *, batch, n_blk):
    m = kv.shape[0]
    n_t = m // batch // (n_blk * CMP_BLOCK)
    half = n_blk // 2
    const = lambda shape: pl.BlockSpec(shape, lambda b, t: (0,) * len(shape))
    out = jax.ShapeDtypeStruct((batch, n_t * half, NSA_ROW), F32)
    tiles = NSA_ROW // LANES
    lane_tile = lambda i: pl.BlockSpec((n_blk * CMP_BLOCK, LANES), lambda b, t: (b * n_t + t, col_block * tiles + i))
    return pl.pallas_call(
        functools.partial(_compress_kernel, n_blk=n_blk),
        grid=(batch, n_t),
        in_specs=[lane_tile(i) for i in range(tiles)]
        + [const((2, CMP_BLOCK, NSA_KVW)), const((2, CMP_BLOCK * NSA_DH, CMP_HID)), const((2, CMP_HID, NSA_DH))],
        out_specs=[pl.BlockSpec((1, half, NSA_ROW), lambda b, t: (b, t, 0))] * 2,
        out_shape=[out, out],
        compiler_params=_params("parallel", "parallel"),
        name="nsa_compress",
    )(kv, kv, kv, kv, pos4, w1, w2)


def nsa_prepare_weights(w_in, pos, w1, w2):
    hq = NSA_HEADS * NSA_DH
    kv_end = hq + 3 * NSA_ROW
    w_qkv = w_in[:, :kv_end].astype(BF16)
    w_kvb = w_in[:, hq + NSA_ROW:kv_end].astype(BF16)
    w_g = w_in[:, kv_end:].reshape(-1, NSA_KVH, 3 * NSA_GROUP)
    w_g = jnp.pad(w_g, ((0, 0), (0, 0), (0, LANES - 3 * NSA_GROUP))).reshape(-1, NSA_KVH * LANES).astype(BF16)
    pos4 = jnp.tile(pos, (1, 1, NSA_KVH))
    w1 = w1.reshape(2, CMP_BLOCK * NSA_DH, CMP_HID)
    return w_qkv, w_kvb, w_g, pos4, w1.astype(BF16), w2.astype(BF16)


def _stack_group_queries(q, par):
    half = lax.broadcasted_iota(jnp.int32, (q.shape[0], LANES), 1) // NSA_DH
    parts = []
    for g in range(NSA_GROUP):
        qg = q[:, g * NSA_DH:(g + 1) * NSA_DH]
        parts.append(jnp.where(half == par, jnp.concatenate([qg, qg], axis=1), 0.0))
    return jnp.concatenate(parts, axis=0).astype(BF16)


def _nsa_prompt_kernel(q_ref, gp_ref, ck_ref, cv_ref, ks_ref, vs_ref, kw_ref, vw_ref, o_ref):
    kvh = pl.program_id(1)
    j = pl.program_id(2)
    par = kvh % 2
    qb, grp = NSA_QBLOCK, NSA_GROUP
    rows = grp * qb
    qp = _stack_group_queries(q_ref[...] * (NSA_DH ** -0.5 * LOG2_E), par)
    t_pos = j * qb + lax.broadcasted_iota(jnp.int32, (qb, 1), 0)

    def grouped_softmax(s, mask):
        n = s.shape[-1]
        return _masked_softmax(s.reshape(grp, qb, n), mask[None], jnp.exp2).reshape(rows, n)

    n_c = ck_ref.shape[1]
    n_s = n_c // 2
    ck = ck_ref[0].astype(BF16)
    cv = cv_ref[0].astype(BF16)
    nn = lax.broadcasted_iota(jnp.int32, (1, n_c), 1)
    c_blk = jnp.where(nn < n_s, 2 * nn, 2 * (nn - n_s) + 1)
    c_vis = (c_blk * CMP_BLOCK + CMP_BLOCK - 1) <= t_pos
    p_c = grouped_softmax(_dot_nt(qp, ck), c_vis)
    o_c = jnp.dot(p_c.astype(BF16), cv, preferred_element_type=F32)
    imp = p_c[0:qb]
    for g in range(1, grp):
        imp = imp + p_c[g * qb:(g + 1) * qb]
    imp = imp[:, :n_s] + imp[:, n_s:]
    blk = lax.broadcasted_iota(jnp.int32, (qb, n_s), 1)
    cur = t_pos // SLC_BLOCK
    forced = (blk == 0) | (blk == cur) | (blk == cur - 1)
    score = jnp.where(forced, SEL_BIG, jnp.where(blk <= cur, imp, -SEL_BIG))
    sel_b = _top_blocks(score.T, SLC_TOPK, 0).T.astype(BF16)

    per = SEL_KEYS // SLC_BLOCK
    e_row = lax.broadcasted_iota(jnp.int32, (n_s, SEL_KEYS), 0)
    e_col = lax.broadcasted_iota(jnp.int32, (n_s, SEL_KEYS), 1) // SLC_BLOCK
    k_off = lax.broadcasted_iota(jnp.int32, (1, SEL_KEYS), 1)

    n_kb = (j * qb + qb - 1) // SEL_KEYS + 1

    def key_rows(kb):
        return pl.ds(pl.multiple_of(kb * SEL_KEYS, SEL_KEYS), SEL_KEYS)

    own_lanes = lax.broadcasted_iota(jnp.int32, (SEL_KEYS, LANES), 1) // NSA_DH == par

    def sel_step(kb, carry):
        m, acc = carry
        expand =(e_row == kb * per + e_col).astype(BF16)
        chosen = jnp.dot(sel_b, expand, preferred_element_type=F32) > 0.5
        ok = chosen & ((kb * SEL_KEYS + k_off) <= t_pos)
        bias = jnp.where(ok, 0.0, MASK_NEG)[None]
        s = _dot_nt(qp, ks_ref[key_rows(kb), :]).reshape(grp, qb, SEL_KEYS) + bias
        m_new = jnp.maximum(m, jnp.max(s, axis=-1, keepdims=True))
        alpha = jnp.exp2(m - m_new)
        e = jnp.exp2(s - m_new)
        v_one = jnp.where(own_lanes, vs_ref[key_rows(kb), :], 1.0)
        pv = jnp.dot(e.reshape(rows, SEL_KEYS).astype(BF16), v_one, preferred_element_type=F32)
        return m_new, alpha * acc + pv.reshape(grp, qb, LANES)

    init = (jnp.full((grp, qb, 1), MASK_NEG, F32), jnp.zeros((grp, qb, LANES), F32))
    acc = lax.fori_loop(0, n_kb, sel_step, init)[1].reshape(rows, LANES)
    o_s = acc / jnp.maximum(pltpu.roll(acc, NSA_DH, 1), 1e-30)

    n_w = WINDOW + qb
    w0 = pl.multiple_of(jnp.maximum(j * qb - WINDOW, 0), qb)
    dist = t_pos - (w0 + lax.broadcasted_iota(jnp.int32, (1, n_w), 1))
    w_bias = jnp.where((dist >= 0) & (dist <= WINDOW), 0.0, MASK_NEG)[None]
    s_w = _dot_nt(qp, kw_ref[pl.ds(w0, n_w), :]).reshape(grp, qb, n_w) + w_bias
    e_w = jnp.exp2(s_w - jnp.max(s_w, axis=-1, keepdims=True)).reshape(rows, n_w)
    own_w = lax.broadcasted_iota(jnp.int32, (n_w, LANES), 1) // NSA_DH == par
    o_w = jnp.dot(e_w.astype(BF16), jnp.where(own_w, vw_ref[pl.ds(w0, n_w), :], 1.0), preferred_element_type=F32)
    o_w = o_w / jnp.maximum(pltpu.roll(o_w, NSA_DH, 1), 1e-30)

    gates = _sigmoid(gp_ref[...])
    gate = lambda br: jnp.concatenate(
        [jnp.broadcast_to(gates[:, 3 * g + br:3 * g + br + 1], (qb, LANES)) for g in range(grp)], axis=0)
    o = gate(0) * o_c + gate(1) * o_s + gate(2) * o_w
    half = lax.broadcasted_iota(jnp.int32, (rows, LANES), 1) // NSA_DH
    o = jnp.where(half == par, o, 0.0)
    o = o[:, :NSA_DH] + o[:, NSA_DH:]
    o_ref[...] = jnp.concatenate([o[g * qb:(g + 1) * qb] for g in range(grp)], axis=1).astype(o_ref.dtype)


def nsa_prompt_attention(qkv, gates_pre, ckv, kvb, *, batch):
    m = qkv.shape[0]
    l_seq = m // batch
    nq = l_seq // NSA_QBLOCK
    n_c = ckv.shape[1]
    qw = NSA_GROUP * NSA_DH
    pair = NSA_KVW // LANES
    kv_spec = lambda part: pl.BlockSpec((l_seq, LANES), lambda b, h, j: (b, part * pair + h // 2))
    return pl.pallas_call(
        _nsa_prompt_kernel,
        grid=(batch, NSA_KVH, nq),
        in_specs=[pl.BlockSpec((NSA_QBLOCK, qw), lambda b, h, j: (b * nq + j, h)),
                  pl.BlockSpec((NSA_QBLOCK, LANES), lambda b, h, j: (b * nq + j, h)),
                  pl.BlockSpec((1, n_c, LANES), lambda b, h, j: (b, 0, h // 2)),
                  pl.BlockSpec((1, n_c, LANES), lambda b, h, j: (b, 0, pair + h // 2)),
                  kv_spec(0), kv_spec(1), kv_spec(2), kv_spec(3)],
        out_specs=pl.BlockSpec((NSA_QBLOCK, qw), lambda b, h, j: (b * nq + j, h)),
        out_shape=jax.ShapeDtypeStruct((m, NSA_HEADS * NSA_DH), BF16),
        compiler_params=_params("parallel", "parallel", "arbitrary"),
        name="nsa_prompt_attention",
    )(qkv, gates_pre, ckv, ckv, kvb, kvb, kvb, kvb)


def _paged_compress_kernel(pt_ref, *refs, n_pages, page):
    del pt_ref
    pages = refs[:n_pages]
    pos_ref, w1_ref, w2_ref, even_ref, odd_ref = refs[n_pages:n_pages + 5]
    rows = refs[n_pages + 5:]
    for p in range(n_pages):
        for i, r in enumerate(rows):
            r[p * page:(p + 1) * page, :] = pages[p][0, :, i * LANES:(i + 1) * LANES]
    n_blk = n_pages * page // CMP_BLOCK
    full = _compress_rows(rows, pos_ref, w1_ref, w2_ref, n_blk)
    even_ref[0] = full[:n_blk // 2]
    odd_ref[0] = full[n_blk // 2:]


def _page_specs(n_pages, page):
    return [pl.BlockSpec((1, page, NSA_ROW), lambda b, pt, p=p: (pt[b, p], 0, 0)) for p in range(n_pages)]


def nsa_compress_paged(cache, page_table, pos4, w1, w2):
    n_seq, n_pages = page_table.shape
    page = cache.shape[1]
    half = n_pages * page // CMP_BLOCK // 2
    const = lambda shape: pl.BlockSpec(shape, lambda b, pt: (0,) * len(shape))
    out = jax.ShapeDtypeStruct((n_seq, half, NSA_ROW), F32)
    return pl.pallas_call(
        functools.partial(_paged_compress_kernel, n_pages=n_pages, page=page),
        grid_spec=pltpu.PrefetchScalarGridSpec(
            num_scalar_prefetch=1,
            grid=(n_seq,),
            in_specs=_page_specs(n_pages, page)
            + [const((2, CMP_BLOCK, NSA_KVW)), const((2, CMP_BLOCK * NSA_DH, CMP_HID)), const((2, CMP_HID, NSA_DH))],
            out_specs=[pl.BlockSpec((1, half, NSA_ROW), lambda b, pt: (b, 0, 0))] * 2,
            scratch_shapes=[pltpu.VMEM((n_pages * page, LANES), F32)] * (NSA_ROW // LANES)),
        out_shape=[out, out],
        compiler_params=_params("arbitrary"),
        name="nsa_compress_paged",
    )(page_table, *([cache] * n_pages), pos4, w1, w2)


def _softmax_with_new_key(s, ok, s_new, new_ok):
    s = jnp.where(ok, s, MASK_NEG)
    s_new = jnp.where(new_ok, s_new, MASK_NEG)
    m = jnp.maximum(jnp.max(s, axis=-1, keepdims=True), s_new)
    e = jnp.where(ok, jnp.exp(s - m), 0.0)
    e_new = jnp.where(new_ok, jnp.exp(s_new - m), 0.0)
    total = jnp.sum(e, axis=-1, keepdims=True) + e_new
    return e, e_new, 1.0 / jnp.maximum(total, 1e-30)


def _nsa_step_kernel(pt_ref, *refs, n_pages, page, past):
    del pt_ref
    pages = refs[:n_pages]
    (q_ref, gp_ref, cke_ref, cko_ref, kvs_new_ref, swa_ref, kvw_new_ref,
     o_ref, swa_out_ref, kb_ref, vb_ref) = refs[n_pages:]
    nh, kw = NSA_HEADS, NSA_KVW
    t = past
    own = (lax.broadcasted_iota(jnp.int32, (nh, kw), 1) // NSA_DH
           == lax.broadcasted_iota(jnp.int32, (nh, kw), 0) // NSA_GROUP)
    q = q_ref[0] * (NSA_DH ** -0.5)
    qp = jnp.where(own, jnp.concatenate([q] * NSA_KVH, axis=1), 0.0)
    qpb = qp.astype(BF16)

    half = cke_ref.shape[1]
    n_c = 2 * half
    ce, co = cke_ref[0], cko_ref[0]
    ck = jnp.concatenate([ce[:, :kw], co[:, :kw]], axis=0).astype(BF16)
    cv = jnp.concatenate([ce[:, kw:], co[:, kw:]], axis=0).astype(BF16)
    nn = lax.broadcasted_iota(jnp.int32, (1, n_c), 1)
    c_blk = jnp.where(nn < half, 2 * nn, 2 * (nn - half) + 1)
    p_c = _masked_softmax(_dot_nt(qpb, ck), (c_blk * CMP_BLOCK + CMP_BLOCK - 1) <= t)
    o_c = jnp.dot(p_c.astype(BF16), cv, preferred_element_type=F32)
    same_group = (lax.broadcasted_iota(jnp.int32, (nh, nh), 0) // NSA_GROUP
                  == lax.broadcasted_iota(jnp.int32, (nh, nh), 1) // NSA_GROUP).astype(F32)
    imp = lax.dot_general(p_c, same_group, (((0,), (0,)), ((), ())), preferred_element_type=F32, precision=HIGHEST)
    imp = imp[:half] + imp[half:]
    imp = jnp.concatenate([imp, jnp.zeros((LANES - half, nh), F32)], axis=0)
    blk = lax.broadcasted_iota(jnp.int32, (LANES, nh), 0)
    cur = t // SLC_BLOCK
    forced = (blk == 0) | (blk == cur) | (blk == cur - 1)
    sel_t = _top_blocks(jnp.where(forced, SEL_BIG, jnp.where(blk <= cur, imp, -SEL_BIG)), SLC_TOPK, 0)

    for p in range(n_pages):
        kb_ref[p * page:(p + 1) * page, :] = pages[p][0, :, :kw].astype(BF16)
        vb_ref[p * page:(p + 1) * page, :] = pages[p][0, :, kw:].astype(BF16)
    n_keys = n_pages * page
    pos = lax.broadcasted_iota(jnp.int32, (LANES, n_keys + LANES), 1)
    expand = (lax.broadcasted_iota(jnp.int32, (LANES, n_keys + LANES), 0)
              == jnp.minimum(pos, t) // SLC_BLOCK).astype(BF16)
    chosen = _dot_tn(sel_t.astype(BF16), expand) > 0.5
    ok = chosen[:, :n_keys] & (lax.broadcasted_iota(jnp.int32, (1, n_keys), 1) <= t)
    kv_new = kvs_new_ref[0]
    s_new = jnp.sum(qp * kv_new[:, :kw], axis=-1, keepdims=True)
    e, e_new, inv = _softmax_with_new_key(_dot_nt(qpb, kb_ref[...]), ok, s_new, chosen[:, n_keys:n_keys + 1])
    o_s = (jnp.dot(e.astype(BF16), vb_ref[...], preferred_element_type=F32) + e_new * kv_new[:, kw:]) * inv

    n_w = swa_ref.shape[1]
    swa = swa_ref[0]
    dist = t - (t - n_w + lax.broadcasted_iota(jnp.int32, (1, n_w), 1))
    w_ok = (dist >= 0) & (dist <= WINDOW) & (t - dist >= 0)
    kvw_new = kvw_new_ref[0]
    s_wn = jnp.sum(qp * kvw_new[:, :kw], axis=-1, keepdims=True)
    new_ok = lax.broadcasted_iota(jnp.int32, (nh, 1), 0) >= 0
    e, e_new, inv = _softmax_with_new_key(_dot_nt(qpb, swa[:, :kw].astype(BF16)), w_ok, s_wn, new_ok)
    o_w = (jnp.dot(e.astype(BF16), swa[:, kw:].astype(BF16), preferred_element_type=F32) + e_new * kvw_new[:, kw:]) * inv

    g = _sigmoid(gp_ref[0])
    o = jnp.where(own, g[:, 0:1] * o_c + g[:, 1:2] * o_s + g[:, 2:3] * o_w, 0.0)
    o_h = o[:, :NSA_DH]
    for h in range(1, NSA_KVH):
        o_h = o_h + o[:, h * NSA_DH:(h + 1) * NSA_DH]
    o_ref[0] = o_h
    swa_out_ref[0, 0:n_w - 1, :] = swa_ref[0, 1:n_w, :]
    swa_out_ref[0, n_w - 1:n_w, :] = kvw_new


def nsa_step_attention(q3, gates3, cke, cko, cache_slc, page_table, kvs_new, swa, kvw_new, *, past):
    n_seq, n_pages = page_table.shape
    page = cache_slc.shape[1]
    n_w = swa.shape[1]
    seq3 = lambda a: pl.BlockSpec((1,) + a.shape[1:], lambda b, pt: (b, 0, 0))
    return pl.pallas_call(
        functools.partial(_nsa_step_kernel, n_pages=n_pages, page=page, past=past),
        grid_spec=pltpu.PrefetchScalarGridSpec(
            num_scalar_prefetch=1,
            grid=(n_seq,),
            in_specs=_page_specs(n_pages, page) + [seq3(a) for a in (q3, gates3, cke, cko, kvs_new, swa, kvw_new)],
            out_specs=[seq3(q3), seq3(swa)],
            scratch_shapes=[pltpu.VMEM((n_pages * page, NSA_KVW), BF16)] * 2),
        out_shape=[jax.ShapeDtypeStruct(q3.shape, F32), jax.ShapeDtypeStruct(swa.shape, F32)],
        compiler_params=_params("arbitrary"),
        name="nsa_step_attention",
    )(page_table, *([cache_slc] * n_pages), q3, gates3, cke, cko, kvs_new, swa, kvw_new)


PROMPT_TM = 1024
FFN_TM = 512
HG_TL = 512
RG_TL = 256
FFN_TF = 1408
CMP_TILE_BLOCKS = 64
MOE_TB = 1024
MOE_TS = 256


def _ffn_layer(x, w_gu, w_down, g, b, *, tm):
    return ffn_residual_ln(x, w_gu.astype(BF16), w_down.astype(BF16), g, b, tm=tm, tf=FFN_TF)


def _moe_layer(x, w_router, w_gu, w_down, g, b):
    m = x.shape[0]
    tb = min(MOE_TB, m)
    gates, slots = moe_route(x, w_router.T, tb=tb)
    n_e = gates.shape[0]
    counts = (jnp.max(slots.reshape(n_e, m // tb, tb), axis=-1) + 1).T.reshape(-1)
    w_gu_t = jnp.swapaxes(w_gu, 1, 2).astype(BF16)
    w_down_t = jnp.swapaxes(w_down, 1, 2).astype(BF16)
    return moe_residual_ln(x, gates, slots, counts, w_gu_t, w_down_t, g, b, tb=tb, ts=min(MOE_TS, tb), tf=FFN_TF)


def kernel(x_prompt, x_sample, state_hgrn, state_rglru_conv, state_rglru_h, cache_nsa_cmp, cache_nsa_slc, cache_nsa_swa, page_table, ln_g, ln_b, hgrn_w_in, hgrn_lb, hgrn_norm_g, hgrn_w_out, rg_w_in, rg_conv_w, rg_conv_b, rg_w_gate, rg_b_gate, rg_lambda, rg_w_out, nsa_w_in, nsa_cmp_pos, nsa_cmp_w1, nsa_cmp_w2, nsa_w_out, ffn_w_gu, ffn_w_down, moe_router, moe_w_gu, moe_w_down):
    bp, l_seq, d = x_prompt.shape
    n_seq = x_sample.shape[0]
    past = page_table.shape[1] * cache_nsa_cmp.shape[2]
    xp = x_prompt.reshape(bp * l_seq, d)
    xs = x_sample.reshape(n_seq, d)
    tm_p, tm_s = PROMPT_TM, n_seq
    kv_shape = (2, NSA_KVH, NSA_DH)
    hg_p, hg_s, rc_p, rc_s, rh_p, rh_s = [], [], [], [], [], []
    cm_p, cm_s, sl_p, sl_s, sw_p, sw_s = [], [], [], [], [], []
    for layer in range(DEPTH):
        kind, slot = layer % 3, layer // 3
        g0, b0 = ln_g[layer, 0], ln_b[layer, 0]
        if kind == 0:
            w_in, w_out = hgrn_w_in[slot].astype(BF16), hgrn_w_out[slot].astype(BF16)
            op, st_p = hgrn_prompt(matmul(xp, w_in, tm=tm_p), hgrn_lb, hgrn_norm_g[slot],
                                   layer=layer, batch=bp, tl=HG_TL)
            os_, st_s = hgrn_step(matmul(xs, w_in, tm=tm_s), state_hgrn[slot], hgrn_lb, hgrn_norm_g[slot],
                                  layer=layer, sb=8)
            hg_p.append(st_p)
            hg_s.append(st_s)
        elif kind == 1:
            w_in, w_out = rg_w_in[slot].astype(BF16), rg_w_out[slot].astype(BF16)
            rg_w = (rg_conv_w[slot], rg_conv_b[slot], rg_w_gate[slot].astype(BF16), rg_b_gate[slot], rg_lambda[slot])
            op, cb_p, h_p = rglru_prompt(matmul(xp, w_in, tm=tm_p), *rg_w, batch=bp, tl=RG_TL)
            os_, cb_s, h_s = rglru_step(matmul(xs, w_in, tm=tm_s), state_rglru_conv[slot], state_rglru_h[slot], *rg_w)
            rc_p.append(cb_p)
            rc_s.append(cb_s)
            rh_p.append(h_p[:, 0])
            rh_s.append(h_s)
        else:
            w_qkv, w_kvb, w_g, pos4, w1, w2 = nsa_prepare_weights(
                nsa_w_in[slot], nsa_cmp_pos[slot], nsa_cmp_w1[slot], nsa_cmp_w2[slot])
            w_out = nsa_w_out[slot].astype(BF16)
            hq = NSA_HEADS * NSA_DH
            qkv = matmul(xp, w_qkv, tm=tm_p)
            kvb = matmul(xp, w_kvb, tm=tm_p, out_dtype=BF16)
            gates_pre = matmul(xp, w_g, tm=tm_p)
            cke, cko = nsa_compress(qkv, hq // NSA_ROW, pos4, w1, w2, batch=bp, n_blk=CMP_TILE_BLOCKS)
            op = nsa_prompt_attention(qkv, gates_pre, jnp.concatenate([cke, cko], axis=1), kvb, batch=bp)
            kv = qkv[:, hq:].reshape(bp, l_seq, 3, *kv_shape)
            cm_p.append(kv[:, :, 0])
            sl_p.append(kv[:, :, 1])
            sw_p.append(kv[:, l_seq - min(WINDOW, l_seq):, 2])
            qkv = matmul(xs, w_qkv, tm=tm_s)
            gates_pre = matmul(xs, w_g, tm=tm_s)
            rows = lambda c: c.reshape(c.shape[0], c.shape[1], NSA_ROW)
            cke, cko = nsa_compress_paged(rows(cache_nsa_cmp[slot]), page_table, pos4, w1, w2)
            gates3 = gates_pre.reshape(n_seq, NSA_KVH, LANES)[:, :, :3 * NSA_GROUP].reshape(n_seq, NSA_HEADS, 3)
            kv = qkv[:, hq:].reshape(n_seq, 3, 1, NSA_ROW)
            o3, swa_new = nsa_step_attention(qkv[:, :hq].reshape(n_seq, NSA_HEADS, NSA_DH), gates3, cke, cko,
                                             rows(cache_nsa_slc[slot]), page_table, kv[:, 1], rows(cache_nsa_swa[slot]),
                                             kv[:, 2], past=past)
            os_ = o3.reshape(n_seq, hq)
            cm_s.append(kv[:, 0].reshape(n_seq, 1, *kv_shape))
            sl_s.append(kv[:, 1].reshape(n_seq, 1, *kv_shape))
            sw_s.append(swa_new.reshape(n_seq, -1, *kv_shape))
        xp = matmul_residual_ln(op, w_out, xp, g0, b0, tm=tm_p)
        xs = matmul_residual_ln(os_, w_out, xs, g0, b0, tm=tm_s)
        g1, b1 = ln_g[layer, 1], ln_b[layer, 1]
        if layer % 2 == 0:
            fw = (ffn_w_gu[layer // 2], ffn_w_down[layer // 2])
            xp = _ffn_layer(xp, *fw, g1, b1, tm=FFN_TM)
            xs = _ffn_layer(xs, *fw, g1, b1, tm=tm_s)
        else:
            fw = (moe_router[layer // 2], moe_w_gu[layer // 2], moe_w_down[layer // 2])
            xp = _moe_layer(xp, *fw, g1, b1)
            xs = _moe_layer(xs, *fw, g1, b1)
    return (xp.reshape(bp, l_seq, d), xs.reshape(n_seq, 1, d),
            jnp.stack(hg_p), jnp.stack(hg_s), jnp.stack(rc_p), jnp.stack(rc_s), jnp.stack(rh_p), jnp.stack(rh_s),
            jnp.stack(cm_p), jnp.stack(cm_s), jnp.stack(sl_p), jnp.stack(sl_s), jnp.stack(sw_p), jnp.stack(sw_s))
```

```python
import functools

import jax
import jax.numpy as jnp
from jax import lax
from jax.experimental import pallas as pl
from jax.experimental.pallas import tpu as pltpu

F32 = jnp.float32
BF16 = jnp.bfloat16
HIGHEST = lax.Precision.HIGHEST

D_MODEL = 1024
DEPTH = 4
ALPHA = (2 * DEPTH) ** 0.25
LN_EPS = 1e-5
RMS_EPS = 1e-6
HG_HEADS = 8
HG_DK = 128
HG_CHUNK = 64
HG_SUB = 16
SUBLANES = 8
HG_HEADS_PER_STEP = 8
LB_FLOOR = 1e-30
RG_WIDTH = 1280
RG_BLOCKS = 10
RG_BS = 128
RG_CONV = 4
RG_C = 8.0
D_FF = 2816
N_EXPERTS = 8

VMEM_LIMIT_BYTES = 56 * 1024 * 1024


def _params(*sem):
    return pltpu.CompilerParams(dimension_semantics=sem, vmem_limit_bytes=VMEM_LIMIT_BYTES)


def _layer_norm(v, g, b):
    mu = jnp.mean(v, axis=-1, keepdims=True)
    vc = v - mu
    var = jnp.mean(vc * vc, axis=-1, keepdims=True)
    return vc * lax.rsqrt(var + LN_EPS) * g + b


def _sigmoid(x):
    return 1.0 / (1.0 + jnp.exp(-x))


def _silu(x):
    return x * _sigmoid(x)


def _mm_kernel(x_ref, w_ref, o_ref, xb_ref):
    @pl.when(pl.program_id(1) == 0)
    def _():
        xb_ref[...] = x_ref[...].astype(BF16)

    o_ref[...] = jnp.dot(xb_ref[...], w_ref[...], preferred_element_type=F32).astype(o_ref.dtype)


MAX_COL_TILE = 1280


def _col_tile(n):
    return max(t for t in range(128, min(n, MAX_COL_TILE) + 1, 128) if n % t == 0)


def matmul(x, w, *, tm, out_dtype=F32):
    m, k = x.shape
    n = w.shape[1]
    tn = _col_tile(n)
    return pl.pallas_call(
        _mm_kernel,
        grid=(m // tm, n // tn),
        in_specs=[pl.BlockSpec((tm, k), lambda i, j: (i, 0)),
                  pl.BlockSpec((k, tn), lambda i, j: (0, j))],
        out_specs=pl.BlockSpec((tm, tn), lambda i, j: (i, j)),
        out_shape=jax.ShapeDtypeStruct((m, n), out_dtype),
        scratch_shapes=[pltpu.VMEM((tm, k), BF16)],
        compiler_params=_params("parallel", "arbitrary"),
        name="matmul",
    )(x, w)


def _mm_t_kernel(x_ref, wt_ref, *o_refs):
    xb = x_ref[...].astype(BF16)
    n = o_refs[0].shape[1]
    for r, o_ref in enumerate(o_refs):
        o_ref[0] = _dot_nt(wt_ref[r * n:(r + 1) * n, :], xb)


def matmul_t(x, w_t, *, batch, n_out, tm):
    m, k = x.shape
    n = w_t.shape[0] // n_out
    l_seq = m // batch
    n_t = l_seq // tm
    return pl.pallas_call(
        _mm_t_kernel,
        grid=(batch, n_t),
        in_specs=[pl.BlockSpec((tm, k), lambda b, i: (b * n_t + i, 0)),
                  pl.BlockSpec((n_out * n, k), lambda b, i: (0, 0))],
        out_specs=[pl.BlockSpec((1, n, tm), lambda b, i: (b, 0, i))] * n_out,
        out_shape=[jax.ShapeDtypeStruct((batch, n, l_seq), F32)] * n_out,
        compiler_params=_params("parallel", "parallel"),
        name="matmul_t",
    )(x, w_t)


def _mm_res_ln_kernel(a_ref, w_ref, r_ref, g_ref, b_ref, o_ref):
    m = jnp.dot(a_ref[...].astype(BF16), w_ref[...], preferred_element_type=F32)
    o_ref[...] = _layer_norm(ALPHA * r_ref[...] + m, g_ref[...], b_ref[...])


def matmul_residual_ln(a, w, res, g, b, *, tm):
    m, k = a.shape
    d = w.shape[1]
    return pl.pallas_call(
        _mm_res_ln_kernel,
        grid=(m // tm,),
        in_specs=[pl.BlockSpec((tm, k), lambda i: (i, 0)),
                  pl.BlockSpec((k, d), lambda i: (0, 0)),
                  pl.BlockSpec((tm, d), lambda i: (i, 0)),
                  pl.BlockSpec((1, d), lambda i: (0, 0)),
                  pl.BlockSpec((1, d), lambda i: (0, 0))],
        out_specs=pl.BlockSpec((tm, d), lambda i: (i, 0)),
        out_shape=jax.ShapeDtypeStruct((m, d), F32),
        compiler_params=_params("parallel"),
        name="matmul_residual_ln",
    )(a, w, res, g.reshape(1, d), b.reshape(1, d))


def _ffn_kernel(x_ref, wg_ref, wu_ref, wd_ref, g_ref, b_ref, o_ref, xb_ref, acc_ref, *, n_f):
    f = pl.program_id(1)

    @pl.when(f == 0)
    def _():
        xb_ref[...] = x_ref[...].astype(BF16)

    xb = xb_ref[...]
    h = jnp.dot(xb, wg_ref[...], preferred_element_type=F32)
    u = jnp.dot(xb, wu_ref[...], preferred_element_type=F32)
    part = jnp.dot((_silu(h) * u).astype(BF16), wd_ref[...], preferred_element_type=F32)

    @pl.when(f == 0)
    def _():
        acc_ref[...] = part

    @pl.when(f > 0)
    def _():
        acc_ref[...] += part

    @pl.when(f == n_f - 1)
    def _():
        o_ref[...] = _layer_norm(ALPHA * x_ref[...] + acc_ref[...], g_ref[...], b_ref[...])


def ffn_residual_ln(x, w_gu, w_down, g, b, *, tm, tf):
    m, d = x.shape
    ff = w_down.shape[0]
    n_f = ff // tf
    return pl.pallas_call(
        functools.partial(_ffn_kernel, n_f=n_f),
        grid=(m // tm, n_f),
        in_specs=[pl.BlockSpec((tm, d), lambda i, f: (i, 0)),
                  pl.BlockSpec((d, tf), lambda i, f: (0, f)),
                  pl.BlockSpec((d, tf), lambda i, f: (0, n_f + f)),
                  pl.BlockSpec((tf, d), lambda i, f: (f, 0)),
                  pl.BlockSpec((1, d), lambda i, f: (0, 0)),
                  pl.BlockSpec((1, d), lambda i, f: (0, 0))],
        out_specs=pl.BlockSpec((tm, d), lambda i, f: (i, 0)),
        out_shape=jax.ShapeDtypeStruct((m, d), F32),
        scratch_shapes=[pltpu.VMEM((tm, d), BF16), pltpu.VMEM((tm, d), F32)],
        compiler_params=_params("parallel", "arbitrary"),
        name="ffn_residual_ln",
    )(x, w_gu, w_gu, w_down, g.reshape(1, d), b.reshape(1, d))


def _router_kernel(x_ref, wt_ref, gate_ref, pos_ref, *, n_tokens):
    logits = lax.dot_general(wt_ref[...], x_ref[...], (((1,), (1,)), ((), ())),
                             preferred_element_type=F32, precision=HIGHEST)
    n_e, tm = logits.shape
    expert = lax.broadcasted_iota(jnp.int32, logits.shape, 0)
    m1 = jnp.max(logits, axis=0, keepdims=True)
    i1 = jnp.min(jnp.where(logits == m1, expert, n_e), axis=0, keepdims=True)
    rest = jnp.where(expert == i1, -jnp.inf, logits)
    m2 = jnp.max(rest, axis=0, keepdims=True)
    i2 = jnp.min(jnp.where(rest == m2, expert, n_e), axis=0, keepdims=True)
    e2 = jnp.exp(m2 - m1)
    denom = 1.0 + e2
    first = expert == i1
    in_range = pl.program_id(0) * tm + lax.broadcasted_iota(jnp.int32, logits.shape, 1) < n_tokens
    routed = (first | (expert == i2)) & in_range
    gate_ref[...] = jnp.where(first & in_range, 1.0 / denom, jnp.where(routed, e2 / denom, 0.0))
    upper = (lax.broadcasted_iota(jnp.int32, (tm, tm), 0) <= lax.broadcasted_iota(jnp.int32, (tm, tm), 1)).astype(BF16)
    upto = jnp.dot(routed.astype(BF16), upper, preferred_element_type=F32)
    pos_ref[...] = jnp.where(routed, upto - 1.0, -1.0).astype(jnp.int32)


def moe_route(x, w_router_t, *, tb):
    m, d = x.shape
    n_e = w_router_t.shape[0]
    n_blk = pl.cdiv(m, tb)
    return pl.pallas_call(
        functools.partial(_router_kernel, n_tokens=m),
        grid=(n_blk,),
        in_specs=[pl.BlockSpec((tb, d), lambda i: (i, 0)),
                  pl.BlockSpec((n_e, d), lambda i: (0, 0))],
        out_specs=[pl.BlockSpec((n_e, tb), lambda i: (0, i))] * 2,
        out_shape=[jax.ShapeDtypeStruct((n_e, n_blk * tb), F32), jax.ShapeDtypeStruct((n_e, n_blk * tb), jnp.int32)],
        compiler_params=_params("parallel"),
        name="moe_route",
    )(x, w_router_t)


def _moe_kernel(cnt_ref, x_ref, gate_ref, slot_ref, wg_ref, wu_ref, wd_ref, g_ref, b_ref, o_ref,
                xt_ref, xc_ref, acc_ref, yt_ref, *, n_e, n_f, ts, n_tokens):
    i, e, f = pl.program_id(0), pl.program_id(1), pl.program_id(2)
    tb = x_ref.shape[0]
    n_sub = (cnt_ref[i * n_e + e] + ts - 1) // ts
    row = lax.broadcasted_iota(jnp.int32, (ts, tb), 0)

    def one_hot(sub):
        return (slot_ref[pl.ds(e, 1), :] == row + sub * ts).astype(BF16)

    @pl.when((e == 0) & (f == 0))
    def _():
        in_range = i * tb + lax.broadcasted_iota(jnp.int32, (tb, 1), 0) < n_tokens
        xt_ref[...] = jnp.where(in_range, x_ref[...], 0.0).T.astype(BF16)
        yt_ref[...] = jnp.zeros_like(yt_ref)

    tf, d = wg_ref.shape[1], wd_ref.shape[1]
    halves = lambda n: (slice(0, n // 2), slice(n // 2, n))

    @pl.when(f == 0)
    def _():
        def gather(sub, c):
            pick = one_hot(sub)
            for rows in halves(d):
                xc_ref[sub, rows, :] = _dot_nt(xt_ref[rows, :], pick).astype(BF16)
            return c

        lax.fori_loop(0, n_sub, gather, 0)

    def expert_ffn(sub, c):
        xc = xc_ref[sub]
        act = []
        for rows in halves(tf):
            h = jnp.dot(wg_ref[0, rows, :], xc, preferred_element_type=F32)
            u = jnp.dot(wu_ref[0, rows, :], xc, preferred_element_type=F32)
            act.append((_silu(h) * u).astype(BF16))
        act = jnp.concatenate(act, axis=0)
        part = jnp.concatenate([jnp.dot(wd_ref[0, rows, :], act, preferred_element_type=F32) for rows in halves(d)], axis=0)

        @pl.when(f == 0)
        def _():
            acc_ref[sub] = part

        @pl.when(f > 0)
        def _():
            acc_ref[sub] += part

        return c

    lax.fori_loop(0, n_sub, expert_ffn, 0)

    @pl.when(f == n_f - 1)
    def _():
        def scatter(sub, c):
            pick = one_hot(sub)
            gate = gate_ref[pl.ds(e, 1), :]
            for rows in halves(d):
                back = jnp.dot(acc_ref[sub, rows, :].astype(BF16), pick, preferred_element_type=F32)
                yt_ref[rows, :] += gate * back
            return c

        lax.fori_loop(0, n_sub, scatter, 0)

    @pl.when((e == n_e - 1) & (f == n_f - 1))
    def _():
        o_ref[...] = _layer_norm(ALPHA * x_ref[...] + yt_ref[...].T, g_ref[...], b_ref[...])


def moe_residual_ln(x, gates, slots, counts, w_gu_t, w_down_t, g, b, *, tb, ts, tf):
    m, d = x.shape
    n_e, ff2, _ = w_gu_t.shape
    n_f = ff2 // 2 // tf
    n_sub = pl.cdiv(tb, ts)
    return pl.pallas_call(
        functools.partial(_moe_kernel, n_e=n_e, n_f=n_f, ts=ts, n_tokens=m),
        grid_spec=pltpu.PrefetchScalarGridSpec(
            num_scalar_prefetch=1,
            grid=(pl.cdiv(m, tb), n_e, n_f),
            in_specs=[pl.BlockSpec((tb, d), lambda i, e, f, c: (i, 0)),
                      pl.BlockSpec((n_e, tb), lambda i, e, f, c: (0, i)),
                      pl.BlockSpec((n_e, tb), lambda i, e, f, c: (0, i)),
                      pl.BlockSpec((1, tf, d), lambda i, e, f, c: (e, f, 0)),
                      pl.BlockSpec((1, tf, d), lambda i, e, f, c: (e, n_f + f, 0)),
                      pl.BlockSpec((1, d, tf), lambda i, e, f, c: (e, 0, f)),
                      pl.BlockSpec((1, d), lambda i, e, f, c: (0, 0)),
                      pl.BlockSpec((1, d), lambda i, e, f, c: (0, 0))],
            out_specs=pl.BlockSpec((tb, d), lambda i, e, f, c: (i, 0)),
            scratch_shapes=[pltpu.VMEM((d, tb), BF16), pltpu.VMEM((n_sub, d, ts), BF16),
                            pltpu.VMEM((n_sub, d, ts), F32), pltpu.VMEM((d, tb), F32)]),
        out_shape=jax.ShapeDtypeStruct((m, d), F32),
        compiler_params=_params("parallel", "arbitrary", "arbitrary"),
        name="moe_residual_ln",
    )(counts, x, gates, slots, w_gu_t, w_gu_t, w_down_t, g.reshape(1, d), b.reshape(1, d))


def _hgrn_lower_bound(lbp, layer):
    m = jnp.max(lbp, axis=0, keepdims=True)
    e = jnp.exp(lbp - m)
    p = e / jnp.sum(e, axis=0, keepdims=True)
    c = p[0:1]
    for i in range(1, layer + 1):
        c = c + p[i:i + 1]
    return c - p[0:1]


def _hgrn_gates(q, fx, lb):
    log_lb = jnp.log(jnp.maximum(lb, LB_FLOOR))
    log1m = jnp.log1p(-lb)
    log_sig = -(jnp.maximum(-fx, 0.0) + jnp.log1p(jnp.exp(-jnp.abs(fx))))
    b2 = log1m + log_sig
    log_f = jnp.maximum(log_lb, b2) + jnp.log1p(jnp.exp(-jnp.abs(log_lb - b2)))
    kk = (1.0 - lb) * _sigmoid(-fx) - (jnp.maximum(lb, LB_FLOOR) - lb)
    return _silu(q), log_f, kk


def _hgrn_out_norm(o, g, ng):
    o = o * lax.rsqrt(jnp.mean(o * o, axis=-1, keepdims=True) + RMS_EPS)
    return o * ng * _silu(g)


def _dot_nt(a, b):
    return lax.dot_general(a, b, (((1,), (1,)), ((), ())), preferred_element_type=F32)


def _dot_tn(a, b):
    return lax.dot_general(a, b, (((0,), (0,)), ((), ())), preferred_element_type=F32)


def _gla_kernel(q_ref, f_ref, i_ref, g_ref, lbp_ref, ng_ref, o_ref, s_out_ref, st_ref, *, layer, n_t, tl, hp):
    t = pl.program_id(2)
    c_len, sub = HG_CHUNK, HG_SUB
    n_sub = c_len // sub
    n_stack = sub * (n_sub * (n_sub - 1) // 2)

    @pl.when(t == 0)
    def _():
        st_ref[...] = jnp.zeros_like(st_ref)

    lb = _hgrn_lower_bound(lbp_ref[...], layer)
    ng = ng_ref[...]
    row = lax.broadcasted_iota(jnp.int32, (c_len, c_len), 0)
    col = lax.broadcasted_iota(jnp.int32, (c_len, c_len), 1)
    tri = (col <= row).astype(F32)
    srow = lax.broadcasted_iota(jnp.int32, (c_len, n_stack), 0) // sub
    scol = lax.broadcasted_iota(jnp.int32, (c_len, n_stack), 1)
    part = jnp.zeros((c_len, n_stack), jnp.int32)
    for p in range(1, n_sub):
        part = part + (scol >= sub * (p * (p - 1) // 2)).astype(jnp.int32)
    inter_mask = srow == part

    def one_head(hh, r):
        lanes = slice(hh * HG_DK, (hh + 1) * HG_DK)
        qs, log_f, kk = _hgrn_gates(q_ref[r, lanes], f_ref[r, lanes], lb[:, lanes])
        v = i_ref[r, lanes]
        cum = jnp.dot(tri, log_f, preferred_element_type=F32, precision=HIGHEST)
        last = cum[c_len - 1:c_len, :]
        st = st_ref[hh]
        o = _dot_nt((qs * jnp.exp(cum)).astype(BF16), st.astype(BF16))
        bounds = [cum[sub * a - 1:sub * a, :] for a in range(1, n_sub)]
        cq = jnp.concatenate([cum[0:sub]] + [jnp.broadcast_to(b, (sub, HG_DK)) for b in bounds], axis=0)
        q_rel = (qs * jnp.exp(cum - cq)).astype(BF16)
        k_st = jnp.concatenate([kk[0:sub * a] * jnp.exp(bounds[a - 1] - cum[0:sub * a]) for a in range(1, n_sub)], axis=0)
        v_st = jnp.concatenate([v[0:sub * a] for a in range(1, n_sub)], axis=0)
        att = jnp.where(inter_mask, _dot_nt(q_rel, k_st.astype(BF16)), 0.0)
        o = o + jnp.dot(att.astype(BF16), v_st.astype(BF16), preferred_element_type=F32)
        parts = []
        for a in range(n_sub):
            base = sub * a
            rest = o[base:base + sub]
            for lo in range(0, sub, SUBLANES):
                tail = slice(base + lo, base + sub)
                qa, ca = qs[tail], cum[tail]
                row_t = lo + lax.broadcasted_iota(jnp.int32, (sub - lo, 1), 0)
                for s in range(lo, lo + SUBLANES):
                    src = slice(base + s, base + s + 1)
                    w = jnp.sum(qa * (kk[src] * jnp.exp(ca - cum[src])), axis=-1, keepdims=True)
                    rest = rest + jnp.where(row_t >= s, w, 0.0) * v[src]
                parts.append(rest[:SUBLANES])
                rest = rest[SUBLANES:]
        o = jnp.concatenate(parts, axis=0)
        o_ref[r, lanes] = _hgrn_out_norm(o, g_ref[r, lanes], ng[:, lanes]).astype(o_ref.dtype)
        k_end = (kk * jnp.exp(last - cum)).astype(BF16)
        st_ref[hh] = st * jnp.exp(last) + _dot_tn(v.astype(BF16), k_end)

    def chunk(c, carry):
        r = pl.ds(pl.multiple_of(c * c_len, c_len), c_len)
        for hh in range(hp):
            one_head(hh, r)
        return carry

    lax.fori_loop(0, tl // c_len, chunk, 0)

    @pl.when(t == n_t - 1)
    def _():
        for hh in range(hp):
            s_out_ref[0, hh] = st_ref[hh].T


def hgrn_prompt(z, lb_param, norm_g, *, layer, batch, tl):
    m = z.shape[0]
    n_t = m // batch // tl
    h, dk = HG_HEADS, HG_DK
    hp = HG_HEADS_PER_STEP
    n_hg = h // hp
    zspec = lambda part: pl.BlockSpec((tl, hp * dk), lambda b, hh, t: (b * n_t + t, part * n_hg + hh))
    return pl.pallas_call(
        functools.partial(_gla_kernel, layer=layer, n_t=n_t, tl=tl, hp=hp),
        grid=(batch, n_hg, n_t),
        in_specs=[zspec(0), zspec(1), zspec(2), zspec(3),
                  pl.BlockSpec((DEPTH, hp * dk), lambda b, hh, t: (0, hh)),
                  pl.BlockSpec((1, hp * dk), lambda b, hh, t: (0, hh))],
        out_specs=[pl.BlockSpec((tl, hp * dk), lambda b, hh, t: (b * n_t + t, hh)),
                   pl.BlockSpec((1, hp, dk, dk), lambda b, hh, t: (b, hh, 0, 0))],
        out_shape=[jax.ShapeDtypeStruct((m, h * dk), BF16),
                   jax.ShapeDtypeStruct((batch, h, dk, dk), F32)],
        scratch_shapes=[pltpu.VMEM((hp, dk, dk), F32)],
        compiler_params=_params("parallel", "parallel", "arbitrary"),
        name="hgrn_prompt",
    )(z, z, z, z, lb_param, norm_g.reshape(1, h * dk))


def _hgrn_step_kernel(q_ref, f_ref, i_ref, g_ref, lbp_ref, ng_ref, s0_ref, o_ref, s_ref,
                      qt_ref, ft_ref, kt_ref, oacc_ref, *, layer, n_j, sb):
    j = pl.program_id(1)
    n_seq = q_ref.shape[0]

    def split3(x, ref):
        hi = x.astype(BF16)
        r1 = x - hi.astype(F32)
        mid = r1.astype(BF16)
        ref[0], ref[1], ref[2] = hi, mid, (r1 - mid.astype(F32)).astype(BF16)

    @pl.when(j == 0)
    def _():
        lb = _hgrn_lower_bound(lbp_ref[...], layer)
        qs, log_f, kk = _hgrn_gates(q_ref[...], f_ref[...], lb)
        split3(qs.T, qt_ref)
        split3(jnp.exp(log_f).T, ft_ref)
        split3(kk.T, kt_ref)

    seq = lax.broadcasted_iota(jnp.int32, (n_seq, HG_DK), 0)
    for i in range(sb):
        b = j * sb + i
        pick = (seq == b).astype(BF16)
        col = lambda ref: sum(jnp.dot(ref[p], pick, preferred_element_type=F32) for p in range(3))
        s_new = s0_ref[i, 0] * col(ft_ref) + col(kt_ref) * i_ref[pl.ds(b, 1), :]
        s_ref[i, 0] = s_new
        oacc_ref[pl.ds(b, 1), :] = jnp.sum(col(qt_ref) * s_new, axis=0, keepdims=True)

    @pl.when(j == n_j - 1)
    def _():
        o_ref[...] = _hgrn_out_norm(oacc_ref[...], g_ref[...], ng_ref[...]).astype(o_ref.dtype)


def hgrn_step(z, s0, lb_param, norm_g, *, layer, sb):
    n_seq = z.shape[0]
    h, dk = HG_HEADS, HG_DK
    n_j = n_seq // sb
    zspec = lambda off: pl.BlockSpec((n_seq, dk), lambda hh, j: (0, off + hh))
    return pl.pallas_call(
        functools.partial(_hgrn_step_kernel, layer=layer, n_j=n_j, sb=sb),
        grid=(h, n_j),
        in_specs=[zspec(0), zspec(h), zspec(2 * h), zspec(3 * h),
                  pl.BlockSpec((DEPTH, dk), lambda hh, j: (0, hh)),
                  pl.BlockSpec((1, dk), lambda hh, j: (0, hh)),
                  pl.BlockSpec((sb, 1, dk, dk), lambda hh, j: (j, hh, 0, 0))],
        out_specs=[pl.BlockSpec((n_seq, dk), lambda hh, j: (0, hh)),
                   pl.BlockSpec((sb, 1, dk, dk), lambda hh, j: (j, hh, 0, 0))],
        out_shape=[jax.ShapeDtypeStruct((n_seq, h * dk), BF16),
                   jax.ShapeDtypeStruct(s0.shape, F32)],
        scratch_shapes=[pltpu.VMEM((3, dk, n_seq), BF16)] * 3 + [pltpu.VMEM((n_seq, dk), F32)],
        compiler_params=_params("parallel", "arbitrary"),
        name="hgrn_step",
    )(z, z, z, z, lb_param, norm_g.reshape(1, h * dk), s0)


def _gelu_tanh(x):
    return x * (0.5 * (1.0 + jnp.tanh(0.7978845608028654 * (x + 0.044715 * (x * x * x)))))


def _rg_decay_and_input(xc, wg_ref, bg_ref, lam_ref):
    xb = xc.astype(BF16)
    gates = []
    for gi in range(2):
        blocks = [jnp.dot(xb[:, n * RG_BS:(n + 1) * RG_BS], wg_ref[gi, n], preferred_element_type=F32)
                  for n in range(RG_BLOCKS)]
        gates.append(jnp.concatenate(blocks, axis=1) + bg_ref[gi:gi + 1, :])
    r = _sigmoid(gates[0])
    i = _sigmoid(gates[1])
    neg_lam = -lam_ref[...]
    softplus = jnp.maximum(neg_lam, 0.0) + jnp.log1p(jnp.exp(-jnp.abs(neg_lam)))
    log_a = -RG_C * r * softplus
    one_minus_a2 = -jnp.tanh(log_a) * (jnp.exp(2.0 * log_a) + 1.0)
    return jnp.exp(log_a), jnp.sqrt(one_minus_a2) * i * xc


def _rg_prompt_kernel(y_ref, x_ref, cw_ref, cb_ref, wg_ref, bg_ref, lam_ref,
                      o_ref, conv_out_ref, h_out_ref, xpad_ref, h_ref, *, n_t, tl):
    t = pl.program_id(1)
    halo = 8

    @pl.when(t == 0)
    def _():
        xpad_ref[0:halo, :] = jnp.zeros((halo, RG_WIDTH), F32)
        h_ref[...] = jnp.zeros_like(h_ref)

    x = x_ref[...]
    xpad_ref[halo:halo + tl, :] = x
    xc = cb_ref[...]
    for j in range(RG_CONV):
        off = halo - (RG_CONV - 1) + j
        xc = xc + xpad_ref[off:off + tl, :] * cw_ref[j:j + 1, :]
    xpad_ref[0:halo, :] = x[tl - halo:tl, :]
    a, u = _rg_decay_and_input(xc, wg_ref, bg_ref, lam_ref)
    row = lax.broadcasted_iota(jnp.int32, (tl, 1), 0)
    s = 1
    while s < tl:
        keep = row >= s
        a_prev = jnp.where(keep, pltpu.roll(a, s, 0), 1.0)
        u_prev = jnp.where(keep, pltpu.roll(u, s, 0), 0.0)
        u = a * u_prev + u
        a = a * a_prev
        s *= 2
    h = a * h_ref[...] + u
    h_ref[...] = h[tl - 1:tl, :]
    o_ref[...] = (_gelu_tanh(y_ref[...]) * h).astype(o_ref.dtype)

    @pl.when(t == n_t - 1)
    def _():
        conv_out_ref[0] = x[tl - (RG_CONV - 1):tl, :]
        h_out_ref[0] = h[tl - 1:tl, :]


def rglru_prompt(z, conv_w, conv_b, w_gate, b_gate, lam, *, batch, tl):
    m = z.shape[0]
    n_t = m // batch // tl
    w = RG_WIDTH
    const = lambda shape: pl.BlockSpec(shape, lambda b, t: (0,) * len(shape))
    return pl.pallas_call(
        functools.partial(_rg_prompt_kernel, n_t=n_t, tl=tl),
        grid=(batch, n_t),
        in_specs=[pl.BlockSpec((tl, w), lambda b, t: (b * n_t + t, 0)),
                  pl.BlockSpec((tl, w), lambda b, t: (b * n_t + t, 1)),
                  const((RG_CONV, w)), const((1, w)), const((2, RG_BLOCKS, RG_BS, RG_BS)), const((2, w)), const((1, w))],
        out_specs=[pl.BlockSpec((tl, w), lambda b, t: (b * n_t + t, 0)),
                   pl.BlockSpec((1, RG_CONV - 1, w), lambda b, t: (b, 0, 0)),
                   pl.BlockSpec((1, 1, w), lambda b, t: (b, 0, 0))],
        out_shape=[jax.ShapeDtypeStruct((m, w), BF16),
                   jax.ShapeDtypeStruct((batch, RG_CONV - 1, w), F32),
                   jax.ShapeDtypeStruct((batch, 1, w), F32)],
        scratch_shapes=[pltpu.VMEM((tl + 8, w), F32), pltpu.VMEM((1, w), F32)],
        compiler_params=_params("parallel", "arbitrary"),
        name="rglru_prompt",
    )(z, z, conv_w, conv_b.reshape(1, w), w_gate, b_gate, lam.reshape(1, w))


def _rg_step_kernel(y_ref, x_ref, buf_ref, h0_ref, cw_ref, cb_ref, wg_ref, bg_ref, lam_ref,
                    o_ref, conv_out_ref, h_out_ref):
    x = x_ref[...]
    xc = cb_ref[...]
    for j in range(RG_CONV - 1):
        xc = xc + buf_ref[:, j, :] * cw_ref[j:j + 1, :]
    xc = xc + x * cw_ref[RG_CONV - 1:RG_CONV, :]
    a, u = _rg_decay_and_input(xc, wg_ref, bg_ref, lam_ref)
    h = a * h0_ref[...] + u
    o_ref[...] = (_gelu_tanh(y_ref[...]) * h).astype(o_ref.dtype)
    h_out_ref[...] = h
    for j in range(RG_CONV - 2):
        conv_out_ref[:, j, :] = buf_ref[:, j + 1, :]
    conv_out_ref[:, RG_CONV - 2, :] = x


def rglru_step(z, conv_buf, h0, conv_w, conv_b, w_gate, b_gate, lam):
    n_seq = z.shape[0]
    w = RG_WIDTH
    const = lambda shape: pl.BlockSpec(shape, lambda i: (0,) * len(shape))
    return pl.pallas_call(
        _rg_step_kernel,
        grid=(1,),
        in_specs=[pl.BlockSpec((n_seq, w), lambda i: (0, 0)),
                  pl.BlockSpec((n_seq, w), lambda i: (0, 1)),
                  const((n_seq, RG_CONV - 1, w)), const((n_seq, w)),
                  const((RG_CONV, w)), const((1, w)), const((2, RG_BLOCKS, RG_BS, RG_BS)), const((2, w)), const((1, w))],
        out_specs=[const((n_seq, w)), const((n_seq, RG_CONV - 1, w)), const((n_seq, w))],
        out_shape=[jax.ShapeDtypeStruct((n_seq, w), BF16),
                   jax.ShapeDtypeStruct((n_seq, RG_CONV - 1, w), F32),
                   jax.ShapeDtypeStruct((n_seq, w), F32)],
        compiler_params=_params("arbitrary"),
        name="rglru_step",
    )(z, z, conv_buf, h0, conv_w, conv_b.reshape(1, w), w_gate, b_gate, lam.reshape(1, w))


NSA_HEADS = 16
NSA_KVH = 4
NSA_GROUP = 4
NSA_DH = 64
NSA_KVW = NSA_KVH * NSA_DH
NSA_ROW = 2 * NSA_KVW
CMP_BLOCK = 32
CMP_HID = 128
SLC_BLOCK = 64
SLC_TOPK = 16
WINDOW = 512
NSA_QBLOCK = 128
MASK_NEG = -1e30
SEL_BIG = 1e9
LANES = 128
SEL_KEYS = 1024
LOG2_E = 1.4426950408889634


def _masked_softmax(s, mask, exp=jnp.exp):
    s = jnp.where(mask, s, MASK_NEG)
    m = jnp.max(s, axis=-1, keepdims=True)
    e = jnp.where(mask, exp(s - m), 0.0)
    return e / jnp.maximum(jnp.sum(e, axis=-1, keepdims=True), 1e-30)


def _top_blocks(score, k, axis):
    pos = lax.broadcasted_iota(jnp.int32, score.shape, axis)
    n = score.shape[axis]

    def pick(_, carry):
        sc, sel = carry
        mx = jnp.max(sc, axis=axis, keepdims=True)
        idx = jnp.min(jnp.where(sc == mx, pos, n), axis=axis, keepdims=True)
        hit = pos == idx
        return jnp.where(hit, -jnp.inf, sc), jnp.where(hit, 1.0, sel)

    return lax.fori_loop(0, k, pick, (score, jnp.zeros(score.shape, F32)))[1]


def _compress_rows(x_refs, pos_ref, w1_ref, w2_ref, n_blk):
    half = n_blk // 2
    tiles = NSA_KVW // LANES
    low = lax.broadcasted_iota(jnp.int32, (n_blk, LANES), 1) < NSA_DH
    outs = []
    for c in range(2):
        cols = [[] for _ in range(NSA_KVH)]
        for l in range(0, CMP_BLOCK, 2):
            for tl in range(tiles):
                ref = x_refs[c * tiles + tl]

                def rows(r, ref=ref, tl=tl):
                    xe = ref[pl.ds(r, half, stride=2 * CMP_BLOCK), :]
                    xo = ref[pl.ds(r + CMP_BLOCK, half, stride=2 * CMP_BLOCK), :]
                    return jnp.concatenate([xe, xo], axis=0) + pos_ref[c, r:r + 1, tl * LANES:(tl + 1) * LANES]

                x0, x1 = rows(l), rows(l + 1)
                cols[2 * tl].append(jnp.where(low, x0, pltpu.roll(x1, NSA_DH, 1)))
                cols[2 * tl + 1].append(jnp.where(low, pltpu.roll(x0, NSA_DH, 1), x1))
        xs = jnp.concatenate([jnp.concatenate(cols[h], axis=1) for h in range(NSA_KVH)], axis=0)
        acc = jnp.dot(xs.astype(BF16), w1_ref[c], preferred_element_type=F32)
        out = jnp.dot(_silu(acc).astype(BF16), w2_ref[c], preferred_element_type=F32)
        outs.append(jnp.concatenate([out[h * n_blk:(h + 1) * n_blk] for h in range(NSA_KVH)], axis=1))
    return jnp.concatenate(outs, axis=1)


def _compress_kernel(x0_ref, x1_ref, x2_ref, x3_ref, pos_ref, w1_ref, w2_ref, even_ref, odd_ref, *, n_blk):
    full = _compress_rows((x0_ref, x1_ref, x2_ref, x3_ref), pos_ref, w1_ref, w2_ref, n_blk)
    even_ref[0] = full[:n_blk // 2]
    odd_ref[0] = full[n_blk // 2:]


def nsa_compress(kv, col_block, pos4, w1, w2, *, batch, n_blk):
    m = kv.shape[0]
    n_t = m // batch // (n_blk * CMP_BLOCK)
    half = n_blk // 2
    const = lambda shape: pl.BlockSpec(shape, lambda b, t: (0,) * len(shape))
    out = jax.ShapeDtypeStruct((batch, n_t * half, NSA_ROW), F32)
    tiles = NSA_ROW // LANES
    lane_tile = lambda i: pl.BlockSpec((n_blk * CMP_BLOCK, LANES), lambda b, t: (b * n_t + t, col_block * tiles + i))
    return pl.pallas_call(
        functools.partial(_compress_kernel, n_blk=n_blk),
        grid=(batch, n_t),
        in_specs=[lane_tile(i) for i in range(tiles)]
        + [const((2, CMP_BLOCK, NSA_KVW)), const((2, CMP_BLOCK * NSA_DH, CMP_HID)), const((2, CMP_HID, NSA_DH))],
        out_specs=[pl.BlockSpec((1, half, NSA_ROW), lambda b, t: (b, t, 0))] * 2,
        out_shape=[out, out],
        compiler_params=_params("parallel", "parallel"),
        name="nsa_compress",
    )(kv, kv, kv, kv, pos4, w1, w2)


def nsa_prepare_weights(w_in, pos, w1, w2):
    hq = NSA_HEADS * NSA_DH
    kv_end = hq + 3 * NSA_ROW
    w_qkv = w_in[:, :kv_end].astype(BF16)
    w_kvb = w_in[:, hq + NSA_ROW:kv_end].astype(BF16)
    w_g = w_in[:, kv_end:].reshape(-1, NSA_KVH, 3 * NSA_GROUP)
    w_g = jnp.pad(w_g, ((0, 0), (0, 0), (0, LANES - 3 * NSA_GROUP))).reshape(-1, NSA_KVH * LANES).astype(BF16)
    pos4 = jnp.tile(pos, (1, 1, NSA_KVH))
    w1 = w1.reshape(2, CMP_BLOCK * NSA_DH, CMP_HID)
    w_kv_t = w_in[:, hq:kv_end].T.astype(BF16)
    return w_qkv, w_kvb, w_g, w_kv_t, pos4, w1.astype(BF16), w2.astype(BF16)


def _stack_group_queries(q, par):
    half = lax.broadcasted_iota(jnp.int32, (q.shape[0], LANES), 1) // NSA_DH
    parts = []
    for g in range(NSA_GROUP):
        qg = q[:, g * NSA_DH:(g + 1) * NSA_DH]
        parts.append(jnp.where(half == par, jnp.concatenate([qg, qg], axis=1), 0.0))
    return jnp.concatenate(parts, axis=0).astype(BF16)


def _nsa_prompt_kernel(q_ref, gp_ref, ck_ref, cv_ref, ks_ref, vs_ref, kw_ref, vw_ref, o_ref):
    kvh = pl.program_id(1)
    j = pl.program_id(2)
    par = kvh % 2
    qb, grp = NSA_QBLOCK, NSA_GROUP
    rows = grp * qb
    qp = _stack_group_queries(q_ref[...] * (NSA_DH ** -0.5 * LOG2_E), par)
    t_pos = j * qb + lax.broadcasted_iota(jnp.int32, (qb, 1), 0)

    def grouped_softmax(s, mask):
        n = s.shape[-1]
        return _masked_softmax(s.reshape(grp, qb, n), mask[None], jnp.exp2).reshape(rows, n)

    n_c = ck_ref.shape[1]
    n_s = n_c // 2
    ck = ck_ref[0].astype(BF16)
    cv = cv_ref[0].astype(BF16)
    nn = lax.broadcasted_iota(jnp.int32, (1, n_c), 1)
    c_blk = jnp.where(nn < n_s, 2 * nn, 2 * (nn - n_s) + 1)
    c_vis = (c_blk * CMP_BLOCK + CMP_BLOCK - 1) <= t_pos
    p_c = grouped_softmax(_dot_nt(qp, ck), c_vis)
    o_c = jnp.dot(p_c.astype(BF16), cv, preferred_element_type=F32)
    imp = p_c[0:qb]
    for g in range(1, grp):
        imp = imp + p_c[g * qb:(g + 1) * qb]
    imp = imp[:, :n_s] + imp[:, n_s:]
    blk = lax.broadcasted_iota(jnp.int32, (qb, n_s), 1)
    cur = t_pos // SLC_BLOCK
    forced = (blk == 0) | (blk == cur) | (blk == cur - 1)
    score = jnp.where(forced, SEL_BIG, jnp.where(blk <= cur, imp, -SEL_BIG))
    sel_b = _top_blocks(score.T, SLC_TOPK, 0).T.astype(BF16)

    per = SEL_KEYS // SLC_BLOCK
    e_row = lax.broadcasted_iota(jnp.int32, (n_s, SEL_KEYS), 0)
    e_col = lax.broadcasted_iota(jnp.int32, (n_s, SEL_KEYS), 1) // SLC_BLOCK
    k_off = lax.broadcasted_iota(jnp.int32, (1, SEL_KEYS), 1)

    n_kb = (j * qb + qb - 1) // SEL_KEYS + 1

    def key_rows(kb):
        return pl.ds(pl.multiple_of(kb * SEL_KEYS, SEL_KEYS), SEL_KEYS)

    own_lanes = lax.broadcasted_iota(jnp.int32, (SEL_KEYS, LANES), 1) // NSA_DH == par

    def sel_step(kb, carry):
        m, acc = carry
        expand =(e_row == kb * per + e_col).astype(BF16)
        chosen = jnp.dot(sel_b, expand, preferred_element_type=F32) > 0.5
        ok = chosen & ((kb * SEL_KEYS + k_off) <= t_pos)
        bias = jnp.where(ok, 0.0, MASK_NEG)[None]
        s = _dot_nt(qp, ks_ref[key_rows(kb), :]).reshape(grp, qb, SEL_KEYS) + bias
        m_new = jnp.maximum(m, jnp.max(s, axis=-1, keepdims=True))
        alpha = jnp.exp2(m - m_new)
        e = jnp.exp2(s - m_new)
        v_one = jnp.where(own_lanes, vs_ref[key_rows(kb), :], 1.0)
        pv = jnp.dot(e.reshape(rows, SEL_KEYS).astype(BF16), v_one, preferred_element_type=F32)
        return m_new, alpha * acc + pv.reshape(grp, qb, LANES)

    init = (jnp.full((grp, qb, 1), MASK_NEG, F32), jnp.zeros((grp, qb, LANES), F32))
    acc = lax.fori_loop(0, n_kb, sel_step, init)[1].reshape(rows, LANES)
    o_s = acc / jnp.maximum(pltpu.roll(acc, NSA_DH, 1), 1e-30)

    n_w = WINDOW + qb
    w0 = pl.multiple_of(jnp.maximum(j * qb - WINDOW, 0), qb)
    dist = t_pos - (w0 + lax.broadcasted_iota(jnp.int32, (1, n_w), 1))
    w_bias = jnp.where((dist >= 0) & (dist <= WINDOW), 0.0, MASK_NEG)[None]
    s_w = _dot_nt(qp, kw_ref[pl.ds(w0, n_w), :]).reshape(grp, qb, n_w) + w_bias
    e_w = jnp.exp2(s_w - jnp.max(s_w, axis=-1, keepdims=True)).reshape(rows, n_w)
    own_w = lax.broadcasted_iota(jnp.int32, (n_w, LANES), 1) // NSA_DH == par
    o_w = jnp.dot(e_w.astype(BF16), jnp.where(own_w, vw_ref[pl.ds(w0, n_w), :], 1.0), preferred_element_type=F32)
    o_w = o_w / jnp.maximum(pltpu.roll(o_w, NSA_DH, 1), 1e-30)

    gates = _sigmoid(gp_ref[...])
    gate = lambda br: jnp.concatenate(
        [jnp.broadcast_to(gates[:, 3 * g + br:3 * g + br + 1], (qb, LANES)) for g in range(grp)], axis=0)
    o = gate(0) * o_c + gate(1) * o_s + gate(2) * o_w
    half = lax.broadcasted_iota(jnp.int32, (rows, LANES), 1) // NSA_DH
    o = jnp.where(half == par, o, 0.0)
    o = o[:, :NSA_DH] + o[:, NSA_DH:]
    o_ref[...] = jnp.concatenate([o[g * qb:(g + 1) * qb] for g in range(grp)], axis=1).astype(o_ref.dtype)


def nsa_prompt_attention(qkv, gates_pre, ckv, kvb, *, batch):
    m = qkv.shape[0]
    l_seq = m // batch
    nq = l_seq // NSA_QBLOCK
    n_c = ckv.shape[1]
    qw = NSA_GROUP * NSA_DH
    pair = NSA_KVW // LANES
    kv_spec = lambda part: pl.BlockSpec((l_seq, LANES), lambda b, h, j: (b, part * pair + h // 2))
    return pl.pallas_call(
        _nsa_prompt_kernel,
        grid=(batch, NSA_KVH, nq),
        in_specs=[pl.BlockSpec((NSA_QBLOCK, qw), lambda b, h, j: (b * nq + j, h)),
                  pl.BlockSpec((NSA_QBLOCK, LANES), lambda b, h, j: (b * nq + j, h)),
                  pl.BlockSpec((1, n_c, LANES), lambda b, h, j: (b, 0, h // 2)),
                  pl.BlockSpec((1, n_c, LANES), lambda b, h, j: (b, 0, pair + h // 2)),
                  kv_spec(0), kv_spec(1), kv_spec(2), kv_spec(3)],
        out_specs=pl.BlockSpec((NSA_QBLOCK, qw), lambda b, h, j: (b * nq + j, h)),
        out_shape=jax.ShapeDtypeStruct((m, NSA_HEADS * NSA_DH), BF16),
        compiler_params=_params("parallel", "parallel", "arbitrary"),
        name="nsa_prompt_attention",
    )(qkv, gates_pre, ckv, ckv, kvb, kvb, kvb, kvb)


def _paged_compress_kernel(pt_ref, *refs, n_pages, page):
    del pt_ref
    pages = refs[:n_pages]
    pos_ref, w1_ref, w2_ref, even_ref, odd_ref = refs[n_pages:n_pages + 5]
    rows = refs[n_pages + 5:]
    for p in range(n_pages):
        for i, r in enumerate(rows):
            r[p * page:(p + 1) * page, :] = pages[p][0, :, i * LANES:(i + 1) * LANES]
    n_blk = n_pages * page // CMP_BLOCK
    full = _compress_rows(rows, pos_ref, w1_ref, w2_ref, n_blk)
    even_ref[0] = full[:n_blk // 2]
    odd_ref[0] = full[n_blk // 2:]


def _page_specs(n_pages, page):
    return [pl.BlockSpec((1, page, NSA_ROW), lambda b, pt, p=p: (pt[b, p], 0, 0)) for p in range(n_pages)]


def nsa_compress_paged(cache, page_table, pos4, w1, w2):
    n_seq, n_pages = page_table.shape
    page = cache.shape[1]
    half = n_pages * page // CMP_BLOCK // 2
    const = lambda shape: pl.BlockSpec(shape, lambda b, pt: (0,) * len(shape))
    out = jax.ShapeDtypeStruct((n_seq, half, NSA_ROW), F32)
    return pl.pallas_call(
        functools.partial(_paged_compress_kernel, n_pages=n_pages, page=page),
        grid_spec=pltpu.PrefetchScalarGridSpec(
            num_scalar_prefetch=1,
            grid=(n_seq,),
            in_specs=_page_specs(n_pages, page)
            + [const((2, CMP_BLOCK, NSA_KVW)), const((2, CMP_BLOCK * NSA_DH, CMP_HID)), const((2, CMP_HID, NSA_DH))],
            out_specs=[pl.BlockSpec((1, half, NSA_ROW), lambda b, pt: (b, 0, 0))] * 2,
            scratch_shapes=[pltpu.VMEM((n_pages * page, LANES), F32)] * (NSA_ROW // LANES)),
        out_shape=[out, out],
        compiler_params=_params("arbitrary"),
        name="nsa_compress_paged",
    )(page_table, *([cache] * n_pages), pos4, w1, w2)


def _softmax_with_new_key(s, ok, s_new, new_ok):
    s = jnp.where(ok, s, MASK_NEG)
    s_new = jnp.where(new_ok, s_new, MASK_NEG)
    m = jnp.maximum(jnp.max(s, axis=-1, keepdims=True), s_new)
    e = jnp.where(ok, jnp.exp(s - m), 0.0)
    e_new = jnp.where(new_ok, jnp.exp(s_new - m), 0.0)
    total = jnp.sum(e, axis=-1, keepdims=True) + e_new
    return e, e_new, 1.0 / jnp.maximum(total, 1e-30)


def _nsa_step_kernel(pt_ref, *refs, n_pages, page, past):
    del pt_ref
    pages = refs[:n_pages]
    (q_ref, gp_ref, cke_ref, cko_ref, kvs_new_ref, swa_ref, kvw_new_ref,
     o_ref, swa_out_ref, kt_ref, vt_ref) = refs[n_pages:]
    nh, kw = NSA_HEADS, NSA_KVW
    t = past
    own = (lax.broadcasted_iota(jnp.int32, (nh, kw), 1) // NSA_DH
           == lax.broadcasted_iota(jnp.int32, (nh, kw), 0) // NSA_GROUP)
    q = q_ref[0] * (NSA_DH ** -0.5)
    qp = jnp.where(own, jnp.concatenate([q] * NSA_KVH, axis=1), 0.0)
    qpb = qp.astype(BF16)

    half = cke_ref.shape[1]
    n_c = 2 * half
    ce, co = cke_ref[0], cko_ref[0]
    ck = jnp.concatenate([ce[:, :kw], co[:, :kw]], axis=0).astype(BF16)
    cv = jnp.concatenate([ce[:, kw:], co[:, kw:]], axis=0).astype(BF16)
    nn = lax.broadcasted_iota(jnp.int32, (1, n_c), 1)
    c_blk = jnp.where(nn < half, 2 * nn, 2 * (nn - half) + 1)
    p_c = _masked_softmax(_dot_nt(qpb, ck), (c_blk * CMP_BLOCK + CMP_BLOCK - 1) <= t)
    o_c = jnp.dot(p_c.astype(BF16), cv, preferred_element_type=F32)
    same_group = (lax.broadcasted_iota(jnp.int32, (nh, nh), 0) // NSA_GROUP
                  == lax.broadcasted_iota(jnp.int32, (nh, nh), 1) // NSA_GROUP).astype(F32)
    imp = lax.dot_general(p_c, same_group, (((0,), (0,)), ((), ())), preferred_element_type=F32, precision=HIGHEST)
    imp = imp[:half] + imp[half:]
    imp = jnp.concatenate([imp, jnp.zeros((LANES - half, nh), F32)], axis=0)
    blk = lax.broadcasted_iota(jnp.int32, (LANES, nh), 0)
    cur = t // SLC_BLOCK
    forced = (blk == 0) | (blk == cur) | (blk == cur - 1)
    sel_t = _top_blocks(jnp.where(forced, SEL_BIG, jnp.where(blk <= cur, imp, -SEL_BIG)), SLC_TOPK, 0)

    for p in range(n_pages):
        kt_ref[:, p * page:(p + 1) * page] = pages[p][0, 0].astype(BF16)
        vt_ref[:, p * page:(p + 1) * page] = pages[p][0, 1].astype(BF16)
    n_keys = n_pages * page
    pos = lax.broadcasted_iota(jnp.int32, (LANES, n_keys + LANES), 1)
    expand = (lax.broadcasted_iota(jnp.int32, (LANES, n_keys + LANES), 0)
              == jnp.minimum(pos, t) // SLC_BLOCK).astype(BF16)
    chosen = _dot_tn(sel_t.astype(BF16), expand) > 0.5
    ok = chosen[:, :n_keys] & (lax.broadcasted_iota(jnp.int32, (1, n_keys), 1) <= t)
    kv_new = kvs_new_ref[0]
    s_new = jnp.sum(qp * kv_new[:, :kw], axis=-1, keepdims=True)
    scores = jnp.dot(qpb, kt_ref[...], preferred_element_type=F32)
    e, e_new, inv = _softmax_with_new_key(scores, ok, s_new, chosen[:, n_keys:n_keys + 1])
    o_s = (_dot_nt(e.astype(BF16), vt_ref[...]) + e_new * kv_new[:, kw:]) * inv

    n_w = swa_ref.shape[3]
    dist = t - (t - n_w + lax.broadcasted_iota(jnp.int32, (1, n_w), 1))
    w_ok = (dist >= 0) & (dist <= WINDOW) & (t - dist >= 0)
    kvw_new = kvw_new_ref[0]
    s_wn = jnp.sum(qp * kvw_new[:, :kw], axis=-1, keepdims=True)
    new_ok = lax.broadcasted_iota(jnp.int32, (nh, 1), 0) >= 0
    scores = jnp.dot(qpb, swa_ref[0, 0].astype(BF16), preferred_element_type=F32)
    e, e_new, inv = _softmax_with_new_key(scores, w_ok, s_wn, new_ok)
    o_w = (_dot_nt(e.astype(BF16), swa_ref[0, 1].astype(BF16)) + e_new * kvw_new[:, kw:]) * inv

    g = _sigmoid(gp_ref[0])
    o = jnp.where(own, g[:, 0:1] * o_c + g[:, 1:2] * o_s + g[:, 2:3] * o_w, 0.0)
    o_h = o[:, :NSA_DH]
    for h in range(1, NSA_KVH):
        o_h = o_h + o[:, h * NSA_DH:(h + 1) * NSA_DH]
    o_ref[0] = o_h
    last = lax.broadcasted_iota(jnp.int32, (kw, n_w), 1) == n_w - 1
    diag = lax.broadcasted_iota(jnp.int32, (kw, kw), 0) == lax.broadcasted_iota(jnp.int32, (kw, kw), 1)
    for c in range(2):
        column = jnp.sum(jnp.where(diag, kvw_new[:, c * kw:(c + 1) * kw], 0.0), axis=-1, keepdims=True)
        swa_out_ref[0, c] = jnp.where(last, column, pltpu.roll(swa_ref[0, c], n_w - 1, 1))


def nsa_step_attention(q3, gates3, cke, cko, cache_slc_t, page_table, kvs_new, swa_t, kvw_new, *, past):
    n_seq, n_pages = page_table.shape
    page = cache_slc_t.shape[3]
    seq = lambda a: pl.BlockSpec((1,) + a.shape[1:], lambda b, pt: (b,) + (0,) * (a.ndim - 1))
    page_specs = [pl.BlockSpec((1, 2, NSA_KVW, page), lambda b, pt, p=p: (pt[b, p], 0, 0, 0)) for p in range(n_pages)]
    return pl.pallas_call(
        functools.partial(_nsa_step_kernel, n_pages=n_pages, page=page, past=past),
        grid_spec=pltpu.PrefetchScalarGridSpec(
            num_scalar_prefetch=1,
            grid=(n_seq,),
            in_specs=page_specs + [seq(a) for a in (q3, gates3, cke, cko, kvs_new, swa_t, kvw_new)],
            out_specs=[seq(q3), seq(swa_t)],
            scratch_shapes=[pltpu.VMEM((NSA_KVW, n_pages * page), BF16)] * 2),
        out_shape=[jax.ShapeDtypeStruct(q3.shape, F32), jax.ShapeDtypeStruct(swa_t.shape, F32)],
        compiler_params=_params("arbitrary"),
        name="nsa_step_attention",
    )(page_table, *([cache_slc_t] * n_pages), q3, gates3, cke, cko, kvs_new, swa_t, kvw_new)


PROMPT_TM = 1024
FFN_TM = 512
HG_TL = 512
RG_TL = 256
FFN_TF = 1408
CMP_TILE_BLOCKS = 64
MOE_TB = 896
MOE_TS = 256


def _ffn_layer(x, w_gu, w_down, g, b, *, tm):
    return ffn_residual_ln(x, w_gu.astype(BF16), w_down.astype(BF16), g, b, tm=tm, tf=FFN_TF)


def _moe_layer(x, w_router, w_gu, w_down, g, b):
    m = x.shape[0]
    tb = min(MOE_TB, m)
    gates, slots = moe_route(x, w_router.T, tb=tb)
    n_e = gates.shape[0]
    counts = (jnp.max(slots.reshape(n_e, -1, tb), axis=-1) + 1).T.reshape(-1)
    w_gu_t = jnp.swapaxes(w_gu, 1, 2).astype(BF16)
    w_down_t = jnp.swapaxes(w_down, 1, 2).astype(BF16)
    return moe_residual_ln(x, gates, slots, counts, w_gu_t, w_down_t, g, b, tb=tb, ts=min(MOE_TS, tb), tf=FFN_TF)


def kernel(x_prompt, x_sample, state_hgrn, state_rglru_conv, state_rglru_h, cache_nsa_cmp, cache_nsa_slc, cache_nsa_swa, page_table, ln_g, ln_b, hgrn_w_in, hgrn_lb, hgrn_norm_g, hgrn_w_out, rg_w_in, rg_conv_w, rg_conv_b, rg_w_gate, rg_b_gate, rg_lambda, rg_w_out, nsa_w_in, nsa_cmp_pos, nsa_cmp_w1, nsa_cmp_w2, nsa_w_out, ffn_w_gu, ffn_w_down, moe_router, moe_w_gu, moe_w_down):
    bp, l_seq, d = x_prompt.shape
    n_seq = x_sample.shape[0]
    past = page_table.shape[1] * cache_nsa_cmp.shape[2]
    xp = x_prompt.reshape(bp * l_seq, d)
    xs = x_sample.reshape(n_seq, d)
    tm_p, tm_s = PROMPT_TM, n_seq
    kv_shape = (2, NSA_KVH, NSA_DH)
    hg_p, hg_s, rc_p, rc_s, rh_p, rh_s = [], [], [], [], [], []
    cm_p, cm_s, sl_p, sl_s, sw_p, sw_s = [], [], [], [], [], []
    for layer in range(DEPTH):
        kind, slot = layer % 3, layer // 3
        g0, b0 = ln_g[layer, 0], ln_b[layer, 0]
        if kind == 0:
            w_in, w_out = hgrn_w_in[slot].astype(BF16), hgrn_w_out[slot].astype(BF16)
            op, st_p = hgrn_prompt(matmul(xp, w_in, tm=tm_p), hgrn_lb, hgrn_norm_g[slot],
                                   layer=layer, batch=bp, tl=HG_TL)
            os_, st_s = hgrn_step(matmul(xs, w_in, tm=tm_s), state_hgrn[slot], hgrn_lb, hgrn_norm_g[slot],
                                  layer=layer, sb=8)
            hg_p.append(st_p)
            hg_s.append(st_s)
        elif kind == 1:
            w_in, w_out = rg_w_in[slot].astype(BF16), rg_w_out[slot].astype(BF16)
            rg_w = (rg_conv_w[slot], rg_conv_b[slot], rg_w_gate[slot].astype(BF16), rg_b_gate[slot], rg_lambda[slot])
            op, cb_p, h_p = rglru_prompt(matmul(xp, w_in, tm=tm_p), *rg_w, batch=bp, tl=RG_TL)
            os_, cb_s, h_s = rglru_step(matmul(xs, w_in, tm=tm_s), state_rglru_conv[slot], state_rglru_h[slot], *rg_w)
            rc_p.append(cb_p)
            rc_s.append(cb_s)
            rh_p.append(h_p[:, 0])
            rh_s.append(h_s)
        else:
            w_qkv, w_kvb, w_g, w_kv_t, pos4, w1, w2 = nsa_prepare_weights(
                nsa_w_in[slot], nsa_cmp_pos[slot], nsa_cmp_w1[slot], nsa_cmp_w2[slot])
            w_out = nsa_w_out[slot].astype(BF16)
            hq = NSA_HEADS * NSA_DH
            qkv = matmul(xp, w_qkv[:, :hq + NSA_ROW], tm=tm_p)
            kvb = matmul(xp, w_kvb, tm=tm_p, out_dtype=BF16)
            gates_pre = matmul(xp, w_g, tm=tm_p)
            cke, cko = nsa_compress(qkv, hq // NSA_ROW, pos4, w1, w2, batch=bp, n_blk=CMP_TILE_BLOCKS)
            op = nsa_prompt_attention(qkv, gates_pre, jnp.concatenate([cke, cko], axis=1), kvb, batch=bp)
            as_rows = lambda a: a.reshape(bp, *kv_shape, -1).transpose(0, 4, 1, 2, 3)
            kv_c, kv_s, kv_w = matmul_t(xp, w_kv_t, batch=bp, n_out=3, tm=tm_p)
            cm_p.append(as_rows(kv_c))
            sl_p.append(as_rows(kv_s))
            sw_p.append(as_rows(kv_w[:, :, l_seq - min(WINDOW, l_seq):]))
            qkv = matmul(xs, w_qkv, tm=tm_s)
            gates_pre = matmul(xs, w_g, tm=tm_s)
            cmp_rows = cache_nsa_cmp[slot].reshape(cache_nsa_cmp.shape[1], cache_nsa_cmp.shape[2], NSA_ROW)
            cke, cko = nsa_compress_paged(cmp_rows, page_table, pos4, w1, w2)
            gates3 = gates_pre.reshape(n_seq, NSA_KVH, LANES)[:, :, :3 * NSA_GROUP].reshape(n_seq, NSA_HEADS, 3)
            kv = qkv[:, hq:].reshape(n_seq, 3, 1, NSA_ROW)
            rows_last = lambda c: c.transpose(0, 2, 3, 4, 1).reshape(c.shape[0], 2, NSA_KVW, c.shape[1])
            o3, swa_new = nsa_step_attention(qkv[:, :hq].reshape(n_seq, NSA_HEADS, NSA_DH), gates3, cke, cko,
                                             rows_last(cache_nsa_slc[slot]), page_table, kv[:, 1],
                                             rows_last(cache_nsa_swa[slot]), kv[:, 2], past=past)
            os_ = o3.reshape(n_seq, hq)
            cm_s.append(kv[:, 0].reshape(n_seq, 1, *kv_shape))
            sl_s.append(kv[:, 1].reshape(n_seq, 1, *kv_shape))
            sw_s.append(swa_new.reshape(n_seq, *kv_shape, -1).transpose(0, 4, 1, 2, 3))
        xp = matmul_residual_ln(op, w_out, xp, g0, b0, tm=tm_p)
        xs = matmul_residual_ln(os_, w_out, xs, g0, b0, tm=tm_s)
        g1, b1 = ln_g[layer, 1], ln_b[layer, 1]
        if layer % 2 == 0:
            fw = (ffn_w_gu[layer // 2], ffn_w_down[layer // 2])
            xp = _ffn_layer(xp, *fw, g1, b1, tm=FFN_TM)
            xs = _ffn_layer(xs, *fw, g1, b1, tm=tm_s)
        else:
            fw = (moe_router[layer // 2], moe_w_gu[layer // 2], moe_w_down[layer // 2])
            xp = _moe_layer(xp, *fw, g1, b1)
            xs = _moe_layer(xs, *fw, g1, b1)
    return (xp.reshape(bp, l_seq, d), xs.reshape(n_seq, 1, d),
            jnp.stack(hg_p), jnp.stack(hg_s), jnp.stack(rc_p), jnp.stack(rc_s), jnp.stack(rh_p), jnp.stack(rh_s),
            jnp.stack(cm_p), jnp.stack(cm_s), jnp.stack(sl_p), jnp.stack(sl_s), jnp.stack(sw_p), jnp.stack(sw_s))
```

```python
import functools

import jax
import jax.numpy as jnp
from jax import lax
from jax.experimental import pallas as pl
from jax.experimental.pallas import tpu as pltpu

F32 = jnp.float32
BF16 = jnp.bfloat16
HIGHEST = lax.Precision.HIGHEST

D_MODEL = 1024
DEPTH = 4
ALPHA = (2 * DEPTH) ** 0.25
LN_EPS = 1e-5
RMS_EPS = 1e-6
HG_HEADS = 8
HG_DK = 128
HG_CHUNK = 64
HG_SUB = 16
SUBLANES = 8
HG_HEADS_PER_STEP = 8
LB_FLOOR = 1e-30
RG_WIDTH = 1280
RG_BLOCKS = 10
RG_BS = 128
RG_CONV = 4
RG_C = 8.0
D_FF = 2816
N_EXPERTS = 8

VMEM_LIMIT_BYTES = 56 * 1024 * 1024


def _params(*sem):
    return pltpu.CompilerParams(dimension_semantics=sem, vmem_limit_bytes=VMEM_LIMIT_BYTES)


def _layer_norm(v, g, b):
    mu = jnp.mean(v, axis=-1, keepdims=True)
    vc = v - mu
    var = jnp.mean(vc * vc, axis=-1, keepdims=True)
    return vc * lax.rsqrt(var + LN_EPS) * g + b


def _sigmoid(x):
    return 1.0 / (1.0 + jnp.exp(-x))


def _silu(x):
    return x * _sigmoid(x)


def _mm_kernel(x_ref, w_ref, o_ref, xb_ref):
    @pl.when(pl.program_id(1) == 0)
    def _():
        xb_ref[...] = x_ref[...].astype(BF16)

    o_ref[...] = jnp.dot(xb_ref[...], w_ref[...], preferred_element_type=F32).astype(o_ref.dtype)


MAX_COL_TILE = 1280


def _col_tile(n):
    return max(t for t in range(128, min(n, MAX_COL_TILE) + 1, 128) if n % t == 0)


def matmul(x, w, *, tm, out_dtype=F32):
    m, k = x.shape
    n = w.shape[1]
    tn = _col_tile(n)
    return pl.pallas_call(
        _mm_kernel,
        grid=(m // tm, n // tn),
        in_specs=[pl.BlockSpec((tm, k), lambda i, j: (i, 0)),
                  pl.BlockSpec((k, tn), lambda i, j: (0, j))],
        out_specs=pl.BlockSpec((tm, tn), lambda i, j: (i, j)),
        out_shape=jax.ShapeDtypeStruct((m, n), out_dtype),
        scratch_shapes=[pltpu.VMEM((tm, k), BF16)],
        compiler_params=_params("parallel", "arbitrary"),
        name="matmul",
    )(x, w)


def _mm_t_kernel(x_ref, wt_ref, *o_refs):
    xb = x_ref[...].astype(BF16)
    n = o_refs[0].shape[1]
    for r, o_ref in enumerate(o_refs):
        o_ref[0] = _dot_nt(wt_ref[r * n:(r + 1) * n, :], xb)


def matmul_t(x, w_t, *, batch, n_out, tm):
    m, k = x.shape
    n = w_t.shape[0] // n_out
    l_seq = m // batch
    n_t = l_seq // tm
    return pl.pallas_call(
        _mm_t_kernel,
        grid=(batch, n_t),
        in_specs=[pl.BlockSpec((tm, k), lambda b, i: (b * n_t + i, 0)),
                  pl.BlockSpec((n_out * n, k), lambda b, i: (0, 0))],
        out_specs=[pl.BlockSpec((1, n, tm), lambda b, i: (b, 0, i))] * n_out,
        out_shape=[jax.ShapeDtypeStruct((batch, n, l_seq), F32)] * n_out,
        compiler_params=_params("parallel", "parallel"),
        name="matmul_t",
    )(x, w_t)


def _mm_res_ln_kernel(a_ref, w_ref, r_ref, g_ref, b_ref, o_ref):
    m = jnp.dot(a_ref[...].astype(BF16), w_ref[...], preferred_element_type=F32)
    o_ref[...] = _layer_norm(ALPHA * r_ref[...] + m, g_ref[...], b_ref[...])


def matmul_residual_ln(a, w, res, g, b, *, tm):
    m, k = a.shape
    d = w.shape[1]
    return pl.pallas_call(
        _mm_res_ln_kernel,
        grid=(m // tm,),
        in_specs=[pl.BlockSpec((tm, k), lambda i: (i, 0)),
                  pl.BlockSpec((k, d), lambda i: (0, 0)),
                  pl.BlockSpec((tm, d), lambda i: (i, 0)),
                  pl.BlockSpec((1, d), lambda i: (0, 0)),
                  pl.BlockSpec((1, d), lambda i: (0, 0))],
        out_specs=pl.BlockSpec((tm, d), lambda i: (i, 0)),
        out_shape=jax.ShapeDtypeStruct((m, d), F32),
        compiler_params=_params("parallel"),
        name="matmul_residual_ln",
    )(a, w, res, g.reshape(1, d), b.reshape(1, d))


def _ffn_kernel(x_ref, wg_ref, wu_ref, wd_ref, g_ref, b_ref, o_ref, xb_ref, acc_ref, *, n_f):
    f = pl.program_id(1)

    @pl.when(f == 0)
    def _():
        xb_ref[...] = x_ref[...].astype(BF16)

    xb = xb_ref[...]
    h = jnp.dot(xb, wg_ref[...], preferred_element_type=F32)
    u = jnp.dot(xb, wu_ref[...], preferred_element_type=F32)
    part = jnp.dot((_silu(h) * u).astype(BF16), wd_ref[...], preferred_element_type=F32)

    @pl.when(f == 0)
    def _():
        acc_ref[...] = part

    @pl.when(f > 0)
    def _():
        acc_ref[...] += part

    @pl.when(f == n_f - 1)
    def _():
        o_ref[...] = _layer_norm(ALPHA * x_ref[...] + acc_ref[...], g_ref[...], b_ref[...])


def ffn_residual_ln(x, w_gu, w_down, g, b, *, tm, tf):
    m, d = x.shape
    ff = w_down.shape[0]
    n_f = ff // tf
    return pl.pallas_call(
        functools.partial(_ffn_kernel, n_f=n_f),
        grid=(m // tm, n_f),
        in_specs=[pl.BlockSpec((tm, d), lambda i, f: (i, 0)),
                  pl.BlockSpec((d, tf), lambda i, f: (0, f)),
                  pl.BlockSpec((d, tf), lambda i, f: (0, n_f + f)),
                  pl.BlockSpec((tf, d), lambda i, f: (f, 0)),
                  pl.BlockSpec((1, d), lambda i, f: (0, 0)),
                  pl.BlockSpec((1, d), lambda i, f: (0, 0))],
        out_specs=pl.BlockSpec((tm, d), lambda i, f: (i, 0)),
        out_shape=jax.ShapeDtypeStruct((m, d), F32),
        scratch_shapes=[pltpu.VMEM((tm, d), BF16), pltpu.VMEM((tm, d), F32)],
        compiler_params=_params("parallel", "arbitrary"),
        name="ffn_residual_ln",
    )(x, w_gu, w_gu, w_down, g.reshape(1, d), b.reshape(1, d))


def _router_kernel(x_ref, wt_ref, gate_ref, pos_ref, *, n_tokens):
    logits = lax.dot_general(wt_ref[...], x_ref[...], (((1,), (1,)), ((), ())),
                             preferred_element_type=F32, precision=HIGHEST)
    n_e, tm = logits.shape
    expert = lax.broadcasted_iota(jnp.int32, logits.shape, 0)
    m1 = jnp.max(logits, axis=0, keepdims=True)
    i1 = jnp.min(jnp.where(logits == m1, expert, n_e), axis=0, keepdims=True)
    rest = jnp.where(expert == i1, -jnp.inf, logits)
    m2 = jnp.max(rest, axis=0, keepdims=True)
    i2 = jnp.min(jnp.where(rest == m2, expert, n_e), axis=0, keepdims=True)
    e2 = jnp.exp(m2 - m1)
    denom = 1.0 + e2
    first = expert == i1
    in_range = pl.program_id(0) * tm + lax.broadcasted_iota(jnp.int32, logits.shape, 1) < n_tokens
    routed = (first | (expert == i2)) & in_range
    gate_ref[...] = jnp.where(first & in_range, 1.0 / denom, jnp.where(routed, e2 / denom, 0.0))
    upper = (lax.broadcasted_iota(jnp.int32, (tm, tm), 0) <= lax.broadcasted_iota(jnp.int32, (tm, tm), 1)).astype(BF16)
    upto = jnp.dot(routed.astype(BF16), upper, preferred_element_type=F32)
    pos_ref[...] = jnp.where(routed, upto - 1.0, -1.0).astype(jnp.int32)


def moe_route(x, w_router_t, *, tb):
    m, d = x.shape
    n_e = w_router_t.shape[0]
    n_blk = pl.cdiv(m, tb)
    return pl.pallas_call(
        functools.partial(_router_kernel, n_tokens=m),
        grid=(n_blk,),
        in_specs=[pl.BlockSpec((tb, d), lambda i: (i, 0)),
                  pl.BlockSpec((n_e, d), lambda i: (0, 0))],
        out_specs=[pl.BlockSpec((n_e, tb), lambda i: (0, i))] * 2,
        out_shape=[jax.ShapeDtypeStruct((n_e, n_blk * tb), F32), jax.ShapeDtypeStruct((n_e, n_blk * tb), jnp.int32)],
        compiler_params=_params("parallel"),
        name="moe_route",
    )(x, w_router_t)


def _moe_kernel(cnt_ref, x_ref, gate_ref, slot_ref, wg_ref, wu_ref, wd_ref, g_ref, b_ref, o_ref,
                xt_ref, xc_ref, acc_ref, yt_ref, *, n_e, n_f, ts, n_tokens):
    i, e, f = pl.program_id(0), pl.program_id(1), pl.program_id(2)
    tb = x_ref.shape[0]
    n_sub = (cnt_ref[i * n_e + e] + ts - 1) // ts
    row = lax.broadcasted_iota(jnp.int32, (ts, tb), 0)

    def one_hot(sub):
        return (slot_ref[pl.ds(e, 1), :] == row + sub * ts).astype(BF16)

    @pl.when((e == 0) & (f == 0))
    def _():
        in_range = i * tb + lax.broadcasted_iota(jnp.int32, (tb, 1), 0) < n_tokens
        xt_ref[...] = jnp.where(in_range, x_ref[...], 0.0).T.astype(BF16)
        yt_ref[...] = jnp.zeros_like(yt_ref)

    tf, d = wg_ref.shape[1], wd_ref.shape[1]
    halves = lambda n: (slice(0, n // 2), slice(n // 2, n))

    @pl.when(f == 0)
    def _():
        def gather(sub, c):
            pick = one_hot(sub)
            for rows in halves(d):
                xc_ref[sub, rows, :] = _dot_nt(xt_ref[rows, :], pick).astype(BF16)
            return c

        lax.fori_loop(0, n_sub, gather, 0)

    def expert_ffn(sub, c):
        xc = xc_ref[sub]
        act = []
        for rows in halves(tf):
            h = jnp.dot(wg_ref[0, rows, :], xc, preferred_element_type=F32)
            u = jnp.dot(wu_ref[0, rows, :], xc, preferred_element_type=F32)
            act.append((_silu(h) * u).astype(BF16))
        act = jnp.concatenate(act, axis=0)
        part = jnp.concatenate([jnp.dot(wd_ref[0, rows, :], act, preferred_element_type=F32) for rows in halves(d)], axis=0)

        @pl.when(f == 0)
        def _():
            acc_ref[sub] = part

        @pl.when(f > 0)
        def _():
            acc_ref[sub] += part

        return c

    lax.fori_loop(0, n_sub, expert_ffn, 0)

    @pl.when(f == n_f - 1)
    def _():
        def scatter(sub, c):
            pick = one_hot(sub)
            gate = gate_ref[pl.ds(e, 1), :]
            for rows in halves(d):
                back = jnp.dot(acc_ref[sub, rows, :].astype(BF16), pick, preferred_element_type=F32)
                yt_ref[rows, :] += gate * back
            return c

        lax.fori_loop(0, n_sub, scatter, 0)

    @pl.when((e == n_e - 1) & (f == n_f - 1))
    def _():
        o_ref[...] = _layer_norm(ALPHA * x_ref[...] + yt_ref[...].T, g_ref[...], b_ref[...])


def moe_residual_ln(x, gates, slots, counts, w_gu_t, w_down_t, slot, g, b, *, tb, ts, tf):
    m, d = x.shape
    _, n_e, ff2, _ = w_gu_t.shape
    n_f = ff2 // 2 // tf
    n_sub = pl.cdiv(tb, ts)
    return pl.pallas_call(
        functools.partial(_moe_kernel, n_e=n_e, n_f=n_f, ts=ts, n_tokens=m),
        grid_spec=pltpu.PrefetchScalarGridSpec(
            num_scalar_prefetch=1,
            grid=(pl.cdiv(m, tb), n_e, n_f),
            in_specs=[pl.BlockSpec((tb, d), lambda i, e, f, c: (i, 0)),
                      pl.BlockSpec((n_e, tb), lambda i, e, f, c: (0, i)),
                      pl.BlockSpec((n_e, tb), lambda i, e, f, c: (0, i)),
                      pl.BlockSpec((None, 1, tf, d), lambda i, e, f, c: (slot, e, f, 0)),
                      pl.BlockSpec((None, 1, tf, d), lambda i, e, f, c: (slot, e, n_f + f, 0)),
                      pl.BlockSpec((None, 1, d, tf), lambda i, e, f, c: (slot, e, 0, f)),
                      pl.BlockSpec((1, d), lambda i, e, f, c: (0, 0)),
                      pl.BlockSpec((1, d), lambda i, e, f, c: (0, 0))],
            out_specs=pl.BlockSpec((tb, d), lambda i, e, f, c: (i, 0)),
            scratch_shapes=[pltpu.VMEM((d, tb), BF16), pltpu.VMEM((n_sub, d, ts), BF16),
                            pltpu.VMEM((n_sub, d, ts), F32), pltpu.VMEM((d, tb), F32)]),
        out_shape=jax.ShapeDtypeStruct((m, d), F32),
        compiler_params=_params("parallel", "arbitrary", "arbitrary"),
        name="moe_residual_ln",
    )(counts, x, gates, slots, w_gu_t, w_gu_t, w_down_t, g.reshape(1, d), b.reshape(1, d))


def _hgrn_lower_bound(lbp, layer):
    m = jnp.max(lbp, axis=0, keepdims=True)
    e = jnp.exp(lbp - m)
    p = e / jnp.sum(e, axis=0, keepdims=True)
    c = p[0:1]
    for i in range(1, layer + 1):
        c = c + p[i:i + 1]
    return c - p[0:1]


def _hgrn_gates(q, fx, lb):
    log_lb = jnp.log(jnp.maximum(lb, LB_FLOOR))
    log1m = jnp.log1p(-lb)
    log_sig = -(jnp.maximum(-fx, 0.0) + jnp.log1p(jnp.exp(-jnp.abs(fx))))
    b2 = log1m + log_sig
    log_f = jnp.maximum(log_lb, b2) + jnp.log1p(jnp.exp(-jnp.abs(log_lb - b2)))
    kk = (1.0 - lb) * _sigmoid(-fx) - (jnp.maximum(lb, LB_FLOOR) - lb)
    return _silu(q), log_f, kk


def _hgrn_out_norm(o, g, ng):
    o = o * lax.rsqrt(jnp.mean(o * o, axis=-1, keepdims=True) + RMS_EPS)
    return o * ng * _silu(g)


def _dot_nt(a, b):
    return lax.dot_general(a, b, (((1,), (1,)), ((), ())), preferred_element_type=F32)


def _dot_tn(a, b):
    return lax.dot_general(a, b, (((0,), (0,)), ((), ())), preferred_element_type=F32)


def _gla_kernel(q_ref, f_ref, i_ref, g_ref, lbp_ref, ng_ref, o_ref, s_out_ref, st_ref, *, layer, n_t, tl, hp):
    t = pl.program_id(2)
    c_len, sub = HG_CHUNK, HG_SUB
    n_sub = c_len // sub
    n_stack = sub * (n_sub * (n_sub - 1) // 2)

    @pl.when(t == 0)
    def _():
        st_ref[...] = jnp.zeros_like(st_ref)

    lb = _hgrn_lower_bound(lbp_ref[...], layer)
    ng = ng_ref[...]
    row = lax.broadcasted_iota(jnp.int32, (c_len, c_len), 0)
    col = lax.broadcasted_iota(jnp.int32, (c_len, c_len), 1)
    tri = (col <= row).astype(F32)
    srow = lax.broadcasted_iota(jnp.int32, (c_len, n_stack), 0) // sub
    scol = lax.broadcasted_iota(jnp.int32, (c_len, n_stack), 1)
    part = jnp.zeros((c_len, n_stack), jnp.int32)
    for p in range(1, n_sub):
        part = part + (scol >= sub * (p * (p - 1) // 2)).astype(jnp.int32)
    inter_mask = srow == part

    def one_head(hh, r):
        lanes = slice(hh * HG_DK, (hh + 1) * HG_DK)
        qs, log_f, kk = _hgrn_gates(q_ref[r, lanes], f_ref[r, lanes], lb[:, lanes])
        v = i_ref[r, lanes]
        cum = jnp.dot(tri, log_f, preferred_element_type=F32, precision=HIGHEST)
        last = cum[c_len - 1:c_len, :]
        st = st_ref[hh]
        o = _dot_nt((qs * jnp.exp(cum)).astype(BF16), st.astype(BF16))
        bounds = [cum[sub * a - 1:sub * a, :] for a in range(1, n_sub)]
        cq = jnp.concatenate([cum[0:sub]] + [jnp.broadcast_to(b, (sub, HG_DK)) for b in bounds], axis=0)
        q_rel = (qs * jnp.exp(cum - cq)).astype(BF16)
        k_st = jnp.concatenate([kk[0:sub * a] * jnp.exp(bounds[a - 1] - cum[0:sub * a]) for a in range(1, n_sub)], axis=0)
        v_st = jnp.concatenate([v[0:sub * a] for a in range(1, n_sub)], axis=0)
        att = jnp.where(inter_mask, _dot_nt(q_rel, k_st.astype(BF16)), 0.0)
        o = o + jnp.dot(att.astype(BF16), v_st.astype(BF16), preferred_element_type=F32)
        parts = []
        for a in range(n_sub):
            base = sub * a
            rest = o[base:base + sub]
            for lo in range(0, sub, SUBLANES):
                tail = slice(base + lo, base + sub)
                qa, ca = qs[tail], cum[tail]
                row_t = lo + lax.broadcasted_iota(jnp.int32, (sub - lo, 1), 0)
                for s in range(lo, lo + SUBLANES):
                    src = slice(base + s, base + s + 1)
                    w = jnp.sum(qa * (kk[src] * jnp.exp(ca - cum[src])), axis=-1, keepdims=True)
                    rest = rest + jnp.where(row_t >= s, w, 0.0) * v[src]
                parts.append(rest[:SUBLANES])
                rest = rest[SUBLANES:]
        o = jnp.concatenate(parts, axis=0)
        o_ref[r, lanes] = _hgrn_out_norm(o, g_ref[r, lanes], ng[:, lanes]).astype(o_ref.dtype)
        k_end = (kk * jnp.exp(last - cum)).astype(BF16)
        st_ref[hh] = st * jnp.exp(last) + _dot_tn(v.astype(BF16), k_end)

    def chunk(c, carry):
        r = pl.ds(pl.multiple_of(c * c_len, c_len), c_len)
        for hh in range(hp):
            one_head(hh, r)
        return carry

    lax.fori_loop(0, tl // c_len, chunk, 0)

    @pl.when(t == n_t - 1)
    def _():
        for hh in range(hp):
            s_out_ref[0, hh] = st_ref[hh].T


def hgrn_prompt(z, lb_param, norm_g, *, layer, batch, tl):
    m = z.shape[0]
    n_t = m // batch // tl
    h, dk = HG_HEADS, HG_DK
    hp = HG_HEADS_PER_STEP
    n_hg = h // hp
    zspec = lambda part: pl.BlockSpec((tl, hp * dk), lambda b, hh, t: (b * n_t + t, part * n_hg + hh))
    return pl.pallas_call(
        functools.partial(_gla_kernel, layer=layer, n_t=n_t, tl=tl, hp=hp),
        grid=(batch, n_hg, n_t),
        in_specs=[zspec(0), zspec(1), zspec(2), zspec(3),
                  pl.BlockSpec((DEPTH, hp * dk), lambda b, hh, t: (0, hh)),
                  pl.BlockSpec((1, hp * dk), lambda b, hh, t: (0, hh))],
        out_specs=[pl.BlockSpec((tl, hp * dk), lambda b, hh, t: (b * n_t + t, hh)),
                   pl.BlockSpec((1, hp, dk, dk), lambda b, hh, t: (b, hh, 0, 0))],
        out_shape=[jax.ShapeDtypeStruct((m, h * dk), BF16),
                   jax.ShapeDtypeStruct((batch, h, dk, dk), F32)],
        scratch_shapes=[pltpu.VMEM((hp, dk, dk), F32)],
        compiler_params=_params("parallel", "parallel", "arbitrary"),
        name="hgrn_prompt",
    )(z, z, z, z, lb_param, norm_g.reshape(1, h * dk))


def _hgrn_step_kernel(q_ref, f_ref, i_ref, g_ref, lbp_ref, ng_ref, s0_ref, o_ref, s_ref,
                      qt_ref, ft_ref, kt_ref, oacc_ref, *, layer, n_j, sb):
    j = pl.program_id(1)
    n_seq = q_ref.shape[0]

    def split3(x, ref):
        hi = x.astype(BF16)
        r1 = x - hi.astype(F32)
        mid = r1.astype(BF16)
        ref[0], ref[1], ref[2] = hi, mid, (r1 - mid.astype(F32)).astype(BF16)

    @pl.when(j == 0)
    def _():
        lb = _hgrn_lower_bound(lbp_ref[...], layer)
        qs, log_f, kk = _hgrn_gates(q_ref[...], f_ref[...], lb)
        split3(qs.T, qt_ref)
        split3(jnp.exp(log_f).T, ft_ref)
        split3(kk.T, kt_ref)

    seq = lax.broadcasted_iota(jnp.int32, (n_seq, HG_DK), 0)
    for i in range(sb):
        b = j * sb + i
        pick = (seq == b).astype(BF16)
        col = lambda ref: sum(jnp.dot(ref[p], pick, preferred_element_type=F32) for p in range(3))
        s_new = s0_ref[i, 0] * col(ft_ref) + col(kt_ref) * i_ref[pl.ds(b, 1), :]
        s_ref[i, 0] = s_new
        oacc_ref[pl.ds(b, 1), :] = jnp.sum(col(qt_ref) * s_new, axis=0, keepdims=True)

    @pl.when(j == n_j - 1)
    def _():
        o_ref[...] = _hgrn_out_norm(oacc_ref[...], g_ref[...], ng_ref[...]).astype(o_ref.dtype)


def hgrn_step(z, s0, slot, lb_param, norm_g, *, layer, sb):
    n_seq = z.shape[0]
    h, dk = HG_HEADS, HG_DK
    n_j = n_seq // sb
    zspec = lambda off: pl.BlockSpec((n_seq, dk), lambda hh, j: (0, off + hh))
    return pl.pallas_call(
        functools.partial(_hgrn_step_kernel, layer=layer, n_j=n_j, sb=sb),
        grid=(h, n_j),
        in_specs=[zspec(0), zspec(h), zspec(2 * h), zspec(3 * h),
                  pl.BlockSpec((DEPTH, dk), lambda hh, j: (0, hh)),
                  pl.BlockSpec((1, dk), lambda hh, j: (0, hh)),
                  pl.BlockSpec((None, sb, 1, dk, dk), lambda hh, j: (slot, j, hh, 0, 0))],
        out_specs=[pl.BlockSpec((n_seq, dk), lambda hh, j: (0, hh)),
                   pl.BlockSpec((sb, 1, dk, dk), lambda hh, j: (j, hh, 0, 0))],
        out_shape=[jax.ShapeDtypeStruct((n_seq, h * dk), BF16),
                   jax.ShapeDtypeStruct(s0.shape[1:], F32)],
        scratch_shapes=[pltpu.VMEM((3, dk, n_seq), BF16)] * 3 + [pltpu.VMEM((n_seq, dk), F32)],
        compiler_params=_params("parallel", "arbitrary"),
        name="hgrn_step",
    )(z, z, z, z, lb_param, norm_g.reshape(1, h * dk), s0)


def _gelu_tanh(x):
    return x * (0.5 * (1.0 + jnp.tanh(0.7978845608028654 * (x + 0.044715 * (x * x * x)))))


def _rg_decay_and_input(xc, wg_ref, bg_ref, lam_ref):
    xb = xc.astype(BF16)
    gates = []
    for gi in range(2):
        blocks = [jnp.dot(xb[:, n * RG_BS:(n + 1) * RG_BS], wg_ref[gi, n], preferred_element_type=F32)
                  for n in range(RG_BLOCKS)]
        gates.append(jnp.concatenate(blocks, axis=1) + bg_ref[gi:gi + 1, :])
    r = _sigmoid(gates[0])
    i = _sigmoid(gates[1])
    neg_lam = -lam_ref[...]
    softplus = jnp.maximum(neg_lam, 0.0) + jnp.log1p(jnp.exp(-jnp.abs(neg_lam)))
    log_a = -RG_C * r * softplus
    one_minus_a2 = -jnp.tanh(log_a) * (jnp.exp(2.0 * log_a) + 1.0)
    return jnp.exp(log_a), jnp.sqrt(one_minus_a2) * i * xc


def _rg_prompt_kernel(y_ref, x_ref, cw_ref, cb_ref, wg_ref, bg_ref, lam_ref,
                      o_ref, conv_out_ref, h_out_ref, xpad_ref, h_ref, *, n_t, tl):
    t = pl.program_id(1)
    halo = 8

    @pl.when(t == 0)
    def _():
        xpad_ref[0:halo, :] = jnp.zeros((halo, RG_WIDTH), F32)
        h_ref[...] = jnp.zeros_like(h_ref)

    x = x_ref[...]
    xpad_ref[halo:halo + tl, :] = x
    xc = cb_ref[...]
    for j in range(RG_CONV):
        off = halo - (RG_CONV - 1) + j
        xc = xc + xpad_ref[off:off + tl, :] * cw_ref[j:j + 1, :]
    xpad_ref[0:halo, :] = x[tl - halo:tl, :]
    a, u = _rg_decay_and_input(xc, wg_ref, bg_ref, lam_ref)
    row = lax.broadcasted_iota(jnp.int32, (tl, 1), 0)
    s = 1
    while s < tl:
        keep = row >= s
        a_prev = jnp.where(keep, pltpu.roll(a, s, 0), 1.0)
        u_prev = jnp.where(keep, pltpu.roll(u, s, 0), 0.0)
        u = a * u_prev + u
        a = a * a_prev
        s *= 2
    h = a * h_ref[...] + u
    h_ref[...] = h[tl - 1:tl, :]
    o_ref[...] = (_gelu_tanh(y_ref[...]) * h).astype(o_ref.dtype)

    @pl.when(t == n_t - 1)
    def _():
        conv_out_ref[0] = x[tl - (RG_CONV - 1):tl, :]
        h_out_ref[0] = h[tl - 1:tl, :]


def rglru_prompt(z, conv_w, conv_b, w_gate, b_gate, lam, *, batch, tl):
    m = z.shape[0]
    n_t = m // batch // tl
    w = RG_WIDTH
    const = lambda shape: pl.BlockSpec(shape, lambda b, t: (0,) * len(shape))
    return pl.pallas_call(
        functools.partial(_rg_prompt_kernel, n_t=n_t, tl=tl),
        grid=(batch, n_t),
        in_specs=[pl.BlockSpec((tl, w), lambda b, t: (b * n_t + t, 0)),
                  pl.BlockSpec((tl, w), lambda b, t: (b * n_t + t, 1)),
                  const((RG_CONV, w)), const((1, w)), const((2, RG_BLOCKS, RG_BS, RG_BS)), const((2, w)), const((1, w))],
        out_specs=[pl.BlockSpec((tl, w), lambda b, t: (b * n_t + t, 0)),
                   pl.BlockSpec((1, RG_CONV - 1, w), lambda b, t: (b, 0, 0)),
                   pl.BlockSpec((1, 1, w), lambda b, t: (b, 0, 0))],
        out_shape=[jax.ShapeDtypeStruct((m, w), BF16),
                   jax.ShapeDtypeStruct((batch, RG_CONV - 1, w), F32),
                   jax.ShapeDtypeStruct((batch, 1, w), F32)],
        scratch_shapes=[pltpu.VMEM((tl + 8, w), F32), pltpu.VMEM((1, w), F32)],
        compiler_params=_params("parallel", "arbitrary"),
        name="rglru_prompt",
    )(z, z, conv_w, conv_b.reshape(1, w), w_gate, b_gate, lam.reshape(1, w))


def _rg_step_kernel(y_ref, x_ref, buf_ref, h0_ref, cw_ref, cb_ref, wg_ref, bg_ref, lam_ref,
                    o_ref, conv_out_ref, h_out_ref):
    x = x_ref[...]
    xc = cb_ref[...]
    for j in range(RG_CONV - 1):
        xc = xc + buf_ref[:, j, :] * cw_ref[j:j + 1, :]
    xc = xc + x * cw_ref[RG_CONV - 1:RG_CONV, :]
    a, u = _rg_decay_and_input(xc, wg_ref, bg_ref, lam_ref)
    h = a * h0_ref[...] + u
    o_ref[...] = (_gelu_tanh(y_ref[...]) * h).astype(o_ref.dtype)
    h_out_ref[...] = h
    for j in range(RG_CONV - 2):
        conv_out_ref[:, j, :] = buf_ref[:, j + 1, :]
    conv_out_ref[:, RG_CONV - 2, :] = x


def rglru_step(z, conv_buf, h0, conv_w, conv_b, w_gate, b_gate, lam):
    n_seq = z.shape[0]
    w = RG_WIDTH
    const = lambda shape: pl.BlockSpec(shape, lambda i: (0,) * len(shape))
    return pl.pallas_call(
        _rg_step_kernel,
        grid=(1,),
        in_specs=[pl.BlockSpec((n_seq, w), lambda i: (0, 0)),
                  pl.BlockSpec((n_seq, w), lambda i: (0, 1)),
                  const((n_seq, RG_CONV - 1, w)), const((n_seq, w)),
                  const((RG_CONV, w)), const((1, w)), const((2, RG_BLOCKS, RG_BS, RG_BS)), const((2, w)), const((1, w))],
        out_specs=[const((n_seq, w)), const((n_seq, RG_CONV - 1, w)), const((n_seq, w))],
        out_shape=[jax.ShapeDtypeStruct((n_seq, w), BF16),
                   jax.ShapeDtypeStruct((n_seq, RG_CONV - 1, w), F32),
                   jax.ShapeDtypeStruct((n_seq, w), F32)],
        compiler_params=_params("arbitrary"),
        name="rglru_step",
    )(z, z, conv_buf, h0, conv_w, conv_b.reshape(1, w), w_gate, b_gate, lam.reshape(1, w))


NSA_HEADS = 16
NSA_KVH = 4
NSA_GROUP = 4
NSA_DH = 64
NSA_KVW = NSA_KVH * NSA_DH
NSA_ROW = 2 * NSA_KVW
CMP_BLOCK = 32
CMP_HID = 128
SLC_BLOCK = 64
SLC_TOPK = 16
WINDOW = 512
NSA_QBLOCK = 128
MASK_NEG = -1e30
SEL_BIG = 1e9
LANES = 128
SEL_KEYS = 1024
LOG2_E = 1.4426950408889634


def _masked_softmax(s, mask, exp=jnp.exp):
    s = jnp.where(mask, s, MASK_NEG)
    m = jnp.max(s, axis=-1, keepdims=True)
    e = jnp.where(mask, exp(s - m), 0.0)
    return e / jnp.maximum(jnp.sum(e, axis=-1, keepdims=True), 1e-30)


def _top_blocks(score, k, axis):
    pos = lax.broadcasted_iota(jnp.int32, score.shape, axis)
    n = score.shape[axis]

    def pick(_, carry):
        sc, sel = carry
        mx = jnp.max(sc, axis=axis, keepdims=True)
        idx = jnp.min(jnp.where(sc == mx, pos, n), axis=axis, keepdims=True)
        hit = pos == idx
        return jnp.where(hit, -jnp.inf, sc), jnp.where(hit, 1.0, sel)

    return lax.fori_loop(0, k, pick, (score, jnp.zeros(score.shape, F32)))[1]


def _compress_rows(x_refs, pos_ref, w1_ref, w2_ref, n_blk):
    half = n_blk // 2
    tiles = NSA_KVW // LANES
    low = lax.broadcasted_iota(jnp.int32, (n_blk, LANES), 1) < NSA_DH
    outs = []
    for c in range(2):
        cols = [[] for _ in range(NSA_KVH)]
        for l in range(0, CMP_BLOCK, 2):
            for tl in range(tiles):
                ref = x_refs[c * tiles + tl]

                def rows(r, ref=ref, tl=tl):
                    xe = ref[pl.ds(r, half, stride=2 * CMP_BLOCK), :]
                    xo = ref[pl.ds(r + CMP_BLOCK, half, stride=2 * CMP_BLOCK), :]
                    return jnp.concatenate([xe, xo], axis=0) + pos_ref[c, r:r + 1, tl * LANES:(tl + 1) * LANES]

                x0, x1 = rows(l), rows(l + 1)
                cols[2 * tl].append(jnp.where(low, x0, pltpu.roll(x1, NSA_DH, 1)))
                cols[2 * tl + 1].append(jnp.where(low, pltpu.roll(x0, NSA_DH, 1), x1))
        xs = jnp.concatenate([jnp.concatenate(cols[h], axis=1) for h in range(NSA_KVH)], axis=0)
        acc = jnp.dot(xs.astype(BF16), w1_ref[c], preferred_element_type=F32)
        out = jnp.dot(_silu(acc).astype(BF16), w2_ref[c], preferred_element_type=F32)
        outs.append(jnp.concatenate([out[h * n_blk:(h + 1) * n_blk] for h in range(NSA_KVH)], axis=1))
    return jnp.concatenate(outs, axis=1)


def _compress_kernel(x0_ref, x1_ref, x2_ref, x3_ref, pos_ref, w1_ref, w2_ref, even_ref, odd_ref, *, n_blk):
    full = _compress_rows((x0_ref, x1_ref, x2_ref, x3_ref), pos_ref, w1_ref, w2_ref, n_blk)
    even_ref[0] = full[:n_blk // 2]
    odd_ref[0] = full[n_blk // 2:]


def nsa_compress(kv, col_block, pos4, w1, w2, *, batch, n_blk):
    m = kv.shape[0]
    n_t = m // batch // (n_blk * CMP_BLOCK)
    half = n_blk // 2
    const = lambda shape: pl.BlockSpec(shape, lambda b, t: (0,) * len(shape))
    out = jax.ShapeDtypeStruct((batch, n_t * half, NSA_ROW), F32)
    tiles = NSA_ROW // LANES
    lane_tile = lambda i: pl.BlockSpec((n_blk * CMP_BLOCK, LANES), lambda b, t: (b * n_t + t, col_block * tiles + i))
    return pl.pallas_call(
        functools.partial(_compress_kernel, n_blk=n_blk),
        grid=(batch, n_t),
        in_specs=[lane_tile(i) for i in range(tiles)]
        + [const((2, CMP_BLOCK, NSA_KVW)), const((2, CMP_BLOCK * NSA_DH, CMP_HID)), const((2, CMP_HID, NSA_DH))],
        out_specs=[pl.BlockSpec((1, half, NSA_ROW), lambda b, t: (b, t, 0))] * 2,
        out_shape=[out, out],
        compiler_params=_params("parallel", "parallel"),
        name="nsa_compress",
    )(kv, kv, kv, kv, pos4, w1, w2)


def nsa_prepare_weights(w_in, pos, w1, w2):
    hq = NSA_HEADS * NSA_DH
    kv_end = hq + 3 * NSA_ROW
    w_qkv = w_in[:, :kv_end].astype(BF16)
    w_kvb = w_in[:, hq + NSA_ROW:kv_end].astype(BF16)
    w_g = w_in[:, kv_end:].reshape(-1, NSA_KVH, 3 * NSA_GROUP)
    w_g = jnp.pad(w_g, ((0, 0), (0, 0), (0, LANES - 3 * NSA_GROUP))).reshape(-1, NSA_KVH * LANES).astype(BF16)
    pos4 = jnp.tile(pos, (1, 1, NSA_KVH))
    w1 = w1.reshape(2, CMP_BLOCK * NSA_DH, CMP_HID)
    w_kv_t = w_in[:, hq:kv_end].T.astype(BF16)
    return w_qkv, w_kvb, w_g, w_kv_t, pos4, w1.astype(BF16), w2.astype(BF16)


def _stack_group_queries(q, par):
    half = lax.broadcasted_iota(jnp.int32, (q.shape[0], LANES), 1) // NSA_DH
    parts = []
    for g in range(NSA_GROUP):
        qg = q[:, g * NSA_DH:(g + 1) * NSA_DH]
        parts.append(jnp.where(half == par, jnp.concatenate([qg, qg], axis=1), 0.0))
    return jnp.concatenate(parts, axis=0).astype(BF16)


def _nsa_prompt_kernel(q_ref, gp_ref, ck_ref, cv_ref, ks_ref, vs_ref, kw_ref, vw_ref, o_ref):
    kvh = pl.program_id(1)
    j = pl.program_id(2)
    par = kvh % 2
    qb, grp = NSA_QBLOCK, NSA_GROUP
    rows = grp * qb
    qp = _stack_group_queries(q_ref[...] * (NSA_DH ** -0.5 * LOG2_E), par)
    t_pos = j * qb + lax.broadcasted_iota(jnp.int32, (qb, 1), 0)

    def grouped_softmax(s, mask):
        n = s.shape[-1]
        return _masked_softmax(s.reshape(grp, qb, n), mask[None], jnp.exp2).reshape(rows, n)

    n_c = ck_ref.shape[1]
    n_s = n_c // 2
    ck = ck_ref[0].astype(BF16)
    cv = cv_ref[0].astype(BF16)
    nn = lax.broadcasted_iota(jnp.int32, (1, n_c), 1)
    c_blk = jnp.where(nn < n_s, 2 * nn, 2 * (nn - n_s) + 1)
    c_vis = (c_blk * CMP_BLOCK + CMP_BLOCK - 1) <= t_pos
    p_c = grouped_softmax(_dot_nt(qp, ck), c_vis)
    o_c = jnp.dot(p_c.astype(BF16), cv, preferred_element_type=F32)
    imp = p_c[0:qb]
    for g in range(1, grp):
        imp = imp + p_c[g * qb:(g + 1) * qb]
    imp = imp[:, :n_s] + imp[:, n_s:]
    blk = lax.broadcasted_iota(jnp.int32, (qb, n_s), 1)
    cur = t_pos // SLC_BLOCK
    forced = (blk == 0) | (blk == cur) | (blk == cur - 1)
    score = jnp.where(forced, SEL_BIG, jnp.where(blk <= cur, imp, -SEL_BIG))
    sel_b = _top_blocks(score.T, SLC_TOPK, 0).T.astype(BF16)

    per = SEL_KEYS // SLC_BLOCK
    e_row = lax.broadcasted_iota(jnp.int32, (n_s, SEL_KEYS), 0)
    e_col = lax.broadcasted_iota(jnp.int32, (n_s, SEL_KEYS), 1) // SLC_BLOCK
    k_off = lax.broadcasted_iota(jnp.int32, (1, SEL_KEYS), 1)

    n_kb = (j * qb + qb - 1) // SEL_KEYS + 1

    def key_rows(kb):
        return pl.ds(pl.multiple_of(kb * SEL_KEYS, SEL_KEYS), SEL_KEYS)

    own_lanes = lax.broadcasted_iota(jnp.int32, (SEL_KEYS, LANES), 1) // NSA_DH == par

    def sel_step(kb, carry):
        m, acc = carry
        expand =(e_row == kb * per + e_col).astype(BF16)
        chosen = jnp.dot(sel_b, expand, preferred_element_type=F32) > 0.5
        ok = chosen & ((kb * SEL_KEYS + k_off) <= t_pos)
        bias = jnp.where(ok, 0.0, MASK_NEG)[None]
        s = _dot_nt(qp, ks_ref[key_rows(kb), :]).reshape(grp, qb, SEL_KEYS) + bias
        m_new = jnp.maximum(m, jnp.max(s, axis=-1, keepdims=True))
        alpha = jnp.exp2(m - m_new)
        e = jnp.exp2(s - m_new)
        v_one = jnp.where(own_lanes, vs_ref[key_rows(kb), :], 1.0)
        pv = jnp.dot(e.reshape(rows, SEL_KEYS).astype(BF16), v_one, preferred_element_type=F32)
        return m_new, alpha * acc + pv.reshape(grp, qb, LANES)

    init = (jnp.full((grp, qb, 1), MASK_NEG, F32), jnp.zeros((grp, qb, LANES), F32))
    acc = lax.fori_loop(0, n_kb, sel_step, init)[1].reshape(rows, LANES)
    o_s = acc / jnp.maximum(pltpu.roll(acc, NSA_DH, 1), 1e-30)

    n_w = WINDOW + qb
    w0 = pl.multiple_of(jnp.maximum(j * qb - WINDOW, 0), qb)
    dist = t_pos - (w0 + lax.broadcasted_iota(jnp.int32, (1, n_w), 1))
    w_bias = jnp.where((dist >= 0) & (dist <= WINDOW), 0.0, MASK_NEG)[None]
    s_w = _dot_nt(qp, kw_ref[pl.ds(w0, n_w), :]).reshape(grp, qb, n_w) + w_bias
    e_w = jnp.exp2(s_w - jnp.max(s_w, axis=-1, keepdims=True)).reshape(rows, n_w)
    own_w = lax.broadcasted_iota(jnp.int32, (n_w, LANES), 1) // NSA_DH == par
    o_w = jnp.dot(e_w.astype(BF16), jnp.where(own_w, vw_ref[pl.ds(w0, n_w), :], 1.0), preferred_element_type=F32)
    o_w = o_w / jnp.maximum(pltpu.roll(o_w, NSA_DH, 1), 1e-30)

    gates = _sigmoid(gp_ref[...])
    gate = lambda br: jnp.concatenate(
        [jnp.broadcast_to(gates[:, 3 * g + br:3 * g + br + 1], (qb, LANES)) for g in range(grp)], axis=0)
    o = gate(0) * o_c + gate(1) * o_s + gate(2) * o_w
    half = lax.broadcasted_iota(jnp.int32, (rows, LANES), 1) // NSA_DH
    o = jnp.where(half == par, o, 0.0)
    o = o[:, :NSA_DH] + o[:, NSA_DH:]
    o_ref[...] = jnp.concatenate([o[g * qb:(g + 1) * qb] for g in range(grp)], axis=1).astype(o_ref.dtype)


def nsa_prompt_attention(qkv, gates_pre, ckv, kvb, *, batch):
    m = qkv.shape[0]
    l_seq = m // batch
    nq = l_seq // NSA_QBLOCK
    n_c = ckv.shape[1]
    qw = NSA_GROUP * NSA_DH
    pair = NSA_KVW // LANES
    kv_spec = lambda part: pl.BlockSpec((l_seq, LANES), lambda b, h, j: (b, part * pair + h // 2))
    return pl.pallas_call(
        _nsa_prompt_kernel,
        grid=(batch, NSA_KVH, nq),
        in_specs=[pl.BlockSpec((NSA_QBLOCK, qw), lambda b, h, j: (b * nq + j, h)),
                  pl.BlockSpec((NSA_QBLOCK, LANES), lambda b, h, j: (b * nq + j, h)),
                  pl.BlockSpec((1, n_c, LANES), lambda b, h, j: (b, 0, h // 2)),
                  pl.BlockSpec((1, n_c, LANES), lambda b, h, j: (b, 0, pair + h // 2)),
                  kv_spec(0), kv_spec(1), kv_spec(2), kv_spec(3)],
        out_specs=pl.BlockSpec((NSA_QBLOCK, qw), lambda b, h, j: (b * nq + j, h)),
        out_shape=jax.ShapeDtypeStruct((m, NSA_HEADS * NSA_DH), BF16),
        compiler_params=_params("parallel", "parallel", "arbitrary"),
        name="nsa_prompt_attention",
    )(qkv, gates_pre, ckv, ckv, kvb, kvb, kvb, kvb)


def _paged_compress_kernel(pt_ref, *refs, n_pages, page):
    del pt_ref
    pages = refs[:n_pages]
    pos_ref, w1_ref, w2_ref, even_ref, odd_ref = refs[n_pages:n_pages + 5]
    rows = refs[n_pages + 5:]
    tiles = NSA_KVW // LANES
    for p in range(n_pages):
        for i, r in enumerate(rows):
            r[p * page:(p + 1) * page, :] = pages[p][0, i // tiles, (i % tiles) * LANES:(i % tiles + 1) * LANES, :].T
    n_blk = n_pages * page // CMP_BLOCK
    full = _compress_rows(rows, pos_ref, w1_ref, w2_ref, n_blk)
    even_ref[0] = full[:n_blk // 2]
    odd_ref[0] = full[n_blk // 2:]


def _page_specs(n_pages, page):
    return [pl.BlockSpec((1, 2, NSA_KVW, page), lambda b, pt, p=p: (pt[b, p], 0, 0, 0)) for p in range(n_pages)]


def nsa_compress_paged(cache_t, page_table, pos4, w1, w2):
    n_seq, n_pages = page_table.shape
    page = cache_t.shape[3]
    half = n_pages * page // CMP_BLOCK // 2
    const = lambda shape: pl.BlockSpec(shape, lambda b, pt: (0,) * len(shape))
    out = jax.ShapeDtypeStruct((n_seq, half, NSA_ROW), F32)
    return pl.pallas_call(
        functools.partial(_paged_compress_kernel, n_pages=n_pages, page=page),
        grid_spec=pltpu.PrefetchScalarGridSpec(
            num_scalar_prefetch=1,
            grid=(n_seq,),
            in_specs=_page_specs(n_pages, page)
            + [const((2, CMP_BLOCK, NSA_KVW)), const((2, CMP_BLOCK * NSA_DH, CMP_HID)), const((2, CMP_HID, NSA_DH))],
            out_specs=[pl.BlockSpec((1, half, NSA_ROW), lambda b, pt: (b, 0, 0))] * 2,
            scratch_shapes=[pltpu.VMEM((n_pages * page, LANES), F32)] * (NSA_ROW // LANES)),
        out_shape=[out, out],
        compiler_params=_params("arbitrary"),
        name="nsa_compress_paged",
    )(page_table, *([cache_t] * n_pages), pos4, w1, w2)


def _softmax_with_new_key(s, ok, s_new, new_ok):
    s = jnp.where(ok, s, MASK_NEG)
    s_new = jnp.where(new_ok, s_new, MASK_NEG)
    m = jnp.maximum(jnp.max(s, axis=-1, keepdims=True), s_new)
    e = jnp.where(ok, jnp.exp(s - m), 0.0)
    e_new = jnp.where(new_ok, jnp.exp(s_new - m), 0.0)
    total = jnp.sum(e, axis=-1, keepdims=True) + e_new
    return e, e_new, 1.0 / jnp.maximum(total, 1e-30)


def _nsa_step_kernel(pt_ref, *refs, n_pages, page, past):
    del pt_ref
    pages = refs[:n_pages]
    (q_ref, gp_ref, cke_ref, cko_ref, kvs_new_ref, swa_ref, kvw_new_ref,
     o_ref, swa_out_ref, kt_ref, vt_ref) = refs[n_pages:]
    nh, kw = NSA_HEADS, NSA_KVW
    t = past
    own = (lax.broadcasted_iota(jnp.int32, (nh, kw), 1) // NSA_DH
           == lax.broadcasted_iota(jnp.int32, (nh, kw), 0) // NSA_GROUP)
    q = q_ref[0] * (NSA_DH ** -0.5)
    qp = jnp.where(own, jnp.concatenate([q] * NSA_KVH, axis=1), 0.0)
    qpb = qp.astype(BF16)

    half = cke_ref.shape[1]
    n_c = 2 * half
    ce, co = cke_ref[0], cko_ref[0]
    ck = jnp.concatenate([ce[:, :kw], co[:, :kw]], axis=0).astype(BF16)
    cv = jnp.concatenate([ce[:, kw:], co[:, kw:]], axis=0).astype(BF16)
    nn = lax.broadcasted_iota(jnp.int32, (1, n_c), 1)
    c_blk = jnp.where(nn < half, 2 * nn, 2 * (nn - half) + 1)
    p_c = _masked_softmax(_dot_nt(qpb, ck), (c_blk * CMP_BLOCK + CMP_BLOCK - 1) <= t)
    o_c = jnp.dot(p_c.astype(BF16), cv, preferred_element_type=F32)
    same_group = (lax.broadcasted_iota(jnp.int32, (nh, nh), 0) // NSA_GROUP
                  == lax.broadcasted_iota(jnp.int32, (nh, nh), 1) // NSA_GROUP).astype(F32)
    imp = lax.dot_general(p_c, same_group, (((0,), (0,)), ((), ())), preferred_element_type=F32, precision=HIGHEST)
    imp = imp[:half] + imp[half:]
    imp = jnp.concatenate([imp, jnp.zeros((LANES - half, nh), F32)], axis=0)
    blk = lax.broadcasted_iota(jnp.int32, (LANES, nh), 0)
    cur = t // SLC_BLOCK
    forced = (blk == 0) | (blk == cur) | (blk == cur - 1)
    sel_t = _top_blocks(jnp.where(forced, SEL_BIG, jnp.where(blk <= cur, imp, -SEL_BIG)), SLC_TOPK, 0)

    for p in range(n_pages):
        kt_ref[:, p * page:(p + 1) * page] = pages[p][0, 0].astype(BF16)
        vt_ref[:, p * page:(p + 1) * page] = pages[p][0, 1].astype(BF16)
    n_keys = n_pages * page
    pos = lax.broadcasted_iota(jnp.int32, (LANES, n_keys + LANES), 1)
    expand = (lax.broadcasted_iota(jnp.int32, (LANES, n_keys + LANES), 0)
              == jnp.minimum(pos, t) // SLC_BLOCK).astype(BF16)
    chosen = _dot_tn(sel_t.astype(BF16), expand) > 0.5
    ok = chosen[:, :n_keys] & (lax.broadcasted_iota(jnp.int32, (1, n_keys), 1) <= t)
    kv_new = kvs_new_ref[0]
    s_new = jnp.sum(qp * kv_new[:, :kw], axis=-1, keepdims=True)
    scores = jnp.dot(qpb, kt_ref[...], preferred_element_type=F32)
    e, e_new, inv = _softmax_with_new_key(scores, ok, s_new, chosen[:, n_keys:n_keys + 1])
    o_s = (_dot_nt(e.astype(BF16), vt_ref[...]) + e_new * kv_new[:, kw:]) * inv

    n_w = swa_ref.shape[3]
    dist = t - (t - n_w + lax.broadcasted_iota(jnp.int32, (1, n_w), 1))
    w_ok = (dist >= 0) & (dist <= WINDOW) & (t - dist >= 0)
    kvw_new = kvw_new_ref[0]
    s_wn = jnp.sum(qp * kvw_new[:, :kw], axis=-1, keepdims=True)
    new_ok = lax.broadcasted_iota(jnp.int32, (nh, 1), 0) >= 0
    scores = jnp.dot(qpb, swa_ref[0, 0].astype(BF16), preferred_element_type=F32)
    e, e_new, inv = _softmax_with_new_key(scores, w_ok, s_wn, new_ok)
    o_w = (_dot_nt(e.astype(BF16), swa_ref[0, 1].astype(BF16)) + e_new * kvw_new[:, kw:]) * inv

    g = _sigmoid(gp_ref[0])
    o = jnp.where(own, g[:, 0:1] * o_c + g[:, 1:2] * o_s + g[:, 2:3] * o_w, 0.0)
    o_h = o[:, :NSA_DH]
    for h in range(1, NSA_KVH):
        o_h = o_h + o[:, h * NSA_DH:(h + 1) * NSA_DH]
    o_ref[0] = o_h
    last = lax.broadcasted_iota(jnp.int32, (kw, n_w), 1) == n_w - 1
    diag = lax.broadcasted_iota(jnp.int32, (kw, kw), 0) == lax.broadcasted_iota(jnp.int32, (kw, kw), 1)
    for c in range(2):
        column = jnp.sum(jnp.where(diag, kvw_new[:, c * kw:(c + 1) * kw], 0.0), axis=-1, keepdims=True)
        swa_out_ref[0, c] = jnp.where(last, column, pltpu.roll(swa_ref[0, c], n_w - 1, 1))


def nsa_step_attention(q3, gates3, cke, cko, cache_slc_t, page_table, kvs_new, swa_t, kvw_new, *, past):
    n_seq, n_pages = page_table.shape
    page = cache_slc_t.shape[3]
    seq = lambda a: pl.BlockSpec((1,) + a.shape[1:], lambda b, pt: (b,) + (0,) * (a.ndim - 1))
    return pl.pallas_call(
        functools.partial(_nsa_step_kernel, n_pages=n_pages, page=page, past=past),
        grid_spec=pltpu.PrefetchScalarGridSpec(
            num_scalar_prefetch=1,
            grid=(n_seq,),
            in_specs=_page_specs(n_pages, page) + [seq(a) for a in (q3, gates3, cke, cko, kvs_new, swa_t, kvw_new)],
            out_specs=[seq(q3), seq(swa_t)],
            scratch_shapes=[pltpu.VMEM((NSA_KVW, n_pages * page), BF16)] * 2),
        out_shape=[jax.ShapeDtypeStruct(q3.shape, F32), jax.ShapeDtypeStruct(swa_t.shape, F32)],
        compiler_params=_params("arbitrary"),
        name="nsa_step_attention",
    )(page_table, *([cache_slc_t] * n_pages), q3, gates3, cke, cko, kvs_new, swa_t, kvw_new)


PROMPT_TM = 1024
FFN_TM = 512
HG_TL = 512
RG_TL = 256
FFN_TF = 1408
CMP_TILE_BLOCKS = 64
MOE_TB = 896
MOE_TS = 256


def _ffn_layer(x, w_gu, w_down, g, b, *, tm):
    return ffn_residual_ln(x, w_gu.astype(BF16), w_down.astype(BF16), g, b, tm=tm, tf=FFN_TF)


def _moe_layer(x, w_router, w_gu_t, w_down_t, slot, g, b):
    m = x.shape[0]
    tb = min(MOE_TB, m)
    gates, slots = moe_route(x, w_router.T, tb=tb)
    n_e = gates.shape[0]
    counts = (jnp.max(slots.reshape(n_e, -1, tb), axis=-1) + 1).T.reshape(-1)
    return moe_residual_ln(x, gates, slots, counts, w_gu_t, w_down_t, slot, g, b,
                           tb=tb, ts=min(MOE_TS, tb), tf=FFN_TF)


def kernel(x_prompt, x_sample, state_hgrn, state_rglru_conv, state_rglru_h, cache_nsa_cmp, cache_nsa_slc, cache_nsa_swa, page_table, ln_g, ln_b, hgrn_w_in, hgrn_lb, hgrn_norm_g, hgrn_w_out, rg_w_in, rg_conv_w, rg_conv_b, rg_w_gate, rg_b_gate, rg_lambda, rg_w_out, nsa_w_in, nsa_cmp_pos, nsa_cmp_w1, nsa_cmp_w2, nsa_w_out, ffn_w_gu, ffn_w_down, moe_router, moe_w_gu, moe_w_down):
    bp, l_seq, d = x_prompt.shape
    n_seq = x_sample.shape[0]
    past = page_table.shape[1] * cache_nsa_cmp.shape[2]
    xp = x_prompt.reshape(bp * l_seq, d)
    xs = x_sample.reshape(n_seq, d)
    tm_p, tm_s = PROMPT_TM, n_seq
    kv_shape = (2, NSA_KVH, NSA_DH)
    moe_w_gu_t = jnp.swapaxes(moe_w_gu, 2, 3).astype(BF16)
    moe_w_down_t = jnp.swapaxes(moe_w_down, 2, 3).astype(BF16)
    hg_p, hg_s, rc_p, rc_s, rh_p, rh_s = [], [], [], [], [], []
    cm_p, cm_s, sl_p, sl_s, sw_p, sw_s = [], [], [], [], [], []
    for layer in range(DEPTH):
        kind, slot = layer % 3, layer // 3
        g0, b0 = ln_g[layer, 0], ln_b[layer, 0]
        if kind == 0:
            w_in, w_out = hgrn_w_in[slot].astype(BF16), hgrn_w_out[slot].astype(BF16)
            op, st_p = hgrn_prompt(matmul(xp, w_in, tm=tm_p), hgrn_lb, hgrn_norm_g[slot],
                                   layer=layer, batch=bp, tl=HG_TL)
            os_, st_s = hgrn_step(matmul(xs, w_in, tm=tm_s), state_hgrn, slot, hgrn_lb, hgrn_norm_g[slot],
                                  layer=layer, sb=8)
            hg_p.append(st_p)
            hg_s.append(st_s)
        elif kind == 1:
            w_in, w_out = rg_w_in[slot].astype(BF16), rg_w_out[slot].astype(BF16)
            rg_w = (rg_conv_w[slot], rg_conv_b[slot], rg_w_gate[slot].astype(BF16), rg_b_gate[slot], rg_lambda[slot])
            op, cb_p, h_p = rglru_prompt(matmul(xp, w_in, tm=tm_p), *rg_w, batch=bp, tl=RG_TL)
            os_, cb_s, h_s = rglru_step(matmul(xs, w_in, tm=tm_s), state_rglru_conv[slot], state_rglru_h[slot], *rg_w)
            rc_p.append(cb_p)
            rc_s.append(cb_s)
            rh_p.append(h_p[:, 0])
            rh_s.append(h_s)
        else:
            w_qkv, w_kvb, w_g, w_kv_t, pos4, w1, w2 = nsa_prepare_weights(
                nsa_w_in[slot], nsa_cmp_pos[slot], nsa_cmp_w1[slot], nsa_cmp_w2[slot])
            w_out = nsa_w_out[slot].astype(BF16)
            hq = NSA_HEADS * NSA_DH
            qkv = matmul(xp, w_qkv[:, :hq + NSA_ROW], tm=tm_p)
            kvb = matmul(xp, w_kvb, tm=tm_p, out_dtype=BF16)
            gates_pre = matmul(xp, w_g, tm=tm_p)
            cke, cko = nsa_compress(qkv, hq // NSA_ROW, pos4, w1, w2, batch=bp, n_blk=CMP_TILE_BLOCKS)
            op = nsa_prompt_attention(qkv, gates_pre, jnp.concatenate([cke, cko], axis=1), kvb, batch=bp)
            as_rows = lambda a: a.reshape(bp, *kv_shape, -1).transpose(0, 4, 1, 2, 3)
            kv_c, kv_s, kv_w = matmul_t(xp, w_kv_t, batch=bp, n_out=3, tm=tm_p)
            cm_p.append(as_rows(kv_c))
            sl_p.append(as_rows(kv_s))
            sw_p.append(as_rows(kv_w[:, :, l_seq - min(WINDOW, l_seq):]))
            qkv = matmul(xs, w_qkv, tm=tm_s)
            gates_pre = matmul(xs, w_g, tm=tm_s)
            rows_last = lambda c: c.transpose(0, 2, 3, 4, 1).reshape(c.shape[0], 2, NSA_KVW, c.shape[1])
            cke, cko = nsa_compress_paged(rows_last(cache_nsa_cmp[slot]), page_table, pos4, w1, w2)
            gates3 = gates_pre.reshape(n_seq, NSA_KVH, LANES)[:, :, :3 * NSA_GROUP].reshape(n_seq, NSA_HEADS, 3)
            kv = qkv[:, hq:].reshape(n_seq, 3, 1, NSA_ROW)
            o3, swa_new = nsa_step_attention(qkv[:, :hq].reshape(n_seq, NSA_HEADS, NSA_DH), gates3, cke, cko,
                                             rows_last(cache_nsa_slc[slot]), page_table, kv[:, 1],
                                             rows_last(cache_nsa_swa[slot]), kv[:, 2], past=past)
            os_ = o3.reshape(n_seq, hq)
            cm_s.append(kv[:, 0].reshape(n_seq, 1, *kv_shape))
            sl_s.append(kv[:, 1].reshape(n_seq, 1, *kv_shape))
            sw_s.append(swa_new.reshape(n_seq, *kv_shape, -1).transpose(0, 4, 1, 2, 3))
        xp = matmul_residual_ln(op, w_out, xp, g0, b0, tm=tm_p)
        xs = matmul_residual_ln(os_, w_out, xs, g0, b0, tm=tm_s)
        g1, b1 = ln_g[layer, 1], ln_b[layer, 1]
        if layer % 2 == 0:
            fw = (ffn_w_gu[layer // 2], ffn_w_down[layer // 2])
            xp = _ffn_layer(xp, *fw, g1, b1, tm=FFN_TM)
            xs = _ffn_layer(xs, *fw, g1, b1, tm=tm_s)
        else:
            fw = (moe_router[layer // 2], moe_w_gu_t, moe_w_down_t, layer // 2)
            xp = _moe_layer(xp, *fw, g1, b1)
            xs = _moe_layer(xs, *fw, g1, b1)
    return (xp.reshape(bp, l_seq, d), xs.reshape(n_seq, 1, d),
            jnp.stack(hg_p), jnp.stack(hg_s), jnp.stack(rc_p), jnp.stack(rc_s), jnp.stack(rh_p), jnp.stack(rh_s),
            jnp.stack(cm_p), jnp.stack(cm_s), jnp.stack(sl_p), jnp.stack(sl_s), jnp.stack(sw_p), jnp.stack(sw_s))
```

```python
import functools

import jax
import jax.numpy as jnp
from jax import lax
from jax.experimental import pallas as pl
from jax.experimental.pallas import tpu as pltpu

F32 = jnp.float32
BF16 = jnp.bfloat16
HIGHEST = lax.Precision.HIGHEST

D_MODEL = 1024
DEPTH = 4
ALPHA = (2 * DEPTH) ** 0.25
LN_EPS = 1e-5
RMS_EPS = 1e-6
HG_HEADS = 8
HG_DK = 128
HG_CHUNK = 64
HG_SUB = 16
SUBLANES = 8
HG_HEADS_PER_STEP = 8
LB_FLOOR = 1e-30
RG_WIDTH = 1280
RG_BLOCKS = 10
RG_BS = 128
RG_CONV = 4
RG_C = 8.0
D_FF = 2816
N_EXPERTS = 8

VMEM_LIMIT_BYTES = 56 * 1024 * 1024


def _params(*sem):
    return pltpu.CompilerParams(dimension_semantics=sem, vmem_limit_bytes=VMEM_LIMIT_BYTES)


def _layer_norm(v, g, b):
    mu = jnp.mean(v, axis=-1, keepdims=True)
    vc = v - mu
    var = jnp.mean(vc * vc, axis=-1, keepdims=True)
    return vc * lax.rsqrt(var + LN_EPS) * g + b


def _sigmoid(x):
    return 1.0 / (1.0 + jnp.exp(-x))


def _silu(x):
    return x * _sigmoid(x)


def _mm_kernel(x_ref, w_ref, o_ref, xb_ref):
    @pl.when(pl.program_id(1) == 0)
    def _():
        xb_ref[...] = x_ref[...].astype(BF16)

    o_ref[...] = jnp.dot(xb_ref[...], w_ref[...], preferred_element_type=F32).astype(o_ref.dtype)


MAX_COL_TILE = 1280


def _col_tile(n):
    return max(t for t in range(128, min(n, MAX_COL_TILE) + 1, 128) if n % t == 0)


def matmul(x, w, *, tm, out_dtype=F32):
    m, k = x.shape
    n = w.shape[1]
    tn = _col_tile(n)
    return pl.pallas_call(
        _mm_kernel,
        grid=(m // tm, n // tn),
        in_specs=[pl.BlockSpec((tm, k), lambda i, j: (i, 0)),
                  pl.BlockSpec((k, tn), lambda i, j: (0, j))],
        out_specs=pl.BlockSpec((tm, tn), lambda i, j: (i, j)),
        out_shape=jax.ShapeDtypeStruct((m, n), out_dtype),
        scratch_shapes=[pltpu.VMEM((tm, k), BF16)],
        compiler_params=_params("parallel", "arbitrary"),
        name="matmul",
    )(x, w)


def _mm_t_kernel(x_ref, wt_ref, *o_refs):
    xb = x_ref[...].astype(BF16)
    n = o_refs[0].shape[1]
    for r, o_ref in enumerate(o_refs):
        o_ref[0] = _dot_nt(wt_ref[r * n:(r + 1) * n, :], xb)


def matmul_t(x, w_t, *, batch, n_out, tm):
    m, k = x.shape
    n = w_t.shape[0] // n_out
    l_seq = m // batch
    n_t = l_seq // tm
    return pl.pallas_call(
        _mm_t_kernel,
        grid=(batch, n_t),
        in_specs=[pl.BlockSpec((tm, k), lambda b, i: (b * n_t + i, 0)),
                  pl.BlockSpec((n_out * n, k), lambda b, i: (0, 0))],
        out_specs=[pl.BlockSpec((1, n, tm), lambda b, i: (b, 0, i))] * n_out,
        out_shape=[jax.ShapeDtypeStruct((batch, n, l_seq), F32)] * n_out,
        compiler_params=_params("parallel", "parallel"),
        name="matmul_t",
    )(x, w_t)


def _mm_res_ln_kernel(a_ref, w_ref, r_ref, g_ref, b_ref, o_ref):
    m = jnp.dot(a_ref[...].astype(BF16), w_ref[...], preferred_element_type=F32)
    o_ref[...] = _layer_norm(ALPHA * r_ref[...] + m, g_ref[...], b_ref[...])


def matmul_residual_ln(a, w, res, g, b, *, tm):
    m, k = a.shape
    d = w.shape[1]
    return pl.pallas_call(
        _mm_res_ln_kernel,
        grid=(m // tm,),
        in_specs=[pl.BlockSpec((tm, k), lambda i: (i, 0)),
                  pl.BlockSpec((k, d), lambda i: (0, 0)),
                  pl.BlockSpec((tm, d), lambda i: (i, 0)),
                  pl.BlockSpec((1, d), lambda i: (0, 0)),
                  pl.BlockSpec((1, d), lambda i: (0, 0))],
        out_specs=pl.BlockSpec((tm, d), lambda i: (i, 0)),
        out_shape=jax.ShapeDtypeStruct((m, d), F32),
        compiler_params=_params("parallel"),
        name="matmul_residual_ln",
    )(a, w, res, g.reshape(1, d), b.reshape(1, d))


def _ffn_kernel(x_ref, wg_ref, wu_ref, wd_ref, g_ref, b_ref, o_ref, xb_ref, acc_ref, *, n_f):
    f = pl.program_id(1)

    @pl.when(f == 0)
    def _():
        xb_ref[...] = x_ref[...].astype(BF16)

    xb = xb_ref[...]
    h = jnp.dot(xb, wg_ref[...], preferred_element_type=F32)
    u = jnp.dot(xb, wu_ref[...], preferred_element_type=F32)
    part = jnp.dot((_silu(h) * u).astype(BF16), wd_ref[...], preferred_element_type=F32)

    @pl.when(f == 0)
    def _():
        acc_ref[...] = part

    @pl.when(f > 0)
    def _():
        acc_ref[...] += part

    @pl.when(f == n_f - 1)
    def _():
        o_ref[...] = _layer_norm(ALPHA * x_ref[...] + acc_ref[...], g_ref[...], b_ref[...])


def ffn_residual_ln(x, w_gu, w_down, g, b, *, tm, tf):
    m, d = x.shape
    ff = w_down.shape[0]
    n_f = ff // tf
    return pl.pallas_call(
        functools.partial(_ffn_kernel, n_f=n_f),
        grid=(m // tm, n_f),
        in_specs=[pl.BlockSpec((tm, d), lambda i, f: (i, 0)),
                  pl.BlockSpec((d, tf), lambda i, f: (0, f)),
                  pl.BlockSpec((d, tf), lambda i, f: (0, n_f + f)),
                  pl.BlockSpec((tf, d), lambda i, f: (f, 0)),
                  pl.BlockSpec((1, d), lambda i, f: (0, 0)),
                  pl.BlockSpec((1, d), lambda i, f: (0, 0))],
        out_specs=pl.BlockSpec((tm, d), lambda i, f: (i, 0)),
        out_shape=jax.ShapeDtypeStruct((m, d), F32),
        scratch_shapes=[pltpu.VMEM((tm, d), BF16), pltpu.VMEM((tm, d), F32)],
        compiler_params=_params("parallel", "arbitrary"),
        name="ffn_residual_ln",
    )(x, w_gu, w_gu, w_down, g.reshape(1, d), b.reshape(1, d))


def _router_kernel(x_ref, wt_ref, gate_ref, pos_ref, *, n_tokens):
    logits = lax.dot_general(wt_ref[...], x_ref[...], (((1,), (1,)), ((), ())),
                             preferred_element_type=F32, precision=HIGHEST)
    n_e, tm = logits.shape
    expert = lax.broadcasted_iota(jnp.int32, logits.shape, 0)
    m1 = jnp.max(logits, axis=0, keepdims=True)
    i1 = jnp.min(jnp.where(logits == m1, expert, n_e), axis=0, keepdims=True)
    rest = jnp.where(expert == i1, -jnp.inf, logits)
    m2 = jnp.max(rest, axis=0, keepdims=True)
    i2 = jnp.min(jnp.where(rest == m2, expert, n_e), axis=0, keepdims=True)
    e2 = jnp.exp(m2 - m1)
    denom = 1.0 + e2
    first = expert == i1
    in_range = pl.program_id(0) * tm + lax.broadcasted_iota(jnp.int32, logits.shape, 1) < n_tokens
    routed = (first | (expert == i2)) & in_range
    gate_ref[...] = jnp.where(first & in_range, 1.0 / denom, jnp.where(routed, e2 / denom, 0.0))
    upper = (lax.broadcasted_iota(jnp.int32, (tm, tm), 0) <= lax.broadcasted_iota(jnp.int32, (tm, tm), 1)).astype(BF16)
    upto = jnp.dot(routed.astype(BF16), upper, preferred_element_type=F32)
    pos_ref[...] = jnp.where(routed, upto - 1.0, -1.0).astype(jnp.int32)


def moe_route(x, w_router_t, *, tb):
    m, d = x.shape
    n_e = w_router_t.shape[0]
    n_blk = pl.cdiv(m, tb)
    return pl.pallas_call(
        functools.partial(_router_kernel, n_tokens=m),
        grid=(n_blk,),
        in_specs=[pl.BlockSpec((tb, d), lambda i: (i, 0)),
                  pl.BlockSpec((n_e, d), lambda i: (0, 0))],
        out_specs=[pl.BlockSpec((n_e, tb), lambda i: (0, i))] * 2,
        out_shape=[jax.ShapeDtypeStruct((n_e, n_blk * tb), F32), jax.ShapeDtypeStruct((n_e, n_blk * tb), jnp.int32)],
        compiler_params=_params("parallel"),
        name="moe_route",
    )(x, w_router_t)


def _moe_kernel(cnt_ref, x_ref, gate_ref, slot_ref, wg_ref, wu_ref, wd_ref, g_ref, b_ref, o_ref,
                xt_ref, xc_ref, acc_ref, yt_ref, *, n_e, n_f, ts, n_tokens):
    i, e, f = pl.program_id(0), pl.program_id(1), pl.program_id(2)
    tb = x_ref.shape[0]
    n_sub = (cnt_ref[i * n_e + e] + ts - 1) // ts
    row = lax.broadcasted_iota(jnp.int32, (ts, tb), 0)

    def one_hot(sub):
        return (slot_ref[pl.ds(e, 1), :] == row + sub * ts).astype(BF16)

    @pl.when((e == 0) & (f == 0))
    def _():
        in_range = i * tb + lax.broadcasted_iota(jnp.int32, (tb, 1), 0) < n_tokens
        xt_ref[...] = jnp.where(in_range, x_ref[...], 0.0).T.astype(BF16)
        yt_ref[...] = jnp.zeros_like(yt_ref)

    tf, d = wg_ref.shape[1], wd_ref.shape[1]
    halves = lambda n: (slice(0, n // 2), slice(n // 2, n))

    @pl.when(f == 0)
    def _():
        def gather(sub, c):
            pick = one_hot(sub)
            for rows in halves(d):
                xc_ref[sub, rows, :] = _dot_nt(xt_ref[rows, :], pick).astype(BF16)
            return c

        lax.fori_loop(0, n_sub, gather, 0)

    def expert_ffn(sub, c):
        xc = xc_ref[sub]
        act = []
        for rows in halves(tf):
            h = jnp.dot(wg_ref[0, rows, :], xc, preferred_element_type=F32)
            u = jnp.dot(wu_ref[0, rows, :], xc, preferred_element_type=F32)
            act.append((_silu(h) * u).astype(BF16))
        act = jnp.concatenate(act, axis=0)
        part = jnp.concatenate([jnp.dot(wd_ref[0, rows, :], act, preferred_element_type=F32) for rows in halves(d)], axis=0)

        @pl.when(f == 0)
        def _():
            acc_ref[sub] = part

        @pl.when(f > 0)
        def _():
            acc_ref[sub] += part

        return c

    lax.fori_loop(0, n_sub, expert_ffn, 0)

    @pl.when(f == n_f - 1)
    def _():
        def scatter(sub, c):
            pick = one_hot(sub)
            gate = gate_ref[pl.ds(e, 1), :]
            for rows in halves(d):
                back = jnp.dot(acc_ref[sub, rows, :].astype(BF16), pick, preferred_element_type=F32)
                yt_ref[rows, :] += gate * back
            return c

        lax.fori_loop(0, n_sub, scatter, 0)

    @pl.when((e == n_e - 1) & (f == n_f - 1))
    def _():
        o_ref[...] = _layer_norm(ALPHA * x_ref[...] + yt_ref[...].T, g_ref[...], b_ref[...])


def moe_residual_ln(x, gates, slots, counts, w_gu_t, w_down_t, slot, g, b, *, tb, ts, tf):
    m, d = x.shape
    _, n_e, ff2, _ = w_gu_t.shape
    n_f = ff2 // 2 // tf
    n_sub = pl.cdiv(tb, ts)
    return pl.pallas_call(
        functools.partial(_moe_kernel, n_e=n_e, n_f=n_f, ts=ts, n_tokens=m),
        grid_spec=pltpu.PrefetchScalarGridSpec(
            num_scalar_prefetch=1,
            grid=(pl.cdiv(m, tb), n_e, n_f),
            in_specs=[pl.BlockSpec((tb, d), lambda i, e, f, c: (i, 0)),
                      pl.BlockSpec((n_e, tb), lambda i, e, f, c: (0, i)),
                      pl.BlockSpec((n_e, tb), lambda i, e, f, c: (0, i)),
                      pl.BlockSpec((None, 1, tf, d), lambda i, e, f, c: (slot, e, f, 0)),
                      pl.BlockSpec((None, 1, tf, d), lambda i, e, f, c: (slot, e, n_f + f, 0)),
                      pl.BlockSpec((None, 1, d, tf), lambda i, e, f, c: (slot, e, 0, f)),
                      pl.BlockSpec((1, d), lambda i, e, f, c: (0, 0)),
                      pl.BlockSpec((1, d), lambda i, e, f, c: (0, 0))],
            out_specs=pl.BlockSpec((tb, d), lambda i, e, f, c: (i, 0)),
            scratch_shapes=[pltpu.VMEM((d, tb), BF16), pltpu.VMEM((n_sub, d, ts), BF16),
                            pltpu.VMEM((n_sub, d, ts), F32), pltpu.VMEM((d, tb), F32)]),
        out_shape=jax.ShapeDtypeStruct((m, d), F32),
        compiler_params=_params("parallel", "arbitrary", "arbitrary"),
        name="moe_residual_ln",
    )(counts, x, gates, slots, w_gu_t, w_gu_t, w_down_t, g.reshape(1, d), b.reshape(1, d))


def _hgrn_lower_bound(lbp, layer):
    m = jnp.max(lbp, axis=0, keepdims=True)
    e = jnp.exp(lbp - m)
    p = e / jnp.sum(e, axis=0, keepdims=True)
    c = p[0:1]
    for i in range(1, layer + 1):
        c = c + p[i:i + 1]
    return c - p[0:1]


def _hgrn_gates(q, fx, lb):
    log_lb = jnp.log(jnp.maximum(lb, LB_FLOOR))
    log1m = jnp.log1p(-lb)
    log_sig = -(jnp.maximum(-fx, 0.0) + jnp.log1p(jnp.exp(-jnp.abs(fx))))
    b2 = log1m + log_sig
    log_f = jnp.maximum(log_lb, b2) + jnp.log1p(jnp.exp(-jnp.abs(log_lb - b2)))
    kk = (1.0 - lb) * _sigmoid(-fx) - (jnp.maximum(lb, LB_FLOOR) - lb)
    return _silu(q), log_f, kk


def _hgrn_out_norm(o, g, ng):
    o = o * lax.rsqrt(jnp.mean(o * o, axis=-1, keepdims=True) + RMS_EPS)
    return o * ng * _silu(g)


def _dot_nt(a, b):
    return lax.dot_general(a, b, (((1,), (1,)), ((), ())), preferred_element_type=F32)


def _dot_tn(a, b):
    return lax.dot_general(a, b, (((0,), (0,)), ((), ())), preferred_element_type=F32)


def _gla_kernel(q_ref, f_ref, i_ref, g_ref, lbp_ref, ng_ref, o_ref, s_out_ref, st_ref, *, layer, n_t, tl, hp):
    t = pl.program_id(2)
    c_len, sub = HG_CHUNK, HG_SUB
    n_sub = c_len // sub
    n_stack = sub * (n_sub * (n_sub - 1) // 2)

    @pl.when(t == 0)
    def _():
        st_ref[...] = jnp.zeros_like(st_ref)

    lb = _hgrn_lower_bound(lbp_ref[...], layer)
    ng = ng_ref[...]
    row = lax.broadcasted_iota(jnp.int32, (c_len, c_len), 0)
    col = lax.broadcasted_iota(jnp.int32, (c_len, c_len), 1)
    tri = (col <= row).astype(F32)
    srow = lax.broadcasted_iota(jnp.int32, (c_len, n_stack), 0) // sub
    scol = lax.broadcasted_iota(jnp.int32, (c_len, n_stack), 1)
    part = jnp.zeros((c_len, n_stack), jnp.int32)
    for p in range(1, n_sub):
        part = part + (scol >= sub * (p * (p - 1) // 2)).astype(jnp.int32)
    inter_mask = srow == part

    def one_head(hh, r):
        lanes = slice(hh * HG_DK, (hh + 1) * HG_DK)
        qs, log_f, kk = _hgrn_gates(q_ref[r, lanes], f_ref[r, lanes], lb[:, lanes])
        v = i_ref[r, lanes]
        cum = jnp.dot(tri, log_f, preferred_element_type=F32, precision=HIGHEST)
        last = cum[c_len - 1:c_len, :]
        st = st_ref[hh]
        o = _dot_nt((qs * jnp.exp(cum)).astype(BF16), st.astype(BF16))
        bounds = [cum[sub * a - 1:sub * a, :] for a in range(1, n_sub)]
        cq = jnp.concatenate([cum[0:sub]] + [jnp.broadcast_to(b, (sub, HG_DK)) for b in bounds], axis=0)
        q_rel = (qs * jnp.exp(cum - cq)).astype(BF16)
        k_st = jnp.concatenate([kk[0:sub * a] * jnp.exp(bounds[a - 1] - cum[0:sub * a]) for a in range(1, n_sub)], axis=0)
        v_st = jnp.concatenate([v[0:sub * a] for a in range(1, n_sub)], axis=0)
        att = jnp.where(inter_mask, _dot_nt(q_rel, k_st.astype(BF16)), 0.0)
        o = o + jnp.dot(att.astype(BF16), v_st.astype(BF16), preferred_element_type=F32)
        parts = []
        for a in range(n_sub):
            base = sub * a
            rest = o[base:base + sub]
            for lo in range(0, sub, SUBLANES):
                tail = slice(base + lo, base + sub)
                qa, ca = qs[tail], cum[tail]
                row_t = lo + lax.broadcasted_iota(jnp.int32, (sub - lo, 1), 0)
                for s in range(lo, lo + SUBLANES):
                    src = slice(base + s, base + s + 1)
                    w = jnp.sum(qa * (kk[src] * jnp.exp(ca - cum[src])), axis=-1, keepdims=True)
                    rest = rest + jnp.where(row_t >= s, w, 0.0) * v[src]
                parts.append(rest[:SUBLANES])
                rest = rest[SUBLANES:]
        o = jnp.concatenate(parts, axis=0)
        o_ref[r, lanes] = _hgrn_out_norm(o, g_ref[r, lanes], ng[:, lanes]).astype(o_ref.dtype)
        k_end = (kk * jnp.exp(last - cum)).astype(BF16)
        st_ref[hh] = st * jnp.exp(last) + _dot_tn(v.astype(BF16), k_end)

    def chunk(c, carry):
        r = pl.ds(pl.multiple_of(c * c_len, c_len), c_len)
        for hh in range(hp):
            one_head(hh, r)
        return carry

    lax.fori_loop(0, tl // c_len, chunk, 0)

    @pl.when(t == n_t - 1)
    def _():
        for hh in range(hp):
            s_out_ref[0, hh] = st_ref[hh].T


def hgrn_prompt(z, lb_param, norm_g, *, layer, batch, tl):
    m = z.shape[0]
    n_t = m // batch // tl
    h, dk = HG_HEADS, HG_DK
    hp = HG_HEADS_PER_STEP
    n_hg = h // hp
    zspec = lambda part: pl.BlockSpec((tl, hp * dk), lambda b, hh, t: (b * n_t + t, part * n_hg + hh))
    return pl.pallas_call(
        functools.partial(_gla_kernel, layer=layer, n_t=n_t, tl=tl, hp=hp),
        grid=(batch, n_hg, n_t),
        in_specs=[zspec(0), zspec(1), zspec(2), zspec(3),
                  pl.BlockSpec((DEPTH, hp * dk), lambda b, hh, t: (0, hh)),
                  pl.BlockSpec((1, hp * dk), lambda b, hh, t: (0, hh))],
        out_specs=[pl.BlockSpec((tl, hp * dk), lambda b, hh, t: (b * n_t + t, hh)),
                   pl.BlockSpec((1, hp, dk, dk), lambda b, hh, t: (b, hh, 0, 0))],
        out_shape=[jax.ShapeDtypeStruct((m, h * dk), BF16),
                   jax.ShapeDtypeStruct((batch, h, dk, dk), F32)],
        scratch_shapes=[pltpu.VMEM((hp, dk, dk), F32)],
        compiler_params=_params("parallel", "parallel", "arbitrary"),
        name="hgrn_prompt",
    )(z, z, z, z, lb_param, norm_g.reshape(1, h * dk))


def _hgrn_step_kernel(q_ref, f_ref, i_ref, g_ref, lbp_ref, ng_ref, s0_ref, o_ref, s_ref,
                      qt_ref, ft_ref, kt_ref, oacc_ref, *, layer, n_j, sb):
    j = pl.program_id(1)
    n_seq = q_ref.shape[0]

    def split3(x, ref):
        hi = x.astype(BF16)
        r1 = x - hi.astype(F32)
        mid = r1.astype(BF16)
        ref[0], ref[1], ref[2] = hi, mid, (r1 - mid.astype(F32)).astype(BF16)

    @pl.when(j == 0)
    def _():
        lb = _hgrn_lower_bound(lbp_ref[...], layer)
        qs, log_f, kk = _hgrn_gates(q_ref[...], f_ref[...], lb)
        split3(qs.T, qt_ref)
        split3(jnp.exp(log_f).T, ft_ref)
        split3(kk.T, kt_ref)

    seq = lax.broadcasted_iota(jnp.int32, (n_seq, HG_DK), 0)
    for i in range(sb):
        b = j * sb + i
        pick = (seq == b).astype(BF16)
        col = lambda ref: sum(jnp.dot(ref[p], pick, preferred_element_type=F32) for p in range(3))
        s_new = s0_ref[i, 0] * col(ft_ref) + col(kt_ref) * i_ref[pl.ds(b, 1), :]
        s_ref[i, 0] = s_new
        oacc_ref[pl.ds(b, 1), :] = jnp.sum(col(qt_ref) * s_new, axis=0, keepdims=True)

    @pl.when(j == n_j - 1)
    def _():
        o_ref[...] = _hgrn_out_norm(oacc_ref[...], g_ref[...], ng_ref[...]).astype(o_ref.dtype)


def hgrn_step(z, s0, slot, lb_param, norm_g, *, layer, sb):
    n_seq = z.shape[0]
    h, dk = HG_HEADS, HG_DK
    n_j = n_seq // sb
    zspec = lambda off: pl.BlockSpec((n_seq, dk), lambda hh, j: (0, off + hh))
    return pl.pallas_call(
        functools.partial(_hgrn_step_kernel, layer=layer, n_j=n_j, sb=sb),
        grid=(h, n_j),
        in_specs=[zspec(0), zspec(h), zspec(2 * h), zspec(3 * h),
                  pl.BlockSpec((DEPTH, dk), lambda hh, j: (0, hh)),
                  pl.BlockSpec((1, dk), lambda hh, j: (0, hh)),
                  pl.BlockSpec((None, sb, 1, dk, dk), lambda hh, j: (slot, j, hh, 0, 0))],
        out_specs=[pl.BlockSpec((n_seq, dk), lambda hh, j: (0, hh)),
                   pl.BlockSpec((sb, 1, dk, dk), lambda hh, j: (j, hh, 0, 0))],
        out_shape=[jax.ShapeDtypeStruct((n_seq, h * dk), BF16),
                   jax.ShapeDtypeStruct(s0.shape[1:], F32)],
        scratch_shapes=[pltpu.VMEM((3, dk, n_seq), BF16)] * 3 + [pltpu.VMEM((n_seq, dk), F32)],
        compiler_params=_params("parallel", "arbitrary"),
        name="hgrn_step",
    )(z, z, z, z, lb_param, norm_g.reshape(1, h * dk), s0)


def _gelu_tanh(x):
    return x * (0.5 * (1.0 + jnp.tanh(0.7978845608028654 * (x + 0.044715 * (x * x * x)))))


def _rg_decay_and_input(xc, wg_ref, bg_ref, lam_ref):
    xb = xc.astype(BF16)
    gates = []
    for gi in range(2):
        blocks = [jnp.dot(xb[:, n * RG_BS:(n + 1) * RG_BS], wg_ref[gi, n], preferred_element_type=F32)
                  for n in range(RG_BLOCKS)]
        gates.append(jnp.concatenate(blocks, axis=1) + bg_ref[gi:gi + 1, :])
    r = _sigmoid(gates[0])
    i = _sigmoid(gates[1])
    neg_lam = -lam_ref[...]
    softplus = jnp.maximum(neg_lam, 0.0) + jnp.log1p(jnp.exp(-jnp.abs(neg_lam)))
    log_a = -RG_C * r * softplus
    one_minus_a2 = -jnp.tanh(log_a) * (jnp.exp(2.0 * log_a) + 1.0)
    return jnp.exp(log_a), jnp.sqrt(one_minus_a2) * i * xc


def _rg_prompt_kernel(y_ref, x_ref, cw_ref, cb_ref, wg_ref, bg_ref, lam_ref,
                      o_ref, conv_out_ref, h_out_ref, xpad_ref, h_ref, *, n_t, tl):
    t = pl.program_id(1)
    halo = 8

    @pl.when(t == 0)
    def _():
        xpad_ref[0:halo, :] = jnp.zeros((halo, RG_WIDTH), F32)
        h_ref[...] = jnp.zeros_like(h_ref)

    x = x_ref[...]
    xpad_ref[halo:halo + tl, :] = x
    xc = cb_ref[...]
    for j in range(RG_CONV):
        off = halo - (RG_CONV - 1) + j
        xc = xc + xpad_ref[off:off + tl, :] * cw_ref[j:j + 1, :]
    xpad_ref[0:halo, :] = x[tl - halo:tl, :]
    a, u = _rg_decay_and_input(xc, wg_ref, bg_ref, lam_ref)
    row = lax.broadcasted_iota(jnp.int32, (tl, 1), 0) % SUBLANES
    s = 1
    while s < SUBLANES:
        keep = row >= s
        a_prev = jnp.where(keep, pltpu.roll(a, s, 0), 1.0)
        u_prev = jnp.where(keep, pltpu.roll(u, s, 0), 0.0)
        u = a * u_prev + u
        a = a * a_prev
        s *= 2
    carry = h_ref[...]
    tiles = []
    for r in range(0, tl, SUBLANES):
        tiles.append(a[r:r + SUBLANES] * carry + u[r:r + SUBLANES])
        carry = tiles[-1][SUBLANES - 1:SUBLANES]
    h = jnp.concatenate(tiles, axis=0)
    h_ref[...] = carry
    o_ref[...] = (_gelu_tanh(y_ref[...]) * h).astype(o_ref.dtype)

    @pl.when(t == n_t - 1)
    def _():
        conv_out_ref[0] = x[tl - (RG_CONV - 1):tl, :]
        h_out_ref[0] = h[tl - 1:tl, :]


def rglru_prompt(z, conv_w, conv_b, w_gate, b_gate, lam, *, batch, tl):
    m = z.shape[0]
    n_t = m // batch // tl
    w = RG_WIDTH
    const = lambda shape: pl.BlockSpec(shape, lambda b, t: (0,) * len(shape))
    return pl.pallas_call(
        functools.partial(_rg_prompt_kernel, n_t=n_t, tl=tl),
        grid=(batch, n_t),
        in_specs=[pl.BlockSpec((tl, w), lambda b, t: (b * n_t + t, 0)),
                  pl.BlockSpec((tl, w), lambda b, t: (b * n_t + t, 1)),
                  const((RG_CONV, w)), const((1, w)), const((2, RG_BLOCKS, RG_BS, RG_BS)), const((2, w)), const((1, w))],
        out_specs=[pl.BlockSpec((tl, w), lambda b, t: (b * n_t + t, 0)),
                   pl.BlockSpec((1, RG_CONV - 1, w), lambda b, t: (b, 0, 0)),
                   pl.BlockSpec((1, 1, w), lambda b, t: (b, 0, 0))],
        out_shape=[jax.ShapeDtypeStruct((m, w), BF16),
                   jax.ShapeDtypeStruct((batch, RG_CONV - 1, w), F32),
                   jax.ShapeDtypeStruct((batch, 1, w), F32)],
        scratch_shapes=[pltpu.VMEM((tl + 8, w), F32), pltpu.VMEM((1, w), F32)],
        compiler_params=_params("parallel", "arbitrary"),
        name="rglru_prompt",
    )(z, z, conv_w, conv_b.reshape(1, w), w_gate, b_gate, lam.reshape(1, w))


def _rg_step_kernel(y_ref, x_ref, buf_ref, h0_ref, cw_ref, cb_ref, wg_ref, bg_ref, lam_ref,
                    o_ref, conv_out_ref, h_out_ref):
    x = x_ref[...]
    xc = cb_ref[...]
    for j in range(RG_CONV - 1):
        xc = xc + buf_ref[:, j, :] * cw_ref[j:j + 1, :]
    xc = xc + x * cw_ref[RG_CONV - 1:RG_CONV, :]
    a, u = _rg_decay_and_input(xc, wg_ref, bg_ref, lam_ref)
    h = a * h0_ref[...] + u
    o_ref[...] = (_gelu_tanh(y_ref[...]) * h).astype(o_ref.dtype)
    h_out_ref[...] = h
    for j in range(RG_CONV - 2):
        conv_out_ref[:, j, :] = buf_ref[:, j + 1, :]
    conv_out_ref[:, RG_CONV - 2, :] = x


def rglru_step(z, conv_buf, h0, conv_w, conv_b, w_gate, b_gate, lam):
    n_seq = z.shape[0]
    w = RG_WIDTH
    const = lambda shape: pl.BlockSpec(shape, lambda i: (0,) * len(shape))
    return pl.pallas_call(
        _rg_step_kernel,
        grid=(1,),
        in_specs=[pl.BlockSpec((n_seq, w), lambda i: (0, 0)),
                  pl.BlockSpec((n_seq, w), lambda i: (0, 1)),
                  const((n_seq, RG_CONV - 1, w)), const((n_seq, w)),
                  const((RG_CONV, w)), const((1, w)), const((2, RG_BLOCKS, RG_BS, RG_BS)), const((2, w)), const((1, w))],
        out_specs=[const((n_seq, w)), const((n_seq, RG_CONV - 1, w)), const((n_seq, w))],
        out_shape=[jax.ShapeDtypeStruct((n_seq, w), BF16),
                   jax.ShapeDtypeStruct((n_seq, RG_CONV - 1, w), F32),
                   jax.ShapeDtypeStruct((n_seq, w), F32)],
        compiler_params=_params("arbitrary"),
        name="rglru_step",
    )(z, z, conv_buf, h0, conv_w, conv_b.reshape(1, w), w_gate, b_gate, lam.reshape(1, w))


NSA_HEADS = 16
NSA_KVH = 4
NSA_GROUP = 4
NSA_DH = 64
NSA_KVW = NSA_KVH * NSA_DH
NSA_ROW = 2 * NSA_KVW
CMP_BLOCK = 32
CMP_HID = 128
SLC_BLOCK = 64
SLC_TOPK = 16
WINDOW = 512
NSA_QBLOCK = 128
MASK_NEG = -1e30
SEL_BIG = 1e9
LANES = 128
SEL_KEYS = 1024
LOG2_E = 1.4426950408889634


def _masked_softmax(s, mask, exp=jnp.exp):
    s = jnp.where(mask, s, MASK_NEG)
    m = jnp.max(s, axis=-1, keepdims=True)
    e = jnp.where(mask, exp(s - m), 0.0)
    return e * (1.0 / jnp.maximum(jnp.sum(e, axis=-1, keepdims=True), 1e-30))


def _top_blocks(score, k, axis):
    pos = lax.broadcasted_iota(jnp.int32, score.shape, axis)
    n = score.shape[axis]

    def pick(_, carry):
        sc, sel = carry
        mx = jnp.max(sc, axis=axis, keepdims=True)
        idx = jnp.min(jnp.where(sc == mx, pos, n), axis=axis, keepdims=True)
        hit = pos == idx
        return jnp.where(hit, -jnp.inf, sc), jnp.where(hit, 1.0, sel)

    return lax.fori_loop(0, k, pick, (score, jnp.zeros(score.shape, F32)))[1]


def _compress_rows(x_refs, pos_ref, w1_ref, w2_ref, n_blk):
    half = n_blk // 2
    tiles = NSA_KVW // LANES
    low = lax.broadcasted_iota(jnp.int32, (n_blk, LANES), 1) < NSA_DH
    outs = []
    for c in range(2):
        cols = [[] for _ in range(NSA_KVH)]
        for l in range(0, CMP_BLOCK, 2):
            for tl in range(tiles):
                ref = x_refs[c * tiles + tl]

                def rows(r, ref=ref, tl=tl):
                    xe = ref[pl.ds(r, half, stride=2 * CMP_BLOCK), :]
                    xo = ref[pl.ds(r + CMP_BLOCK, half, stride=2 * CMP_BLOCK), :]
                    return jnp.concatenate([xe, xo], axis=0) + pos_ref[c, r:r + 1, tl * LANES:(tl + 1) * LANES]

                x0, x1 = rows(l), rows(l + 1)
                cols[2 * tl].append(jnp.where(low, x0, pltpu.roll(x1, NSA_DH, 1)))
                cols[2 * tl + 1].append(jnp.where(low, pltpu.roll(x0, NSA_DH, 1), x1))
        xs = jnp.concatenate([jnp.concatenate(cols[h], axis=1) for h in range(NSA_KVH)], axis=0)
        acc = jnp.dot(xs.astype(BF16), w1_ref[c], preferred_element_type=F32)
        out = jnp.dot(_silu(acc).astype(BF16), w2_ref[c], preferred_element_type=F32)
        outs.append(jnp.concatenate([out[h * n_blk:(h + 1) * n_blk] for h in range(NSA_KVH)], axis=1))
    return jnp.concatenate(outs, axis=1)


def _compress_kernel(x0_ref, x1_ref, x2_ref, x3_ref, pos_ref, w1_ref, w2_ref, even_ref, odd_ref, *, n_blk):
    full = _compress_rows((x0_ref, x1_ref, x2_ref, x3_ref), pos_ref, w1_ref, w2_ref, n_blk)
    even_ref[0] = full[:n_blk // 2]
    odd_ref[0] = full[n_blk // 2:]


def nsa_compress(kv, col_block, pos4, w1, w2, *, batch, n_blk):
    m = kv.shape[0]
    n_t = m // batch // (n_blk * CMP_BLOCK)
    half = n_blk // 2
    const = lambda shape: pl.BlockSpec(shape, lambda b, t: (0,) * len(shape))
    out = jax.ShapeDtypeStruct((batch, n_t * half, NSA_ROW), F32)
    tiles = NSA_ROW // LANES
    lane_tile = lambda i: pl.BlockSpec((n_blk * CMP_BLOCK, LANES), lambda b, t: (b * n_t + t, col_block * tiles + i))
    return pl.pallas_call(
        functools.partial(_compress_kernel, n_blk=n_blk),
        grid=(batch, n_t),
        in_specs=[lane_tile(i) for i in range(tiles)]
        + [const((2, CMP_BLOCK, NSA_KVW)), const((2, CMP_BLOCK * NSA_DH, CMP_HID)), const((2, CMP_HID, NSA_DH))],
        out_specs=[pl.BlockSpec((1, half, NSA_ROW), lambda b, t: (b, t, 0))] * 2,
        out_shape=[out, out],
        compiler_params=_params("parallel", "parallel"),
        name="nsa_compress",
    )(kv, kv, kv, kv, pos4, w1, w2)


def nsa_prepare_weights(w_in, pos, w1, w2):
    hq = NSA_HEADS * NSA_DH
    kv_end = hq + 3 * NSA_ROW
    w_qkv = w_in[:, :kv_end].astype(BF16)
    w_kvb = w_in[:, hq + NSA_ROW:kv_end].astype(BF16)
    w_g = w_in[:, kv_end:].reshape(-1, NSA_KVH, 3 * NSA_GROUP)
    w_g = jnp.pad(w_g, ((0, 0), (0, 0), (0, LANES - 3 * NSA_GROUP))).reshape(-1, NSA_KVH * LANES).astype(BF16)
    pos4 = jnp.tile(pos, (1, 1, NSA_KVH))
    w1 = w1.reshape(2, CMP_BLOCK * NSA_DH, CMP_HID)
    w_kv_t = w_in[:, hq:kv_end].T.astype(BF16)
    return w_qkv, w_kvb, w_g, w_kv_t, pos4, w1.astype(BF16), w2.astype(BF16)


def _stack_group_queries(q, par):
    half = lax.broadcasted_iota(jnp.int32, (q.shape[0], LANES), 1) // NSA_DH
    parts = []
    for g in range(NSA_GROUP):
        qg = q[:, g * NSA_DH:(g + 1) * NSA_DH]
        parts.append(jnp.where(half == par, jnp.concatenate([qg, qg], axis=1), 0.0))
    return jnp.concatenate(parts, axis=0).astype(BF16)


def _nsa_prompt_kernel(q_ref, gp_ref, ck_ref, cv_ref, ks_ref, vs_ref, kw_ref, vw_ref, o_ref):
    j = pl.program_id(2)
    qb, grp = NSA_QBLOCK, NSA_GROUP
    rows = grp * qb
    qw = grp * NSA_DH
    pair = range(LANES // NSA_DH)
    t_pos = j * qb + lax.broadcasted_iota(jnp.int32, (qb, 1), 0)

    def grouped_softmax(s, mask):
        n = s.shape[-1]
        return _masked_softmax(s.reshape(grp, qb, n), mask[None], jnp.exp2).reshape(rows, n)

    n_c = ck_ref.shape[1]
    n_s = n_c // 2
    ck = ck_ref[0].astype(BF16)
    cv = cv_ref[0].astype(BF16)
    nn = lax.broadcasted_iota(jnp.int32, (1, n_c), 1)
    c_blk = jnp.where(nn < n_s, 2 * nn, 2 * (nn - n_s) + 1)
    c_vis = (c_blk * CMP_BLOCK + CMP_BLOCK - 1) <= t_pos
    n_w = WINDOW + qb
    w0 = pl.multiple_of(jnp.maximum(j * qb - WINDOW, 0), qb)
    dist = t_pos - (w0 + lax.broadcasted_iota(jnp.int32, (1, n_w), 1))
    w_bias = jnp.where((dist >= 0) & (dist <= WINDOW), 0.0, MASK_NEG)[None]
    k_win, v_win = kw_ref[pl.ds(w0, n_w), :], vw_ref[pl.ds(w0, n_w), :]
    blk = lax.broadcasted_iota(jnp.int32, (qb, n_s), 1)
    cur = t_pos // SLC_BLOCK
    forced = (blk == 0) | (blk == cur) | (blk == cur - 1)

    def before_selection(par):
        qp = _stack_group_queries(q_ref[:, par * qw:(par + 1) * qw] * (NSA_DH ** -0.5 * LOG2_E), par)
        p_c = grouped_softmax(_dot_nt(qp, ck), c_vis)
        o_c = jnp.dot(p_c.astype(BF16), cv, preferred_element_type=F32)
        s_w = _dot_nt(qp, k_win).reshape(grp, qb, n_w) + w_bias
        e_w = jnp.exp2(s_w - jnp.max(s_w, axis=-1, keepdims=True)).reshape(rows, n_w)
        own_w = lax.broadcasted_iota(jnp.int32, (n_w, LANES), 1) // NSA_DH == par
        o_w = jnp.dot(e_w.astype(BF16), jnp.where(own_w, v_win, 1.0), preferred_element_type=F32)
        o_w = o_w / jnp.maximum(pltpu.roll(o_w, NSA_DH, 1), 1e-30)
        gates = _sigmoid(gp_ref[:, par * LANES:(par + 1) * LANES])
        gate = lambda br: jnp.concatenate(
            [jnp.broadcast_to(gates[:, 3 * g + br:3 * g + br + 1], (qb, LANES)) for g in range(grp)], axis=0)
        imp = p_c[0:qb]
        for g in range(1, grp):
            imp = imp + p_c[g * qb:(g + 1) * qb]
        imp = imp[:, :n_s] + imp[:, n_s:]
        score = jnp.where(forced, SEL_BIG, jnp.where(blk <= cur, imp, -SEL_BIG))
        return qp, gate(0) * o_c + gate(2) * o_w, gate(1), score.T

    qps, o_cws, gate_s, scores_t = zip(*[before_selection(par) for par in pair])
    sel_t = _top_blocks(jnp.concatenate(scores_t, axis=1), SLC_TOPK, 0)
    sel_bs = [sel_t[:, par * qb:(par + 1) * qb].T.astype(BF16) for par in pair]

    per = SEL_KEYS // SLC_BLOCK
    e_row = lax.broadcasted_iota(jnp.int32, (n_s, SEL_KEYS), 0)
    e_col = lax.broadcasted_iota(jnp.int32, (n_s, SEL_KEYS), 1) // SLC_BLOCK
    k_off = lax.broadcasted_iota(jnp.int32, (1, SEL_KEYS), 1)

    n_kb = (j * qb + qb - 1) // SEL_KEYS + 1

    def key_rows(kb):
        return pl.ds(pl.multiple_of(kb * SEL_KEYS, SEL_KEYS), SEL_KEYS)

    lane_head = lax.broadcasted_iota(jnp.int32, (SEL_KEYS, LANES), 1) // NSA_DH

    def sel_step(kb, carry):
        expand = (e_row == kb * per + e_col).astype(BF16)
        causal = (kb * SEL_KEYS + k_off) <= t_pos
        k_blk, v_blk = ks_ref[key_rows(kb), :], vs_ref[key_rows(kb), :]
        out = []
        for par in pair:
            m, acc = carry[par]
            chosen = jnp.dot(sel_bs[par], expand, preferred_element_type=F32) > 0.5
            bias = jnp.where(chosen & causal, 0.0, MASK_NEG)[None]
            s = _dot_nt(qps[par], k_blk).reshape(grp, qb, SEL_KEYS) + bias
            m_new = jnp.maximum(m, jnp.max(s, axis=-1, keepdims=True))
            alpha = jnp.exp2(m - m_new)
            e = jnp.exp2(s - m_new)
            v_one = jnp.where(lane_head == par, v_blk, 1.0)
            pv = jnp.dot(e.reshape(rows, SEL_KEYS).astype(BF16), v_one, preferred_element_type=F32)
            out.append((m_new, alpha * acc + pv.reshape(grp, qb, LANES)))
        return tuple(out)

    init = tuple((jnp.full((grp, qb, 1), MASK_NEG, F32), jnp.zeros((grp, qb, LANES), F32)) for _ in pair)
    fin = lax.fori_loop(0, n_kb, sel_step, init)
    heads = []
    for par in pair:
        acc = fin[par][1].reshape(rows, LANES)
        o_s = acc / jnp.maximum(pltpu.roll(acc, NSA_DH, 1), 1e-30)
        o = (o_cws[par] + gate_s[par] * o_s)[:, par * NSA_DH:(par + 1) * NSA_DH]
        heads += [o[g * qb:(g + 1) * qb] for g in range(grp)]
    o_ref[...] = jnp.concatenate(heads, axis=1).astype(o_ref.dtype)


def nsa_prompt_attention(qkv, gates_pre, ckv, kvb, *, batch):
    m = qkv.shape[0]
    l_seq = m // batch
    nq = l_seq // NSA_QBLOCK
    n_c = ckv.shape[1]
    per_tile = LANES // NSA_DH
    n_pairs = NSA_KVH // per_tile
    qw = per_tile * NSA_GROUP * NSA_DH
    kv_spec = lambda part: pl.BlockSpec((l_seq, LANES), lambda b, p, j: (b, part * n_pairs + p))
    return pl.pallas_call(
        _nsa_prompt_kernel,
        grid=(batch, n_pairs, nq),
        in_specs=[pl.BlockSpec((NSA_QBLOCK, qw), lambda b, p, j: (b * nq + j, p)),
                  pl.BlockSpec((NSA_QBLOCK, per_tile * LANES), lambda b, p, j: (b * nq + j, p)),
                  pl.BlockSpec((1, n_c, LANES), lambda b, p, j: (b, 0, p)),
                  pl.BlockSpec((1, n_c, LANES), lambda b, p, j: (b, 0, n_pairs + p)),
                  kv_spec(0), kv_spec(1), kv_spec(2), kv_spec(3)],
        out_specs=pl.BlockSpec((NSA_QBLOCK, qw), lambda b, p, j: (b * nq + j, p)),
        out_shape=jax.ShapeDtypeStruct((m, NSA_HEADS * NSA_DH), BF16),
        compiler_params=_params("parallel", "parallel", "arbitrary"),
        name="nsa_prompt_attention",
    )(qkv, gates_pre, ckv, ckv, kvb, kvb, kvb, kvb)


def _paged_compress_kernel(pt_ref, *refs, n_pages, page):
    del pt_ref
    pages = refs[:n_pages]
    pos_ref, w1_ref, w2_ref, even_ref, odd_ref = refs[n_pages:n_pages + 5]
    rows = refs[n_pages + 5:]
    tiles = NSA_KVW // LANES
    for p in range(n_pages):
        for i, r in enumerate(rows):
            r[p * page:(p + 1) * page, :] = pages[p][0, i // tiles, (i % tiles) * LANES:(i % tiles + 1) * LANES, :].T
    n_blk = n_pages * page // CMP_BLOCK
    full = _compress_rows(rows, pos_ref, w1_ref, w2_ref, n_blk)
    even_ref[0] = full[:n_blk // 2]
    odd_ref[0] = full[n_blk // 2:]


def _page_specs(n_pages, page):
    return [pl.BlockSpec((1, 2, NSA_KVW, page), lambda b, pt, p=p: (pt[b, p], 0, 0, 0)) for p in range(n_pages)]


def nsa_compress_paged(cache_t, page_table, pos4, w1, w2):
    n_seq, n_pages = page_table.shape
    page = cache_t.shape[3]
    half = n_pages * page // CMP_BLOCK // 2
    const = lambda shape: pl.BlockSpec(shape, lambda b, pt: (0,) * len(shape))
    out = jax.ShapeDtypeStruct((n_seq, half, NSA_ROW), F32)
    return pl.pallas_call(
        functools.partial(_paged_compress_kernel, n_pages=n_pages, page=page),
        grid_spec=pltpu.PrefetchScalarGridSpec(
            num_scalar_prefetch=1,
            grid=(n_seq,),
            in_specs=_page_specs(n_pages, page)
            + [const((2, CMP_BLOCK, NSA_KVW)), const((2, CMP_BLOCK * NSA_DH, CMP_HID)), const((2, CMP_HID, NSA_DH))],
            out_specs=[pl.BlockSpec((1, half, NSA_ROW), lambda b, pt: (b, 0, 0))] * 2,
            scratch_shapes=[pltpu.VMEM((n_pages * page, LANES), F32)] * (NSA_ROW // LANES)),
        out_shape=[out, out],
        compiler_params=_params("arbitrary"),
        name="nsa_compress_paged",
    )(page_table, *([cache_t] * n_pages), pos4, w1, w2)


def _softmax_with_new_key(s, ok, s_new, new_ok):
    s = jnp.where(ok, s, MASK_NEG)
    s_new = jnp.where(new_ok, s_new, MASK_NEG)
    m = jnp.maximum(jnp.max(s, axis=-1, keepdims=True), s_new)
    e = jnp.where(ok, jnp.exp(s - m), 0.0)
    e_new = jnp.where(new_ok, jnp.exp(s_new - m), 0.0)
    total = jnp.sum(e, axis=-1, keepdims=True) + e_new
    return e, e_new, 1.0 / jnp.maximum(total, 1e-30)


def _nsa_step_kernel(pt_ref, *refs, n_pages, page, past):
    del pt_ref
    pages = refs[:n_pages]
    (q_ref, gp_ref, cke_ref, cko_ref, kvs_new_ref, swa_ref, kvw_new_ref,
     o_ref, swa_out_ref, kt_ref, vt_ref) = refs[n_pages:]
    nh, kw = NSA_HEADS, NSA_KVW
    t = past
    own = (lax.broadcasted_iota(jnp.int32, (nh, kw), 1) // NSA_DH
           == lax.broadcasted_iota(jnp.int32, (nh, kw), 0) // NSA_GROUP)
    q = q_ref[0] * (NSA_DH ** -0.5)
    qp = jnp.where(own, jnp.concatenate([q] * NSA_KVH, axis=1), 0.0)
    qpb = qp.astype(BF16)

    half = cke_ref.shape[1]
    n_c = 2 * half
    ce, co = cke_ref[0], cko_ref[0]
    ck = jnp.concatenate([ce[:, :kw], co[:, :kw]], axis=0).astype(BF16)
    cv = jnp.concatenate([ce[:, kw:], co[:, kw:]], axis=0).astype(BF16)
    nn = lax.broadcasted_iota(jnp.int32, (1, n_c), 1)
    c_blk = jnp.where(nn < half, 2 * nn, 2 * (nn - half) + 1)
    p_c = _masked_softmax(_dot_nt(qpb, ck), (c_blk * CMP_BLOCK + CMP_BLOCK - 1) <= t)
    o_c = jnp.dot(p_c.astype(BF16), cv, preferred_element_type=F32)
    same_group = (lax.broadcasted_iota(jnp.int32, (nh, nh), 0) // NSA_GROUP
                  == lax.broadcasted_iota(jnp.int32, (nh, nh), 1) // NSA_GROUP).astype(F32)
    imp = lax.dot_general(p_c, same_group, (((0,), (0,)), ((), ())), preferred_element_type=F32, precision=HIGHEST)
    imp = imp[:half] + imp[half:]
    imp = jnp.concatenate([imp, jnp.zeros((LANES - half, nh), F32)], axis=0)
    blk = lax.broadcasted_iota(jnp.int32, (LANES, nh), 0)
    cur = t // SLC_BLOCK
    forced = (blk == 0) | (blk == cur) | (blk == cur - 1)
    sel_t = _top_blocks(jnp.where(forced, SEL_BIG, jnp.where(blk <= cur, imp, -SEL_BIG)), SLC_TOPK, 0)

    for p in range(n_pages):
        kt_ref[:, p * page:(p + 1) * page] = pages[p][0, 0].astype(BF16)
        vt_ref[:, p * page:(p + 1) * page] = pages[p][0, 1].astype(BF16)
    n_keys = n_pages * page
    pos = lax.broadcasted_iota(jnp.int32, (LANES, n_keys + LANES), 1)
    expand = (lax.broadcasted_iota(jnp.int32, (LANES, n_keys + LANES), 0)
              == jnp.minimum(pos, t) // SLC_BLOCK).astype(BF16)
    chosen = _dot_tn(sel_t.astype(BF16), expand) > 0.5
    ok = chosen[:, :n_keys] & (lax.broadcasted_iota(jnp.int32, (1, n_keys), 1) <= t)
    kv_new = kvs_new_ref[0]
    s_new = jnp.sum(qp * kv_new[:, :kw], axis=-1, keepdims=True)
    scores = jnp.dot(qpb, kt_ref[...], preferred_element_type=F32)
    e, e_new, inv = _softmax_with_new_key(scores, ok, s_new, chosen[:, n_keys:n_keys + 1])
    o_s = (_dot_nt(e.astype(BF16), vt_ref[...]) + e_new * kv_new[:, kw:]) * inv

    n_w = swa_ref.shape[3]
    dist = t - (t - n_w + lax.broadcasted_iota(jnp.int32, (1, n_w), 1))
    w_ok = (dist >= 0) & (dist <= WINDOW) & (t - dist >= 0)
    kvw_new = kvw_new_ref[0]
    s_wn = jnp.sum(qp * kvw_new[:, :kw], axis=-1, keepdims=True)
    new_ok = lax.broadcasted_iota(jnp.int32, (nh, 1), 0) >= 0
    scores = jnp.dot(qpb, swa_ref[0, 0].astype(BF16), preferred_element_type=F32)
    e, e_new, inv = _softmax_with_new_key(scores, w_ok, s_wn, new_ok)
    o_w = (_dot_nt(e.astype(BF16), swa_ref[0, 1].astype(BF16)) + e_new * kvw_new[:, kw:]) * inv

    g = _sigmoid(gp_ref[0])
    o = jnp.where(own, g[:, 0:1] * o_c + g[:, 1:2] * o_s + g[:, 2:3] * o_w, 0.0)
    o_h = o[:, :NSA_DH]
    for h in range(1, NSA_KVH):
        o_h = o_h + o[:, h * NSA_DH:(h + 1) * NSA_DH]
    o_ref[0] = o_h
    last = lax.broadcasted_iota(jnp.int32, (kw, n_w), 1) == n_w - 1
    diag = lax.broadcasted_iota(jnp.int32, (kw, kw), 0) == lax.broadcasted_iota(jnp.int32, (kw, kw), 1)
    for c in range(2):
        column = jnp.sum(jnp.where(diag, kvw_new[:, c * kw:(c + 1) * kw], 0.0), axis=-1, keepdims=True)
        swa_out_ref[0, c] = jnp.where(last, column, pltpu.roll(swa_ref[0, c], n_w - 1, 1))


def nsa_step_attention(q3, gates3, cke, cko, cache_slc_t, page_table, kvs_new, swa_t, kvw_new, *, past):
    n_seq, n_pages = page_table.shape
    page = cache_slc_t.shape[3]
    seq = lambda a: pl.BlockSpec((1,) + a.shape[1:], lambda b, pt: (b,) + (0,) * (a.ndim - 1))
    return pl.pallas_call(
        functools.partial(_nsa_step_kernel, n_pages=n_pages, page=page, past=past),
        grid_spec=pltpu.PrefetchScalarGridSpec(
            num_scalar_prefetch=1,
            grid=(n_seq,),
            in_specs=_page_specs(n_pages, page) + [seq(a) for a in (q3, gates3, cke, cko, kvs_new, swa_t, kvw_new)],
            out_specs=[seq(q3), seq(swa_t)],
            scratch_shapes=[pltpu.VMEM((NSA_KVW, n_pages * page), BF16)] * 2),
        out_shape=[jax.ShapeDtypeStruct(q3.shape, F32), jax.ShapeDtypeStruct(swa_t.shape, F32)],
        compiler_params=_params("arbitrary"),
        name="nsa_step_attention",
    )(page_table, *([cache_slc_t] * n_pages), q3, gates3, cke, cko, kvs_new, swa_t, kvw_new)


PROMPT_TM = 1024
FFN_TM = 512
HG_TL = 512
RG_TL = 256
FFN_TF = 1408
CMP_TILE_BLOCKS = 64
MOE_TB = 896
MOE_TS = 256


def _ffn_layer(x, w_gu, w_down, g, b, *, tm):
    return ffn_residual_ln(x, w_gu.astype(BF16), w_down.astype(BF16), g, b, tm=tm, tf=FFN_TF)


def _moe_layer(x, w_router, w_gu_t, w_down_t, slot, g, b):
    m = x.shape[0]
    tb = min(MOE_TB, m)
    gates, slots = moe_route(x, w_router.T, tb=tb)
    n_e = gates.shape[0]
    counts = (jnp.max(slots.reshape(n_e, -1, tb), axis=-1) + 1).T.reshape(-1)
    return moe_residual_ln(x, gates, slots, counts, w_gu_t, w_down_t, slot, g, b,
                           tb=tb, ts=min(MOE_TS, tb), tf=FFN_TF)


def kernel(x_prompt, x_sample, state_hgrn, state_rglru_conv, state_rglru_h, cache_nsa_cmp, cache_nsa_slc, cache_nsa_swa, page_table, ln_g, ln_b, hgrn_w_in, hgrn_lb, hgrn_norm_g, hgrn_w_out, rg_w_in, rg_conv_w, rg_conv_b, rg_w_gate, rg_b_gate, rg_lambda, rg_w_out, nsa_w_in, nsa_cmp_pos, nsa_cmp_w1, nsa_cmp_w2, nsa_w_out, ffn_w_gu, ffn_w_down, moe_router, moe_w_gu, moe_w_down):
    bp, l_seq, d = x_prompt.shape
    n_seq = x_sample.shape[0]
    past = page_table.shape[1] * cache_nsa_cmp.shape[2]
    xp = x_prompt.reshape(bp * l_seq, d)
    xs = x_sample.reshape(n_seq, d)
    tm_p, tm_s = PROMPT_TM, n_seq
    kv_shape = (2, NSA_KVH, NSA_DH)
    moe_w_gu_t = jnp.swapaxes(moe_w_gu, 2, 3).astype(BF16)
    moe_w_down_t = jnp.swapaxes(moe_w_down, 2, 3).astype(BF16)
    hg_p, hg_s, rc_p, rc_s, rh_p, rh_s = [], [], [], [], [], []
    cm_p, cm_s, sl_p, sl_s, sw_p, sw_s = [], [], [], [], [], []
    for layer in range(DEPTH):
        kind, slot = layer % 3, layer // 3
        g0, b0 = ln_g[layer, 0], ln_b[layer, 0]
        if kind == 0:
            w_in, w_out = hgrn_w_in[slot].astype(BF16), hgrn_w_out[slot].astype(BF16)
            op, st_p = hgrn_prompt(matmul(xp, w_in, tm=tm_p), hgrn_lb, hgrn_norm_g[slot],
                                   layer=layer, batch=bp, tl=HG_TL)
            os_, st_s = hgrn_step(matmul(xs, w_in, tm=tm_s), state_hgrn, slot, hgrn_lb, hgrn_norm_g[slot],
                                  layer=layer, sb=8)
            hg_p.append(st_p)
            hg_s.append(st_s)
        elif kind == 1:
            w_in, w_out = rg_w_in[slot].astype(BF16), rg_w_out[slot].astype(BF16)
            rg_w = (rg_conv_w[slot], rg_conv_b[slot], rg_w_gate[slot].astype(BF16), rg_b_gate[slot], rg_lambda[slot])
            op, cb_p, h_p = rglru_prompt(matmul(xp, w_in, tm=tm_p), *rg_w, batch=bp, tl=RG_TL)
            os_, cb_s, h_s = rglru_step(matmul(xs, w_in, tm=tm_s), state_rglru_conv[slot], state_rglru_h[slot], *rg_w)
            rc_p.append(cb_p)
            rc_s.append(cb_s)
            rh_p.append(h_p[:, 0])
            rh_s.append(h_s)
        else:
            w_qkv, w_kvb, w_g, w_kv_t, pos4, w1, w2 = nsa_prepare_weights(
                nsa_w_in[slot], nsa_cmp_pos[slot], nsa_cmp_w1[slot], nsa_cmp_w2[slot])
            w_out = nsa_w_out[slot].astype(BF16)
            hq = NSA_HEADS * NSA_DH
            qkv = matmul(xp, w_qkv[:, :hq + NSA_ROW], tm=tm_p)
            kvb = matmul(xp, w_kvb, tm=tm_p, out_dtype=BF16)
            gates_pre = matmul(xp, w_g, tm=tm_p)
            cke, cko = nsa_compress(qkv, hq // NSA_ROW, pos4, w1, w2, batch=bp, n_blk=CMP_TILE_BLOCKS)
            op = nsa_prompt_attention(qkv, gates_pre, jnp.concatenate([cke, cko], axis=1), kvb, batch=bp)
            as_rows = lambda a: a.reshape(bp, *kv_shape, -1).transpose(0, 4, 1, 2, 3)
            kv_c, kv_s, kv_w = matmul_t(xp, w_kv_t, batch=bp, n_out=3, tm=tm_p)
            cm_p.append(as_rows(kv_c))
            sl_p.append(as_rows(kv_s))
            sw_p.append(as_rows(kv_w[:, :, l_seq - min(WINDOW, l_seq):]))
            qkv = matmul(xs, w_qkv, tm=tm_s)
            gates_pre = matmul(xs, w_g, tm=tm_s)
            rows_last = lambda c: c.transpose(0, 2, 3, 4, 1).reshape(c.shape[0], 2, NSA_KVW, c.shape[1])
            cke, cko = nsa_compress_paged(rows_last(cache_nsa_cmp[slot]), page_table, pos4, w1, w2)
            gates3 = gates_pre.reshape(n_seq, NSA_KVH, LANES)[:, :, :3 * NSA_GROUP].reshape(n_seq, NSA_HEADS, 3)
            kv = qkv[:, hq:].reshape(n_seq, 3, 1, NSA_ROW)
            o3, swa_new = nsa_step_attention(qkv[:, :hq].reshape(n_seq, NSA_HEADS, NSA_DH), gates3, cke, cko,
                                             rows_last(cache_nsa_slc[slot]), page_table, kv[:, 1],
                                             rows_last(cache_nsa_swa[slot]), kv[:, 2], past=past)
            os_ = o3.reshape(n_seq, hq)
            cm_s.append(kv[:, 0].reshape(n_seq, 1, *kv_shape))
            sl_s.append(kv[:, 1].reshape(n_seq, 1, *kv_shape))
            sw_s.append(swa_new.reshape(n_seq, *kv_shape, -1).transpose(0, 4, 1, 2, 3))
        xp = matmul_residual_ln(op, w_out, xp, g0, b0, tm=tm_p)
        xs = matmul_residual_ln(os_, w_out, xs, g0, b0, tm=tm_s)
        g1, b1 = ln_g[layer, 1], ln_b[layer, 1]
        if layer % 2 == 0:
            fw = (ffn_w_gu[layer // 2], ffn_w_down[layer // 2])
            xp = _ffn_layer(xp, *fw, g1, b1, tm=FFN_TM)
            xs = _ffn_layer(xs, *fw, g1, b1, tm=tm_s)
        else:
            fw = (moe_router[layer // 2], moe_w_gu_t, moe_w_down_t, layer // 2)
            xp = _moe_layer(xp, *fw, g1, b1)
            xs = _moe_layer(xs, *fw, g1, b1)
    return (xp.reshape(bp, l_seq, d), xs.reshape(n_seq, 1, d),
            jnp.stack(hg_p), jnp.stack(hg_s), jnp.stack(rc_p), jnp.stack(rc_s), jnp.stack(rh_p), jnp.stack(rh_s),
            jnp.stack(cm_p), jnp.stack(cm_s), jnp.stack(sl_p), jnp.stack(sl_s), jnp.stack(sw_p), jnp.stack(sw_s))
```

```python
import functools

import jax
import jax.numpy as jnp
from jax import lax
from jax.experimental import pallas as pl
from jax.experimental.pallas import tpu as pltpu

F32 = jnp.float32
BF16 = jnp.bfloat16
HIGHEST = lax.Precision.HIGHEST

D_MODEL = 1024
DEPTH = 4
ALPHA = (2 * DEPTH) ** 0.25
LN_EPS = 1e-5
RMS_EPS = 1e-6
HG_HEADS = 8
HG_DK = 128
HG_CHUNK = 64
HG_SUB = 16
HG_HEADS_PER_STEP = 8
LB_FLOOR = 1e-30
RG_WIDTH = 1280
RG_BLOCKS = 10
RG_BS = 128
RG_CONV = 4
RG_C = 8.0
D_FF = 2816
LOG2_E = 1.4426950408889634

SUBLANES = 8
LANES = 128
VMEM_LIMIT_BYTES = 56 * 1024 * 1024


def _params(*sem):
    return pltpu.CompilerParams(dimension_semantics=sem, vmem_limit_bytes=VMEM_LIMIT_BYTES)


def _layer_norm(v, g, b):
    mu = jnp.mean(v, axis=-1, keepdims=True)
    vc = v - mu
    var = jnp.mean(vc * vc, axis=-1, keepdims=True)
    return vc * lax.rsqrt(var + LN_EPS) * g + b


def _sigmoid(x):
    return 1.0 / (1.0 + jnp.exp(-x))


def _silu(x):
    return x * _sigmoid(x)


def _mm_kernel(x_ref, w_ref, o_ref, xb_ref):
    @pl.when(pl.program_id(1) == 0)
    def _():
        xb_ref[...] = x_ref[...].astype(BF16)

    o_ref[...] = jnp.dot(xb_ref[...], w_ref[...], preferred_element_type=F32).astype(o_ref.dtype)


MAX_COL_TILE = 1280


def _col_tile(n):
    return max(t for t in range(LANES, min(n, MAX_COL_TILE) + 1, LANES) if n % t == 0)


def matmul(x, w, *, tm, out_dtype=F32):
    m, k = x.shape
    n = w.shape[1]
    tn = _col_tile(n)
    return pl.pallas_call(
        _mm_kernel,
        grid=(m // tm, n // tn),
        in_specs=[pl.BlockSpec((tm, k), lambda i, j: (i, 0)),
                  pl.BlockSpec((k, tn), lambda i, j: (0, j))],
        out_specs=pl.BlockSpec((tm, tn), lambda i, j: (i, j)),
        out_shape=jax.ShapeDtypeStruct((m, n), out_dtype),
        scratch_shapes=[pltpu.VMEM((tm, k), BF16)],
        compiler_params=_params("parallel", "arbitrary"),
        name="matmul",
    )(x, w)


def _mm_t_kernel(x_ref, wt_ref, *o_refs):
    xb = x_ref[...].astype(BF16)
    n = o_refs[0].shape[1]
    for r, o_ref in enumerate(o_refs):
        o_ref[0] = _dot_nt(wt_ref[r * n:(r + 1) * n, :], xb)


def matmul_t(x, w_t, *, batch, n_out, tm):
    m, k = x.shape
    n = w_t.shape[0] // n_out
    l_seq = m // batch
    n_t = l_seq // tm
    return pl.pallas_call(
        _mm_t_kernel,
        grid=(batch, n_t),
        in_specs=[pl.BlockSpec((tm, k), lambda b, i: (b * n_t + i, 0)),
                  pl.BlockSpec((n_out * n, k), lambda b, i: (0, 0))],
        out_specs=[pl.BlockSpec((1, n, tm), lambda b, i: (b, 0, i))] * n_out,
        out_shape=[jax.ShapeDtypeStruct((batch, n, l_seq), F32)] * n_out,
        compiler_params=_params("parallel", "parallel"),
        name="matmul_t",
    )(x, w_t)


def _mm_res_ln_kernel(a_ref, w_ref, r_ref, g_ref, b_ref, o_ref):
    m = jnp.dot(a_ref[...].astype(BF16), w_ref[...], preferred_element_type=F32)
    o_ref[...] = _layer_norm(ALPHA * r_ref[...] + m, g_ref[...], b_ref[...])


def matmul_residual_ln(a, w, res, g, b, *, tm):
    m, k = a.shape
    d = w.shape[1]
    return pl.pallas_call(
        _mm_res_ln_kernel,
        grid=(m // tm,),
        in_specs=[pl.BlockSpec((tm, k), lambda i: (i, 0)),
                  pl.BlockSpec((k, d), lambda i: (0, 0)),
                  pl.BlockSpec((tm, d), lambda i: (i, 0)),
                  pl.BlockSpec((1, d), lambda i: (0, 0)),
                  pl.BlockSpec((1, d), lambda i: (0, 0))],
        out_specs=pl.BlockSpec((tm, d), lambda i: (i, 0)),
        out_shape=jax.ShapeDtypeStruct((m, d), F32),
        compiler_params=_params("parallel"),
        name="matmul_residual_ln",
    )(a, w, res, g.reshape(1, d), b.reshape(1, d))


def _ffn_kernel(x_ref, wg_ref, wu_ref, wd_ref, g_ref, b_ref, o_ref, xb_ref, acc_ref, *, n_f):
    f = pl.program_id(1)

    @pl.when(f == 0)
    def _():
        xb_ref[...] = x_ref[...].astype(BF16)

    xb = xb_ref[...]
    h = jnp.dot(xb, wg_ref[...], preferred_element_type=F32)
    u = jnp.dot(xb, wu_ref[...], preferred_element_type=F32)
    part = jnp.dot((_silu(h) * u).astype(BF16), wd_ref[...], preferred_element_type=F32)

    @pl.when(f == 0)
    def _():
        acc_ref[...] = part

    @pl.when(f > 0)
    def _():
        acc_ref[...] += part

    @pl.when(f == n_f - 1)
    def _():
        o_ref[...] = _layer_norm(ALPHA * x_ref[...] + acc_ref[...], g_ref[...], b_ref[...])


def ffn_residual_ln(x, w_gu, w_down, g, b, *, tm, tf):
    m, d = x.shape
    ff = w_down.shape[0]
    n_f = ff // tf
    return pl.pallas_call(
        functools.partial(_ffn_kernel, n_f=n_f),
        grid=(m // tm, n_f),
        in_specs=[pl.BlockSpec((tm, d), lambda i, f: (i, 0)),
                  pl.BlockSpec((d, tf), lambda i, f: (0, f)),
                  pl.BlockSpec((d, tf), lambda i, f: (0, n_f + f)),
                  pl.BlockSpec((tf, d), lambda i, f: (f, 0)),
                  pl.BlockSpec((1, d), lambda i, f: (0, 0)),
                  pl.BlockSpec((1, d), lambda i, f: (0, 0))],
        out_specs=pl.BlockSpec((tm, d), lambda i, f: (i, 0)),
        out_shape=jax.ShapeDtypeStruct((m, d), F32),
        scratch_shapes=[pltpu.VMEM((tm, d), BF16), pltpu.VMEM((tm, d), F32)],
        compiler_params=_params("parallel", "arbitrary"),
        name="ffn_residual_ln",
    )(x, w_gu, w_gu, w_down, g.reshape(1, d), b.reshape(1, d))


def _router_kernel(x_ref, wt_ref, gate_ref, pos_ref, *, n_tokens):
    logits = lax.dot_general(wt_ref[...], x_ref[...], (((1,), (1,)), ((), ())),
                             preferred_element_type=F32, precision=HIGHEST)
    n_e, tm = logits.shape
    expert = lax.broadcasted_iota(jnp.int32, logits.shape, 0)
    m1 = jnp.max(logits, axis=0, keepdims=True)
    i1 = jnp.min(jnp.where(logits == m1, expert, n_e), axis=0, keepdims=True)
    rest = jnp.where(expert == i1, -jnp.inf, logits)
    m2 = jnp.max(rest, axis=0, keepdims=True)
    i2 = jnp.min(jnp.where(rest == m2, expert, n_e), axis=0, keepdims=True)
    e2 = jnp.exp(m2 - m1)
    denom = 1.0 + e2
    first = expert == i1
    in_range = pl.program_id(0) * tm + lax.broadcasted_iota(jnp.int32, logits.shape, 1) < n_tokens
    routed = (first | (expert == i2)) & in_range
    gate_ref[...] = jnp.where(first & in_range, 1.0 / denom, jnp.where(routed, e2 / denom, 0.0))
    upper = (lax.broadcasted_iota(jnp.int32, (tm, tm), 0) <= lax.broadcasted_iota(jnp.int32, (tm, tm), 1)).astype(BF16)
    upto = jnp.dot(routed.astype(BF16), upper, preferred_element_type=F32)
    pos_ref[...] = jnp.where(routed, upto - 1.0, -1.0).astype(jnp.int32)


def moe_route(x, w_router_t, *, tb):
    m, d = x.shape
    n_e = w_router_t.shape[0]
    n_blk = pl.cdiv(m, tb)
    return pl.pallas_call(
        functools.partial(_router_kernel, n_tokens=m),
        grid=(n_blk,),
        in_specs=[pl.BlockSpec((tb, d), lambda i: (i, 0)),
                  pl.BlockSpec((n_e, d), lambda i: (0, 0))],
        out_specs=[pl.BlockSpec((n_e, tb), lambda i: (0, i))] * 2,
        out_shape=[jax.ShapeDtypeStruct((n_e, n_blk * tb), F32), jax.ShapeDtypeStruct((n_e, n_blk * tb), jnp.int32)],
        compiler_params=_params("parallel"),
        name="moe_route",
    )(x, w_router_t)


def _moe_kernel(cnt_ref, x_ref, gate_ref, slot_ref, wg_ref, wu_ref, wd_ref, g_ref, b_ref, o_ref,
                xt_ref, xc_ref, acc_ref, yt_ref, *, n_e, n_f, ts, n_tokens):
    i, e, f = pl.program_id(0), pl.program_id(1), pl.program_id(2)
    tb = x_ref.shape[0]
    n_sub = (cnt_ref[i * n_e + e] + ts - 1) // ts
    row = lax.broadcasted_iota(jnp.int32, (ts, tb), 0)

    def one_hot(sub):
        return (slot_ref[pl.ds(e, 1), :] == row + sub * ts).astype(BF16)

    @pl.when((e == 0) & (f == 0))
    def _():
        in_range = i * tb + lax.broadcasted_iota(jnp.int32, (tb, 1), 0) < n_tokens
        xt_ref[...] = jnp.where(in_range, x_ref[...], 0.0).T.astype(BF16)
        yt_ref[...] = jnp.zeros_like(yt_ref)

    tf, d = wg_ref.shape[1], wd_ref.shape[1]
    halves = lambda n: (slice(0, n // 2), slice(n // 2, n))

    @pl.when(f == 0)
    def _():
        def gather(sub, c):
            pick = one_hot(sub)
            for rows in halves(d):
                xc_ref[sub, rows, :] = _dot_nt(xt_ref[rows, :], pick).astype(BF16)
            return c

        lax.fori_loop(0, n_sub, gather, 0)

    def expert_ffn(sub, c):
        xc = xc_ref[sub]
        act = []
        for rows in halves(tf):
            h = jnp.dot(wg_ref[0, rows, :], xc, preferred_element_type=F32)
            u = jnp.dot(wu_ref[0, rows, :], xc, preferred_element_type=F32)
            act.append((_silu(h) * u).astype(BF16))
        act = jnp.concatenate(act, axis=0)
        part = jnp.concatenate([jnp.dot(wd_ref[0, rows, :], act, preferred_element_type=F32) for rows in halves(d)], axis=0)

        @pl.when(f == 0)
        def _():
            acc_ref[sub] = part

        @pl.when(f > 0)
        def _():
            acc_ref[sub] += part

        return c

    lax.fori_loop(0, n_sub, expert_ffn, 0)

    @pl.when(f == n_f - 1)
    def _():
        def scatter(sub, c):
            pick = one_hot(sub)
            gate = gate_ref[pl.ds(e, 1), :]
            for rows in halves(d):
                back = jnp.dot(acc_ref[sub, rows, :].astype(BF16), pick, preferred_element_type=F32)
                yt_ref[rows, :] += gate * back
            return c

        lax.fori_loop(0, n_sub, scatter, 0)

    @pl.when((e == n_e - 1) & (f == n_f - 1))
    def _():
        o_ref[...] = _layer_norm(ALPHA * x_ref[...] + yt_ref[...].T, g_ref[...], b_ref[...])


def moe_residual_ln(x, gates, slots, counts, w_gu_t, w_down_t, slot, g, b, *, tb, ts, tf):
    m, d = x.shape
    _, n_e, ff2, _ = w_gu_t.shape
    n_f = ff2 // 2 // tf
    n_sub = pl.cdiv(tb, ts)
    return pl.pallas_call(
        functools.partial(_moe_kernel, n_e=n_e, n_f=n_f, ts=ts, n_tokens=m),
        grid_spec=pltpu.PrefetchScalarGridSpec(
            num_scalar_prefetch=1,
            grid=(pl.cdiv(m, tb), n_e, n_f),
            in_specs=[pl.BlockSpec((tb, d), lambda i, e, f, c: (i, 0)),
                      pl.BlockSpec((n_e, tb), lambda i, e, f, c: (0, i)),
                      pl.BlockSpec((n_e, tb), lambda i, e, f, c: (0, i)),
                      pl.BlockSpec((None, 1, tf, d), lambda i, e, f, c: (slot, e, f, 0)),
                      pl.BlockSpec((None, 1, tf, d), lambda i, e, f, c: (slot, e, n_f + f, 0)),
                      pl.BlockSpec((None, 1, d, tf), lambda i, e, f, c: (slot, e, 0, f)),
                      pl.BlockSpec((1, d), lambda i, e, f, c: (0, 0)),
                      pl.BlockSpec((1, d), lambda i, e, f, c: (0, 0))],
            out_specs=pl.BlockSpec((tb, d), lambda i, e, f, c: (i, 0)),
            scratch_shapes=[pltpu.VMEM((d, tb), BF16), pltpu.VMEM((n_sub, d, ts), BF16),
                            pltpu.VMEM((n_sub, d, ts), F32), pltpu.VMEM((d, tb), F32)]),
        out_shape=jax.ShapeDtypeStruct((m, d), F32),
        compiler_params=_params("parallel", "arbitrary", "arbitrary"),
        name="moe_residual_ln",
    )(counts, x, gates, slots, w_gu_t, w_gu_t, w_down_t, g.reshape(1, d), b.reshape(1, d))


def _hgrn_lower_bound(lbp, layer):
    m = jnp.max(lbp, axis=0, keepdims=True)
    e = jnp.exp(lbp - m)
    p = e / jnp.sum(e, axis=0, keepdims=True)
    c = p[0:1]
    for i in range(1, layer + 1):
        c = c + p[i:i + 1]
    return c - p[0:1]


def _hgrn_gates(q, fx, lb):
    log_lb = jnp.log(jnp.maximum(lb, LB_FLOOR))
    log1m = jnp.log1p(-lb)
    log_sig = -(jnp.maximum(-fx, 0.0) + jnp.log1p(jnp.exp(-jnp.abs(fx))))
    b2 = log1m + log_sig
    log_f = jnp.maximum(log_lb, b2) + jnp.log1p(jnp.exp(-jnp.abs(log_lb - b2)))
    kk = (1.0 - lb) * _sigmoid(-fx) - (jnp.maximum(lb, LB_FLOOR) - lb)
    return _silu(q), log_f, kk


def _hgrn_out_norm(o, g, ng):
    o = o * lax.rsqrt(jnp.mean(o * o, axis=-1, keepdims=True) + RMS_EPS)
    return o * ng * _silu(g)


def _dot_nt(a, b):
    return lax.dot_general(a, b, (((1,), (1,)), ((), ())), preferred_element_type=F32)


def _dot_tn(a, b):
    return lax.dot_general(a, b, (((0,), (0,)), ((), ())), preferred_element_type=F32)


def _gla_kernel(q_ref, f_ref, i_ref, g_ref, lbp_ref, ng_ref, o_ref, s_out_ref, st_ref, *, layer, n_t, tl, hp):
    t = pl.program_id(2)
    c_len, sub = HG_CHUNK, HG_SUB
    n_sub = c_len // sub
    n_stack = sub * (n_sub * (n_sub - 1) // 2)

    @pl.when(t == 0)
    def _():
        st_ref[...] = jnp.zeros_like(st_ref)

    lb = _hgrn_lower_bound(lbp_ref[...], layer)
    ng = ng_ref[...]
    row = lax.broadcasted_iota(jnp.int32, (c_len, c_len), 0)
    col = lax.broadcasted_iota(jnp.int32, (c_len, c_len), 1)
    tri = jnp.where(col <= row, LOG2_E, 0.0)
    srow = lax.broadcasted_iota(jnp.int32, (c_len, n_stack), 0) // sub
    scol = lax.broadcasted_iota(jnp.int32, (c_len, n_stack), 1)
    part = jnp.zeros((c_len, n_stack), jnp.int32)
    for p in range(1, n_sub):
        part = part + (scol >= sub * (p * (p - 1) // 2)).astype(jnp.int32)
    inter_mask = srow == part

    def one_head(hh, r):
        lanes = slice(hh * HG_DK, (hh + 1) * HG_DK)
        qs, log_f, kk = _hgrn_gates(q_ref[r, lanes], f_ref[r, lanes], lb[:, lanes])
        v = i_ref[r, lanes]
        cum = jnp.dot(tri, log_f, preferred_element_type=F32, precision=HIGHEST)
        last = cum[c_len - 1:c_len, :]
        st = st_ref[hh]
        o = _dot_nt((qs * jnp.exp2(cum)).astype(BF16), st.astype(BF16))
        bounds = [cum[sub * a - 1:sub * a, :] for a in range(1, n_sub)]
        cq = jnp.concatenate([cum[0:sub]] + [jnp.broadcast_to(b, (sub, HG_DK)) for b in bounds], axis=0)
        q_rel = (qs * jnp.exp2(cum - cq)).astype(BF16)
        k_st = jnp.concatenate([kk[0:sub * a] * jnp.exp2(bounds[a - 1] - cum[0:sub * a]) for a in range(1, n_sub)], axis=0)
        v_st = jnp.concatenate([v[0:sub * a] for a in range(1, n_sub)], axis=0)
        att = jnp.where(inter_mask, _dot_nt(q_rel, k_st.astype(BF16)), 0.0)
        o = o + jnp.dot(att.astype(BF16), v_st.astype(BF16), preferred_element_type=F32)
        parts = []
        for a in range(n_sub):
            base = sub * a
            rest = o[base:base + sub]
            for lo in range(0, sub, SUBLANES):
                tail = slice(base + lo, base + sub)
                qa, ca = qs[tail], cum[tail]
                row_t = lo + lax.broadcasted_iota(jnp.int32, (sub - lo, 1), 0)
                for s in range(lo, lo + SUBLANES):
                    src = slice(base + s, base + s + 1)
                    w = jnp.sum(qa * (kk[src] * jnp.exp2(ca - cum[src])), axis=-1, keepdims=True)
                    rest = rest + jnp.where(row_t >= s, w, 0.0) * v[src]
                parts.append(rest[:SUBLANES])
                rest = rest[SUBLANES:]
        o = jnp.concatenate(parts, axis=0)
        o_ref[r, lanes] = _hgrn_out_norm(o, g_ref[r, lanes], ng[:, lanes]).astype(o_ref.dtype)
        k_end = (kk * jnp.exp2(last - cum)).astype(BF16)
        st_ref[hh] = st * jnp.exp2(last) + _dot_tn(v.astype(BF16), k_end)

    def chunk(c, carry):
        r = pl.ds(pl.multiple_of(c * c_len, c_len), c_len)
        for hh in range(hp):
            one_head(hh, r)
        return carry

    lax.fori_loop(0, tl // c_len, chunk, 0)

    @pl.when(t == n_t - 1)
    def _():
        for hh in range(hp):
            s_out_ref[0, hh] = st_ref[hh].T


def hgrn_prompt(z, lb_param, norm_g, *, layer, batch, tl):
    m = z.shape[0]
    n_t = m // batch // tl
    h, dk = HG_HEADS, HG_DK
    hp = HG_HEADS_PER_STEP
    n_hg = h // hp
    zspec = lambda part: pl.BlockSpec((tl, hp * dk), lambda b, hh, t: (b * n_t + t, part * n_hg + hh))
    return pl.pallas_call(
        functools.partial(_gla_kernel, layer=layer, n_t=n_t, tl=tl, hp=hp),
        grid=(batch, n_hg, n_t),
        in_specs=[zspec(0), zspec(1), zspec(2), zspec(3),
                  pl.BlockSpec((DEPTH, hp * dk), lambda b, hh, t: (0, hh)),
                  pl.BlockSpec((1, hp * dk), lambda b, hh, t: (0, hh))],
        out_specs=[pl.BlockSpec((tl, hp * dk), lambda b, hh, t: (b * n_t + t, hh)),
                   pl.BlockSpec((1, hp, dk, dk), lambda b, hh, t: (b, hh, 0, 0))],
        out_shape=[jax.ShapeDtypeStruct((m, h * dk), BF16),
                   jax.ShapeDtypeStruct((batch, h, dk, dk), F32)],
        scratch_shapes=[pltpu.VMEM((hp, dk, dk), F32)],
        compiler_params=_params("parallel", "parallel", "arbitrary"),
        name="hgrn_prompt",
    )(z, z, z, z, lb_param, norm_g.reshape(1, h * dk))


def _hgrn_step_kernel(q_ref, f_ref, i_ref, g_ref, lbp_ref, ng_ref, s0_ref, o_ref, s_ref,
                      qt_ref, ft_ref, kt_ref, oacc_ref, *, layer, n_j, sb):
    j = pl.program_id(1)
    n_seq = q_ref.shape[0]

    def split3(x, ref):
        hi = x.astype(BF16)
        r1 = x - hi.astype(F32)
        mid = r1.astype(BF16)
        ref[0], ref[1], ref[2] = hi, mid, (r1 - mid.astype(F32)).astype(BF16)

    @pl.when(j == 0)
    def _():
        lb = _hgrn_lower_bound(lbp_ref[...], layer)
        qs, log_f, kk = _hgrn_gates(q_ref[...], f_ref[...], lb)
        split3(qs.T, qt_ref)
        split3(jnp.exp(log_f).T, ft_ref)
        split3(kk.T, kt_ref)

    seq = lax.broadcasted_iota(jnp.int32, (n_seq, HG_DK), 0)
    for i in range(sb):
        b = j * sb + i
        pick = (seq == b).astype(BF16)
        col = lambda ref: sum(jnp.dot(ref[p], pick, preferred_element_type=F32) for p in range(3))
        s_new = s0_ref[i, 0] * col(ft_ref) + col(kt_ref) * i_ref[pl.ds(b, 1), :]
        s_ref[i, 0] = s_new
        oacc_ref[pl.ds(b, 1), :] = jnp.sum(col(qt_ref) * s_new, axis=0, keepdims=True)

    @pl.when(j == n_j - 1)
    def _():
        o_ref[...] = _hgrn_out_norm(oacc_ref[...], g_ref[...], ng_ref[...]).astype(o_ref.dtype)


def hgrn_step(z, s0, slot, lb_param, norm_g, *, layer, sb):
    n_seq = z.shape[0]
    h, dk = HG_HEADS, HG_DK
    n_j = n_seq // sb
    zspec = lambda off: pl.BlockSpec((n_seq, dk), lambda hh, j: (0, off + hh))
    return pl.pallas_call(
        functools.partial(_hgrn_step_kernel, layer=layer, n_j=n_j, sb=sb),
        grid=(h, n_j),
        in_specs=[zspec(0), zspec(h), zspec(2 * h), zspec(3 * h),
                  pl.BlockSpec((DEPTH, dk), lambda hh, j: (0, hh)),
                  pl.BlockSpec((1, dk), lambda hh, j: (0, hh)),
                  pl.BlockSpec((None, sb, 1, dk, dk), lambda hh, j: (slot, j, hh, 0, 0))],
        out_specs=[pl.BlockSpec((n_seq, dk), lambda hh, j: (0, hh)),
                   pl.BlockSpec((sb, 1, dk, dk), lambda hh, j: (j, hh, 0, 0))],
        out_shape=[jax.ShapeDtypeStruct((n_seq, h * dk), BF16),
                   jax.ShapeDtypeStruct(s0.shape[1:], F32)],
        scratch_shapes=[pltpu.VMEM((3, dk, n_seq), BF16)] * 3 + [pltpu.VMEM((n_seq, dk), F32)],
        compiler_params=_params("parallel", "arbitrary"),
        name="hgrn_step",
    )(z, z, z, z, lb_param, norm_g.reshape(1, h * dk), s0)


def _gelu_tanh(x):
    return x * (0.5 * (1.0 + jnp.tanh(0.7978845608028654 * (x + 0.044715 * (x * x * x)))))


def _rg_decay_and_input(xc, wg_ref, bg_ref, lam_ref):
    xb = xc.astype(BF16)
    gates = []
    for gi in range(2):
        blocks = [jnp.dot(xb[:, n * RG_BS:(n + 1) * RG_BS], wg_ref[gi, n], preferred_element_type=F32)
                  for n in range(RG_BLOCKS)]
        gates.append(jnp.concatenate(blocks, axis=1) + bg_ref[gi:gi + 1, :])
    r = _sigmoid(gates[0])
    i = _sigmoid(gates[1])
    neg_lam = -lam_ref[...]
    softplus = jnp.maximum(neg_lam, 0.0) + jnp.log1p(jnp.exp(-jnp.abs(neg_lam)))
    log_a = -RG_C * r * softplus
    one_minus_a2 = -jnp.tanh(log_a) * (jnp.exp(2.0 * log_a) + 1.0)
    return jnp.exp(log_a), jnp.sqrt(one_minus_a2) * i * xc


def _rg_prompt_kernel(y_ref, x_ref, cw_ref, cb_ref, wg_ref, bg_ref, lam_ref,
                      o_ref, conv_out_ref, h_out_ref, xpad_ref, h_ref, *, n_t, tl):
    t = pl.program_id(1)
    halo = 8

    @pl.when(t == 0)
    def _():
        xpad_ref[0:halo, :] = jnp.zeros((halo, RG_WIDTH), F32)
        h_ref[...] = jnp.zeros_like(h_ref)

    x = x_ref[...]
    xpad_ref[halo:halo + tl, :] = x
    xc = cb_ref[...]
    for j in range(RG_CONV):
        off = halo - (RG_CONV - 1) + j
        xc = xc + xpad_ref[off:off + tl, :] * cw_ref[j:j + 1, :]
    xpad_ref[0:halo, :] = x[tl - halo:tl, :]
    a, u = _rg_decay_and_input(xc, wg_ref, bg_ref, lam_ref)
    row = lax.broadcasted_iota(jnp.int32, (tl, 1), 0) % SUBLANES
    s = 1
    while s < SUBLANES:
        keep = row >= s
        a_prev = jnp.where(keep, pltpu.roll(a, s, 0), 1.0)
        u_prev = jnp.where(keep, pltpu.roll(u, s, 0), 0.0)
        u = a * u_prev + u
        a = a * a_prev
        s *= 2
    carry = h_ref[...]
    tiles = []
    for r in range(0, tl, SUBLANES):
        tiles.append(a[r:r + SUBLANES] * carry + u[r:r + SUBLANES])
        carry = tiles[-1][SUBLANES - 1:SUBLANES]
    h = jnp.concatenate(tiles, axis=0)
    h_ref[...] = carry
    o_ref[...] = (_gelu_tanh(y_ref[...]) * h).astype(o_ref.dtype)

    @pl.when(t == n_t - 1)
    def _():
        conv_out_ref[0] = x[tl - (RG_CONV - 1):tl, :]
        h_out_ref[0] = h[tl - 1:tl, :]


def rglru_prompt(z, conv_w, conv_b, w_gate, b_gate, lam, *, batch, tl):
    m = z.shape[0]
    n_t = m // batch // tl
    w = RG_WIDTH
    const = lambda shape: pl.BlockSpec(shape, lambda b, t: (0,) * len(shape))
    return pl.pallas_call(
        functools.partial(_rg_prompt_kernel, n_t=n_t, tl=tl),
        grid=(batch, n_t),
        in_specs=[pl.BlockSpec((tl, w), lambda b, t: (b * n_t + t, 0)),
                  pl.BlockSpec((tl, w), lambda b, t: (b * n_t + t, 1)),
                  const((RG_CONV, w)), const((1, w)), const((2, RG_BLOCKS, RG_BS, RG_BS)), const((2, w)), const((1, w))],
        out_specs=[pl.BlockSpec((tl, w), lambda b, t: (b * n_t + t, 0)),
                   pl.BlockSpec((1, RG_CONV - 1, w), lambda b, t: (b, 0, 0)),
                   pl.BlockSpec((1, 1, w), lambda b, t: (b, 0, 0))],
        out_shape=[jax.ShapeDtypeStruct((m, w), BF16),
                   jax.ShapeDtypeStruct((batch, RG_CONV - 1, w), F32),
                   jax.ShapeDtypeStruct((batch, 1, w), F32)],
        scratch_shapes=[pltpu.VMEM((tl + 8, w), F32), pltpu.VMEM((1, w), F32)],
        compiler_params=_params("parallel", "arbitrary"),
        name="rglru_prompt",
    )(z, z, conv_w, conv_b.reshape(1, w), w_gate, b_gate, lam.reshape(1, w))


def _rg_step_kernel(y_ref, x_ref, buf_ref, h0_ref, cw_ref, cb_ref, wg_ref, bg_ref, lam_ref,
                    o_ref, conv_out_ref, h_out_ref):
    x = x_ref[...]
    xc = cb_ref[...]
    for j in range(RG_CONV - 1):
        xc = xc + buf_ref[:, j, :] * cw_ref[j:j + 1, :]
    xc = xc + x * cw_ref[RG_CONV - 1:RG_CONV, :]
    a, u = _rg_decay_and_input(xc, wg_ref, bg_ref, lam_ref)
    h = a * h0_ref[...] + u
    o_ref[...] = (_gelu_tanh(y_ref[...]) * h).astype(o_ref.dtype)
    h_out_ref[...] = h
    for j in range(RG_CONV - 2):
        conv_out_ref[:, j, :] = buf_ref[:, j + 1, :]
    conv_out_ref[:, RG_CONV - 2, :] = x


def rglru_step(z, conv_buf, h0, conv_w, conv_b, w_gate, b_gate, lam):
    n_seq = z.shape[0]
    w = RG_WIDTH
    const = lambda shape: pl.BlockSpec(shape, lambda i: (0,) * len(shape))
    return pl.pallas_call(
        _rg_step_kernel,
        grid=(1,),
        in_specs=[pl.BlockSpec((n_seq, w), lambda i: (0, 0)),
                  pl.BlockSpec((n_seq, w), lambda i: (0, 1)),
                  const((n_seq, RG_CONV - 1, w)), const((n_seq, w)),
                  const((RG_CONV, w)), const((1, w)), const((2, RG_BLOCKS, RG_BS, RG_BS)), const((2, w)), const((1, w))],
        out_specs=[const((n_seq, w)), const((n_seq, RG_CONV - 1, w)), const((n_seq, w))],
        out_shape=[jax.ShapeDtypeStruct((n_seq, w), BF16),
                   jax.ShapeDtypeStruct((n_seq, RG_CONV - 1, w), F32),
                   jax.ShapeDtypeStruct((n_seq, w), F32)],
        compiler_params=_params("arbitrary"),
        name="rglru_step",
    )(z, z, conv_buf, h0, conv_w, conv_b.reshape(1, w), w_gate, b_gate, lam.reshape(1, w))


NSA_HEADS = 16
NSA_KVH = 4
NSA_GROUP = 4
NSA_DH = 64
NSA_KVW = NSA_KVH * NSA_DH
NSA_ROW = 2 * NSA_KVW
CMP_BLOCK = 32
CMP_HID = 128
SLC_BLOCK = 64
SLC_TOPK = 16
WINDOW = 512
NSA_QBLOCK = 128
MASK_NEG = -1e30
SEL_BIG = 1e9
SEL_KEYS = 1024


def _masked_softmax(s, mask, exp=jnp.exp):
    s = jnp.where(mask, s, MASK_NEG)
    m = jnp.max(s, axis=-1, keepdims=True)
    e = jnp.where(mask, exp(s - m), 0.0)
    return e * (1.0 / jnp.maximum(jnp.sum(e, axis=-1, keepdims=True), 1e-30))


def _top_blocks(score, k, axis):
    pos = lax.broadcasted_iota(jnp.int32, score.shape, axis)
    n = score.shape[axis]

    def pick(_, carry):
        sc, sel = carry
        mx = jnp.max(sc, axis=axis, keepdims=True)
        idx = jnp.min(jnp.where(sc == mx, pos, n), axis=axis, keepdims=True)
        hit = pos == idx
        return jnp.where(hit, -jnp.inf, sc), jnp.where(hit, 1.0, sel)

    return lax.fori_loop(0, k, pick, (score, jnp.zeros(score.shape, F32)))[1]


def _compress_rows(x_refs, pos_ref, w1_ref, w2_ref, n_blk):
    half = n_blk // 2
    tiles = NSA_KVW // LANES
    low = lax.broadcasted_iota(jnp.int32, (n_blk, LANES), 1) < NSA_DH
    outs = []
    for c in range(2):
        cols = [[] for _ in range(NSA_KVH)]
        for l in range(0, CMP_BLOCK, 2):
            for tl in range(tiles):
                ref = x_refs[c * tiles + tl]

                def rows(r, ref=ref, tl=tl):
                    xe = ref[pl.ds(r, half, stride=2 * CMP_BLOCK), :]
                    xo = ref[pl.ds(r + CMP_BLOCK, half, stride=2 * CMP_BLOCK), :]
                    return jnp.concatenate([xe, xo], axis=0) + pos_ref[c, r:r + 1, tl * LANES:(tl + 1) * LANES]

                x0, x1 = rows(l), rows(l + 1)
                cols[2 * tl].append(jnp.where(low, x0, pltpu.roll(x1, NSA_DH, 1)))
                cols[2 * tl + 1].append(jnp.where(low, pltpu.roll(x0, NSA_DH, 1), x1))
        xs = jnp.concatenate([jnp.concatenate(cols[h], axis=1) for h in range(NSA_KVH)], axis=0)
        acc = jnp.dot(xs.astype(BF16), w1_ref[c], preferred_element_type=F32)
        out = jnp.dot(_silu(acc).astype(BF16), w2_ref[c], preferred_element_type=F32)
        outs.append(jnp.concatenate([out[h * n_blk:(h + 1) * n_blk] for h in range(NSA_KVH)], axis=1))
    return jnp.concatenate(outs, axis=1)


def _compress_kernel(x0_ref, x1_ref, x2_ref, x3_ref, pos_ref, w1_ref, w2_ref, even_ref, odd_ref, *, n_blk):
    full = _compress_rows((x0_ref, x1_ref, x2_ref, x3_ref), pos_ref, w1_ref, w2_ref, n_blk)
    even_ref[0] = full[:n_blk // 2]
    odd_ref[0] = full[n_blk // 2:]


def nsa_compress(kv, col_block, pos4, w1, w2, *, batch, n_blk):
    m = kv.shape[0]
    n_t = m // batch // (n_blk * CMP_BLOCK)
    half = n_blk // 2
    const = lambda shape: pl.BlockSpec(shape, lambda b, t: (0,) * len(shape))
    out = jax.ShapeDtypeStruct((batch, n_t * half, NSA_ROW), F32)
    tiles = NSA_ROW // LANES
    lane_tile = lambda i: pl.BlockSpec((n_blk * CMP_BLOCK, LANES), lambda b, t: (b * n_t + t, col_block * tiles + i))
    return pl.pallas_call(
        functools.partial(_compress_kernel, n_blk=n_blk),
        grid=(batch, n_t),
        in_specs=[lane_tile(i) for i in range(tiles)]
        + [const((2, CMP_BLOCK, NSA_KVW)), const((2, CMP_BLOCK * NSA_DH, CMP_HID)), const((2, CMP_HID, NSA_DH))],
        out_specs=[pl.BlockSpec((1, half, NSA_ROW), lambda b, t: (b, t, 0))] * 2,
        out_shape=[out, out],
        compiler_params=_params("parallel", "parallel"),
        name="nsa_compress",
    )(kv, kv, kv, kv, pos4, w1, w2)


def nsa_prepare_weights(w_in, pos, w1, w2):
    hq = NSA_HEADS * NSA_DH
    kv_end = hq + 3 * NSA_ROW
    w_qkv = w_in[:, :kv_end].astype(BF16)
    w_kvb = w_in[:, hq + NSA_ROW:kv_end].astype(BF16)
    w_g = w_in[:, kv_end:].reshape(-1, NSA_KVH, 3 * NSA_GROUP)
    w_g = jnp.pad(w_g, ((0, 0), (0, 0), (0, LANES - 3 * NSA_GROUP))).reshape(-1, NSA_KVH * LANES).astype(BF16)
    pos4 = jnp.tile(pos, (1, 1, NSA_KVH))
    w1 = w1.reshape(2, CMP_BLOCK * NSA_DH, CMP_HID)
    w_kv_t = w_in[:, hq:kv_end].T.astype(BF16)
    return w_qkv, w_kvb, w_g, w_kv_t, pos4, w1.astype(BF16), w2.astype(BF16)


def _stack_group_queries(q, par):
    half = lax.broadcasted_iota(jnp.int32, (q.shape[0], LANES), 1) // NSA_DH
    parts = []
    for g in range(NSA_GROUP):
        qg = q[:, g * NSA_DH:(g + 1) * NSA_DH]
        parts.append(jnp.where(half == par, jnp.concatenate([qg, qg], axis=1), 0.0))
    return jnp.concatenate(parts, axis=0).astype(BF16)


def _nsa_prompt_kernel(q_ref, gp_ref, ck_ref, cv_ref, ks_ref, vs_ref, kw_ref, vw_ref, o_ref):
    j = pl.program_id(2)
    qb, grp = NSA_QBLOCK, NSA_GROUP
    rows = grp * qb
    qw = grp * NSA_DH
    pair = range(LANES // NSA_DH)
    t_pos = j * qb + lax.broadcasted_iota(jnp.int32, (qb, 1), 0)

    def grouped_softmax(s, mask):
        n = s.shape[-1]
        return _masked_softmax(s.reshape(grp, qb, n), mask[None], jnp.exp2).reshape(rows, n)

    n_c = ck_ref.shape[1]
    n_s = n_c // 2
    ck = ck_ref[0].astype(BF16)
    cv = cv_ref[0].astype(BF16)
    nn = lax.broadcasted_iota(jnp.int32, (1, n_c), 1)
    c_blk = jnp.where(nn < n_s, 2 * nn, 2 * (nn - n_s) + 1)
    c_vis = (c_blk * CMP_BLOCK + CMP_BLOCK - 1) <= t_pos
    n_w = WINDOW + qb
    w0 = pl.multiple_of(jnp.maximum(j * qb - WINDOW, 0), qb)
    dist = t_pos - (w0 + lax.broadcasted_iota(jnp.int32, (1, n_w), 1))
    w_bias = jnp.where((dist >= 0) & (dist <= WINDOW), 0.0, MASK_NEG)[None]
    k_win, v_win = kw_ref[pl.ds(w0, n_w), :], vw_ref[pl.ds(w0, n_w), :]
    blk = lax.broadcasted_iota(jnp.int32, (qb, n_s), 1)
    cur = t_pos // SLC_BLOCK
    forced = (blk == 0) | (blk == cur) | (blk == cur - 1)

    def before_selection(par):
        qp = _stack_group_queries(q_ref[:, par * qw:(par + 1) * qw] * (NSA_DH ** -0.5 * LOG2_E), par)
        p_c = grouped_softmax(_dot_nt(qp, ck), c_vis)
        o_c = jnp.dot(p_c.astype(BF16), cv, preferred_element_type=F32)
        s_w = _dot_nt(qp, k_win).reshape(grp, qb, n_w) + w_bias
        e_w = jnp.exp2(s_w - jnp.max(s_w, axis=-1, keepdims=True)).reshape(rows, n_w)
        own_w = lax.broadcasted_iota(jnp.int32, (n_w, LANES), 1) // NSA_DH == par
        o_w = jnp.dot(e_w.astype(BF16), jnp.where(own_w, v_win, 1.0), preferred_element_type=F32)
        o_w = o_w / jnp.maximum(pltpu.roll(o_w, NSA_DH, 1), 1e-30)
        gates = _sigmoid(gp_ref[:, par * LANES:(par + 1) * LANES])
        gate = lambda br: jnp.concatenate(
            [jnp.broadcast_to(gates[:, 3 * g + br:3 * g + br + 1], (qb, LANES)) for g in range(grp)], axis=0)
        imp = p_c[0:qb]
        for g in range(1, grp):
            imp = imp + p_c[g * qb:(g + 1) * qb]
        imp = imp[:, :n_s] + imp[:, n_s:]
        score = jnp.where(forced, SEL_BIG, jnp.where(blk <= cur, imp, -SEL_BIG))
        return qp, gate(0) * o_c + gate(2) * o_w, gate(1), score.T

    qps, o_cws, gate_s, scores_t = zip(*[before_selection(par) for par in pair])
    sel_t = _top_blocks(jnp.concatenate(scores_t, axis=1), SLC_TOPK, 0)
    sel_bs = [sel_t[:, par * qb:(par + 1) * qb].T.astype(BF16) for par in pair]

    per = SEL_KEYS // SLC_BLOCK
    e_row = lax.broadcasted_iota(jnp.int32, (n_s, SEL_KEYS), 0)
    e_col = lax.broadcasted_iota(jnp.int32, (n_s, SEL_KEYS), 1) // SLC_BLOCK
    k_off = lax.broadcasted_iota(jnp.int32, (1, SEL_KEYS), 1)

    n_kb = (j * qb + qb - 1) // SEL_KEYS + 1

    def key_rows(kb):
        return pl.ds(pl.multiple_of(kb * SEL_KEYS, SEL_KEYS), SEL_KEYS)

    lane_head = lax.broadcasted_iota(jnp.int32, (SEL_KEYS, LANES), 1) // NSA_DH

    def sel_step(kb, carry):
        expand = (e_row == kb * per + e_col).astype(BF16)
        causal = (kb * SEL_KEYS + k_off) <= t_pos
        k_blk, v_blk = ks_ref[key_rows(kb), :], vs_ref[key_rows(kb), :]
        out = []
        for par in pair:
            m, acc = carry[par]
            chosen = jnp.dot(sel_bs[par], expand, preferred_element_type=F32) > 0.5
            bias = jnp.where(chosen & causal, 0.0, MASK_NEG)[None]
            s = _dot_nt(qps[par], k_blk).reshape(grp, qb, SEL_KEYS) + bias
            m_new = jnp.maximum(m, jnp.max(s, axis=-1, keepdims=True))
            alpha = jnp.exp2(m - m_new)
            e = jnp.exp2(s - m_new)
            v_one = jnp.where(lane_head == par, v_blk, 1.0)
            pv = jnp.dot(e.reshape(rows, SEL_KEYS).astype(BF16), v_one, preferred_element_type=F32)
            out.append((m_new, alpha * acc + pv.reshape(grp, qb, LANES)))
        return tuple(out)

    init = tuple((jnp.full((grp, qb, 1), MASK_NEG, F32), jnp.zeros((grp, qb, LANES), F32)) for _ in pair)
    fin = lax.fori_loop(0, n_kb, sel_step, init)
    heads = []
    for par in pair:
        acc = fin[par][1].reshape(rows, LANES)
        o_s = acc / jnp.maximum(pltpu.roll(acc, NSA_DH, 1), 1e-30)
        o = (o_cws[par] + gate_s[par] * o_s)[:, par * NSA_DH:(par + 1) * NSA_DH]
        heads += [o[g * qb:(g + 1) * qb] for g in range(grp)]
    o_ref[...] = jnp.concatenate(heads, axis=1).astype(o_ref.dtype)


def nsa_prompt_attention(qkv, gates_pre, ckv, kvb, *, batch):
    m = qkv.shape[0]
    l_seq = m // batch
    nq = l_seq // NSA_QBLOCK
    n_c = ckv.shape[1]
    per_tile = LANES // NSA_DH
    n_pairs = NSA_KVH // per_tile
    qw = per_tile * NSA_GROUP * NSA_DH
    kv_spec = lambda part: pl.BlockSpec((l_seq, LANES), lambda b, p, j: (b, part * n_pairs + p))
    return pl.pallas_call(
        _nsa_prompt_kernel,
        grid=(batch, n_pairs, nq),
        in_specs=[pl.BlockSpec((NSA_QBLOCK, qw), lambda b, p, j: (b * nq + j, p)),
                  pl.BlockSpec((NSA_QBLOCK, per_tile * LANES), lambda b, p, j: (b * nq + j, p)),
                  pl.BlockSpec((1, n_c, LANES), lambda b, p, j: (b, 0, p)),
                  pl.BlockSpec((1, n_c, LANES), lambda b, p, j: (b, 0, n_pairs + p)),
                  kv_spec(0), kv_spec(1), kv_spec(2), kv_spec(3)],
        out_specs=pl.BlockSpec((NSA_QBLOCK, qw), lambda b, p, j: (b * nq + j, p)),
        out_shape=jax.ShapeDtypeStruct((m, NSA_HEADS * NSA_DH), BF16),
        compiler_params=_params("parallel", "parallel", "arbitrary"),
        name="nsa_prompt_attention",
    )(qkv, gates_pre, ckv, ckv, kvb, kvb, kvb, kvb)


def _paged_compress_kernel(pt_ref, *refs, n_pages, page):
    del pt_ref
    pages = refs[:n_pages]
    pos_ref, w1_ref, w2_ref, even_ref, odd_ref = refs[n_pages:n_pages + 5]
    rows = refs[n_pages + 5:]
    tiles = NSA_KVW // LANES
    for p in range(n_pages):
        for i, r in enumerate(rows):
            r[p * page:(p + 1) * page, :] = pages[p][0, i // tiles, (i % tiles) * LANES:(i % tiles + 1) * LANES, :].T
    n_blk = n_pages * page // CMP_BLOCK
    full = _compress_rows(rows, pos_ref, w1_ref, w2_ref, n_blk)
    even_ref[0] = full[:n_blk // 2]
    odd_ref[0] = full[n_blk // 2:]


def _page_specs(n_pages, page):
    return [pl.BlockSpec((1, 2, NSA_KVW, page), lambda b, pt, p=p: (pt[b, p], 0, 0, 0)) for p in range(n_pages)]


def nsa_compress_paged(cache_t, page_table, pos4, w1, w2):
    n_seq, n_pages = page_table.shape
    page = cache_t.shape[3]
    half = n_pages * page // CMP_BLOCK // 2
    const = lambda shape: pl.BlockSpec(shape, lambda b, pt: (0,) * len(shape))
    out = jax.ShapeDtypeStruct((n_seq, half, NSA_ROW), F32)
    return pl.pallas_call(
        functools.partial(_paged_compress_kernel, n_pages=n_pages, page=page),
        grid_spec=pltpu.PrefetchScalarGridSpec(
            num_scalar_prefetch=1,
            grid=(n_seq,),
            in_specs=_page_specs(n_pages, page)
            + [const((2, CMP_BLOCK, NSA_KVW)), const((2, CMP_BLOCK * NSA_DH, CMP_HID)), const((2, CMP_HID, NSA_DH))],
            out_specs=[pl.BlockSpec((1, half, NSA_ROW), lambda b, pt: (b, 0, 0))] * 2,
            scratch_shapes=[pltpu.VMEM((n_pages * page, LANES), F32)] * (NSA_ROW // LANES)),
        out_shape=[out, out],
        compiler_params=_params("arbitrary"),
        name="nsa_compress_paged",
    )(page_table, *([cache_t] * n_pages), pos4, w1, w2)


def _softmax_with_new_key(s, ok, s_new, new_ok):
    s = jnp.where(ok, s, MASK_NEG)
    s_new = jnp.where(new_ok, s_new, MASK_NEG)
    m = jnp.maximum(jnp.max(s, axis=-1, keepdims=True), s_new)
    e = jnp.where(ok, jnp.exp(s - m), 0.0)
    e_new = jnp.where(new_ok, jnp.exp(s_new - m), 0.0)
    total = jnp.sum(e, axis=-1, keepdims=True) + e_new
    return e, e_new, 1.0 / jnp.maximum(total, 1e-30)


def _nsa_step_kernel(pt_ref, *refs, n_pages, page, past):
    del pt_ref
    pages = refs[:n_pages]
    (q_ref, gp_ref, cke_ref, cko_ref, kvs_new_ref, swa_ref, kvw_new_ref,
     o_ref, swa_out_ref, kt_ref, vt_ref) = refs[n_pages:]
    nh, kw = NSA_HEADS, NSA_KVW
    t = past
    own = (lax.broadcasted_iota(jnp.int32, (nh, kw), 1) // NSA_DH
           == lax.broadcasted_iota(jnp.int32, (nh, kw), 0) // NSA_GROUP)
    q = q_ref[0] * (NSA_DH ** -0.5)
    qp = jnp.where(own, jnp.concatenate([q] * NSA_KVH, axis=1), 0.0)
    qpb = qp.astype(BF16)

    half = cke_ref.shape[1]
    n_c = 2 * half
    ce, co = cke_ref[0], cko_ref[0]
    ck = jnp.concatenate([ce[:, :kw], co[:, :kw]], axis=0).astype(BF16)
    cv = jnp.concatenate([ce[:, kw:], co[:, kw:]], axis=0).astype(BF16)
    nn = lax.broadcasted_iota(jnp.int32, (1, n_c), 1)
    c_blk = jnp.where(nn < half, 2 * nn, 2 * (nn - half) + 1)
    p_c = _masked_softmax(_dot_nt(qpb, ck), (c_blk * CMP_BLOCK + CMP_BLOCK - 1) <= t)
    o_c = jnp.dot(p_c.astype(BF16), cv, preferred_element_type=F32)
    same_group = (lax.broadcasted_iota(jnp.int32, (nh, nh), 0) // NSA_GROUP
                  == lax.broadcasted_iota(jnp.int32, (nh, nh), 1) // NSA_GROUP).astype(F32)
    imp = lax.dot_general(p_c, same_group, (((0,), (0,)), ((), ())), preferred_element_type=F32, precision=HIGHEST)
    imp = imp[:half] + imp[half:]
    imp = jnp.concatenate([imp, jnp.zeros((LANES - half, nh), F32)], axis=0)
    blk = lax.broadcasted_iota(jnp.int32, (LANES, nh), 0)
    cur = t // SLC_BLOCK
    forced = (blk == 0) | (blk == cur) | (blk == cur - 1)
    sel_t = _top_blocks(jnp.where(forced, SEL_BIG, jnp.where(blk <= cur, imp, -SEL_BIG)), SLC_TOPK, 0)

    for p in range(n_pages):
        kt_ref[:, p * page:(p + 1) * page] = pages[p][0, 0].astype(BF16)
        vt_ref[:, p * page:(p + 1) * page] = pages[p][0, 1].astype(BF16)
    n_keys = n_pages * page
    pos = lax.broadcasted_iota(jnp.int32, (LANES, n_keys + LANES), 1)
    expand = (lax.broadcasted_iota(jnp.int32, (LANES, n_keys + LANES), 0)
              == jnp.minimum(pos, t) // SLC_BLOCK).astype(BF16)
    chosen = _dot_tn(sel_t.astype(BF16), expand) > 0.5
    ok = chosen[:, :n_keys] & (lax.broadcasted_iota(jnp.int32, (1, n_keys), 1) <= t)
    kv_new = kvs_new_ref[0]
    s_new = jnp.sum(qp * kv_new[:, :kw], axis=-1, keepdims=True)
    scores = jnp.dot(qpb, kt_ref[...], preferred_element_type=F32)
    e, e_new, inv = _softmax_with_new_key(scores, ok, s_new, chosen[:, n_keys:n_keys + 1])
    o_s = (_dot_nt(e.astype(BF16), vt_ref[...]) + e_new * kv_new[:, kw:]) * inv

    n_w = swa_ref.shape[3]
    dist = t - (t - n_w + lax.broadcasted_iota(jnp.int32, (1, n_w), 1))
    w_ok = (dist >= 0) & (dist <= WINDOW) & (t - dist >= 0)
    kvw_new = kvw_new_ref[0]
    s_wn = jnp.sum(qp * kvw_new[:, :kw], axis=-1, keepdims=True)
    new_ok = lax.broadcasted_iota(jnp.int32, (nh, 1), 0) >= 0
    scores = jnp.dot(qpb, swa_ref[0, 0].astype(BF16), preferred_element_type=F32)
    e, e_new, inv = _softmax_with_new_key(scores, w_ok, s_wn, new_ok)
    o_w = (_dot_nt(e.astype(BF16), swa_ref[0, 1].astype(BF16)) + e_new * kvw_new[:, kw:]) * inv

    g = _sigmoid(gp_ref[0])
    o = jnp.where(own, g[:, 0:1] * o_c + g[:, 1:2] * o_s + g[:, 2:3] * o_w, 0.0)
    o_h = o[:, :NSA_DH]
    for h in range(1, NSA_KVH):
        o_h = o_h + o[:, h * NSA_DH:(h + 1) * NSA_DH]
    o_ref[0] = o_h
    last = lax.broadcasted_iota(jnp.int32, (kw, n_w), 1) == n_w - 1
    diag = lax.broadcasted_iota(jnp.int32, (kw, kw), 0) == lax.broadcasted_iota(jnp.int32, (kw, kw), 1)
    for c in range(2):
        column = jnp.sum(jnp.where(diag, kvw_new[:, c * kw:(c + 1) * kw], 0.0), axis=-1, keepdims=True)
        swa_out_ref[0, c] = jnp.where(last, column, pltpu.roll(swa_ref[0, c], n_w - 1, 1))


def nsa_step_attention(q3, gates3, cke, cko, cache_slc_t, page_table, kvs_new, swa_t, kvw_new, *, past):
    n_seq, n_pages = page_table.shape
    page = cache_slc_t.shape[3]
    seq = lambda a: pl.BlockSpec((1,) + a.shape[1:], lambda b, pt: (b,) + (0,) * (a.ndim - 1))
    return pl.pallas_call(
        functools.partial(_nsa_step_kernel, n_pages=n_pages, page=page, past=past),
        grid_spec=pltpu.PrefetchScalarGridSpec(
            num_scalar_prefetch=1,
            grid=(n_seq,),
            in_specs=_page_specs(n_pages, page) + [seq(a) for a in (q3, gates3, cke, cko, kvs_new, swa_t, kvw_new)],
            out_specs=[seq(q3), seq(swa_t)],
            scratch_shapes=[pltpu.VMEM((NSA_KVW, n_pages * page), BF16)] * 2),
        out_shape=[jax.ShapeDtypeStruct(q3.shape, F32), jax.ShapeDtypeStruct(swa_t.shape, F32)],
        compiler_params=_params("arbitrary"),
        name="nsa_step_attention",
    )(page_table, *([cache_slc_t] * n_pages), q3, gates3, cke, cko, kvs_new, swa_t, kvw_new)


PROMPT_TM = 1024
FFN_TM = 512
HG_TL = 512
RG_TL = 256
FFN_TF = 1408
CMP_TILE_BLOCKS = 64
MOE_TB = 896
MOE_TS = 256


def _ffn_layer(x, w_gu, w_down, g, b, *, tm):
    return ffn_residual_ln(x, w_gu.astype(BF16), w_down.astype(BF16), g, b, tm=tm, tf=FFN_TF)


def _moe_layer(x, w_router, w_gu_t, w_down_t, slot, g, b):
    m = x.shape[0]
    tb = min(MOE_TB, m)
    gates, slots = moe_route(x, w_router.T, tb=tb)
    n_e = gates.shape[0]
    counts = (jnp.max(slots.reshape(n_e, -1, tb), axis=-1) + 1).T.reshape(-1)
    return moe_residual_ln(x, gates, slots, counts, w_gu_t, w_down_t, slot, g, b,
                           tb=tb, ts=min(MOE_TS, tb), tf=FFN_TF)


def kernel(x_prompt, x_sample, state_hgrn, state_rglru_conv, state_rglru_h, cache_nsa_cmp, cache_nsa_slc, cache_nsa_swa, page_table, ln_g, ln_b, hgrn_w_in, hgrn_lb, hgrn_norm_g, hgrn_w_out, rg_w_in, rg_conv_w, rg_conv_b, rg_w_gate, rg_b_gate, rg_lambda, rg_w_out, nsa_w_in, nsa_cmp_pos, nsa_cmp_w1, nsa_cmp_w2, nsa_w_out, ffn_w_gu, ffn_w_down, moe_router, moe_w_gu, moe_w_down):
    bp, l_seq, d = x_prompt.shape
    n_seq = x_sample.shape[0]
    past = page_table.shape[1] * cache_nsa_cmp.shape[2]
    xp = x_prompt.reshape(bp * l_seq, d)
    xs = x_sample.reshape(n_seq, d)
    tm_p, tm_s = PROMPT_TM, n_seq
    kv_shape = (2, NSA_KVH, NSA_DH)
    moe_w_gu_t = jnp.swapaxes(moe_w_gu, 2, 3).astype(BF16)
    moe_w_down_t = jnp.swapaxes(moe_w_down, 2, 3).astype(BF16)
    hg_p, hg_s, rc_p, rc_s, rh_p, rh_s = [], [], [], [], [], []
    cm_p, cm_s, sl_p, sl_s, sw_p, sw_s = [], [], [], [], [], []
    for layer in range(DEPTH):
        kind, slot = layer % 3, layer // 3
        g0, b0 = ln_g[layer, 0], ln_b[layer, 0]
        if kind == 0:
            w_in, w_out = hgrn_w_in[slot].astype(BF16), hgrn_w_out[slot].astype(BF16)
            op, st_p = hgrn_prompt(matmul(xp, w_in, tm=tm_p), hgrn_lb, hgrn_norm_g[slot],
                                   layer=layer, batch=bp, tl=HG_TL)
            os_, st_s = hgrn_step(matmul(xs, w_in, tm=tm_s), state_hgrn, slot, hgrn_lb, hgrn_norm_g[slot],
                                  layer=layer, sb=8)
            hg_p.append(st_p)
            hg_s.append(st_s)
        elif kind == 1:
            w_in, w_out = rg_w_in[slot].astype(BF16), rg_w_out[slot].astype(BF16)
            rg_w = (rg_conv_w[slot], rg_conv_b[slot], rg_w_gate[slot].astype(BF16), rg_b_gate[slot], rg_lambda[slot])
            op, cb_p, h_p = rglru_prompt(matmul(xp, w_in, tm=tm_p), *rg_w, batch=bp, tl=RG_TL)
            os_, cb_s, h_s = rglru_step(matmul(xs, w_in, tm=tm_s), state_rglru_conv[slot], state_rglru_h[slot], *rg_w)
            rc_p.append(cb_p)
            rc_s.append(cb_s)
            rh_p.append(h_p[:, 0])
            rh_s.append(h_s)
        else:
            w_qkv, w_kvb, w_g, w_kv_t, pos4, w1, w2 = nsa_prepare_weights(
                nsa_w_in[slot], nsa_cmp_pos[slot], nsa_cmp_w1[slot], nsa_cmp_w2[slot])
            w_out = nsa_w_out[slot].astype(BF16)
            hq = NSA_HEADS * NSA_DH
            qkv = matmul(xp, w_qkv[:, :hq + NSA_ROW], tm=tm_p)
            kvb = matmul(xp, w_kvb, tm=tm_p, out_dtype=BF16)
            gates_pre = matmul(xp, w_g, tm=tm_p)
            cke, cko = nsa_compress(qkv, hq // NSA_ROW, pos4, w1, w2, batch=bp, n_blk=CMP_TILE_BLOCKS)
            op = nsa_prompt_attention(qkv, gates_pre, jnp.concatenate([cke, cko], axis=1), kvb, batch=bp)
            as_rows = lambda a: a.reshape(bp, *kv_shape, -1).transpose(0, 4, 1, 2, 3)
            kv_c, kv_s, kv_w = matmul_t(xp, w_kv_t, batch=bp, n_out=3, tm=tm_p)
            cm_p.append(as_rows(kv_c))
            sl_p.append(as_rows(kv_s))
            sw_p.append(as_rows(kv_w[:, :, l_seq - min(WINDOW, l_seq):]))
            qkv = matmul(xs, w_qkv, tm=tm_s)
            gates_pre = matmul(xs, w_g, tm=tm_s)
            rows_last = lambda c: c.transpose(0, 2, 3, 4, 1).reshape(c.shape[0], 2, NSA_KVW, c.shape[1])
            cke, cko = nsa_compress_paged(rows_last(cache_nsa_cmp[slot]), page_table, pos4, w1, w2)
            gates3 = gates_pre.reshape(n_seq, NSA_KVH, LANES)[:, :, :3 * NSA_GROUP].reshape(n_seq, NSA_HEADS, 3)
            kv = qkv[:, hq:].reshape(n_seq, 3, 1, NSA_ROW)
            o3, swa_new = nsa_step_attention(qkv[:, :hq].reshape(n_seq, NSA_HEADS, NSA_DH), gates3, cke, cko,
                                             rows_last(cache_nsa_slc[slot]), page_table, kv[:, 1],
                                             rows_last(cache_nsa_swa[slot]), kv[:, 2], past=past)
            os_ = o3.reshape(n_seq, hq)
            cm_s.append(kv[:, 0].reshape(n_seq, 1, *kv_shape))
            sl_s.append(kv[:, 1].reshape(n_seq, 1, *kv_shape))
            sw_s.append(swa_new.reshape(n_seq, *kv_shape, -1).transpose(0, 4, 1, 2, 3))
        xp = matmul_residual_ln(op, w_out, xp, g0, b0, tm=tm_p)
        xs = matmul_residual_ln(os_, w_out, xs, g0, b0, tm=tm_s)
        g1, b1 = ln_g[layer, 1], ln_b[layer, 1]
        if layer % 2 == 0:
            fw = (ffn_w_gu[layer // 2], ffn_w_down[layer // 2])
            xp = _ffn_layer(xp, *fw, g1, b1, tm=FFN_TM)
            xs = _ffn_layer(xs, *fw, g1, b1, tm=tm_s)
        else:
            fw = (moe_router[layer // 2], moe_w_gu_t, moe_w_down_t, layer // 2)
            xp = _moe_layer(xp, *fw, g1, b1)
            xs = _moe_layer(xs, *fw, g1, b1)
    return (xp.reshape(bp, l_seq, d), xs.reshape(n_seq, 1, d),
            jnp.stack(hg_p), jnp.stack(hg_s), jnp.stack(rc_p), jnp.stack(rc_s), jnp.stack(rh_p), jnp.stack(rh_s),
            jnp.stack(cm_p), jnp.stack(cm_s), jnp.stack(sl_p), jnp.stack(sl_s), jnp.stack(sw_p), jnp.stack(sw_s))
```

```python
import functools

import jax
import jax.numpy as jnp
from jax import lax
from jax.experimental import pallas as pl
from jax.experimental.pallas import tpu as pltpu

F32 = jnp.float32
BF16 = jnp.bfloat16
HIGHEST = lax.Precision.HIGHEST

D_MODEL = 1024
DEPTH = 4
ALPHA = (2 * DEPTH) ** 0.25
LN_EPS = 1e-5
RMS_EPS = 1e-6
HG_HEADS = 8
HG_DK = 128
HG_CHUNK = 64
HG_SUB = 16
HG_HEADS_PER_STEP = 8
LB_FLOOR = 1e-30
RG_WIDTH = 1280
RG_BLOCKS = 10
RG_BS = 128
RG_CONV = 4
RG_C = 8.0
D_FF = 2816
LOG2_E = 1.4426950408889634

SUBLANES = 8
LANES = 128
VMEM_LIMIT_BYTES = 56 * 1024 * 1024


def _params(*sem):
    return pltpu.CompilerParams(dimension_semantics=sem, vmem_limit_bytes=VMEM_LIMIT_BYTES)


def _layer_norm(v, g, b):
    mu = jnp.mean(v, axis=-1, keepdims=True)
    vc = v - mu
    var = jnp.mean(vc * vc, axis=-1, keepdims=True)
    return vc * lax.rsqrt(var + LN_EPS) * g + b


def _sigmoid(x):
    return 1.0 / (1.0 + jnp.exp(-x))


def _silu(x):
    return x * _sigmoid(x)


def _mm_kernel(x_ref, w_ref, o_ref, xb_ref):
    @pl.when(pl.program_id(1) == 0)
    def _():
        xb_ref[...] = x_ref[...].astype(BF16)

    o_ref[...] = jnp.dot(xb_ref[...], w_ref[...], preferred_element_type=F32).astype(o_ref.dtype)


MAX_COL_TILE = 1280


def _col_tile(n):
    return max(t for t in range(LANES, min(n, MAX_COL_TILE) + 1, LANES) if n % t == 0)


def matmul(x, w, *, tm, out_dtype=F32):
    m, k = x.shape
    n = w.shape[1]
    tn = _col_tile(n)
    return pl.pallas_call(
        _mm_kernel,
        grid=(m // tm, n // tn),
        in_specs=[pl.BlockSpec((tm, k), lambda i, j: (i, 0)),
                  pl.BlockSpec((k, tn), lambda i, j: (0, j))],
        out_specs=pl.BlockSpec((tm, tn), lambda i, j: (i, j)),
        out_shape=jax.ShapeDtypeStruct((m, n), out_dtype),
        scratch_shapes=[pltpu.VMEM((tm, k), BF16)],
        compiler_params=_params("parallel", "arbitrary"),
        name="matmul",
    )(x, w)


def _mm_t_kernel(x_ref, wt_ref, *o_refs):
    xb = x_ref[...].astype(BF16)
    n = o_refs[0].shape[1]
    for r, o_ref in enumerate(o_refs):
        o_ref[0] = _dot_nt(wt_ref[r * n:(r + 1) * n, :], xb)


def matmul_t(x, w_t, *, batch, n_out, tm):
    m, k = x.shape
    n = w_t.shape[0] // n_out
    l_seq = m // batch
    n_t = l_seq // tm
    return pl.pallas_call(
        _mm_t_kernel,
        grid=(batch, n_t),
        in_specs=[pl.BlockSpec((tm, k), lambda b, i: (b * n_t + i, 0)),
                  pl.BlockSpec((n_out * n, k), lambda b, i: (0, 0))],
        out_specs=[pl.BlockSpec((1, n, tm), lambda b, i: (b, 0, i))] * n_out,
        out_shape=[jax.ShapeDtypeStruct((batch, n, l_seq), F32)] * n_out,
        compiler_params=_params("parallel", "parallel"),
        name="matmul_t",
    )(x, w_t)


def _mm_res_ln_kernel(a_ref, w_ref, r_ref, g_ref, b_ref, o_ref):
    m = jnp.dot(a_ref[...].astype(BF16), w_ref[...], preferred_element_type=F32)
    o_ref[...] = _layer_norm(ALPHA * r_ref[...] + m, g_ref[...], b_ref[...])


def matmul_residual_ln(a, w, res, g, b, *, tm):
    m, k = a.shape
    d = w.shape[1]
    return pl.pallas_call(
        _mm_res_ln_kernel,
        grid=(m // tm,),
        in_specs=[pl.BlockSpec((tm, k), lambda i: (i, 0)),
                  pl.BlockSpec((k, d), lambda i: (0, 0)),
                  pl.BlockSpec((tm, d), lambda i: (i, 0)),
                  pl.BlockSpec((1, d), lambda i: (0, 0)),
                  pl.BlockSpec((1, d), lambda i: (0, 0))],
        out_specs=pl.BlockSpec((tm, d), lambda i: (i, 0)),
        out_shape=jax.ShapeDtypeStruct((m, d), F32),
        compiler_params=_params("parallel"),
        name="matmul_residual_ln",
    )(a, w, res, g.reshape(1, d), b.reshape(1, d))


def _ffn_kernel(x_ref, wg_ref, wu_ref, wd_ref, g_ref, b_ref, o_ref, xb_ref, acc_ref, *, n_f):
    f = pl.program_id(1)

    @pl.when(f == 0)
    def _():
        xb_ref[...] = x_ref[...].astype(BF16)

    xb = xb_ref[...]
    h = jnp.dot(xb, wg_ref[...], preferred_element_type=F32)
    u = jnp.dot(xb, wu_ref[...], preferred_element_type=F32)
    part = jnp.dot((_silu(h) * u).astype(BF16), wd_ref[...], preferred_element_type=F32)

    @pl.when(f == 0)
    def _():
        acc_ref[...] = part

    @pl.when(f > 0)
    def _():
        acc_ref[...] += part

    @pl.when(f == n_f - 1)
    def _():
        o_ref[...] = _layer_norm(ALPHA * x_ref[...] + acc_ref[...], g_ref[...], b_ref[...])


def ffn_residual_ln(x, w_gu, w_down, g, b, *, tm, tf):
    m, d = x.shape
    ff = w_down.shape[0]
    n_f = ff // tf
    return pl.pallas_call(
        functools.partial(_ffn_kernel, n_f=n_f),
        grid=(m // tm, n_f),
        in_specs=[pl.BlockSpec((tm, d), lambda i, f: (i, 0)),
                  pl.BlockSpec((d, tf), lambda i, f: (0, f)),
                  pl.BlockSpec((d, tf), lambda i, f: (0, n_f + f)),
                  pl.BlockSpec((tf, d), lambda i, f: (f, 0)),
                  pl.BlockSpec((1, d), lambda i, f: (0, 0)),
                  pl.BlockSpec((1, d), lambda i, f: (0, 0))],
        out_specs=pl.BlockSpec((tm, d), lambda i, f: (i, 0)),
        out_shape=jax.ShapeDtypeStruct((m, d), F32),
        scratch_shapes=[pltpu.VMEM((tm, d), BF16), pltpu.VMEM((tm, d), F32)],
        compiler_params=_params("parallel", "arbitrary"),
        name="ffn_residual_ln",
    )(x, w_gu, w_gu, w_down, g.reshape(1, d), b.reshape(1, d))


def _router_kernel(x_ref, wt_ref, gate_ref, pos_ref, *, n_tokens):
    logits = lax.dot_general(wt_ref[...], x_ref[...], (((1,), (1,)), ((), ())),
                             preferred_element_type=F32, precision=HIGHEST)
    n_e, tm = logits.shape
    expert = lax.broadcasted_iota(jnp.int32, logits.shape, 0)
    m1 = jnp.max(logits, axis=0, keepdims=True)
    i1 = jnp.min(jnp.where(logits == m1, expert, n_e), axis=0, keepdims=True)
    rest = jnp.where(expert == i1, -jnp.inf, logits)
    m2 = jnp.max(rest, axis=0, keepdims=True)
    i2 = jnp.min(jnp.where(rest == m2, expert, n_e), axis=0, keepdims=True)
    e2 = jnp.exp(m2 - m1)
    denom = 1.0 + e2
    first = expert == i1
    in_range = pl.program_id(0) * tm + lax.broadcasted_iota(jnp.int32, logits.shape, 1) < n_tokens
    routed = (first | (expert == i2)) & in_range
    gate_ref[...] = jnp.where(first & in_range, 1.0 / denom, jnp.where(routed, e2 / denom, 0.0))
    upper = (lax.broadcasted_iota(jnp.int32, (tm, tm), 0) <= lax.broadcasted_iota(jnp.int32, (tm, tm), 1)).astype(BF16)
    upto = jnp.dot(routed.astype(BF16), upper, preferred_element_type=F32)
    pos_ref[...] = jnp.where(routed, upto - 1.0, -1.0).astype(jnp.int32)


def moe_route(x, w_router_t, *, tb):
    m, d = x.shape
    n_e = w_router_t.shape[0]
    n_blk = pl.cdiv(m, tb)
    return pl.pallas_call(
        functools.partial(_router_kernel, n_tokens=m),
        grid=(n_blk,),
        in_specs=[pl.BlockSpec((tb, d), lambda i: (i, 0)),
                  pl.BlockSpec((n_e, d), lambda i: (0, 0))],
        out_specs=[pl.BlockSpec((n_e, tb), lambda i: (0, i))] * 2,
        out_shape=[jax.ShapeDtypeStruct((n_e, n_blk * tb), F32), jax.ShapeDtypeStruct((n_e, n_blk * tb), jnp.int32)],
        compiler_params=_params("parallel"),
        name="moe_route",
    )(x, w_router_t)


def _moe_kernel(cnt_ref, x_ref, gate_ref, slot_ref, wg_ref, wu_ref, wd_ref, g_ref, b_ref, o_ref,
                xt_ref, xc_ref, acc_ref, yt_ref, *, n_e, n_f, ts, n_tokens):
    i, e, f = pl.program_id(0), pl.program_id(1), pl.program_id(2)
    tb = x_ref.shape[0]
    n_sub = (cnt_ref[i * n_e + e] + ts - 1) // ts
    row = lax.broadcasted_iota(jnp.int32, (ts, tb), 0)

    def one_hot(sub):
        return (slot_ref[pl.ds(e, 1), :] == row + sub * ts).astype(BF16)

    @pl.when((e == 0) & (f == 0))
    def _():
        in_range = i * tb + lax.broadcasted_iota(jnp.int32, (tb, 1), 0) < n_tokens
        xt_ref[...] = jnp.where(in_range, x_ref[...], 0.0).T.astype(BF16)
        yt_ref[...] = jnp.zeros_like(yt_ref)

    tf, d = wg_ref.shape[1], wd_ref.shape[1]
    halves = lambda n: (slice(0, n // 2), slice(n // 2, n))

    @pl.when(f == 0)
    def _():
        def gather(sub, c):
            pick = one_hot(sub)
            for rows in halves(d):
                xc_ref[sub, rows, :] = _dot_nt(xt_ref[rows, :], pick).astype(BF16)
            return c

        lax.fori_loop(0, n_sub, gather, 0)

    def expert_ffn(sub, c):
        xc = xc_ref[sub]
        act = []
        for rows in halves(tf):
            h = jnp.dot(wg_ref[0, rows, :], xc, preferred_element_type=F32)
            u = jnp.dot(wu_ref[0, rows, :], xc, preferred_element_type=F32)
            act.append((_silu(h) * u).astype(BF16))
        act = jnp.concatenate(act, axis=0)
        part = jnp.concatenate([jnp.dot(wd_ref[0, rows, :], act, preferred_element_type=F32) for rows in halves(d)], axis=0)

        @pl.when(f == 0)
        def _():
            acc_ref[sub] = part

        @pl.when(f > 0)
        def _():
            acc_ref[sub] += part

        return c

    lax.fori_loop(0, n_sub, expert_ffn, 0)

    @pl.when(f == n_f - 1)
    def _():
        def scatter(sub, c):
            pick = one_hot(sub)
            gate = gate_ref[pl.ds(e, 1), :]
            for rows in halves(d):
                back = jnp.dot(acc_ref[sub, rows, :].astype(BF16), pick, preferred_element_type=F32)
                yt_ref[rows, :] += gate * back
            return c

        lax.fori_loop(0, n_sub, scatter, 0)

    @pl.when((e == n_e - 1) & (f == n_f - 1))
    def _():
        o_ref[...] = _layer_norm(ALPHA * x_ref[...] + yt_ref[...].T, g_ref[...], b_ref[...])


def moe_residual_ln(x, gates, slots, counts, w_gu_t, w_down_t, slot, g, b, *, tb, ts, tf):
    m, d = x.shape
    _, n_e, ff2, _ = w_gu_t.shape
    n_f = ff2 // 2 // tf
    n_sub = pl.cdiv(tb, ts)
    return pl.pallas_call(
        functools.partial(_moe_kernel, n_e=n_e, n_f=n_f, ts=ts, n_tokens=m),
        grid_spec=pltpu.PrefetchScalarGridSpec(
            num_scalar_prefetch=1,
            grid=(pl.cdiv(m, tb), n_e, n_f),
            in_specs=[pl.BlockSpec((tb, d), lambda i, e, f, c: (i, 0)),
                      pl.BlockSpec((n_e, tb), lambda i, e, f, c: (0, i)),
                      pl.BlockSpec((n_e, tb), lambda i, e, f, c: (0, i)),
                      pl.BlockSpec((None, 1, tf, d), lambda i, e, f, c: (slot, e, f, 0)),
                      pl.BlockSpec((None, 1, tf, d), lambda i, e, f, c: (slot, e, n_f + f, 0)),
                      pl.BlockSpec((None, 1, d, tf), lambda i, e, f, c: (slot, e, 0, f)),
                      pl.BlockSpec((1, d), lambda i, e, f, c: (0, 0)),
                      pl.BlockSpec((1, d), lambda i, e, f, c: (0, 0))],
            out_specs=pl.BlockSpec((tb, d), lambda i, e, f, c: (i, 0)),
            scratch_shapes=[pltpu.VMEM((d, tb), BF16), pltpu.VMEM((n_sub, d, ts), BF16),
                            pltpu.VMEM((n_sub, d, ts), F32), pltpu.VMEM((d, tb), F32)]),
        out_shape=jax.ShapeDtypeStruct((m, d), F32),
        compiler_params=_params("parallel", "arbitrary", "arbitrary"),
        name="moe_residual_ln",
    )(counts, x, gates, slots, w_gu_t, w_gu_t, w_down_t, g.reshape(1, d), b.reshape(1, d))


def _hgrn_lower_bound(lbp, layer):
    m = jnp.max(lbp, axis=0, keepdims=True)
    e = jnp.exp(lbp - m)
    p = e / jnp.sum(e, axis=0, keepdims=True)
    c = p[0:1]
    for i in range(1, layer + 1):
        c = c + p[i:i + 1]
    return c - p[0:1]


def _hgrn_gates(q, fx, lb):
    log_lb = jnp.log(jnp.maximum(lb, LB_FLOOR))
    log1m = jnp.log1p(-lb)
    log_sig = -(jnp.maximum(-fx, 0.0) + jnp.log1p(jnp.exp(-jnp.abs(fx))))
    b2 = log1m + log_sig
    log_f = jnp.maximum(log_lb, b2) + jnp.log1p(jnp.exp(-jnp.abs(log_lb - b2)))
    kk = (1.0 - lb) * _sigmoid(-fx) - (jnp.maximum(lb, LB_FLOOR) - lb)
    return _silu(q), log_f, kk


def _hgrn_out_norm(o, g, ng):
    o = o * lax.rsqrt(jnp.mean(o * o, axis=-1, keepdims=True) + RMS_EPS)
    return o * ng * _silu(g)


def _dot_nt(a, b):
    return lax.dot_general(a, b, (((1,), (1,)), ((), ())), preferred_element_type=F32)


def _dot_tn(a, b):
    return lax.dot_general(a, b, (((0,), (0,)), ((), ())), preferred_element_type=F32)


def _gla_kernel(q_ref, f_ref, i_ref, g_ref, lbp_ref, ng_ref, o_ref, s_out_ref, st_ref, *, layer, n_t, tl, hp):
    t = pl.program_id(2)
    c_len, sub = HG_CHUNK, HG_SUB
    n_sub = c_len // sub
    n_stack = sub * (n_sub * (n_sub - 1) // 2)

    @pl.when(t == 0)
    def _():
        st_ref[...] = jnp.zeros_like(st_ref)

    lb = _hgrn_lower_bound(lbp_ref[...], layer)
    ng = ng_ref[...]
    row = lax.broadcasted_iota(jnp.int32, (c_len, c_len), 0)
    col = lax.broadcasted_iota(jnp.int32, (c_len, c_len), 1)
    tri = jnp.where(col <= row, LOG2_E, 0.0)
    srow = lax.broadcasted_iota(jnp.int32, (c_len, n_stack), 0) // sub
    scol = lax.broadcasted_iota(jnp.int32, (c_len, n_stack), 1)
    part = jnp.zeros((c_len, n_stack), jnp.int32)
    for p in range(1, n_sub):
        part = part + (scol >= sub * (p * (p - 1) // 2)).astype(jnp.int32)
    inter_mask = srow == part

    def one_head(hh, r):
        lanes = slice(hh * HG_DK, (hh + 1) * HG_DK)
        qs, log_f, kk = _hgrn_gates(q_ref[r, lanes], f_ref[r, lanes], lb[:, lanes])
        v = i_ref[r, lanes]
        cum = jnp.dot(tri, log_f, preferred_element_type=F32, precision=HIGHEST)
        last = cum[c_len - 1:c_len, :]
        st = st_ref[hh]
        o = _dot_nt((qs * jnp.exp2(cum)).astype(BF16), st.astype(BF16))
        bounds = [cum[sub * a - 1:sub * a, :] for a in range(1, n_sub)]
        cq = jnp.concatenate([cum[0:sub]] + [jnp.broadcast_to(b, (sub, HG_DK)) for b in bounds], axis=0)
        q_rel = (qs * jnp.exp2(cum - cq)).astype(BF16)
        k_st = jnp.concatenate([kk[0:sub * a] * jnp.exp2(bounds[a - 1] - cum[0:sub * a]) for a in range(1, n_sub)], axis=0)
        v_st = jnp.concatenate([v[0:sub * a] for a in range(1, n_sub)], axis=0)
        att = jnp.where(inter_mask, _dot_nt(q_rel, k_st.astype(BF16)), 0.0)
        o = o + jnp.dot(att.astype(BF16), v_st.astype(BF16), preferred_element_type=F32)
        parts = []
        for a in range(n_sub):
            base = sub * a
            rest = o[base:base + sub]
            for lo in range(0, sub, SUBLANES):
                tail = slice(base + lo, base + sub)
                qa, ca = qs[tail], cum[tail]
                row_t = lo + lax.broadcasted_iota(jnp.int32, (sub - lo, 1), 0)
                for s in range(lo, lo + SUBLANES):
                    src = slice(base + s, base + s + 1)
                    w = jnp.sum(qa * (kk[src] * jnp.exp2(ca - cum[src])), axis=-1, keepdims=True)
                    rest = rest + jnp.where(row_t >= s, w, 0.0) * v[src]
                parts.append(rest[:SUBLANES])
                rest = rest[SUBLANES:]
        o = jnp.concatenate(parts, axis=0)
        o_ref[r, lanes] = _hgrn_out_norm(o, g_ref[r, lanes], ng[:, lanes]).astype(o_ref.dtype)
        k_end = (kk * jnp.exp2(last - cum)).astype(BF16)
        st_ref[hh] = st * jnp.exp2(last) + _dot_tn(v.astype(BF16), k_end)

    def chunk(c, carry):
        r = pl.ds(pl.multiple_of(c * c_len, c_len), c_len)
        for hh in range(hp):
            one_head(hh, r)
        return carry

    lax.fori_loop(0, tl // c_len, chunk, 0)

    @pl.when(t == n_t - 1)
    def _():
        for hh in range(hp):
            s_out_ref[0, hh] = st_ref[hh].T


def hgrn_prompt(z, lb_param, norm_g, *, layer, batch, tl):
    m = z.shape[0]
    n_t = m // batch // tl
    h, dk = HG_HEADS, HG_DK
    hp = HG_HEADS_PER_STEP
    n_hg = h // hp
    zspec = lambda part: pl.BlockSpec((tl, hp * dk), lambda b, hh, t: (b * n_t + t, part * n_hg + hh))
    return pl.pallas_call(
        functools.partial(_gla_kernel, layer=layer, n_t=n_t, tl=tl, hp=hp),
        grid=(batch, n_hg, n_t),
        in_specs=[zspec(0), zspec(1), zspec(2), zspec(3),
                  pl.BlockSpec((DEPTH, hp * dk), lambda b, hh, t: (0, hh)),
                  pl.BlockSpec((1, hp * dk), lambda b, hh, t: (0, hh))],
        out_specs=[pl.BlockSpec((tl, hp * dk), lambda b, hh, t: (b * n_t + t, hh)),
                   pl.BlockSpec((1, hp, dk, dk), lambda b, hh, t: (b, hh, 0, 0))],
        out_shape=[jax.ShapeDtypeStruct((m, h * dk), BF16),
                   jax.ShapeDtypeStruct((batch, h, dk, dk), F32)],
        scratch_shapes=[pltpu.VMEM((hp, dk, dk), F32)],
        compiler_params=_params("parallel", "parallel", "arbitrary"),
        name="hgrn_prompt",
    )(z, z, z, z, lb_param, norm_g.reshape(1, h * dk))


def _hgrn_step_kernel(q_ref, f_ref, i_ref, g_ref, lbp_ref, ng_ref, s0_ref, o_ref, s_ref,
                      qt_ref, ft_ref, kt_ref, oacc_ref, *, layer, n_j, sb):
    j = pl.program_id(1)
    n_seq = q_ref.shape[0]

    def split3(x, ref):
        hi = x.astype(BF16)
        r1 = x - hi.astype(F32)
        mid = r1.astype(BF16)
        ref[0], ref[1], ref[2] = hi, mid, (r1 - mid.astype(F32)).astype(BF16)

    @pl.when(j == 0)
    def _():
        lb = _hgrn_lower_bound(lbp_ref[...], layer)
        qs, log_f, kk = _hgrn_gates(q_ref[...], f_ref[...], lb)
        split3(qs.T, qt_ref)
        split3(jnp.exp(log_f).T, ft_ref)
        split3(kk.T, kt_ref)

    seq = lax.broadcasted_iota(jnp.int32, (n_seq, HG_DK), 0)
    for i in range(sb):
        b = j * sb + i
        pick = (seq == b).astype(BF16)
        col = lambda ref: sum(jnp.dot(ref[p], pick, preferred_element_type=F32) for p in range(3))
        s_new = s0_ref[i, 0] * col(ft_ref) + col(kt_ref) * i_ref[pl.ds(b, 1), :]
        s_ref[i, 0] = s_new
        oacc_ref[pl.ds(b, 1), :] = jnp.sum(col(qt_ref) * s_new, axis=0, keepdims=True)

    @pl.when(j == n_j - 1)
    def _():
        o_ref[...] = _hgrn_out_norm(oacc_ref[...], g_ref[...], ng_ref[...]).astype(o_ref.dtype)


def hgrn_step(z, s0, slot, lb_param, norm_g, *, layer, sb):
    n_seq = z.shape[0]
    h, dk = HG_HEADS, HG_DK
    n_j = n_seq // sb
    zspec = lambda off: pl.BlockSpec((n_seq, dk), lambda hh, j: (0, off + hh))
    return pl.pallas_call(
        functools.partial(_hgrn_step_kernel, layer=layer, n_j=n_j, sb=sb),
        grid=(h, n_j),
        in_specs=[zspec(0), zspec(h), zspec(2 * h), zspec(3 * h),
                  pl.BlockSpec((DEPTH, dk), lambda hh, j: (0, hh)),
                  pl.BlockSpec((1, dk), lambda hh, j: (0, hh)),
                  pl.BlockSpec((None, sb, 1, dk, dk), lambda hh, j: (slot, j, hh, 0, 0))],
        out_specs=[pl.BlockSpec((n_seq, dk), lambda hh, j: (0, hh)),
                   pl.BlockSpec((sb, 1, dk, dk), lambda hh, j: (j, hh, 0, 0))],
        out_shape=[jax.ShapeDtypeStruct((n_seq, h * dk), BF16),
                   jax.ShapeDtypeStruct(s0.shape[1:], F32)],
        scratch_shapes=[pltpu.VMEM((3, dk, n_seq), BF16)] * 3 + [pltpu.VMEM((n_seq, dk), F32)],
        compiler_params=_params("parallel", "arbitrary"),
        name="hgrn_step",
    )(z, z, z, z, lb_param, norm_g.reshape(1, h * dk), s0)


def _gelu_tanh(x):
    return x * (0.5 * (1.0 + jnp.tanh(0.7978845608028654 * (x + 0.044715 * (x * x * x)))))


def _rg_decay_and_input(xc, wg_ref, bg_ref, lam_ref):
    xb = xc.astype(BF16)
    gates = []
    for gi in range(2):
        blocks = [jnp.dot(xb[:, n * RG_BS:(n + 1) * RG_BS], wg_ref[gi, n], preferred_element_type=F32)
                  for n in range(RG_BLOCKS)]
        gates.append(jnp.concatenate(blocks, axis=1) + bg_ref[gi:gi + 1, :])
    r = _sigmoid(gates[0])
    i = _sigmoid(gates[1])
    neg_lam = -lam_ref[...]
    softplus = jnp.maximum(neg_lam, 0.0) + jnp.log1p(jnp.exp(-jnp.abs(neg_lam)))
    log_a = -RG_C * r * softplus
    one_minus_a2 = -jnp.tanh(log_a) * (jnp.exp(2.0 * log_a) + 1.0)
    return jnp.exp(log_a), jnp.sqrt(one_minus_a2) * i * xc


def _rg_prompt_kernel(y_ref, x_ref, cw_ref, cb_ref, wg_ref, bg_ref, lam_ref,
                      o_ref, conv_out_ref, h_out_ref, xpad_ref, h_ref, *, n_t, tl):
    t = pl.program_id(1)
    halo = 8

    @pl.when(t == 0)
    def _():
        xpad_ref[0:halo, :] = jnp.zeros((halo, RG_WIDTH), F32)
        h_ref[...] = jnp.zeros_like(h_ref)

    x = x_ref[...]
    xpad_ref[halo:halo + tl, :] = x
    xc = cb_ref[...]
    for j in range(RG_CONV):
        off = halo - (RG_CONV - 1) + j
        xc = xc + xpad_ref[off:off + tl, :] * cw_ref[j:j + 1, :]
    xpad_ref[0:halo, :] = x[tl - halo:tl, :]
    a, u = _rg_decay_and_input(xc, wg_ref, bg_ref, lam_ref)
    row = lax.broadcasted_iota(jnp.int32, (tl, 1), 0) % SUBLANES
    s = 1
    while s < SUBLANES:
        keep = row >= s
        a_prev = jnp.where(keep, pltpu.roll(a, s, 0), 1.0)
        u_prev = jnp.where(keep, pltpu.roll(u, s, 0), 0.0)
        u = a * u_prev + u
        a = a * a_prev
        s *= 2
    carry = h_ref[...]
    tiles = []
    for r in range(0, tl, SUBLANES):
        tiles.append(a[r:r + SUBLANES] * carry + u[r:r + SUBLANES])
        carry = tiles[-1][SUBLANES - 1:SUBLANES]
    h = jnp.concatenate(tiles, axis=0)
    h_ref[...] = carry
    o_ref[...] = (_gelu_tanh(y_ref[...]) * h).astype(o_ref.dtype)

    @pl.when(t == n_t - 1)
    def _():
        conv_out_ref[0] = x[tl - (RG_CONV - 1):tl, :]
        h_out_ref[0] = h[tl - 1:tl, :]


def rglru_prompt(z, conv_w, conv_b, w_gate, b_gate, lam, *, batch, tl):
    m = z.shape[0]
    n_t = m // batch // tl
    w = RG_WIDTH
    const = lambda shape: pl.BlockSpec(shape, lambda b, t: (0,) * len(shape))
    return pl.pallas_call(
        functools.partial(_rg_prompt_kernel, n_t=n_t, tl=tl),
        grid=(batch, n_t),
        in_specs=[pl.BlockSpec((tl, w), lambda b, t: (b * n_t + t, 0)),
                  pl.BlockSpec((tl, w), lambda b, t: (b * n_t + t, 1)),
                  const((RG_CONV, w)), const((1, w)), const((2, RG_BLOCKS, RG_BS, RG_BS)), const((2, w)), const((1, w))],
        out_specs=[pl.BlockSpec((tl, w), lambda b, t: (b * n_t + t, 0)),
                   pl.BlockSpec((1, RG_CONV - 1, w), lambda b, t: (b, 0, 0)),
                   pl.BlockSpec((1, 1, w), lambda b, t: (b, 0, 0))],
        out_shape=[jax.ShapeDtypeStruct((m, w), BF16),
                   jax.ShapeDtypeStruct((batch, RG_CONV - 1, w), F32),
                   jax.ShapeDtypeStruct((batch, 1, w), F32)],
        scratch_shapes=[pltpu.VMEM((tl + 8, w), F32), pltpu.VMEM((1, w), F32)],
        compiler_params=_params("parallel", "arbitrary"),
        name="rglru_prompt",
    )(z, z, conv_w, conv_b.reshape(1, w), w_gate, b_gate, lam.reshape(1, w))


def _rg_step_kernel(y_ref, x_ref, buf_ref, h0_ref, cw_ref, cb_ref, wg_ref, bg_ref, lam_ref,
                    o_ref, conv_out_ref, h_out_ref):
    x = x_ref[...]
    xc = cb_ref[...]
    for j in range(RG_CONV - 1):
        xc = xc + buf_ref[:, j, :] * cw_ref[j:j + 1, :]
    xc = xc + x * cw_ref[RG_CONV - 1:RG_CONV, :]
    a, u = _rg_decay_and_input(xc, wg_ref, bg_ref, lam_ref)
    h = a * h0_ref[...] + u
    o_ref[...] = (_gelu_tanh(y_ref[...]) * h).astype(o_ref.dtype)
    h_out_ref[...] = h
    for j in range(RG_CONV - 2):
        conv_out_ref[:, j, :] = buf_ref[:, j + 1, :]
    conv_out_ref[:, RG_CONV - 2, :] = x


def rglru_step(z, conv_buf, h0, conv_w, conv_b, w_gate, b_gate, lam):
    n_seq = z.shape[0]
    w = RG_WIDTH
    const = lambda shape: pl.BlockSpec(shape, lambda i: (0,) * len(shape))
    return pl.pallas_call(
        _rg_step_kernel,
        grid=(1,),
        in_specs=[pl.BlockSpec((n_seq, w), lambda i: (0, 0)),
                  pl.BlockSpec((n_seq, w), lambda i: (0, 1)),
                  const((n_seq, RG_CONV - 1, w)), const((n_seq, w)),
                  const((RG_CONV, w)), const((1, w)), const((2, RG_BLOCKS, RG_BS, RG_BS)), const((2, w)), const((1, w))],
        out_specs=[const((n_seq, w)), const((n_seq, RG_CONV - 1, w)), const((n_seq, w))],
        out_shape=[jax.ShapeDtypeStruct((n_seq, w), BF16),
                   jax.ShapeDtypeStruct((n_seq, RG_CONV - 1, w), F32),
                   jax.ShapeDtypeStruct((n_seq, w), F32)],
        compiler_params=_params("arbitrary"),
        name="rglru_step",
    )(z, z, conv_buf, h0, conv_w, conv_b.reshape(1, w), w_gate, b_gate, lam.reshape(1, w))


NSA_HEADS = 16
NSA_KVH = 4
NSA_GROUP = 4
NSA_DH = 64
NSA_KVW = NSA_KVH * NSA_DH
NSA_ROW = 2 * NSA_KVW
CMP_BLOCK = 32
CMP_HID = 128
SLC_BLOCK = 64
SLC_TOPK = 16
WINDOW = 512
NSA_QBLOCK = 128
MASK_NEG = -1e30
SEL_BIG = 1e9
SEL_KEYS = 1024


def _masked_softmax(s, mask, exp=jnp.exp):
    s = jnp.where(mask, s, MASK_NEG)
    m = jnp.max(s, axis=-1, keepdims=True)
    e = jnp.where(mask, exp(s - m), 0.0)
    return e * (1.0 / jnp.maximum(jnp.sum(e, axis=-1, keepdims=True), 1e-30))


def _top_blocks(score, k, axis):
    pos = lax.broadcasted_iota(jnp.int32, score.shape, axis)
    n = score.shape[axis]

    def pick(_, carry):
        sc, sel = carry
        mx = jnp.max(sc, axis=axis, keepdims=True)
        idx = jnp.min(jnp.where(sc == mx, pos, n), axis=axis, keepdims=True)
        hit = pos == idx
        return jnp.where(hit, -jnp.inf, sc), jnp.where(hit, 1.0, sel)

    return lax.fori_loop(0, k, pick, (score, jnp.zeros(score.shape, F32)))[1]


def _compress_rows(block_row, w1_ref, w2_ref, n_blk):
    tiles = NSA_KVW // LANES
    low = lax.broadcasted_iota(jnp.int32, (n_blk, LANES), 1) < NSA_DH
    outs = []
    for c in range(2):
        cols = [[] for _ in range(NSA_KVH)]
        for l in range(0, CMP_BLOCK, 2):
            for tl in range(tiles):
                x0, x1 = block_row(c, tl, l), block_row(c, tl, l + 1)
                cols[2 * tl].append(jnp.where(low, x0, pltpu.roll(x1, NSA_DH, 1)))
                cols[2 * tl + 1].append(jnp.where(low, pltpu.roll(x0, NSA_DH, 1), x1))
        xs = jnp.concatenate([jnp.concatenate(cols[h], axis=1) for h in range(NSA_KVH)], axis=0)
        acc = jnp.dot(xs.astype(BF16), w1_ref[c], preferred_element_type=F32)
        out = jnp.dot(_silu(acc).astype(BF16), w2_ref[c], preferred_element_type=F32)
        outs.append(jnp.concatenate([out[h * n_blk:(h + 1) * n_blk] for h in range(NSA_KVH)], axis=1))
    return jnp.concatenate(outs, axis=1)


def _compress_kernel(x0_ref, x1_ref, x2_ref, x3_ref, pos_ref, w1_ref, w2_ref, even_ref, odd_ref, *, n_blk):
    x_refs = (x0_ref, x1_ref, x2_ref, x3_ref)
    half = n_blk // 2
    tiles = NSA_KVW // LANES

    def block_row(c, tl, r):
        ref = x_refs[c * tiles + tl]
        xe = ref[pl.ds(r, half, stride=2 * CMP_BLOCK), :]
        xo = ref[pl.ds(r + CMP_BLOCK, half, stride=2 * CMP_BLOCK), :]
        return jnp.concatenate([xe, xo], axis=0) + pos_ref[c, r:r + 1, tl * LANES:(tl + 1) * LANES]

    full = _compress_rows(block_row, w1_ref, w2_ref, n_blk)
    even_ref[0] = full[:half]
    odd_ref[0] = full[half:]


def nsa_compress(kv, col_block, pos4, w1, w2, *, batch, n_blk):
    m = kv.shape[0]
    n_t = m // batch // (n_blk * CMP_BLOCK)
    half = n_blk // 2
    const = lambda shape: pl.BlockSpec(shape, lambda b, t: (0,) * len(shape))
    out = jax.ShapeDtypeStruct((batch, n_t * half, NSA_ROW), F32)
    tiles = NSA_ROW // LANES
    lane_tile = lambda i: pl.BlockSpec((n_blk * CMP_BLOCK, LANES), lambda b, t: (b * n_t + t, col_block * tiles + i))
    return pl.pallas_call(
        functools.partial(_compress_kernel, n_blk=n_blk),
        grid=(batch, n_t),
        in_specs=[lane_tile(i) for i in range(tiles)]
        + [const((2, CMP_BLOCK, NSA_KVW)), const((2, CMP_BLOCK * NSA_DH, CMP_HID)), const((2, CMP_HID, NSA_DH))],
        out_specs=[pl.BlockSpec((1, half, NSA_ROW), lambda b, t: (b, t, 0))] * 2,
        out_shape=[out, out],
        compiler_params=_params("parallel", "parallel"),
        name="nsa_compress",
    )(kv, kv, kv, kv, pos4, w1, w2)


def nsa_prepare_weights(w_in, pos, w1, w2):
    hq = NSA_HEADS * NSA_DH
    kv_end = hq + 3 * NSA_ROW
    w_qkv = w_in[:, :kv_end].astype(BF16)
    w_kvb = w_in[:, hq + NSA_ROW:kv_end].astype(BF16)
    w_g = w_in[:, kv_end:].reshape(-1, NSA_KVH, 3 * NSA_GROUP)
    w_g = jnp.pad(w_g, ((0, 0), (0, 0), (0, LANES - 3 * NSA_GROUP))).reshape(-1, NSA_KVH * LANES).astype(BF16)
    pos4 = jnp.tile(pos, (1, 1, NSA_KVH))
    w1 = w1.reshape(2, CMP_BLOCK * NSA_DH, CMP_HID)
    w_kv_t = w_in[:, hq:kv_end].T.astype(BF16)
    return w_qkv, w_kvb, w_g, w_kv_t, pos4, w1.astype(BF16), w2.astype(BF16)


def _stack_group_queries(q, par):
    half = lax.broadcasted_iota(jnp.int32, (q.shape[0], LANES), 1) // NSA_DH
    parts = []
    for g in range(NSA_GROUP):
        qg = q[:, g * NSA_DH:(g + 1) * NSA_DH]
        parts.append(jnp.where(half == par, jnp.concatenate([qg, qg], axis=1), 0.0))
    return jnp.concatenate(parts, axis=0).astype(BF16)


def _nsa_prompt_kernel(q_ref, gp_ref, ck_ref, cv_ref, ks_ref, vs_ref, kw_ref, vw_ref, o_ref):
    j = pl.program_id(2)
    qb, grp = NSA_QBLOCK, NSA_GROUP
    rows = grp * qb
    qw = grp * NSA_DH
    pair = range(LANES // NSA_DH)
    t_pos = j * qb + lax.broadcasted_iota(jnp.int32, (qb, 1), 0)

    def grouped_softmax(s, mask):
        n = s.shape[-1]
        return _masked_softmax(s.reshape(grp, qb, n), mask[None], jnp.exp2).reshape(rows, n)

    n_c = ck_ref.shape[1]
    n_s = n_c // 2
    ck = ck_ref[0].astype(BF16)
    cv = cv_ref[0].astype(BF16)
    nn = lax.broadcasted_iota(jnp.int32, (1, n_c), 1)
    c_blk = jnp.where(nn < n_s, 2 * nn, 2 * (nn - n_s) + 1)
    c_vis = (c_blk * CMP_BLOCK + CMP_BLOCK - 1) <= t_pos
    n_w = WINDOW + qb
    w0 = pl.multiple_of(jnp.maximum(j * qb - WINDOW, 0), qb)
    dist = t_pos - (w0 + lax.broadcasted_iota(jnp.int32, (1, n_w), 1))
    w_bias = jnp.where((dist >= 0) & (dist <= WINDOW), 0.0, MASK_NEG)[None]
    k_win, v_win = kw_ref[pl.ds(w0, n_w), :], vw_ref[pl.ds(w0, n_w), :]
    blk = lax.broadcasted_iota(jnp.int32, (qb, n_s), 1)
    cur = t_pos // SLC_BLOCK
    forced = (blk == 0) | (blk == cur) | (blk == cur - 1)

    def before_selection(par):
        qp = _stack_group_queries(q_ref[:, par * qw:(par + 1) * qw] * (NSA_DH ** -0.5 * LOG2_E), par)
        p_c = grouped_softmax(_dot_nt(qp, ck), c_vis)
        o_c = jnp.dot(p_c.astype(BF16), cv, preferred_element_type=F32)
        s_w = _dot_nt(qp, k_win).reshape(grp, qb, n_w) + w_bias
        e_w = jnp.exp2(s_w - jnp.max(s_w, axis=-1, keepdims=True)).reshape(rows, n_w)
        own_w = lax.broadcasted_iota(jnp.int32, (n_w, LANES), 1) // NSA_DH == par
        o_w = jnp.dot(e_w.astype(BF16), jnp.where(own_w, v_win, 1.0), preferred_element_type=F32)
        o_w = o_w / jnp.maximum(pltpu.roll(o_w, NSA_DH, 1), 1e-30)
        gates = _sigmoid(gp_ref[:, par * LANES:(par + 1) * LANES])
        gate = lambda br: jnp.concatenate(
            [jnp.broadcast_to(gates[:, 3 * g + br:3 * g + br + 1], (qb, LANES)) for g in range(grp)], axis=0)
        imp = p_c[0:qb]
        for g in range(1, grp):
            imp = imp + p_c[g * qb:(g + 1) * qb]
        imp = imp[:, :n_s] + imp[:, n_s:]
        score = jnp.where(forced, SEL_BIG, jnp.where(blk <= cur, imp, -SEL_BIG))
        return qp, gate(0) * o_c + gate(2) * o_w, gate(1), score.T

    qps, o_cws, gate_s, scores_t = zip(*[before_selection(par) for par in pair])
    sel_t = _top_blocks(jnp.concatenate(scores_t, axis=1), SLC_TOPK, 0)
    sel_bs = [sel_t[:, par * qb:(par + 1) * qb].T.astype(BF16) for par in pair]

    per = SEL_KEYS // SLC_BLOCK
    e_row = lax.broadcasted_iota(jnp.int32, (n_s, SEL_KEYS), 0)
    e_col = lax.broadcasted_iota(jnp.int32, (n_s, SEL_KEYS), 1) // SLC_BLOCK
    k_off = lax.broadcasted_iota(jnp.int32, (1, SEL_KEYS), 1)

    n_kb = (j * qb + qb - 1) // SEL_KEYS + 1

    def key_rows(kb):
        return pl.ds(pl.multiple_of(kb * SEL_KEYS, SEL_KEYS), SEL_KEYS)

    lane_head = lax.broadcasted_iota(jnp.int32, (SEL_KEYS, LANES), 1) // NSA_DH

    def sel_step(kb, carry):
        expand = (e_row == kb * per + e_col).astype(BF16)
        causal = (kb * SEL_KEYS + k_off) <= t_pos
        k_blk, v_blk = ks_ref[key_rows(kb), :], vs_ref[key_rows(kb), :]
        out = []
        for par in pair:
            m, acc = carry[par]
            chosen = jnp.dot(sel_bs[par], expand, preferred_element_type=F32) > 0.5
            bias = jnp.where(chosen & causal, 0.0, MASK_NEG)[None]
            s = _dot_nt(qps[par], k_blk).reshape(grp, qb, SEL_KEYS) + bias
            m_new = jnp.maximum(m, jnp.max(s, axis=-1, keepdims=True))
            alpha = jnp.exp2(m - m_new)
            e = jnp.exp2(s - m_new)
            v_one = jnp.where(lane_head == par, v_blk, 1.0)
            pv = jnp.dot(e.reshape(rows, SEL_KEYS).astype(BF16), v_one, preferred_element_type=F32)
            out.append((m_new, alpha * acc + pv.reshape(grp, qb, LANES)))
        return tuple(out)

    init = tuple((jnp.full((grp, qb, 1), MASK_NEG, F32), jnp.zeros((grp, qb, LANES), F32)) for _ in pair)
    fin = lax.fori_loop(0, n_kb, sel_step, init)
    heads = []
    for par in pair:
        acc = fin[par][1].reshape(rows, LANES)
        o_s = acc / jnp.maximum(pltpu.roll(acc, NSA_DH, 1), 1e-30)
        o = (o_cws[par] + gate_s[par] * o_s)[:, par * NSA_DH:(par + 1) * NSA_DH]
        heads += [o[g * qb:(g + 1) * qb] for g in range(grp)]
    o_ref[...] = jnp.concatenate(heads, axis=1).astype(o_ref.dtype)


def nsa_prompt_attention(qkv, gates_pre, ckv, kvb, *, batch):
    m = qkv.shape[0]
    l_seq = m // batch
    nq = l_seq // NSA_QBLOCK
    n_c = ckv.shape[1]
    per_tile = LANES // NSA_DH
    n_pairs = NSA_KVH // per_tile
    qw = per_tile * NSA_GROUP * NSA_DH
    kv_spec = lambda part: pl.BlockSpec((l_seq, LANES), lambda b, p, j: (b, part * n_pairs + p))
    return pl.pallas_call(
        _nsa_prompt_kernel,
        grid=(batch, n_pairs, nq),
        in_specs=[pl.BlockSpec((NSA_QBLOCK, qw), lambda b, p, j: (b * nq + j, p)),
                  pl.BlockSpec((NSA_QBLOCK, per_tile * LANES), lambda b, p, j: (b * nq + j, p)),
                  pl.BlockSpec((1, n_c, LANES), lambda b, p, j: (b, 0, p)),
                  pl.BlockSpec((1, n_c, LANES), lambda b, p, j: (b, 0, n_pairs + p)),
                  kv_spec(0), kv_spec(1), kv_spec(2), kv_spec(3)],
        out_specs=pl.BlockSpec((NSA_QBLOCK, qw), lambda b, p, j: (b * nq + j, p)),
        out_shape=jax.ShapeDtypeStruct((m, NSA_HEADS * NSA_DH), BF16),
        compiler_params=_params("parallel", "parallel", "arbitrary"),
        name="nsa_prompt_attention",
    )(qkv, gates_pre, ckv, ckv, kvb, kvb, kvb, kvb)


def _paged_compress_kernel(pt_ref, *refs, n_pages, page):
    del pt_ref
    pages = refs[:n_pages]
    pos_ref, w1_ref, w2_ref, out_ref = refs[n_pages:n_pages + 4]
    rows = refs[n_pages + 4:]
    tiles = NSA_KVW // LANES
    per_page = page // CMP_BLOCK
    group = SUBLANES // per_page
    width = group * page
    dst = lax.broadcasted_iota(jnp.int32, (width, width), 0)
    slot = dst % SUBLANES
    src = (slot // per_page) * page + (slot % per_page) * CMP_BLOCK + dst // SUBLANES
    perm = (lax.broadcasted_iota(jnp.int32, (width, width), 1) == src).astype(BF16)
    for q in range(n_pages // group):
        for i, ref in enumerate(rows):
            c, tl = divmod(i, tiles)
            lanes = slice(tl * LANES, (tl + 1) * LANES)
            xt = jnp.concatenate([pages[q * group + g][0, c, lanes, :] for g in range(group)], axis=1)
            moved = _dot_nt(perm, (xt + pos_ref[c, lanes, :]).astype(BF16))
            ref[:, q * SUBLANES:(q + 1) * SUBLANES, :] = moved.reshape(CMP_BLOCK, SUBLANES, LANES)
    n_blk = n_pages * per_page
    out_ref[0] = _compress_rows(lambda c, tl, r: rows[c * tiles + tl][r], w1_ref, w2_ref, n_blk)


def _page_specs(n_pages, page):
    return [pl.BlockSpec((1, 2, NSA_KVW, page), lambda b, pt, p=p: (pt[b, p], 0, 0, 0)) for p in range(n_pages)]


def nsa_compress_paged(cache_t, page_table, pos, w1, w2):
    n_seq, n_pages = page_table.shape
    page = cache_t.shape[3]
    n_blk = n_pages * page // CMP_BLOCK
    width = SUBLANES // (page // CMP_BLOCK) * page
    pos_t = jnp.tile(pos.transpose(0, 2, 1), (1, NSA_KVH, width // CMP_BLOCK))
    const = lambda shape: pl.BlockSpec(shape, lambda b, pt: (0,) * len(shape))
    return pl.pallas_call(
        functools.partial(_paged_compress_kernel, n_pages=n_pages, page=page),
        grid_spec=pltpu.PrefetchScalarGridSpec(
            num_scalar_prefetch=1,
            grid=(n_seq,),
            in_specs=_page_specs(n_pages, page)
            + [const((2, NSA_KVW, width)), const((2, CMP_BLOCK * NSA_DH, CMP_HID)), const((2, CMP_HID, NSA_DH))],
            out_specs=pl.BlockSpec((1, n_blk, NSA_ROW), lambda b, pt: (b, 0, 0)),
            scratch_shapes=[pltpu.VMEM((CMP_BLOCK, n_blk, LANES), F32)] * (NSA_ROW // LANES)),
        out_shape=jax.ShapeDtypeStruct((n_seq, n_blk, NSA_ROW), F32),
        compiler_params=_params("arbitrary"),
        name="nsa_compress_paged",
    )(page_table, *([cache_t] * n_pages), pos_t, w1, w2)


def _softmax_with_new_key(s, ok, s_new, new_ok):
    s = jnp.where(ok, s, MASK_NEG)
    s_new = jnp.where(new_ok, s_new, MASK_NEG)
    m = jnp.maximum(jnp.max(s, axis=-1, keepdims=True), s_new)
    e = jnp.where(ok, jnp.exp(s - m), 0.0)
    e_new = jnp.where(new_ok, jnp.exp(s_new - m), 0.0)
    total = jnp.sum(e, axis=-1, keepdims=True) + e_new
    return e, e_new, 1.0 / jnp.maximum(total, 1e-30)


def _nsa_step_kernel(pt_ref, *refs, n_pages, page, past):
    del pt_ref
    pages = refs[:n_pages]
    (q_ref, gp_ref, cke_ref, cko_ref, kvs_new_ref, swa_ref, kvw_new_ref,
     o_ref, swa_out_ref, kt_ref, vt_ref) = refs[n_pages:]
    nh, kw = NSA_HEADS, NSA_KVW
    t = past
    own = (lax.broadcasted_iota(jnp.int32, (nh, kw), 1) // NSA_DH
           == lax.broadcasted_iota(jnp.int32, (nh, kw), 0) // NSA_GROUP)
    q = q_ref[0] * (NSA_DH ** -0.5)
    qp = jnp.where(own, jnp.concatenate([q] * NSA_KVH, axis=1), 0.0)
    qpb = qp.astype(BF16)

    half = cke_ref.shape[1]
    n_c = 2 * half
    ce, co = cke_ref[0], cko_ref[0]
    ck = jnp.concatenate([ce[:, :kw], co[:, :kw]], axis=0).astype(BF16)
    cv = jnp.concatenate([ce[:, kw:], co[:, kw:]], axis=0).astype(BF16)
    nn = lax.broadcasted_iota(jnp.int32, (1, n_c), 1)
    c_blk = jnp.where(nn < half, 2 * nn, 2 * (nn - half) + 1)
    p_c = _masked_softmax(_dot_nt(qpb, ck), (c_blk * CMP_BLOCK + CMP_BLOCK - 1) <= t)
    o_c = jnp.dot(p_c.astype(BF16), cv, preferred_element_type=F32)
    same_group = (lax.broadcasted_iota(jnp.int32, (nh, nh), 0) // NSA_GROUP
                  == lax.broadcasted_iota(jnp.int32, (nh, nh), 1) // NSA_GROUP).astype(F32)
    imp = lax.dot_general(p_c, same_group, (((0,), (0,)), ((), ())), preferred_element_type=F32, precision=HIGHEST)
    imp = imp[:half] + imp[half:]
    imp = jnp.concatenate([imp, jnp.zeros((LANES - half, nh), F32)], axis=0)
    blk = lax.broadcasted_iota(jnp.int32, (LANES, nh), 0)
    cur = t // SLC_BLOCK
    forced = (blk == 0) | (blk == cur) | (blk == cur - 1)
    sel_t = _top_blocks(jnp.where(forced, SEL_BIG, jnp.where(blk <= cur, imp, -SEL_BIG)), SLC_TOPK, 0)

    for p in range(n_pages):
        kt_ref[:, p * page:(p + 1) * page] = pages[p][0, 0].astype(BF16)
        vt_ref[:, p * page:(p + 1) * page] = pages[p][0, 1].astype(BF16)
    n_keys = n_pages * page
    pos = lax.broadcasted_iota(jnp.int32, (LANES, n_keys + LANES), 1)
    expand = (lax.broadcasted_iota(jnp.int32, (LANES, n_keys + LANES), 0)
              == jnp.minimum(pos, t) // SLC_BLOCK).astype(BF16)
    chosen = _dot_tn(sel_t.astype(BF16), expand) > 0.5
    ok = chosen[:, :n_keys] & (lax.broadcasted_iota(jnp.int32, (1, n_keys), 1) <= t)
    kv_new = kvs_new_ref[0]
    s_new = jnp.sum(qp * kv_new[:, :kw], axis=-1, keepdims=True)
    scores = jnp.dot(qpb, kt_ref[...], preferred_element_type=F32)
    e, e_new, inv = _softmax_with_new_key(scores, ok, s_new, chosen[:, n_keys:n_keys + 1])
    o_s = (_dot_nt(e.astype(BF16), vt_ref[...]) + e_new * kv_new[:, kw:]) * inv

    n_w = swa_ref.shape[3]
    dist = t - (t - n_w + lax.broadcasted_iota(jnp.int32, (1, n_w), 1))
    w_ok = (dist >= 0) & (dist <= WINDOW) & (t - dist >= 0)
    kvw_new = kvw_new_ref[0]
    s_wn = jnp.sum(qp * kvw_new[:, :kw], axis=-1, keepdims=True)
    new_ok = lax.broadcasted_iota(jnp.int32, (nh, 1), 0) >= 0
    scores = jnp.dot(qpb, swa_ref[0, 0].astype(BF16), preferred_element_type=F32)
    e, e_new, inv = _softmax_with_new_key(scores, w_ok, s_wn, new_ok)
    o_w = (_dot_nt(e.astype(BF16), swa_ref[0, 1].astype(BF16)) + e_new * kvw_new[:, kw:]) * inv

    g = _sigmoid(gp_ref[0])
    o = jnp.where(own, g[:, 0:1] * o_c + g[:, 1:2] * o_s + g[:, 2:3] * o_w, 0.0)
    o_h = o[:, :NSA_DH]
    for h in range(1, NSA_KVH):
        o_h = o_h + o[:, h * NSA_DH:(h + 1) * NSA_DH]
    o_ref[0] = o_h
    last = lax.broadcasted_iota(jnp.int32, (kw, n_w), 1) == n_w - 1
    diag = lax.broadcasted_iota(jnp.int32, (kw, kw), 0) == lax.broadcasted_iota(jnp.int32, (kw, kw), 1)
    for c in range(2):
        column = jnp.sum(jnp.where(diag, kvw_new[:, c * kw:(c + 1) * kw], 0.0), axis=-1, keepdims=True)
        swa_out_ref[0, c] = jnp.where(last, column, pltpu.roll(swa_ref[0, c], n_w - 1, 1))


def nsa_step_attention(q3, gates3, cke, cko, cache_slc_t, page_table, kvs_new, swa_t, kvw_new, *, past):
    n_seq, n_pages = page_table.shape
    page = cache_slc_t.shape[3]
    seq = lambda a: pl.BlockSpec((1,) + a.shape[1:], lambda b, pt: (b,) + (0,) * (a.ndim - 1))
    return pl.pallas_call(
        functools.partial(_nsa_step_kernel, n_pages=n_pages, page=page, past=past),
        grid_spec=pltpu.PrefetchScalarGridSpec(
            num_scalar_prefetch=1,
            grid=(n_seq,),
            in_specs=_page_specs(n_pages, page) + [seq(a) for a in (q3, gates3, cke, cko, kvs_new, swa_t, kvw_new)],
            out_specs=[seq(q3), seq(swa_t)],
            scratch_shapes=[pltpu.VMEM((NSA_KVW, n_pages * page), BF16)] * 2),
        out_shape=[jax.ShapeDtypeStruct(q3.shape, F32), jax.ShapeDtypeStruct(swa_t.shape, F32)],
        compiler_params=_params("arbitrary"),
        name="nsa_step_attention",
    )(page_table, *([cache_slc_t] * n_pages), q3, gates3, cke, cko, kvs_new, swa_t, kvw_new)


PROMPT_TM = 1024
FFN_TM = 512
HG_TL = 512
RG_TL = 256
FFN_TF = 1408
CMP_TILE_BLOCKS = 64
MOE_TB = 896
MOE_TS = 256


def _ffn_layer(x, w_gu, w_down, g, b, *, tm):
    return ffn_residual_ln(x, w_gu.astype(BF16), w_down.astype(BF16), g, b, tm=tm, tf=FFN_TF)


def _moe_layer(x, w_router, w_gu_t, w_down_t, slot, g, b):
    m = x.shape[0]
    tb = min(MOE_TB, m)
    gates, slots = moe_route(x, w_router.T, tb=tb)
    n_e = gates.shape[0]
    counts = (jnp.max(slots.reshape(n_e, -1, tb), axis=-1) + 1).T.reshape(-1)
    return moe_residual_ln(x, gates, slots, counts, w_gu_t, w_down_t, slot, g, b,
                           tb=tb, ts=min(MOE_TS, tb), tf=FFN_TF)


def kernel(x_prompt, x_sample, state_hgrn, state_rglru_conv, state_rglru_h, cache_nsa_cmp, cache_nsa_slc, cache_nsa_swa, page_table, ln_g, ln_b, hgrn_w_in, hgrn_lb, hgrn_norm_g, hgrn_w_out, rg_w_in, rg_conv_w, rg_conv_b, rg_w_gate, rg_b_gate, rg_lambda, rg_w_out, nsa_w_in, nsa_cmp_pos, nsa_cmp_w1, nsa_cmp_w2, nsa_w_out, ffn_w_gu, ffn_w_down, moe_router, moe_w_gu, moe_w_down):
    bp, l_seq, d = x_prompt.shape
    n_seq = x_sample.shape[0]
    past = page_table.shape[1] * cache_nsa_cmp.shape[2]
    xp = x_prompt.reshape(bp * l_seq, d)
    xs = x_sample.reshape(n_seq, d)
    tm_p, tm_s = PROMPT_TM, n_seq
    kv_shape = (2, NSA_KVH, NSA_DH)
    moe_w_gu_t = jnp.swapaxes(moe_w_gu, 2, 3).astype(BF16)
    moe_w_down_t = jnp.swapaxes(moe_w_down, 2, 3).astype(BF16)
    hg_p, hg_s, rc_p, rc_s, rh_p, rh_s = [], [], [], [], [], []
    cm_p, cm_s, sl_p, sl_s, sw_p, sw_s = [], [], [], [], [], []
    for layer in range(DEPTH):
        kind, slot = layer % 3, layer // 3
        g0, b0 = ln_g[layer, 0], ln_b[layer, 0]
        if kind == 0:
            w_in, w_out = hgrn_w_in[slot].astype(BF16), hgrn_w_out[slot].astype(BF16)
            op, st_p = hgrn_prompt(matmul(xp, w_in, tm=tm_p), hgrn_lb, hgrn_norm_g[slot],
                                   layer=layer, batch=bp, tl=HG_TL)
            os_, st_s = hgrn_step(matmul(xs, w_in, tm=tm_s), state_hgrn, slot, hgrn_lb, hgrn_norm_g[slot],
                                  layer=layer, sb=8)
            hg_p.append(st_p)
            hg_s.append(st_s)
        elif kind == 1:
            w_in, w_out = rg_w_in[slot].astype(BF16), rg_w_out[slot].astype(BF16)
            rg_w = (rg_conv_w[slot], rg_conv_b[slot], rg_w_gate[slot].astype(BF16), rg_b_gate[slot], rg_lambda[slot])
            op, cb_p, h_p = rglru_prompt(matmul(xp, w_in, tm=tm_p), *rg_w, batch=bp, tl=RG_TL)
            os_, cb_s, h_s = rglru_step(matmul(xs, w_in, tm=tm_s), state_rglru_conv[slot], state_rglru_h[slot], *rg_w)
            rc_p.append(cb_p)
            rc_s.append(cb_s)
            rh_p.append(h_p[:, 0])
            rh_s.append(h_s)
        else:
            w_qkv, w_kvb, w_g, w_kv_t, pos4, w1, w2 = nsa_prepare_weights(
                nsa_w_in[slot], nsa_cmp_pos[slot], nsa_cmp_w1[slot], nsa_cmp_w2[slot])
            w_out = nsa_w_out[slot].astype(BF16)
            hq = NSA_HEADS * NSA_DH
            qkv = matmul(xp, w_qkv[:, :hq + NSA_ROW], tm=tm_p)
            kvb = matmul(xp, w_kvb, tm=tm_p, out_dtype=BF16)
            gates_pre = matmul(xp, w_g, tm=tm_p)
            cke, cko = nsa_compress(qkv, hq // NSA_ROW, pos4, w1, w2, batch=bp, n_blk=CMP_TILE_BLOCKS)
            op = nsa_prompt_attention(qkv, gates_pre, jnp.concatenate([cke, cko], axis=1), kvb, batch=bp)
            as_rows = lambda a: a.reshape(bp, *kv_shape, -1).transpose(0, 4, 1, 2, 3)
            kv_c, kv_s, kv_w = matmul_t(xp, w_kv_t, batch=bp, n_out=3, tm=tm_p)
            cm_p.append(as_rows(kv_c))
            sl_p.append(as_rows(kv_s))
            sw_p.append(as_rows(kv_w[:, :, l_seq - min(WINDOW, l_seq):]))
            qkv = matmul(xs, w_qkv, tm=tm_s)
            gates_pre = matmul(xs, w_g, tm=tm_s)
            rows_last = lambda c: c.transpose(0, 2, 3, 4, 1).reshape(c.shape[0], 2, NSA_KVW, c.shape[1])
            ckv = nsa_compress_paged(rows_last(cache_nsa_cmp[slot]), page_table, nsa_cmp_pos[slot], w1, w2)
            cke, cko = ckv[:, 0::2], ckv[:, 1::2]
            gates3 = gates_pre.reshape(n_seq, NSA_KVH, LANES)[:, :, :3 * NSA_GROUP].reshape(n_seq, NSA_HEADS, 3)
            kv = qkv[:, hq:].reshape(n_seq, 3, 1, NSA_ROW)
            o3, swa_new = nsa_step_attention(qkv[:, :hq].reshape(n_seq, NSA_HEADS, NSA_DH), gates3, cke, cko,
                                             rows_last(cache_nsa_slc[slot]), page_table, kv[:, 1],
                                             rows_last(cache_nsa_swa[slot]), kv[:, 2], past=past)
            os_ = o3.reshape(n_seq, hq)
            cm_s.append(kv[:, 0].reshape(n_seq, 1, *kv_shape))
            sl_s.append(kv[:, 1].reshape(n_seq, 1, *kv_shape))
            sw_s.append(swa_new.reshape(n_seq, *kv_shape, -1).transpose(0, 4, 1, 2, 3))
        xp = matmul_residual_ln(op, w_out, xp, g0, b0, tm=tm_p)
        xs = matmul_residual_ln(os_, w_out, xs, g0, b0, tm=tm_s)
        g1, b1 = ln_g[layer, 1], ln_b[layer, 1]
        if layer % 2 == 0:
            fw = (ffn_w_gu[layer // 2], ffn_w_down[layer // 2])
            xp = _ffn_layer(xp, *fw, g1, b1, tm=FFN_TM)
            xs = _ffn_layer(xs, *fw, g1, b1, tm=tm_s)
        else:
            fw = (moe_router[layer // 2], moe_w_gu_t, moe_w_down_t, layer // 2)
            xp = _moe_layer(xp, *fw, g1, b1)
            xs = _moe_layer(xs, *fw, g1, b1)
    return (xp.reshape(bp, l_seq, d), xs.reshape(n_seq, 1, d),
            jnp.stack(hg_p), jnp.stack(hg_s), jnp.stack(rc_p), jnp.stack(rc_s), jnp.stack(rh_p), jnp.stack(rh_s),
            jnp.stack(cm_p), jnp.stack(cm_s), jnp.stack(sl_p), jnp.stack(sl_s), jnp.stack(sw_p), jnp.stack(sw_s))
```

```python
import functools

import jax
import jax.numpy as jnp
from jax import lax
from jax.experimental import pallas as pl
from jax.experimental.pallas import tpu as pltpu

F32 = jnp.float32
BF16 = jnp.bfloat16
HIGHEST = lax.Precision.HIGHEST

D_MODEL = 1024
DEPTH = 4
ALPHA = (2 * DEPTH) ** 0.25
LN_EPS = 1e-5
RMS_EPS = 1e-6
HG_HEADS = 8
HG_DK = 128
HG_CHUNK = 64
HG_SUB = 16
HG_HEADS_PER_STEP = 8
LB_FLOOR = 1e-30
RG_WIDTH = 1280
RG_BLOCKS = 10
RG_BS = 128
RG_CONV = 4
RG_C = 8.0
D_FF = 2816
LOG2_E = 1.4426950408889634

SUBLANES = 8
LANES = 128
VMEM_LIMIT_BYTES = 56 * 1024 * 1024


def _params(*sem):
    return pltpu.CompilerParams(dimension_semantics=sem, vmem_limit_bytes=VMEM_LIMIT_BYTES)


def _layer_norm(v, g, b):
    mu = jnp.mean(v, axis=-1, keepdims=True)
    vc = v - mu
    var = jnp.mean(vc * vc, axis=-1, keepdims=True)
    return vc * lax.rsqrt(var + LN_EPS) * g + b


def _sigmoid(x):
    return 1.0 / (1.0 + jnp.exp(-x))


def _silu(x):
    return x * _sigmoid(x)


def _mm_kernel(x_ref, w_ref, o_ref, xb_ref):
    @pl.when(pl.program_id(1) == 0)
    def _():
        xb_ref[...] = x_ref[...].astype(BF16)

    o_ref[...] = jnp.dot(xb_ref[...], w_ref[...], preferred_element_type=F32).astype(o_ref.dtype)


MAX_COL_TILE = 1280


def _col_tile(n):
    return max(t for t in range(LANES, min(n, MAX_COL_TILE) + 1, LANES) if n % t == 0)


def matmul(x, w, *, tm, out_dtype=F32):
    m, k = x.shape
    n = w.shape[1]
    tn = _col_tile(n)
    return pl.pallas_call(
        _mm_kernel,
        grid=(m // tm, n // tn),
        in_specs=[pl.BlockSpec((tm, k), lambda i, j: (i, 0)),
                  pl.BlockSpec((k, tn), lambda i, j: (0, j))],
        out_specs=pl.BlockSpec((tm, tn), lambda i, j: (i, j)),
        out_shape=jax.ShapeDtypeStruct((m, n), out_dtype),
        scratch_shapes=[pltpu.VMEM((tm, k), BF16)],
        compiler_params=_params("parallel", "arbitrary"),
        name="matmul",
    )(x, w)


def _mm_t_kernel(x_ref, wt_ref, *o_refs):
    xb = x_ref[...].astype(BF16)
    n = o_refs[0].shape[1]
    for r, o_ref in enumerate(o_refs):
        o_ref[0] = _dot_nt(wt_ref[r * n:(r + 1) * n, :], xb)


def matmul_t(x, w_t, *, batch, n_out, tm):
    m, k = x.shape
    n = w_t.shape[0] // n_out
    l_seq = m // batch
    n_t = l_seq // tm
    return pl.pallas_call(
        _mm_t_kernel,
        grid=(batch, n_t),
        in_specs=[pl.BlockSpec((tm, k), lambda b, i: (b * n_t + i, 0)),
                  pl.BlockSpec((n_out * n, k), lambda b, i: (0, 0))],
        out_specs=[pl.BlockSpec((1, n, tm), lambda b, i: (b, 0, i))] * n_out,
        out_shape=[jax.ShapeDtypeStruct((batch, n, l_seq), F32)] * n_out,
        compiler_params=_params("parallel", "parallel"),
        name="matmul_t",
    )(x, w_t)


def _mm_res_ln_kernel(a_ref, w_ref, r_ref, g_ref, b_ref, o_ref):
    m = jnp.dot(a_ref[...].astype(BF16), w_ref[...], preferred_element_type=F32)
    o_ref[...] = _layer_norm(ALPHA * r_ref[...] + m, g_ref[...], b_ref[...])


def matmul_residual_ln(a, w, res, g, b, *, tm):
    m, k = a.shape
    d = w.shape[1]
    return pl.pallas_call(
        _mm_res_ln_kernel,
        grid=(m // tm,),
        in_specs=[pl.BlockSpec((tm, k), lambda i: (i, 0)),
                  pl.BlockSpec((k, d), lambda i: (0, 0)),
                  pl.BlockSpec((tm, d), lambda i: (i, 0)),
                  pl.BlockSpec((1, d), lambda i: (0, 0)),
                  pl.BlockSpec((1, d), lambda i: (0, 0))],
        out_specs=pl.BlockSpec((tm, d), lambda i: (i, 0)),
        out_shape=jax.ShapeDtypeStruct((m, d), F32),
        compiler_params=_params("parallel"),
        name="matmul_residual_ln",
    )(a, w, res, g.reshape(1, d), b.reshape(1, d))


def _ffn_kernel(x_ref, wg_ref, wu_ref, wd_ref, g_ref, b_ref, o_ref, xb_ref, acc_ref, *, n_f):
    f = pl.program_id(1)

    @pl.when(f == 0)
    def _():
        xb_ref[...] = x_ref[...].astype(BF16)

    xb = xb_ref[...]
    h = jnp.dot(xb, wg_ref[...], preferred_element_type=F32)
    u = jnp.dot(xb, wu_ref[...], preferred_element_type=F32)
    part = jnp.dot((_silu(h) * u).astype(BF16), wd_ref[...], preferred_element_type=F32)

    @pl.when(f == 0)
    def _():
        acc_ref[...] = part

    @pl.when(f > 0)
    def _():
        acc_ref[...] += part

    @pl.when(f == n_f - 1)
    def _():
        o_ref[...] = _layer_norm(ALPHA * x_ref[...] + acc_ref[...], g_ref[...], b_ref[...])


def ffn_residual_ln(x, w_gu, w_down, g, b, *, tm, tf):
    m, d = x.shape
    ff = w_down.shape[0]
    n_f = ff // tf
    return pl.pallas_call(
        functools.partial(_ffn_kernel, n_f=n_f),
        grid=(m // tm, n_f),
        in_specs=[pl.BlockSpec((tm, d), lambda i, f: (i, 0)),
                  pl.BlockSpec((d, tf), lambda i, f: (0, f)),
                  pl.BlockSpec((d, tf), lambda i, f: (0, n_f + f)),
                  pl.BlockSpec((tf, d), lambda i, f: (f, 0)),
                  pl.BlockSpec((1, d), lambda i, f: (0, 0)),
                  pl.BlockSpec((1, d), lambda i, f: (0, 0))],
        out_specs=pl.BlockSpec((tm, d), lambda i, f: (i, 0)),
        out_shape=jax.ShapeDtypeStruct((m, d), F32),
        scratch_shapes=[pltpu.VMEM((tm, d), BF16), pltpu.VMEM((tm, d), F32)],
        compiler_params=_params("parallel", "arbitrary"),
        name="ffn_residual_ln",
    )(x, w_gu, w_gu, w_down, g.reshape(1, d), b.reshape(1, d))


def _router_kernel(x_ref, wt_ref, gate_ref, pos_ref, *, n_tokens):
    logits = lax.dot_general(wt_ref[...], x_ref[...], (((1,), (1,)), ((), ())),
                             preferred_element_type=F32, precision=HIGHEST)
    n_e, tm = logits.shape
    expert = lax.broadcasted_iota(jnp.int32, logits.shape, 0)
    m1 = jnp.max(logits, axis=0, keepdims=True)
    i1 = jnp.min(jnp.where(logits == m1, expert, n_e), axis=0, keepdims=True)
    rest = jnp.where(expert == i1, -jnp.inf, logits)
    m2 = jnp.max(rest, axis=0, keepdims=True)
    i2 = jnp.min(jnp.where(rest == m2, expert, n_e), axis=0, keepdims=True)
    e2 = jnp.exp(m2 - m1)
    denom = 1.0 + e2
    first = expert == i1
    in_range = pl.program_id(0) * tm + lax.broadcasted_iota(jnp.int32, logits.shape, 1) < n_tokens
    routed = (first | (expert == i2)) & in_range
    gate_ref[...] = jnp.where(first & in_range, 1.0 / denom, jnp.where(routed, e2 / denom, 0.0))
    upper = (lax.broadcasted_iota(jnp.int32, (tm, tm), 0) <= lax.broadcasted_iota(jnp.int32, (tm, tm), 1)).astype(BF16)
    upto = jnp.dot(routed.astype(BF16), upper, preferred_element_type=F32)
    pos_ref[...] = jnp.where(routed, upto - 1.0, -1.0).astype(jnp.int32)


def moe_route(x, w_router_t, *, tb):
    m, d = x.shape
    n_e = w_router_t.shape[0]
    n_blk = pl.cdiv(m, tb)
    return pl.pallas_call(
        functools.partial(_router_kernel, n_tokens=m),
        grid=(n_blk,),
        in_specs=[pl.BlockSpec((tb, d), lambda i: (i, 0)),
                  pl.BlockSpec((n_e, d), lambda i: (0, 0))],
        out_specs=[pl.BlockSpec((n_e, tb), lambda i: (0, i))] * 2,
        out_shape=[jax.ShapeDtypeStruct((n_e, n_blk * tb), F32), jax.ShapeDtypeStruct((n_e, n_blk * tb), jnp.int32)],
        compiler_params=_params("parallel"),
        name="moe_route",
    )(x, w_router_t)


def _moe_kernel(cnt_ref, x_ref, gate_ref, slot_ref, wg_ref, wu_ref, wd_ref, g_ref, b_ref, o_ref,
                xt_ref, xc_ref, acc_ref, yt_ref, *, n_e, n_f, ts, n_tokens):
    i, e, f = pl.program_id(0), pl.program_id(1), pl.program_id(2)
    tb = x_ref.shape[0]
    n_sub = (cnt_ref[i * n_e + e] + ts - 1) // ts
    row = lax.broadcasted_iota(jnp.int32, (ts, tb), 0)

    def one_hot(sub):
        return (slot_ref[pl.ds(e, 1), :] == row + sub * ts).astype(BF16)

    @pl.when((e == 0) & (f == 0))
    def _():
        in_range = i * tb + lax.broadcasted_iota(jnp.int32, (tb, 1), 0) < n_tokens
        xt_ref[...] = jnp.where(in_range, x_ref[...], 0.0).T.astype(BF16)
        yt_ref[...] = jnp.zeros_like(yt_ref)

    tf, d = wg_ref.shape[1], wd_ref.shape[1]
    halves = lambda n: (slice(0, n // 2), slice(n // 2, n))

    @pl.when(f == 0)
    def _():
        def gather(sub, c):
            pick = one_hot(sub)
            for rows in halves(d):
                xc_ref[sub, rows, :] = _dot_nt(xt_ref[rows, :], pick).astype(BF16)
            return c

        lax.fori_loop(0, n_sub, gather, 0)

    def expert_ffn(sub, c):
        xc = xc_ref[sub]
        act = []
        for rows in halves(tf):
            h = jnp.dot(wg_ref[0, rows, :], xc, preferred_element_type=F32)
            u = jnp.dot(wu_ref[0, rows, :], xc, preferred_element_type=F32)
            act.append((_silu(h) * u).astype(BF16))
        act = jnp.concatenate(act, axis=0)
        part = jnp.concatenate([jnp.dot(wd_ref[0, rows, :], act, preferred_element_type=F32) for rows in halves(d)], axis=0)

        @pl.when(f == 0)
        def _():
            acc_ref[sub] = part

        @pl.when(f > 0)
        def _():
            acc_ref[sub] += part

        return c

    lax.fori_loop(0, n_sub, expert_ffn, 0)

    @pl.when(f == n_f - 1)
    def _():
        def scatter(sub, c):
            pick = one_hot(sub)
            gate = gate_ref[pl.ds(e, 1), :]
            for rows in halves(d):
                back = jnp.dot(acc_ref[sub, rows, :].astype(BF16), pick, preferred_element_type=F32)
                yt_ref[rows, :] += gate * back
            return c

        lax.fori_loop(0, n_sub, scatter, 0)

    @pl.when((e == n_e - 1) & (f == n_f - 1))
    def _():
        o_ref[...] = _layer_norm(ALPHA * x_ref[...] + yt_ref[...].T, g_ref[...], b_ref[...])


def moe_residual_ln(x, gates, slots, counts, w_gu_t, w_down_t, slot, g, b, *, tb, ts, tf):
    m, d = x.shape
    _, n_e, ff2, _ = w_gu_t.shape
    n_f = ff2 // 2 // tf
    n_sub = pl.cdiv(tb, ts)
    return pl.pallas_call(
        functools.partial(_moe_kernel, n_e=n_e, n_f=n_f, ts=ts, n_tokens=m),
        grid_spec=pltpu.PrefetchScalarGridSpec(
            num_scalar_prefetch=1,
            grid=(pl.cdiv(m, tb), n_e, n_f),
            in_specs=[pl.BlockSpec((tb, d), lambda i, e, f, c: (i, 0)),
                      pl.BlockSpec((n_e, tb), lambda i, e, f, c: (0, i)),
                      pl.BlockSpec((n_e, tb), lambda i, e, f, c: (0, i)),
                      pl.BlockSpec((None, 1, tf, d), lambda i, e, f, c: (slot, e, f, 0)),
                      pl.BlockSpec((None, 1, tf, d), lambda i, e, f, c: (slot, e, n_f + f, 0)),
                      pl.BlockSpec((None, 1, d, tf), lambda i, e, f, c: (slot, e, 0, f)),
                      pl.BlockSpec((1, d), lambda i, e, f, c: (0, 0)),
                      pl.BlockSpec((1, d), lambda i, e, f, c: (0, 0))],
            out_specs=pl.BlockSpec((tb, d), lambda i, e, f, c: (i, 0)),
            scratch_shapes=[pltpu.VMEM((d, tb), BF16), pltpu.VMEM((n_sub, d, ts), BF16),
                            pltpu.VMEM((n_sub, d, ts), F32), pltpu.VMEM((d, tb), F32)]),
        out_shape=jax.ShapeDtypeStruct((m, d), F32),
        compiler_params=_params("parallel", "arbitrary", "arbitrary"),
        name="moe_residual_ln",
    )(counts, x, gates, slots, w_gu_t, w_gu_t, w_down_t, g.reshape(1, d), b.reshape(1, d))


def _hgrn_lower_bound(lbp, layer):
    m = jnp.max(lbp, axis=0, keepdims=True)
    e = jnp.exp(lbp - m)
    p = e / jnp.sum(e, axis=0, keepdims=True)
    c = p[0:1]
    for i in range(1, layer + 1):
        c = c + p[i:i + 1]
    return c - p[0:1]


def _hgrn_gates(q, fx, lb):
    log_lb = jnp.log(jnp.maximum(lb, LB_FLOOR))
    log1m = jnp.log1p(-lb)
    log_sig = -(jnp.maximum(-fx, 0.0) + jnp.log1p(jnp.exp(-jnp.abs(fx))))
    b2 = log1m + log_sig
    log_f = jnp.maximum(log_lb, b2) + jnp.log1p(jnp.exp(-jnp.abs(log_lb - b2)))
    kk = (1.0 - lb) * _sigmoid(-fx) - (jnp.maximum(lb, LB_FLOOR) - lb)
    return _silu(q), log_f, kk


def _hgrn_out_norm(o, g, ng):
    o = o * lax.rsqrt(jnp.mean(o * o, axis=-1, keepdims=True) + RMS_EPS)
    return o * ng * _silu(g)


def _dot_nt(a, b):
    return lax.dot_general(a, b, (((1,), (1,)), ((), ())), preferred_element_type=F32)


def _dot_tn(a, b):
    return lax.dot_general(a, b, (((0,), (0,)), ((), ())), preferred_element_type=F32)


def _gla_kernel(q_ref, f_ref, i_ref, g_ref, lbp_ref, ng_ref, o_ref, s_out_ref, st_ref, *, layer, n_t, tl, hp):
    t = pl.program_id(2)
    c_len, sub = HG_CHUNK, HG_SUB
    n_sub = c_len // sub
    n_stack = sub * (n_sub * (n_sub - 1) // 2)

    @pl.when(t == 0)
    def _():
        st_ref[...] = jnp.zeros_like(st_ref)

    lb = _hgrn_lower_bound(lbp_ref[...], layer)
    ng = ng_ref[...]
    row = lax.broadcasted_iota(jnp.int32, (c_len, c_len), 0)
    col = lax.broadcasted_iota(jnp.int32, (c_len, c_len), 1)
    tri = jnp.where(col <= row, LOG2_E, 0.0)
    srow = lax.broadcasted_iota(jnp.int32, (c_len, n_stack), 0) // sub
    scol = lax.broadcasted_iota(jnp.int32, (c_len, n_stack), 1)
    part = jnp.zeros((c_len, n_stack), jnp.int32)
    for p in range(1, n_sub):
        part = part + (scol >= sub * (p * (p - 1) // 2)).astype(jnp.int32)
    inter_mask = srow == part

    def one_head(hh, r):
        lanes = slice(hh * HG_DK, (hh + 1) * HG_DK)
        qs, log_f, kk = _hgrn_gates(q_ref[r, lanes], f_ref[r, lanes], lb[:, lanes])
        v = i_ref[r, lanes]
        cum = jnp.dot(tri, log_f, preferred_element_type=F32, precision=HIGHEST)
        last = cum[c_len - 1:c_len, :]
        st = st_ref[hh]
        o = _dot_nt((qs * jnp.exp2(cum)).astype(BF16), st.astype(BF16))
        bounds = [cum[sub * a - 1:sub * a, :] for a in range(1, n_sub)]
        cq = jnp.concatenate([cum[0:sub]] + [jnp.broadcast_to(b, (sub, HG_DK)) for b in bounds], axis=0)
        q_rel = (qs * jnp.exp2(cum - cq)).astype(BF16)
        k_st = jnp.concatenate([kk[0:sub * a] * jnp.exp2(bounds[a - 1] - cum[0:sub * a]) for a in range(1, n_sub)], axis=0)
        v_st = jnp.concatenate([v[0:sub * a] for a in range(1, n_sub)], axis=0)
        att = jnp.where(inter_mask, _dot_nt(q_rel, k_st.astype(BF16)), 0.0)
        o = o + jnp.dot(att.astype(BF16), v_st.astype(BF16), preferred_element_type=F32)
        parts = []
        for a in range(n_sub):
            base = sub * a
            rest = o[base:base + sub]
            for lo in range(0, sub, SUBLANES):
                tail = slice(base + lo, base + sub)
                qa, ca = qs[tail], cum[tail]
                row_t = lo + lax.broadcasted_iota(jnp.int32, (sub - lo, 1), 0)
                for s in range(lo, lo + SUBLANES):
                    src = slice(base + s, base + s + 1)
                    w = jnp.sum(qa * (kk[src] * jnp.exp2(ca - cum[src])), axis=-1, keepdims=True)
                    rest = rest + jnp.where(row_t >= s, w, 0.0) * v[src]
                parts.append(rest[:SUBLANES])
                rest = rest[SUBLANES:]
        o = jnp.concatenate(parts, axis=0)
        o_ref[r, lanes] = _hgrn_out_norm(o, g_ref[r, lanes], ng[:, lanes]).astype(o_ref.dtype)
        k_end = (kk * jnp.exp2(last - cum)).astype(BF16)
        st_ref[hh] = st * jnp.exp2(last) + _dot_tn(v.astype(BF16), k_end)

    def chunk(c, carry):
        r = pl.ds(pl.multiple_of(c * c_len, c_len), c_len)
        for hh in range(hp):
            one_head(hh, r)
        return carry

    lax.fori_loop(0, tl // c_len, chunk, 0)

    @pl.when(t == n_t - 1)
    def _():
        for hh in range(hp):
            s_out_ref[0, hh] = st_ref[hh].T


def hgrn_prompt(z, lb_param, norm_g, *, layer, batch, tl):
    m = z.shape[0]
    n_t = m // batch // tl
    h, dk = HG_HEADS, HG_DK
    hp = HG_HEADS_PER_STEP
    n_hg = h // hp
    zspec = lambda part: pl.BlockSpec((tl, hp * dk), lambda b, hh, t: (b * n_t + t, part * n_hg + hh))
    return pl.pallas_call(
        functools.partial(_gla_kernel, layer=layer, n_t=n_t, tl=tl, hp=hp),
        grid=(batch, n_hg, n_t),
        in_specs=[zspec(0), zspec(1), zspec(2), zspec(3),
                  pl.BlockSpec((DEPTH, hp * dk), lambda b, hh, t: (0, hh)),
                  pl.BlockSpec((1, hp * dk), lambda b, hh, t: (0, hh))],
        out_specs=[pl.BlockSpec((tl, hp * dk), lambda b, hh, t: (b * n_t + t, hh)),
                   pl.BlockSpec((1, hp, dk, dk), lambda b, hh, t: (b, hh, 0, 0))],
        out_shape=[jax.ShapeDtypeStruct((m, h * dk), BF16),
                   jax.ShapeDtypeStruct((batch, h, dk, dk), F32)],
        scratch_shapes=[pltpu.VMEM((hp, dk, dk), F32)],
        compiler_params=_params("parallel", "parallel", "arbitrary"),
        name="hgrn_prompt",
    )(z, z, z, z, lb_param, norm_g.reshape(1, h * dk))


def _hgrn_step_kernel(q_ref, f_ref, i_ref, g_ref, lbp_ref, ng_ref, s0_ref, o_ref, s_ref,
                      qt_ref, ft_ref, kt_ref, oacc_ref, *, layer, n_j, sb):
    j = pl.program_id(1)
    n_seq = q_ref.shape[0]

    def split3(x, ref):
        hi = x.astype(BF16)
        r1 = x - hi.astype(F32)
        mid = r1.astype(BF16)
        ref[0], ref[1], ref[2] = hi, mid, (r1 - mid.astype(F32)).astype(BF16)

    @pl.when(j == 0)
    def _():
        lb = _hgrn_lower_bound(lbp_ref[...], layer)
        qs, log_f, kk = _hgrn_gates(q_ref[...], f_ref[...], lb)
        split3(qs.T, qt_ref)
        split3(jnp.exp(log_f).T, ft_ref)
        split3(kk.T, kt_ref)

    seq = lax.broadcasted_iota(jnp.int32, (n_seq, HG_DK), 0)
    for i in range(sb):
        b = j * sb + i
        pick = (seq == b).astype(BF16)
        col = lambda ref: sum(jnp.dot(ref[p], pick, preferred_element_type=F32) for p in range(3))
        s_new = s0_ref[i, 0] * col(ft_ref) + col(kt_ref) * i_ref[pl.ds(b, 1), :]
        s_ref[i, 0] = s_new
        oacc_ref[pl.ds(b, 1), :] = jnp.sum(col(qt_ref) * s_new, axis=0, keepdims=True)

    @pl.when(j == n_j - 1)
    def _():
        o_ref[...] = _hgrn_out_norm(oacc_ref[...], g_ref[...], ng_ref[...]).astype(o_ref.dtype)


def hgrn_step(z, s0, slot, lb_param, norm_g, *, layer, sb):
    n_seq = z.shape[0]
    h, dk = HG_HEADS, HG_DK
    n_j = n_seq // sb
    zspec = lambda off: pl.BlockSpec((n_seq, dk), lambda hh, j: (0, off + hh))
    return pl.pallas_call(
        functools.partial(_hgrn_step_kernel, layer=layer, n_j=n_j, sb=sb),
        grid=(h, n_j),
        in_specs=[zspec(0), zspec(h), zspec(2 * h), zspec(3 * h),
                  pl.BlockSpec((DEPTH, dk), lambda hh, j: (0, hh)),
                  pl.BlockSpec((1, dk), lambda hh, j: (0, hh)),
                  pl.BlockSpec((None, sb, 1, dk, dk), lambda hh, j: (slot, j, hh, 0, 0))],
        out_specs=[pl.BlockSpec((n_seq, dk), lambda hh, j: (0, hh)),
                   pl.BlockSpec((sb, 1, dk, dk), lambda hh, j: (j, hh, 0, 0))],
        out_shape=[jax.ShapeDtypeStruct((n_seq, h * dk), BF16),
                   jax.ShapeDtypeStruct(s0.shape[1:], F32)],
        scratch_shapes=[pltpu.VMEM((3, dk, n_seq), BF16)] * 3 + [pltpu.VMEM((n_seq, dk), F32)],
        compiler_params=_params("parallel", "arbitrary"),
        name="hgrn_step",
    )(z, z, z, z, lb_param, norm_g.reshape(1, h * dk), s0)


def _gelu_tanh(x):
    return x * (0.5 * (1.0 + jnp.tanh(0.7978845608028654 * (x + 0.044715 * (x * x * x)))))


def _rg_decay_and_input(xc, wg_ref, bg_ref, lam_ref):
    xb = xc.astype(BF16)
    gates = []
    for gi in range(2):
        blocks = [jnp.dot(xb[:, n * RG_BS:(n + 1) * RG_BS], wg_ref[gi, n], preferred_element_type=F32)
                  for n in range(RG_BLOCKS)]
        gates.append(jnp.concatenate(blocks, axis=1) + bg_ref[gi:gi + 1, :])
    r = _sigmoid(gates[0])
    i = _sigmoid(gates[1])
    neg_lam = -lam_ref[...]
    softplus = jnp.maximum(neg_lam, 0.0) + jnp.log1p(jnp.exp(-jnp.abs(neg_lam)))
    log_a = -RG_C * r * softplus
    one_minus_a2 = -jnp.tanh(log_a) * (jnp.exp(2.0 * log_a) + 1.0)
    return jnp.exp(log_a), jnp.sqrt(one_minus_a2) * i * xc


def _rg_prompt_kernel(y_ref, x_ref, cw_ref, cb_ref, wg_ref, bg_ref, lam_ref,
                      o_ref, conv_out_ref, h_out_ref, xpad_ref, h_ref, *, n_t, tl):
    t = pl.program_id(1)
    halo = 8

    @pl.when(t == 0)
    def _():
        xpad_ref[0:halo, :] = jnp.zeros((halo, RG_WIDTH), F32)
        h_ref[...] = jnp.zeros_like(h_ref)

    x = x_ref[...]
    xpad_ref[halo:halo + tl, :] = x
    xc = cb_ref[...]
    for j in range(RG_CONV):
        off = halo - (RG_CONV - 1) + j
        xc = xc + xpad_ref[off:off + tl, :] * cw_ref[j:j + 1, :]
    xpad_ref[0:halo, :] = x[tl - halo:tl, :]
    a, u = _rg_decay_and_input(xc, wg_ref, bg_ref, lam_ref)
    row = lax.broadcasted_iota(jnp.int32, (tl, 1), 0) % SUBLANES
    s = 1
    while s < SUBLANES:
        keep = row >= s
        a_prev = jnp.where(keep, pltpu.roll(a, s, 0), 1.0)
        u_prev = jnp.where(keep, pltpu.roll(u, s, 0), 0.0)
        u = a * u_prev + u
        a = a * a_prev
        s *= 2
    carry = h_ref[...]
    tiles = []
    for r in range(0, tl, SUBLANES):
        tiles.append(a[r:r + SUBLANES] * carry + u[r:r + SUBLANES])
        carry = tiles[-1][SUBLANES - 1:SUBLANES]
    h = jnp.concatenate(tiles, axis=0)
    h_ref[...] = carry
    o_ref[...] = (_gelu_tanh(y_ref[...]) * h).astype(o_ref.dtype)

    @pl.when(t == n_t - 1)
    def _():
        conv_out_ref[0] = x[tl - (RG_CONV - 1):tl, :]
        h_out_ref[0] = h[tl - 1:tl, :]


def rglru_prompt(z, conv_w, conv_b, w_gate, b_gate, lam, *, batch, tl):
    m = z.shape[0]
    n_t = m // batch // tl
    w = RG_WIDTH
    const = lambda shape: pl.BlockSpec(shape, lambda b, t: (0,) * len(shape))
    return pl.pallas_call(
        functools.partial(_rg_prompt_kernel, n_t=n_t, tl=tl),
        grid=(batch, n_t),
        in_specs=[pl.BlockSpec((tl, w), lambda b, t: (b * n_t + t, 0)),
                  pl.BlockSpec((tl, w), lambda b, t: (b * n_t + t, 1)),
                  const((RG_CONV, w)), const((1, w)), const((2, RG_BLOCKS, RG_BS, RG_BS)), const((2, w)), const((1, w))],
        out_specs=[pl.BlockSpec((tl, w), lambda b, t: (b * n_t + t, 0)),
                   pl.BlockSpec((1, RG_CONV - 1, w), lambda b, t: (b, 0, 0)),
                   pl.BlockSpec((1, 1, w), lambda b, t: (b, 0, 0))],
        out_shape=[jax.ShapeDtypeStruct((m, w), BF16),
                   jax.ShapeDtypeStruct((batch, RG_CONV - 1, w), F32),
                   jax.ShapeDtypeStruct((batch, 1, w), F32)],
        scratch_shapes=[pltpu.VMEM((tl + 8, w), F32), pltpu.VMEM((1, w), F32)],
        compiler_params=_params("parallel", "arbitrary"),
        name="rglru_prompt",
    )(z, z, conv_w, conv_b.reshape(1, w), w_gate, b_gate, lam.reshape(1, w))


def _rg_step_kernel(y_ref, x_ref, buf_ref, h0_ref, cw_ref, cb_ref, wg_ref, bg_ref, lam_ref,
                    o_ref, conv_out_ref, h_out_ref):
    x = x_ref[...]
    xc = cb_ref[...]
    for j in range(RG_CONV - 1):
        xc = xc + buf_ref[:, j, :] * cw_ref[j:j + 1, :]
    xc = xc + x * cw_ref[RG_CONV - 1:RG_CONV, :]
    a, u = _rg_decay_and_input(xc, wg_ref, bg_ref, lam_ref)
    h = a * h0_ref[...] + u
    o_ref[...] = (_gelu_tanh(y_ref[...]) * h).astype(o_ref.dtype)
    h_out_ref[...] = h
    for j in range(RG_CONV - 2):
        conv_out_ref[:, j, :] = buf_ref[:, j + 1, :]
    conv_out_ref[:, RG_CONV - 2, :] = x


def rglru_step(z, conv_buf, h0, conv_w, conv_b, w_gate, b_gate, lam):
    n_seq = z.shape[0]
    w = RG_WIDTH
    const = lambda shape: pl.BlockSpec(shape, lambda i: (0,) * len(shape))
    return pl.pallas_call(
        _rg_step_kernel,
        grid=(1,),
        in_specs=[pl.BlockSpec((n_seq, w), lambda i: (0, 0)),
                  pl.BlockSpec((n_seq, w), lambda i: (0, 1)),
                  const((n_seq, RG_CONV - 1, w)), const((n_seq, w)),
                  const((RG_CONV, w)), const((1, w)), const((2, RG_BLOCKS, RG_BS, RG_BS)), const((2, w)), const((1, w))],
        out_specs=[const((n_seq, w)), const((n_seq, RG_CONV - 1, w)), const((n_seq, w))],
        out_shape=[jax.ShapeDtypeStruct((n_seq, w), BF16),
                   jax.ShapeDtypeStruct((n_seq, RG_CONV - 1, w), F32),
                   jax.ShapeDtypeStruct((n_seq, w), F32)],
        compiler_params=_params("arbitrary"),
        name="rglru_step",
    )(z, z, conv_buf, h0, conv_w, conv_b.reshape(1, w), w_gate, b_gate, lam.reshape(1, w))


NSA_HEADS = 16
NSA_KVH = 4
NSA_GROUP = 4
NSA_DH = 64
NSA_KVW = NSA_KVH * NSA_DH
NSA_ROW = 2 * NSA_KVW
CMP_BLOCK = 32
CMP_HID = 128
SLC_BLOCK = 64
SLC_TOPK = 16
WINDOW = 512
NSA_QBLOCK = 128
MASK_NEG = -1e30
SEL_BIG = 1e9
SEL_KEYS = 1024


def _masked_softmax(s, mask, exp=jnp.exp):
    s = jnp.where(mask, s, MASK_NEG)
    m = jnp.max(s, axis=-1, keepdims=True)
    e = jnp.where(mask, exp(s - m), 0.0)
    return e * (1.0 / jnp.maximum(jnp.sum(e, axis=-1, keepdims=True), 1e-30))


def _top_blocks(score, k, axis):
    pos = lax.broadcasted_iota(jnp.int32, score.shape, axis)
    n = score.shape[axis]

    def pick(_, carry):
        sc, sel = carry
        mx = jnp.max(sc, axis=axis, keepdims=True)
        idx = jnp.min(jnp.where(sc == mx, pos, n), axis=axis, keepdims=True)
        hit = pos == idx
        return jnp.where(hit, -jnp.inf, sc), jnp.where(hit, 1.0, sel)

    return lax.fori_loop(0, k, pick, (score, jnp.zeros(score.shape, F32)))[1]


def _compress_rows(block_row, w1_ref, w2_ref, n_blk):
    tiles = NSA_KVW // LANES
    low = lax.broadcasted_iota(jnp.int32, (n_blk, LANES), 1) < NSA_DH
    outs = []
    for c in range(2):
        cols = [[] for _ in range(NSA_KVH)]
        for l in range(0, CMP_BLOCK, 2):
            for tl in range(tiles):
                x0, x1 = block_row(c, tl, l), block_row(c, tl, l + 1)
                cols[2 * tl].append(jnp.where(low, x0, pltpu.roll(x1, NSA_DH, 1)))
                cols[2 * tl + 1].append(jnp.where(low, pltpu.roll(x0, NSA_DH, 1), x1))
        xs = jnp.concatenate([jnp.concatenate(cols[h], axis=1) for h in range(NSA_KVH)], axis=0)
        acc = jnp.dot(xs.astype(BF16), w1_ref[c], preferred_element_type=F32)
        out = jnp.dot(_silu(acc).astype(BF16), w2_ref[c], preferred_element_type=F32)
        outs.append(jnp.concatenate([out[h * n_blk:(h + 1) * n_blk] for h in range(NSA_KVH)], axis=1))
    return jnp.concatenate(outs, axis=1)


def _compress_kernel(x0_ref, x1_ref, x2_ref, x3_ref, pos_ref, w1_ref, w2_ref, even_ref, odd_ref, *, n_blk):
    x_refs = (x0_ref, x1_ref, x2_ref, x3_ref)
    half = n_blk // 2
    tiles = NSA_KVW // LANES

    def block_row(c, tl, r):
        ref = x_refs[c * tiles + tl]
        xe = ref[pl.ds(r, half, stride=2 * CMP_BLOCK), :]
        xo = ref[pl.ds(r + CMP_BLOCK, half, stride=2 * CMP_BLOCK), :]
        return jnp.concatenate([xe, xo], axis=0) + pos_ref[c, r:r + 1, tl * LANES:(tl + 1) * LANES]

    full = _compress_rows(block_row, w1_ref, w2_ref, n_blk)
    even_ref[0] = full[:half]
    odd_ref[0] = full[half:]


def nsa_compress(kv, col_block, pos4, w1, w2, *, batch, n_blk):
    m = kv.shape[0]
    n_t = m // batch // (n_blk * CMP_BLOCK)
    half = n_blk // 2
    const = lambda shape: pl.BlockSpec(shape, lambda b, t: (0,) * len(shape))
    out = jax.ShapeDtypeStruct((batch, n_t * half, NSA_ROW), F32)
    tiles = NSA_ROW // LANES
    lane_tile = lambda i: pl.BlockSpec((n_blk * CMP_BLOCK, LANES), lambda b, t: (b * n_t + t, col_block * tiles + i))
    return pl.pallas_call(
        functools.partial(_compress_kernel, n_blk=n_blk),
        grid=(batch, n_t),
        in_specs=[lane_tile(i) for i in range(tiles)]
        + [const((2, CMP_BLOCK, NSA_KVW)), const((2, CMP_BLOCK * NSA_DH, CMP_HID)), const((2, CMP_HID, NSA_DH))],
        out_specs=[pl.BlockSpec((1, half, NSA_ROW), lambda b, t: (b, t, 0))] * 2,
        out_shape=[out, out],
        compiler_params=_params("parallel", "parallel"),
        name="nsa_compress",
    )(kv, kv, kv, kv, pos4, w1, w2)


def nsa_prepare_weights(w_in, pos, w1, w2):
    hq = NSA_HEADS * NSA_DH
    kv_end = hq + 3 * NSA_ROW
    w_qkv = w_in[:, :kv_end].astype(BF16)
    w_kvb = w_in[:, hq + NSA_ROW:kv_end].astype(BF16)
    w_g = w_in[:, kv_end:].reshape(-1, NSA_KVH, 3 * NSA_GROUP)
    w_g = jnp.pad(w_g, ((0, 0), (0, 0), (0, LANES - 3 * NSA_GROUP))).reshape(-1, NSA_KVH * LANES).astype(BF16)
    pos4 = jnp.tile(pos, (1, 1, NSA_KVH))
    w1 = w1.reshape(2, CMP_BLOCK * NSA_DH, CMP_HID)
    w_kv_t = w_in[:, hq:kv_end].T.astype(BF16)
    return w_qkv, w_kvb, w_g, w_kv_t, pos4, w1.astype(BF16), w2.astype(BF16)


def _stack_group_queries(q, par):
    half = lax.broadcasted_iota(jnp.int32, (q.shape[0], LANES), 1) // NSA_DH
    parts = []
    for g in range(NSA_GROUP):
        qg = q[:, g * NSA_DH:(g + 1) * NSA_DH]
        parts.append(jnp.where(half == par, jnp.concatenate([qg, qg], axis=1), 0.0))
    return jnp.concatenate(parts, axis=0).astype(BF16)


def _nsa_prompt_kernel(q_ref, gp_ref, ck_ref, cv_ref, ks_ref, vs_ref, kw_ref, vw_ref, o_ref):
    j = pl.program_id(2)
    qb, grp = NSA_QBLOCK, NSA_GROUP
    rows = grp * qb
    qw = grp * NSA_DH
    pair = range(LANES // NSA_DH)
    t_pos = j * qb + lax.broadcasted_iota(jnp.int32, (qb, 1), 0)

    def grouped_softmax(s, mask):
        n = s.shape[-1]
        return _masked_softmax(s.reshape(grp, qb, n), mask[None], jnp.exp2).reshape(rows, n)

    n_c = ck_ref.shape[1]
    n_s = n_c // 2
    ck = ck_ref[0].astype(BF16)
    cv = cv_ref[0].astype(BF16)
    nn = lax.broadcasted_iota(jnp.int32, (1, n_c), 1)
    c_blk = jnp.where(nn < n_s, 2 * nn, 2 * (nn - n_s) + 1)
    c_vis = (c_blk * CMP_BLOCK + CMP_BLOCK - 1) <= t_pos
    n_w = WINDOW + qb
    w0 = pl.multiple_of(jnp.maximum(j * qb - WINDOW, 0), qb)
    dist = t_pos - (w0 + lax.broadcasted_iota(jnp.int32, (1, n_w), 1))
    w_bias = jnp.where((dist >= 0) & (dist <= WINDOW), 0.0, MASK_NEG)[None]
    k_win, v_win = kw_ref[pl.ds(w0, n_w), :], vw_ref[pl.ds(w0, n_w), :]
    blk = lax.broadcasted_iota(jnp.int32, (qb, n_s), 1)
    cur = t_pos // SLC_BLOCK
    forced = (blk == 0) | (blk == cur) | (blk == cur - 1)

    n_gate = 3 * grp
    gate_spread = (lax.broadcasted_iota(jnp.int32, (LANES, n_gate * LANES), 0)
                   == lax.broadcasted_iota(jnp.int32, (LANES, n_gate * LANES), 1) // LANES).astype(BF16)

    def before_selection(par):
        qp = _stack_group_queries(q_ref[:, par * qw:(par + 1) * qw] * (NSA_DH ** -0.5 * LOG2_E), par)
        p_c = grouped_softmax(_dot_nt(qp, ck), c_vis)
        o_c = jnp.dot(p_c.astype(BF16), cv, preferred_element_type=F32)
        s_w = _dot_nt(qp, k_win).reshape(grp, qb, n_w) + w_bias
        e_w = jnp.exp2(s_w - jnp.max(s_w, axis=-1, keepdims=True)).reshape(rows, n_w)
        own_w = lax.broadcasted_iota(jnp.int32, (n_w, LANES), 1) // NSA_DH == par
        o_w = jnp.dot(e_w.astype(BF16), jnp.where(own_w, v_win, 1.0), preferred_element_type=F32)
        o_w = o_w / jnp.maximum(pltpu.roll(o_w, NSA_DH, 1), 1e-30)
        gates = _sigmoid(gp_ref[:, par * LANES:(par + 1) * LANES])
        hi = gates.astype(BF16)
        r1 = gates - hi.astype(F32)
        mid = r1.astype(BF16)
        lo = (r1 - mid.astype(F32)).astype(BF16)
        spread = sum(jnp.dot(piece, gate_spread, preferred_element_type=F32) for piece in (hi, mid, lo))
        gate = lambda br: jnp.concatenate(
            [spread[:, (3 * g + br) * LANES:(3 * g + br + 1) * LANES] for g in range(grp)], axis=0)
        imp = p_c[0:qb]
        for g in range(1, grp):
            imp = imp + p_c[g * qb:(g + 1) * qb]
        imp = imp[:, :n_s] + imp[:, n_s:]
        score = jnp.where(forced, SEL_BIG, jnp.where(blk <= cur, imp, -SEL_BIG))
        return qp, gate(0) * o_c + gate(2) * o_w, gate(1), score.T

    qps, o_cws, gate_s, scores_t = zip(*[before_selection(par) for par in pair])
    sel_t = _top_blocks(jnp.concatenate(scores_t, axis=1), SLC_TOPK, 0)
    sel_bs = [sel_t[:, par * qb:(par + 1) * qb].T.astype(BF16) for par in pair]

    per = SEL_KEYS // SLC_BLOCK
    e_row = lax.broadcasted_iota(jnp.int32, (n_s, SEL_KEYS), 0)
    e_col = lax.broadcasted_iota(jnp.int32, (n_s, SEL_KEYS), 1) // SLC_BLOCK
    k_off = lax.broadcasted_iota(jnp.int32, (1, SEL_KEYS), 1)

    n_kb = (j * qb + qb - 1) // SEL_KEYS + 1

    def key_rows(kb):
        return pl.ds(pl.multiple_of(kb * SEL_KEYS, SEL_KEYS), SEL_KEYS)

    lane_head = lax.broadcasted_iota(jnp.int32, (SEL_KEYS, LANES), 1) // NSA_DH

    def sel_step(kb, carry):
        expand = (e_row == kb * per + e_col).astype(BF16)
        causal = (kb * SEL_KEYS + k_off) <= t_pos
        k_blk, v_blk = ks_ref[key_rows(kb), :], vs_ref[key_rows(kb), :]
        out = []
        for par in pair:
            m, acc = carry[par]
            chosen = jnp.dot(sel_bs[par], expand, preferred_element_type=F32) > 0.5
            bias = jnp.where(chosen & causal, 0.0, MASK_NEG)[None]
            s = _dot_nt(qps[par], k_blk).reshape(grp, qb, SEL_KEYS) + bias
            m_new = jnp.maximum(m, jnp.max(s, axis=-1, keepdims=True))
            alpha = jnp.exp2(m - m_new)
            e = jnp.exp2(s - m_new)
            v_one = jnp.where(lane_head == par, v_blk, 1.0)
            pv = jnp.dot(e.reshape(rows, SEL_KEYS).astype(BF16), v_one, preferred_element_type=F32)
            out.append((m_new, alpha * acc + pv.reshape(grp, qb, LANES)))
        return tuple(out)

    init = tuple((jnp.full((grp, qb, 1), MASK_NEG, F32), jnp.zeros((grp, qb, LANES), F32)) for _ in pair)
    fin = lax.fori_loop(0, n_kb, sel_step, init)
    heads = []
    for par in pair:
        acc = fin[par][1].reshape(rows, LANES)
        o_s = acc / jnp.maximum(pltpu.roll(acc, NSA_DH, 1), 1e-30)
        o = (o_cws[par] + gate_s[par] * o_s)[:, par * NSA_DH:(par + 1) * NSA_DH]
        heads += [o[g * qb:(g + 1) * qb] for g in range(grp)]
    o_ref[...] = jnp.concatenate(heads, axis=1).astype(o_ref.dtype)


def nsa_prompt_attention(qkv, gates_pre, ckv, kvb, *, batch):
    m = qkv.shape[0]
    l_seq = m // batch
    nq = l_seq // NSA_QBLOCK
    n_c = ckv.shape[1]
    per_tile = LANES // NSA_DH
    n_pairs = NSA_KVH // per_tile
    qw = per_tile * NSA_GROUP * NSA_DH
    kv_spec = lambda part: pl.BlockSpec((l_seq, LANES), lambda b, p, j: (b, part * n_pairs + p))
    return pl.pallas_call(
        _nsa_prompt_kernel,
        grid=(batch, n_pairs, nq),
        in_specs=[pl.BlockSpec((NSA_QBLOCK, qw), lambda b, p, j: (b * nq + j, p)),
                  pl.BlockSpec((NSA_QBLOCK, per_tile * LANES), lambda b, p, j: (b * nq + j, p)),
                  pl.BlockSpec((1, n_c, LANES), lambda b, p, j: (b, 0, p)),
                  pl.BlockSpec((1, n_c, LANES), lambda b, p, j: (b, 0, n_pairs + p)),
                  kv_spec(0), kv_spec(1), kv_spec(2), kv_spec(3)],
        out_specs=pl.BlockSpec((NSA_QBLOCK, qw), lambda b, p, j: (b * nq + j, p)),
        out_shape=jax.ShapeDtypeStruct((m, NSA_HEADS * NSA_DH), BF16),
        compiler_params=_params("parallel", "parallel", "arbitrary"),
        name="nsa_prompt_attention",
    )(qkv, gates_pre, ckv, ckv, kvb, kvb, kvb, kvb)


def _paged_compress_kernel(pt_ref, *refs, n_pages, page):
    del pt_ref
    pages = refs[:n_pages]
    pos_ref, w1_ref, w2_ref, out_ref = refs[n_pages:n_pages + 4]
    rows = refs[n_pages + 4:]
    tiles = NSA_KVW // LANES
    per_page = page // CMP_BLOCK
    group = SUBLANES // per_page
    width = group * page
    dst = lax.broadcasted_iota(jnp.int32, (width, width), 0)
    slot = dst % SUBLANES
    src = (slot // per_page) * page + (slot % per_page) * CMP_BLOCK + dst // SUBLANES
    perm = (lax.broadcasted_iota(jnp.int32, (width, width), 1) == src).astype(BF16)
    for q in range(n_pages // group):
        for i, ref in enumerate(rows):
            c, tl = divmod(i, tiles)
            lanes = slice(tl * LANES, (tl + 1) * LANES)
            xt = jnp.concatenate([pages[q * group + g][0, c, lanes, :] for g in range(group)], axis=1)
            moved = _dot_nt(perm, (xt + pos_ref[c, lanes, :]).astype(BF16))
            ref[:, q * SUBLANES:(q + 1) * SUBLANES, :] = moved.reshape(CMP_BLOCK, SUBLANES, LANES)
    n_blk = n_pages * per_page
    out_ref[0] = _compress_rows(lambda c, tl, r: rows[c * tiles + tl][r], w1_ref, w2_ref, n_blk)


def _page_specs(n_pages, page):
    return [pl.BlockSpec((1, 2, NSA_KVW, page), lambda b, pt, p=p: (pt[b, p], 0, 0, 0)) for p in range(n_pages)]


def nsa_compress_paged(cache_t, page_table, pos, w1, w2):
    n_seq, n_pages = page_table.shape
    page = cache_t.shape[3]
    n_blk = n_pages * page // CMP_BLOCK
    width = SUBLANES // (page // CMP_BLOCK) * page
    pos_t = jnp.tile(pos.transpose(0, 2, 1), (1, NSA_KVH, width // CMP_BLOCK))
    const = lambda shape: pl.BlockSpec(shape, lambda b, pt: (0,) * len(shape))
    return pl.pallas_call(
        functools.partial(_paged_compress_kernel, n_pages=n_pages, page=page),
        grid_spec=pltpu.PrefetchScalarGridSpec(
            num_scalar_prefetch=1,
            grid=(n_seq,),
            in_specs=_page_specs(n_pages, page)
            + [const((2, NSA_KVW, width)), const((2, CMP_BLOCK * NSA_DH, CMP_HID)), const((2, CMP_HID, NSA_DH))],
            out_specs=pl.BlockSpec((1, n_blk, NSA_ROW), lambda b, pt: (b, 0, 0)),
            scratch_shapes=[pltpu.VMEM((CMP_BLOCK, n_blk, LANES), F32)] * (NSA_ROW // LANES)),
        out_shape=jax.ShapeDtypeStruct((n_seq, n_blk, NSA_ROW), F32),
        compiler_params=_params("arbitrary"),
        name="nsa_compress_paged",
    )(page_table, *([cache_t] * n_pages), pos_t, w1, w2)


def _softmax_with_new_key(s, ok, s_new, new_ok):
    s = jnp.where(ok, s, MASK_NEG)
    s_new = jnp.where(new_ok, s_new, MASK_NEG)
    m = jnp.maximum(jnp.max(s, axis=-1, keepdims=True), s_new)
    e = jnp.where(ok, jnp.exp(s - m), 0.0)
    e_new = jnp.where(new_ok, jnp.exp(s_new - m), 0.0)
    total = jnp.sum(e, axis=-1, keepdims=True) + e_new
    return e, e_new, 1.0 / jnp.maximum(total, 1e-30)


def _nsa_step_kernel(pt_ref, *refs, n_pages, page, past):
    del pt_ref
    pages = refs[:n_pages]
    (q_ref, gp_ref, cke_ref, cko_ref, kvs_new_ref, swa_ref, kvw_new_ref,
     o_ref, swa_out_ref, kt_ref, vt_ref) = refs[n_pages:]
    nh, kw = NSA_HEADS, NSA_KVW
    t = past
    own = (lax.broadcasted_iota(jnp.int32, (nh, kw), 1) // NSA_DH
           == lax.broadcasted_iota(jnp.int32, (nh, kw), 0) // NSA_GROUP)
    q = q_ref[0] * (NSA_DH ** -0.5)
    qp = jnp.where(own, jnp.concatenate([q] * NSA_KVH, axis=1), 0.0)
    qpb = qp.astype(BF16)

    half = cke_ref.shape[1]
    n_c = 2 * half
    ce, co = cke_ref[0], cko_ref[0]
    ck = jnp.concatenate([ce[:, :kw], co[:, :kw]], axis=0).astype(BF16)
    cv = jnp.concatenate([ce[:, kw:], co[:, kw:]], axis=0).astype(BF16)
    nn = lax.broadcasted_iota(jnp.int32, (1, n_c), 1)
    c_blk = jnp.where(nn < half, 2 * nn, 2 * (nn - half) + 1)
    p_c = _masked_softmax(_dot_nt(qpb, ck), (c_blk * CMP_BLOCK + CMP_BLOCK - 1) <= t)
    o_c = jnp.dot(p_c.astype(BF16), cv, preferred_element_type=F32)
    same_group = (lax.broadcasted_iota(jnp.int32, (nh, nh), 0) // NSA_GROUP
                  == lax.broadcasted_iota(jnp.int32, (nh, nh), 1) // NSA_GROUP).astype(F32)
    imp = lax.dot_general(p_c, same_group, (((0,), (0,)), ((), ())), preferred_element_type=F32, precision=HIGHEST)
    imp = imp[:half] + imp[half:]
    imp = jnp.concatenate([imp, jnp.zeros((LANES - half, nh), F32)], axis=0)
    blk = lax.broadcasted_iota(jnp.int32, (LANES, nh), 0)
    cur = t // SLC_BLOCK
    forced = (blk == 0) | (blk == cur) | (blk == cur - 1)
    sel_t = _top_blocks(jnp.where(forced, SEL_BIG, jnp.where(blk <= cur, imp, -SEL_BIG)), SLC_TOPK, 0)

    for p in range(n_pages):
        kt_ref[:, p * page:(p + 1) * page] = pages[p][0, 0].astype(BF16)
        vt_ref[:, p * page:(p + 1) * page] = pages[p][0, 1].astype(BF16)
    n_keys = n_pages * page
    pos = lax.broadcasted_iota(jnp.int32, (LANES, n_keys + LANES), 1)
    expand = (lax.broadcasted_iota(jnp.int32, (LANES, n_keys + LANES), 0)
              == jnp.minimum(pos, t) // SLC_BLOCK).astype(BF16)
    chosen = _dot_tn(sel_t.astype(BF16), expand) > 0.5
    ok = chosen[:, :n_keys] & (lax.broadcasted_iota(jnp.int32, (1, n_keys), 1) <= t)
    kv_new = kvs_new_ref[0]
    s_new = jnp.sum(qp * kv_new[:, :kw], axis=-1, keepdims=True)
    scores = jnp.dot(qpb, kt_ref[...], preferred_element_type=F32)
    e, e_new, inv = _softmax_with_new_key(scores, ok, s_new, chosen[:, n_keys:n_keys + 1])
    o_s = (_dot_nt(e.astype(BF16), vt_ref[...]) + e_new * kv_new[:, kw:]) * inv

    n_w = swa_ref.shape[3]
    dist = t - (t - n_w + lax.broadcasted_iota(jnp.int32, (1, n_w), 1))
    w_ok = (dist >= 0) & (dist <= WINDOW) & (t - dist >= 0)
    kvw_new = kvw_new_ref[0]
    s_wn = jnp.sum(qp * kvw_new[:, :kw], axis=-1, keepdims=True)
    new_ok = lax.broadcasted_iota(jnp.int32, (nh, 1), 0) >= 0
    scores = jnp.dot(qpb, swa_ref[0, 0].astype(BF16), preferred_element_type=F32)
    e, e_new, inv = _softmax_with_new_key(scores, w_ok, s_wn, new_ok)
    o_w = (_dot_nt(e.astype(BF16), swa_ref[0, 1].astype(BF16)) + e_new * kvw_new[:, kw:]) * inv

    g = _sigmoid(gp_ref[0])
    o = jnp.where(own, g[:, 0:1] * o_c + g[:, 1:2] * o_s + g[:, 2:3] * o_w, 0.0)
    o_h = o[:, :NSA_DH]
    for h in range(1, NSA_KVH):
        o_h = o_h + o[:, h * NSA_DH:(h + 1) * NSA_DH]
    o_ref[0] = o_h
    last = lax.broadcasted_iota(jnp.int32, (kw, n_w), 1) == n_w - 1
    diag = lax.broadcasted_iota(jnp.int32, (kw, kw), 0) == lax.broadcasted_iota(jnp.int32, (kw, kw), 1)
    for c in range(2):
        column = jnp.sum(jnp.where(diag, kvw_new[:, c * kw:(c + 1) * kw], 0.0), axis=-1, keepdims=True)
        swa_out_ref[0, c] = jnp.where(last, column, pltpu.roll(swa_ref[0, c], n_w - 1, 1))


def nsa_step_attention(q3, gates3, cke, cko, cache_slc_t, page_table, kvs_new, swa_t, kvw_new, *, past):
    n_seq, n_pages = page_table.shape
    page = cache_slc_t.shape[3]
    seq = lambda a: pl.BlockSpec((1,) + a.shape[1:], lambda b, pt: (b,) + (0,) * (a.ndim - 1))
    return pl.pallas_call(
        functools.partial(_nsa_step_kernel, n_pages=n_pages, page=page, past=past),
        grid_spec=pltpu.PrefetchScalarGridSpec(
            num_scalar_prefetch=1,
            grid=(n_seq,),
            in_specs=_page_specs(n_pages, page) + [seq(a) for a in (q3, gates3, cke, cko, kvs_new, swa_t, kvw_new)],
            out_specs=[seq(q3), seq(swa_t)],
            scratch_shapes=[pltpu.VMEM((NSA_KVW, n_pages * page), BF16)] * 2),
        out_shape=[jax.ShapeDtypeStruct(q3.shape, F32), jax.ShapeDtypeStruct(swa_t.shape, F32)],
        compiler_params=_params("arbitrary"),
        name="nsa_step_attention",
    )(page_table, *([cache_slc_t] * n_pages), q3, gates3, cke, cko, kvs_new, swa_t, kvw_new)


PROMPT_TM = 1024
FFN_TM = 512
HG_TL = 512
RG_TL = 256
FFN_TF = 1408
CMP_TILE_BLOCKS = 64
MOE_TB = 896
MOE_TS = 256


def _ffn_layer(x, w_gu, w_down, g, b, *, tm):
    return ffn_residual_ln(x, w_gu.astype(BF16), w_down.astype(BF16), g, b, tm=tm, tf=FFN_TF)


def _moe_layer(x, w_router, w_gu_t, w_down_t, slot, g, b):
    m = x.shape[0]
    tb = min(MOE_TB, m)
    gates, slots = moe_route(x, w_router.T, tb=tb)
    n_e = gates.shape[0]
    counts = (jnp.max(slots.reshape(n_e, -1, tb), axis=-1) + 1).T.reshape(-1)
    return moe_residual_ln(x, gates, slots, counts, w_gu_t, w_down_t, slot, g, b,
                           tb=tb, ts=min(MOE_TS, tb), tf=FFN_TF)


def kernel(x_prompt, x_sample, state_hgrn, state_rglru_conv, state_rglru_h, cache_nsa_cmp, cache_nsa_slc, cache_nsa_swa, page_table, ln_g, ln_b, hgrn_w_in, hgrn_lb, hgrn_norm_g, hgrn_w_out, rg_w_in, rg_conv_w, rg_conv_b, rg_w_gate, rg_b_gate, rg_lambda, rg_w_out, nsa_w_in, nsa_cmp_pos, nsa_cmp_w1, nsa_cmp_w2, nsa_w_out, ffn_w_gu, ffn_w_down, moe_router, moe_w_gu, moe_w_down):
    bp, l_seq, d = x_prompt.shape
    n_seq = x_sample.shape[0]
    past = page_table.shape[1] * cache_nsa_cmp.shape[2]
    xp = x_prompt.reshape(bp * l_seq, d)
    xs = x_sample.reshape(n_seq, d)
    tm_p, tm_s = PROMPT_TM, n_seq
    kv_shape = (2, NSA_KVH, NSA_DH)
    moe_w_gu_t = jnp.swapaxes(moe_w_gu, 2, 3).astype(BF16)
    moe_w_down_t = jnp.swapaxes(moe_w_down, 2, 3).astype(BF16)
    hg_p, hg_s, rc_p, rc_s, rh_p, rh_s = [], [], [], [], [], []
    cm_p, cm_s, sl_p, sl_s, sw_p, sw_s = [], [], [], [], [], []
    for layer in range(DEPTH):
        kind, slot = layer % 3, layer // 3
        g0, b0 = ln_g[layer, 0], ln_b[layer, 0]
        if kind == 0:
            w_in, w_out = hgrn_w_in[slot].astype(BF16), hgrn_w_out[slot].astype(BF16)
            op, st_p = hgrn_prompt(matmul(xp, w_in, tm=tm_p), hgrn_lb, hgrn_norm_g[slot],
                                   layer=layer, batch=bp, tl=HG_TL)
            os_, st_s = hgrn_step(matmul(xs, w_in, tm=tm_s), state_hgrn, slot, hgrn_lb, hgrn_norm_g[slot],
                                  layer=layer, sb=8)
            hg_p.append(st_p)
            hg_s.append(st_s)
        elif kind == 1:
            w_in, w_out = rg_w_in[slot].astype(BF16), rg_w_out[slot].astype(BF16)
            rg_w = (rg_conv_w[slot], rg_conv_b[slot], rg_w_gate[slot].astype(BF16), rg_b_gate[slot], rg_lambda[slot])
            op, cb_p, h_p = rglru_prompt(matmul(xp, w_in, tm=tm_p), *rg_w, batch=bp, tl=RG_TL)
            os_, cb_s, h_s = rglru_step(matmul(xs, w_in, tm=tm_s), state_rglru_conv[slot], state_rglru_h[slot], *rg_w)
            rc_p.append(cb_p)
            rc_s.append(cb_s)
            rh_p.append(h_p[:, 0])
            rh_s.append(h_s)
        else:
            w_qkv, w_kvb, w_g, w_kv_t, pos4, w1, w2 = nsa_prepare_weights(
                nsa_w_in[slot], nsa_cmp_pos[slot], nsa_cmp_w1[slot], nsa_cmp_w2[slot])
            w_out = nsa_w_out[slot].astype(BF16)
            hq = NSA_HEADS * NSA_DH
            qkv = matmul(xp, w_qkv[:, :hq + NSA_ROW], tm=tm_p)
            kvb = matmul(xp, w_kvb, tm=tm_p, out_dtype=BF16)
            gates_pre = matmul(xp, w_g, tm=tm_p)
            cke, cko = nsa_compress(qkv, hq // NSA_ROW, pos4, w1, w2, batch=bp, n_blk=CMP_TILE_BLOCKS)
            op = nsa_prompt_attention(qkv, gates_pre, jnp.concatenate([cke, cko], axis=1), kvb, batch=bp)
            as_rows = lambda a: a.reshape(bp, *kv_shape, -1).transpose(0, 4, 1, 2, 3)
            kv_c, kv_s, kv_w = matmul_t(xp, w_kv_t, batch=bp, n_out=3, tm=tm_p)
            cm_p.append(as_rows(kv_c))
            sl_p.append(as_rows(kv_s))
            sw_p.append(as_rows(kv_w[:, :, l_seq - min(WINDOW, l_seq):]))
            qkv = matmul(xs, w_qkv, tm=tm_s)
            gates_pre = matmul(xs, w_g, tm=tm_s)
            rows_last = lambda c: c.transpose(0, 2, 3, 4, 1).reshape(c.shape[0], 2, NSA_KVW, c.shape[1])
            ckv = nsa_compress_paged(rows_last(cache_nsa_cmp[slot]), page_table, nsa_cmp_pos[slot], w1, w2)
            cke, cko = ckv[:, 0::2], ckv[:, 1::2]
            gates3 = gates_pre.reshape(n_seq, NSA_KVH, LANES)[:, :, :3 * NSA_GROUP].reshape(n_seq, NSA_HEADS, 3)
            kv = qkv[:, hq:].reshape(n_seq, 3, 1, NSA_ROW)
            o3, swa_new = nsa_step_attention(qkv[:, :hq].reshape(n_seq, NSA_HEADS, NSA_DH), gates3, cke, cko,
                                             rows_last(cache_nsa_slc[slot]), page_table, kv[:, 1],
                                             rows_last(cache_nsa_swa[slot]), kv[:, 2], past=past)
            os_ = o3.reshape(n_seq, hq)
            cm_s.append(kv[:, 0].reshape(n_seq, 1, *kv_shape))
            sl_s.append(kv[:, 1].reshape(n_seq, 1, *kv_shape))
            sw_s.append(swa_new.reshape(n_seq, *kv_shape, -1).transpose(0, 4, 1, 2, 3))
        xp = matmul_residual_ln(op, w_out, xp, g0, b0, tm=tm_p)
        xs = matmul_residual_ln(os_, w_out, xs, g0, b0, tm=tm_s)
        g1, b1 = ln_g[layer, 1], ln_b[layer, 1]
        if layer % 2 == 0:
            fw = (ffn_w_gu[layer // 2], ffn_w_down[layer // 2])
            xp = _ffn_layer(xp, *fw, g1, b1, tm=FFN_TM)
            xs = _ffn_layer(xs, *fw, g1, b1, tm=tm_s)
        else:
            fw = (moe_router[layer // 2], moe_w_gu_t, moe_w_down_t, layer // 2)
            xp = _moe_layer(xp, *fw, g1, b1)
            xs = _moe_layer(xs, *fw, g1, b1)
    return (xp.reshape(bp, l_seq, d), xs.reshape(n_seq, 1, d),
            jnp.stack(hg_p), jnp.stack(hg_s), jnp.stack(rc_p), jnp.stack(rc_s), jnp.stack(rh_p), jnp.stack(rh_s),
            jnp.stack(cm_p), jnp.stack(cm_s), jnp.stack(sl_p), jnp.stack(sl_s), jnp.stack(sw_p), jnp.stack(sw_s))
```

```python
import functools

import jax
import jax.numpy as jnp
from jax import lax
from jax.experimental import pallas as pl
from jax.experimental.pallas import tpu as pltpu

F32 = jnp.float32
BF16 = jnp.bfloat16
HIGHEST = lax.Precision.HIGHEST

D_MODEL = 1024
DEPTH = 4
ALPHA = (2 * DEPTH) ** 0.25
LN_EPS = 1e-5
RMS_EPS = 1e-6
HG_HEADS = 8
HG_DK = 128
HG_CHUNK = 64
HG_SUB = 16
HG_HEADS_PER_STEP = 8
LB_FLOOR = 1e-30
RG_WIDTH = 1280
RG_BLOCKS = 10
RG_BS = 128
RG_CONV = 4
RG_C = 8.0
D_FF = 2816
LOG2_E = 1.4426950408889634

SUBLANES = 8
LANES = 128
VMEM_LIMIT_BYTES = 56 * 1024 * 1024


def _params(*sem):
    return pltpu.CompilerParams(dimension_semantics=sem, vmem_limit_bytes=VMEM_LIMIT_BYTES)


def _layer_norm(v, g, b):
    mu = jnp.mean(v, axis=-1, keepdims=True)
    vc = v - mu
    var = jnp.mean(vc * vc, axis=-1, keepdims=True)
    return vc * lax.rsqrt(var + LN_EPS) * g + b


def _sigmoid(x):
    return 1.0 / (1.0 + jnp.exp(-x))


def _silu(x):
    return x * _sigmoid(x)


def _mm_kernel(x_ref, w_ref, o_ref, xb_ref):
    @pl.when(pl.program_id(1) == 0)
    def _():
        xb_ref[...] = x_ref[...].astype(BF16)

    o_ref[...] = jnp.dot(xb_ref[...], w_ref[...], preferred_element_type=F32).astype(o_ref.dtype)


MAX_COL_TILE = 1280


def _col_tile(n):
    return max(t for t in range(LANES, min(n, MAX_COL_TILE) + 1, LANES) if n % t == 0)


def matmul(x, w, *, tm, out_dtype=F32):
    m, k = x.shape
    n = w.shape[1]
    tn = _col_tile(n)
    return pl.pallas_call(
        _mm_kernel,
        grid=(m // tm, n // tn),
        in_specs=[pl.BlockSpec((tm, k), lambda i, j: (i, 0)),
                  pl.BlockSpec((k, tn), lambda i, j: (0, j))],
        out_specs=pl.BlockSpec((tm, tn), lambda i, j: (i, j)),
        out_shape=jax.ShapeDtypeStruct((m, n), out_dtype),
        scratch_shapes=[pltpu.VMEM((tm, k), BF16)],
        compiler_params=_params("parallel", "arbitrary"),
        name="matmul",
    )(x, w)


def _mm_t_kernel(x_ref, wt_ref, *o_refs):
    xb = x_ref[...].astype(BF16)
    n = o_refs[0].shape[1]
    for r, o_ref in enumerate(o_refs):
        o_ref[0] = _dot_nt(wt_ref[r * n:(r + 1) * n, :], xb)


def matmul_t(x, w_t, *, batch, n_out, tm):
    m, k = x.shape
    n = w_t.shape[0] // n_out
    l_seq = m // batch
    n_t = l_seq // tm
    return pl.pallas_call(
        _mm_t_kernel,
        grid=(batch, n_t),
        in_specs=[pl.BlockSpec((tm, k), lambda b, i: (b * n_t + i, 0)),
                  pl.BlockSpec((n_out * n, k), lambda b, i: (0, 0))],
        out_specs=[pl.BlockSpec((1, n, tm), lambda b, i: (b, 0, i))] * n_out,
        out_shape=[jax.ShapeDtypeStruct((batch, n, l_seq), F32)] * n_out,
        compiler_params=_params("parallel", "parallel"),
        name="matmul_t",
    )(x, w_t)


def _mm_res_ln_kernel(a_ref, w_ref, r_ref, g_ref, b_ref, o_ref):
    m = jnp.dot(a_ref[...].astype(BF16), w_ref[...], preferred_element_type=F32)
    o_ref[...] = _layer_norm(ALPHA * r_ref[...] + m, g_ref[...], b_ref[...])


def matmul_residual_ln(a, w, res, g, b, *, tm):
    m, k = a.shape
    d = w.shape[1]
    return pl.pallas_call(
        _mm_res_ln_kernel,
        grid=(m // tm,),
        in_specs=[pl.BlockSpec((tm, k), lambda i: (i, 0)),
                  pl.BlockSpec((k, d), lambda i: (0, 0)),
                  pl.BlockSpec((tm, d), lambda i: (i, 0)),
                  pl.BlockSpec((1, d), lambda i: (0, 0)),
                  pl.BlockSpec((1, d), lambda i: (0, 0))],
        out_specs=pl.BlockSpec((tm, d), lambda i: (i, 0)),
        out_shape=jax.ShapeDtypeStruct((m, d), F32),
        compiler_params=_params("parallel"),
        name="matmul_residual_ln",
    )(a, w, res, g.reshape(1, d), b.reshape(1, d))


def _ffn_kernel(x_ref, wg_ref, wu_ref, wd_ref, g_ref, b_ref, o_ref, xb_ref, acc_ref, *, n_f):
    f = pl.program_id(1)

    @pl.when(f == 0)
    def _():
        xb_ref[...] = x_ref[...].astype(BF16)

    xb = xb_ref[...]
    h = jnp.dot(xb, wg_ref[...], preferred_element_type=F32)
    u = jnp.dot(xb, wu_ref[...], preferred_element_type=F32)
    part = jnp.dot((_silu(h) * u).astype(BF16), wd_ref[...], preferred_element_type=F32)

    @pl.when(f == 0)
    def _():
        acc_ref[...] = part

    @pl.when(f > 0)
    def _():
        acc_ref[...] += part

    @pl.when(f == n_f - 1)
    def _():
        o_ref[...] = _layer_norm(ALPHA * x_ref[...] + acc_ref[...], g_ref[...], b_ref[...])


def ffn_residual_ln(x, w_gu, w_down, g, b, *, tm, tf):
    m, d = x.shape
    ff = w_down.shape[0]
    n_f = ff // tf
    return pl.pallas_call(
        functools.partial(_ffn_kernel, n_f=n_f),
        grid=(m // tm, n_f),
        in_specs=[pl.BlockSpec((tm, d), lambda i, f: (i, 0)),
                  pl.BlockSpec((d, tf), lambda i, f: (0, f)),
                  pl.BlockSpec((d, tf), lambda i, f: (0, n_f + f)),
                  pl.BlockSpec((tf, d), lambda i, f: (f, 0)),
                  pl.BlockSpec((1, d), lambda i, f: (0, 0)),
                  pl.BlockSpec((1, d), lambda i, f: (0, 0))],
        out_specs=pl.BlockSpec((tm, d), lambda i, f: (i, 0)),
        out_shape=jax.ShapeDtypeStruct((m, d), F32),
        scratch_shapes=[pltpu.VMEM((tm, d), BF16), pltpu.VMEM((tm, d), F32)],
        compiler_params=_params("parallel", "arbitrary"),
        name="ffn_residual_ln",
    )(x, w_gu, w_gu, w_down, g.reshape(1, d), b.reshape(1, d))


def _router_kernel(x_ref, wt_ref, gate_ref, pos_ref, *, n_tokens):
    logits = lax.dot_general(wt_ref[...], x_ref[...], (((1,), (1,)), ((), ())),
                             preferred_element_type=F32, precision=HIGHEST)
    n_e, tm = logits.shape
    expert = lax.broadcasted_iota(jnp.int32, logits.shape, 0)
    m1 = jnp.max(logits, axis=0, keepdims=True)
    i1 = jnp.min(jnp.where(logits == m1, expert, n_e), axis=0, keepdims=True)
    rest = jnp.where(expert == i1, -jnp.inf, logits)
    m2 = jnp.max(rest, axis=0, keepdims=True)
    i2 = jnp.min(jnp.where(rest == m2, expert, n_e), axis=0, keepdims=True)
    e2 = jnp.exp(m2 - m1)
    denom = 1.0 + e2
    first = expert == i1
    in_range = pl.program_id(0) * tm + lax.broadcasted_iota(jnp.int32, logits.shape, 1) < n_tokens
    routed = (first | (expert == i2)) & in_range
    gate_ref[...] = jnp.where(first & in_range, 1.0 / denom, jnp.where(routed, e2 / denom, 0.0))
    upper = (lax.broadcasted_iota(jnp.int32, (tm, tm), 0) <= lax.broadcasted_iota(jnp.int32, (tm, tm), 1)).astype(BF16)
    upto = jnp.dot(routed.astype(BF16), upper, preferred_element_type=F32)
    pos_ref[...] = jnp.where(routed, upto - 1.0, -1.0).astype(jnp.int32)


def moe_route(x, w_router_t, *, tb):
    m, d = x.shape
    n_e = w_router_t.shape[0]
    n_blk = pl.cdiv(m, tb)
    return pl.pallas_call(
        functools.partial(_router_kernel, n_tokens=m),
        grid=(n_blk,),
        in_specs=[pl.BlockSpec((tb, d), lambda i: (i, 0)),
                  pl.BlockSpec((n_e, d), lambda i: (0, 0))],
        out_specs=[pl.BlockSpec((n_e, tb), lambda i: (0, i))] * 2,
        out_shape=[jax.ShapeDtypeStruct((n_e, n_blk * tb), F32), jax.ShapeDtypeStruct((n_e, n_blk * tb), jnp.int32)],
        compiler_params=_params("parallel"),
        name="moe_route",
    )(x, w_router_t)


def _moe_kernel(cnt_ref, x_ref, gate_ref, slot_ref, wg_ref, wu_ref, wd_ref, g_ref, b_ref, o_ref,
                xt_ref, xc_ref, acc_ref, yt_ref, *, n_e, n_f, ts, n_tokens):
    i, e, f = pl.program_id(0), pl.program_id(1), pl.program_id(2)
    tb = x_ref.shape[0]
    n_sub = (cnt_ref[i * n_e + e] + ts - 1) // ts
    row = lax.broadcasted_iota(jnp.int32, (ts, tb), 0)

    def one_hot(sub):
        return (slot_ref[pl.ds(e, 1), :] == row + sub * ts).astype(BF16)

    @pl.when((e == 0) & (f == 0))
    def _():
        in_range = i * tb + lax.broadcasted_iota(jnp.int32, (tb, 1), 0) < n_tokens
        xt_ref[...] = jnp.where(in_range, x_ref[...], 0.0).T.astype(BF16)
        yt_ref[...] = jnp.zeros_like(yt_ref)

    tf, d = wg_ref.shape[1], wd_ref.shape[1]
    halves = lambda n: (slice(0, n // 2), slice(n // 2, n))

    @pl.when(f == 0)
    def _():
        def gather(sub, c):
            pick = one_hot(sub)
            for rows in halves(d):
                xc_ref[sub, rows, :] = _dot_nt(xt_ref[rows, :], pick).astype(BF16)
            return c

        lax.fori_loop(0, n_sub, gather, 0)

    def expert_ffn(sub, c):
        xc = xc_ref[sub]
        act = []
        for rows in halves(tf):
            h = jnp.dot(wg_ref[0, rows, :], xc, preferred_element_type=F32)
            u = jnp.dot(wu_ref[0, rows, :], xc, preferred_element_type=F32)
            act.append((_silu(h) * u).astype(BF16))
        act = jnp.concatenate(act, axis=0)
        part = jnp.concatenate([jnp.dot(wd_ref[0, rows, :], act, preferred_element_type=F32) for rows in halves(d)], axis=0)

        @pl.when(f == 0)
        def _():
            acc_ref[sub] = part

        @pl.when(f > 0)
        def _():
            acc_ref[sub] += part

        return c

    lax.fori_loop(0, n_sub, expert_ffn, 0)

    @pl.when(f == n_f - 1)
    def _():
        def scatter(sub, c):
            pick = one_hot(sub)
            gate = gate_ref[pl.ds(e, 1), :]
            for rows in halves(d):
                back = jnp.dot(acc_ref[sub, rows, :].astype(BF16), pick, preferred_element_type=F32)
                yt_ref[rows, :] += gate * back
            return c

        lax.fori_loop(0, n_sub, scatter, 0)

    @pl.when((e == n_e - 1) & (f == n_f - 1))
    def _():
        o_ref[...] = _layer_norm(ALPHA * x_ref[...] + yt_ref[...].T, g_ref[...], b_ref[...])


def moe_residual_ln(x, gates, slots, counts, w_gu_t, w_down_t, slot, g, b, *, tb, ts, tf):
    m, d = x.shape
    _, n_e, ff2, _ = w_gu_t.shape
    n_f = ff2 // 2 // tf
    n_sub = pl.cdiv(tb, ts)
    return pl.pallas_call(
        functools.partial(_moe_kernel, n_e=n_e, n_f=n_f, ts=ts, n_tokens=m),
        grid_spec=pltpu.PrefetchScalarGridSpec(
            num_scalar_prefetch=1,
            grid=(pl.cdiv(m, tb), n_e, n_f),
            in_specs=[pl.BlockSpec((tb, d), lambda i, e, f, c: (i, 0)),
                      pl.BlockSpec((n_e, tb), lambda i, e, f, c: (0, i)),
                      pl.BlockSpec((n_e, tb), lambda i, e, f, c: (0, i)),
                      pl.BlockSpec((None, 1, tf, d), lambda i, e, f, c: (slot, e, f, 0)),
                      pl.BlockSpec((None, 1, tf, d), lambda i, e, f, c: (slot, e, n_f + f, 0)),
                      pl.BlockSpec((None, 1, d, tf), lambda i, e, f, c: (slot, e, 0, f)),
                      pl.BlockSpec((1, d), lambda i, e, f, c: (0, 0)),
                      pl.BlockSpec((1, d), lambda i, e, f, c: (0, 0))],
            out_specs=pl.BlockSpec((tb, d), lambda i, e, f, c: (i, 0)),
            scratch_shapes=[pltpu.VMEM((d, tb), BF16), pltpu.VMEM((n_sub, d, ts), BF16),
                            pltpu.VMEM((n_sub, d, ts), F32), pltpu.VMEM((d, tb), F32)]),
        out_shape=jax.ShapeDtypeStruct((m, d), F32),
        compiler_params=_params("parallel", "arbitrary", "arbitrary"),
        name="moe_residual_ln",
    )(counts, x, gates, slots, w_gu_t, w_gu_t, w_down_t, g.reshape(1, d), b.reshape(1, d))


def _hgrn_lower_bound(lbp, layer):
    m = jnp.max(lbp, axis=0, keepdims=True)
    e = jnp.exp(lbp - m)
    p = e / jnp.sum(e, axis=0, keepdims=True)
    c = p[0:1]
    for i in range(1, layer + 1):
        c = c + p[i:i + 1]
    return c - p[0:1]


def _hgrn_gates(q, fx, lb):
    log_lb = jnp.log(jnp.maximum(lb, LB_FLOOR))
    log1m = jnp.log1p(-lb)
    log_sig = -(jnp.maximum(-fx, 0.0) + jnp.log1p(jnp.exp(-jnp.abs(fx))))
    b2 = log1m + log_sig
    log_f = jnp.maximum(log_lb, b2) + jnp.log1p(jnp.exp(-jnp.abs(log_lb - b2)))
    kk = (1.0 - lb) * _sigmoid(-fx) - (jnp.maximum(lb, LB_FLOOR) - lb)
    return _silu(q), log_f, kk


def _hgrn_out_norm(o, g, ng):
    o = o * lax.rsqrt(jnp.mean(o * o, axis=-1, keepdims=True) + RMS_EPS)
    return o * ng * _silu(g)


def _dot_nt(a, b):
    return lax.dot_general(a, b, (((1,), (1,)), ((), ())), preferred_element_type=F32)


def _dot_tn(a, b):
    return lax.dot_general(a, b, (((0,), (0,)), ((), ())), preferred_element_type=F32)


def _gla_kernel(q_ref, f_ref, i_ref, g_ref, lbp_ref, ng_ref, o_ref, s_out_ref, st_ref, *, layer, n_t, tl, hp):
    t = pl.program_id(2)
    c_len, sub = HG_CHUNK, HG_SUB
    n_sub = c_len // sub
    n_stack = sub * (n_sub * (n_sub - 1) // 2)

    @pl.when(t == 0)
    def _():
        st_ref[...] = jnp.zeros_like(st_ref)

    lb = _hgrn_lower_bound(lbp_ref[...], layer)
    ng = ng_ref[...]
    row = lax.broadcasted_iota(jnp.int32, (c_len, c_len), 0)
    col = lax.broadcasted_iota(jnp.int32, (c_len, c_len), 1)
    tri = jnp.where(col <= row, LOG2_E, 0.0)
    srow = lax.broadcasted_iota(jnp.int32, (c_len, n_stack), 0) // sub
    scol = lax.broadcasted_iota(jnp.int32, (c_len, n_stack), 1)
    part = jnp.zeros((c_len, n_stack), jnp.int32)
    for p in range(1, n_sub):
        part = part + (scol >= sub * (p * (p - 1) // 2)).astype(jnp.int32)
    inter_mask = srow == part

    def one_head(hh, r):
        lanes = slice(hh * HG_DK, (hh + 1) * HG_DK)
        qs, log_f, kk = _hgrn_gates(q_ref[r, lanes], f_ref[r, lanes], lb[:, lanes])
        v = i_ref[r, lanes]
        cum = jnp.dot(tri, log_f, preferred_element_type=F32, precision=HIGHEST)
        last = cum[c_len - 1:c_len, :]
        st = st_ref[hh]
        o = _dot_nt((qs * jnp.exp2(cum)).astype(BF16), st.astype(BF16))
        bounds = [cum[sub * a - 1:sub * a, :] for a in range(1, n_sub)]
        cq = jnp.concatenate([cum[0:sub]] + [jnp.broadcast_to(b, (sub, HG_DK)) for b in bounds], axis=0)
        q_rel = (qs * jnp.exp2(cum - cq)).astype(BF16)
        k_st = jnp.concatenate([kk[0:sub * a] * jnp.exp2(bounds[a - 1] - cum[0:sub * a]) for a in range(1, n_sub)], axis=0)
        v_st = jnp.concatenate([v[0:sub * a] for a in range(1, n_sub)], axis=0)
        att = jnp.where(inter_mask, _dot_nt(q_rel, k_st.astype(BF16)), 0.0)
        o = o + jnp.dot(att.astype(BF16), v_st.astype(BF16), preferred_element_type=F32)
        parts = []
        for a in range(n_sub):
            base = sub * a
            rest = o[base:base + sub]
            for lo in range(0, sub, SUBLANES):
                tail = slice(base + lo, base + sub)
                qa, ca = qs[tail], cum[tail]
                row_t = lo + lax.broadcasted_iota(jnp.int32, (sub - lo, 1), 0)
                for s in range(lo, lo + SUBLANES):
                    src = slice(base + s, base + s + 1)
                    w = jnp.sum(qa * (kk[src] * jnp.exp2(ca - cum[src])), axis=-1, keepdims=True)
                    rest = rest + jnp.where(row_t >= s, w, 0.0) * v[src]
                parts.append(rest[:SUBLANES])
                rest = rest[SUBLANES:]
        o = jnp.concatenate(parts, axis=0)
        o_ref[r, lanes] = _hgrn_out_norm(o, g_ref[r, lanes], ng[:, lanes]).astype(o_ref.dtype)
        k_end = (kk * jnp.exp2(last - cum)).astype(BF16)
        st_ref[hh] = st * jnp.exp2(last) + _dot_tn(v.astype(BF16), k_end)

    def chunk(c, carry):
        r = pl.ds(pl.multiple_of(c * c_len, c_len), c_len)
        for hh in range(hp):
            one_head(hh, r)
        return carry

    lax.fori_loop(0, tl // c_len, chunk, 0)

    @pl.when(t == n_t - 1)
    def _():
        for hh in range(hp):
            s_out_ref[0, hh] = st_ref[hh].T


def hgrn_prompt(z, lb_param, norm_g, *, layer, batch, tl):
    m = z.shape[0]
    n_t = m // batch // tl
    h, dk = HG_HEADS, HG_DK
    hp = HG_HEADS_PER_STEP
    n_hg = h // hp
    zspec = lambda part: pl.BlockSpec((tl, hp * dk), lambda b, hh, t: (b * n_t + t, part * n_hg + hh))
    return pl.pallas_call(
        functools.partial(_gla_kernel, layer=layer, n_t=n_t, tl=tl, hp=hp),
        grid=(batch, n_hg, n_t),
        in_specs=[zspec(0), zspec(1), zspec(2), zspec(3),
                  pl.BlockSpec((DEPTH, hp * dk), lambda b, hh, t: (0, hh)),
                  pl.BlockSpec((1, hp * dk), lambda b, hh, t: (0, hh))],
        out_specs=[pl.BlockSpec((tl, hp * dk), lambda b, hh, t: (b * n_t + t, hh)),
                   pl.BlockSpec((1, hp, dk, dk), lambda b, hh, t: (b, hh, 0, 0))],
        out_shape=[jax.ShapeDtypeStruct((m, h * dk), BF16),
                   jax.ShapeDtypeStruct((batch, h, dk, dk), F32)],
        scratch_shapes=[pltpu.VMEM((hp, dk, dk), F32)],
        compiler_params=_params("parallel", "parallel", "arbitrary"),
        name="hgrn_prompt",
    )(z, z, z, z, lb_param, norm_g.reshape(1, h * dk))


def _hgrn_step_kernel(q_ref, f_ref, i_ref, g_ref, lbp_ref, ng_ref, s0_ref, o_ref, s_ref,
                      qt_ref, ft_ref, kt_ref, oacc_ref, *, layer, n_j, sb):
    j = pl.program_id(1)
    n_seq = q_ref.shape[0]

    def split3(x, ref):
        hi = x.astype(BF16)
        r1 = x - hi.astype(F32)
        mid = r1.astype(BF16)
        ref[0], ref[1], ref[2] = hi, mid, (r1 - mid.astype(F32)).astype(BF16)

    @pl.when(j == 0)
    def _():
        lb = _hgrn_lower_bound(lbp_ref[...], layer)
        qs, log_f, kk = _hgrn_gates(q_ref[...], f_ref[...], lb)
        split3(qs.T, qt_ref)
        split3(jnp.exp(log_f).T, ft_ref)
        split3(kk.T, kt_ref)

    seq = lax.broadcasted_iota(jnp.int32, (n_seq, HG_DK), 0)
    for i in range(sb):
        b = j * sb + i
        pick = (seq == b).astype(BF16)
        col = lambda ref: sum(jnp.dot(ref[p], pick, preferred_element_type=F32) for p in range(3))
        s_new = s0_ref[i, 0] * col(ft_ref) + col(kt_ref) * i_ref[pl.ds(b, 1), :]
        s_ref[i, 0] = s_new
        oacc_ref[pl.ds(b, 1), :] = jnp.sum(col(qt_ref) * s_new, axis=0, keepdims=True)

    @pl.when(j == n_j - 1)
    def _():
        o_ref[...] = _hgrn_out_norm(oacc_ref[...], g_ref[...], ng_ref[...]).astype(o_ref.dtype)


def hgrn_step(z, s0, slot, lb_param, norm_g, *, layer, sb):
    n_seq = z.shape[0]
    h, dk = HG_HEADS, HG_DK
    n_j = n_seq // sb
    zspec = lambda off: pl.BlockSpec((n_seq, dk), lambda hh, j: (0, off + hh))
    return pl.pallas_call(
        functools.partial(_hgrn_step_kernel, layer=layer, n_j=n_j, sb=sb),
        grid=(h, n_j),
        in_specs=[zspec(0), zspec(h), zspec(2 * h), zspec(3 * h),
                  pl.BlockSpec((DEPTH, dk), lambda hh, j: (0, hh)),
                  pl.BlockSpec((1, dk), lambda hh, j: (0, hh)),
                  pl.BlockSpec((None, sb, 1, dk, dk), lambda hh, j: (slot, j, hh, 0, 0))],
        out_specs=[pl.BlockSpec((n_seq, dk), lambda hh, j: (0, hh)),
                   pl.BlockSpec((sb, 1, dk, dk), lambda hh, j: (j, hh, 0, 0))],
        out_shape=[jax.ShapeDtypeStruct((n_seq, h * dk), BF16),
                   jax.ShapeDtypeStruct(s0.shape[1:], F32)],
        scratch_shapes=[pltpu.VMEM((3, dk, n_seq), BF16)] * 3 + [pltpu.VMEM((n_seq, dk), F32)],
        compiler_params=_params("parallel", "arbitrary"),
        name="hgrn_step",
    )(z, z, z, z, lb_param, norm_g.reshape(1, h * dk), s0)


def _gelu_tanh(x):
    return x * (0.5 * (1.0 + jnp.tanh(0.7978845608028654 * (x + 0.044715 * (x * x * x)))))


def _rg_decay_and_input(xc, wg_ref, bg_ref, lam_ref):
    xb = xc.astype(BF16)
    gates = []
    for gi in range(2):
        blocks = [jnp.dot(xb[:, n * RG_BS:(n + 1) * RG_BS], wg_ref[gi, n], preferred_element_type=F32)
                  for n in range(RG_BLOCKS)]
        gates.append(jnp.concatenate(blocks, axis=1) + bg_ref[gi:gi + 1, :])
    r = _sigmoid(gates[0])
    i = _sigmoid(gates[1])
    neg_lam = -lam_ref[...]
    softplus = jnp.maximum(neg_lam, 0.0) + jnp.log1p(jnp.exp(-jnp.abs(neg_lam)))
    log_a = -RG_C * r * softplus
    one_minus_a2 = -jnp.tanh(log_a) * (jnp.exp(2.0 * log_a) + 1.0)
    return jnp.exp(log_a), jnp.sqrt(one_minus_a2) * i * xc


def _rg_prompt_kernel(y_ref, x_ref, cw_ref, cb_ref, wg_ref, bg_ref, lam_ref,
                      o_ref, conv_out_ref, h_out_ref, xpad_ref, h_ref, *, n_t, tl):
    t = pl.program_id(1)
    halo = 8

    @pl.when(t == 0)
    def _():
        xpad_ref[0:halo, :] = jnp.zeros((halo, RG_WIDTH), F32)
        h_ref[...] = jnp.zeros_like(h_ref)

    x = x_ref[...]
    xpad_ref[halo:halo + tl, :] = x
    xc = cb_ref[...]
    for j in range(RG_CONV):
        off = halo - (RG_CONV - 1) + j
        xc = xc + xpad_ref[off:off + tl, :] * cw_ref[j:j + 1, :]
    xpad_ref[0:halo, :] = x[tl - halo:tl, :]
    a, u = _rg_decay_and_input(xc, wg_ref, bg_ref, lam_ref)
    row = lax.broadcasted_iota(jnp.int32, (tl, 1), 0) % SUBLANES
    s = 1
    while s < SUBLANES:
        keep = row >= s
        a_prev = jnp.where(keep, pltpu.roll(a, s, 0), 1.0)
        u_prev = jnp.where(keep, pltpu.roll(u, s, 0), 0.0)
        u = a * u_prev + u
        a = a * a_prev
        s *= 2
    carry = h_ref[...]
    tiles = []
    for r in range(0, tl, SUBLANES):
        tiles.append(a[r:r + SUBLANES] * carry + u[r:r + SUBLANES])
        carry = tiles[-1][SUBLANES - 1:SUBLANES]
    h = jnp.concatenate(tiles, axis=0)
    h_ref[...] = carry
    o_ref[...] = (_gelu_tanh(y_ref[...]) * h).astype(o_ref.dtype)

    @pl.when(t == n_t - 1)
    def _():
        conv_out_ref[0] = x[tl - (RG_CONV - 1):tl, :]
        h_out_ref[0] = h[tl - 1:tl, :]


def rglru_prompt(z, conv_w, conv_b, w_gate, b_gate, lam, *, batch, tl):
    m = z.shape[0]
    n_t = m // batch // tl
    w = RG_WIDTH
    const = lambda shape: pl.BlockSpec(shape, lambda b, t: (0,) * len(shape))
    return pl.pallas_call(
        functools.partial(_rg_prompt_kernel, n_t=n_t, tl=tl),
        grid=(batch, n_t),
        in_specs=[pl.BlockSpec((tl, w), lambda b, t: (b * n_t + t, 0)),
                  pl.BlockSpec((tl, w), lambda b, t: (b * n_t + t, 1)),
                  const((RG_CONV, w)), const((1, w)), const((2, RG_BLOCKS, RG_BS, RG_BS)), const((2, w)), const((1, w))],
        out_specs=[pl.BlockSpec((tl, w), lambda b, t: (b * n_t + t, 0)),
                   pl.BlockSpec((1, RG_CONV - 1, w), lambda b, t: (b, 0, 0)),
                   pl.BlockSpec((1, 1, w), lambda b, t: (b, 0, 0))],
        out_shape=[jax.ShapeDtypeStruct((m, w), BF16),
                   jax.ShapeDtypeStruct((batch, RG_CONV - 1, w), F32),
                   jax.ShapeDtypeStruct((batch, 1, w), F32)],
        scratch_shapes=[pltpu.VMEM((tl + 8, w), F32), pltpu.VMEM((1, w), F32)],
        compiler_params=_params("parallel", "arbitrary"),
        name="rglru_prompt",
    )(z, z, conv_w, conv_b.reshape(1, w), w_gate, b_gate, lam.reshape(1, w))


def _rg_step_kernel(y_ref, x_ref, buf_ref, h0_ref, cw_ref, cb_ref, wg_ref, bg_ref, lam_ref,
                    o_ref, conv_out_ref, h_out_ref):
    x = x_ref[...]
    xc = cb_ref[...]
    for j in range(RG_CONV - 1):
        xc = xc + buf_ref[:, j, :] * cw_ref[j:j + 1, :]
    xc = xc + x * cw_ref[RG_CONV - 1:RG_CONV, :]
    a, u = _rg_decay_and_input(xc, wg_ref, bg_ref, lam_ref)
    h = a * h0_ref[...] + u
    o_ref[...] = (_gelu_tanh(y_ref[...]) * h).astype(o_ref.dtype)
    h_out_ref[...] = h
    for j in range(RG_CONV - 2):
        conv_out_ref[:, j, :] = buf_ref[:, j + 1, :]
    conv_out_ref[:, RG_CONV - 2, :] = x


def rglru_step(z, conv_buf, h0, conv_w, conv_b, w_gate, b_gate, lam):
    n_seq = z.shape[0]
    w = RG_WIDTH
    const = lambda shape: pl.BlockSpec(shape, lambda i: (0,) * len(shape))
    return pl.pallas_call(
        _rg_step_kernel,
        grid=(1,),
        in_specs=[pl.BlockSpec((n_seq, w), lambda i: (0, 0)),
                  pl.BlockSpec((n_seq, w), lambda i: (0, 1)),
                  const((n_seq, RG_CONV - 1, w)), const((n_seq, w)),
                  const((RG_CONV, w)), const((1, w)), const((2, RG_BLOCKS, RG_BS, RG_BS)), const((2, w)), const((1, w))],
        out_specs=[const((n_seq, w)), const((n_seq, RG_CONV - 1, w)), const((n_seq, w))],
        out_shape=[jax.ShapeDtypeStruct((n_seq, w), BF16),
                   jax.ShapeDtypeStruct((n_seq, RG_CONV - 1, w), F32),
                   jax.ShapeDtypeStruct((n_seq, w), F32)],
        compiler_params=_params("arbitrary"),
        name="rglru_step",
    )(z, z, conv_buf, h0, conv_w, conv_b.reshape(1, w), w_gate, b_gate, lam.reshape(1, w))


NSA_HEADS = 16
NSA_KVH = 4
NSA_GROUP = 4
NSA_DH = 64
NSA_KVW = NSA_KVH * NSA_DH
NSA_ROW = 2 * NSA_KVW
CMP_BLOCK = 32
CMP_HID = 128
SLC_BLOCK = 64
SLC_TOPK = 16
WINDOW = 512
NSA_QBLOCK = 128
MASK_NEG = -1e30
SEL_BIG = 1e9
SEL_KEYS = 1024


def _masked_softmax(s, mask, exp=jnp.exp):
    s = jnp.where(mask, s, MASK_NEG)
    m = jnp.max(s, axis=-1, keepdims=True)
    e = jnp.where(mask, exp(s - m), 0.0)
    return e * (1.0 / jnp.maximum(jnp.sum(e, axis=-1, keepdims=True), 1e-30))


N_FORCED = 3


def _top_blocks(score, k, axis):
    pos = lax.broadcasted_iota(jnp.int32, score.shape, axis)
    n = score.shape[axis]

    def pick(_, carry):
        sc, sel = carry
        mx = jnp.max(sc, axis=axis, keepdims=True)
        idx = jnp.min(jnp.where(sc == mx, pos, n), axis=axis, keepdims=True)
        hit = pos == idx
        return jnp.where(hit, -jnp.inf, sc), jnp.where(hit, 1.0, sel)

    forced = score >= SEL_BIG
    init = (jnp.where(forced, -jnp.inf, score), jnp.where(forced, 1.0, 0.0))
    return lax.fori_loop(0, k - N_FORCED, pick, init)[1]


def _compress_rows(block_row, w1_ref, w2_ref, n_blk):
    tiles = NSA_KVW // LANES
    low = lax.broadcasted_iota(jnp.int32, (n_blk, LANES), 1) < NSA_DH
    outs = []
    for c in range(2):
        cols = [[] for _ in range(NSA_KVH)]
        for l in range(0, CMP_BLOCK, 2):
            for tl in range(tiles):
                x0, x1 = block_row(c, tl, l), block_row(c, tl, l + 1)
                cols[2 * tl].append(jnp.where(low, x0, pltpu.roll(x1, NSA_DH, 1)))
                cols[2 * tl + 1].append(jnp.where(low, pltpu.roll(x0, NSA_DH, 1), x1))
        xs = jnp.concatenate([jnp.concatenate(cols[h], axis=1) for h in range(NSA_KVH)], axis=0)
        acc = jnp.dot(xs.astype(BF16), w1_ref[c], preferred_element_type=F32)
        out = jnp.dot(_silu(acc).astype(BF16), w2_ref[c], preferred_element_type=F32)
        outs.append(jnp.concatenate([out[h * n_blk:(h + 1) * n_blk] for h in range(NSA_KVH)], axis=1))
    return jnp.concatenate(outs, axis=1)


def _compress_kernel(x0_ref, x1_ref, x2_ref, x3_ref, pos_ref, w1_ref, w2_ref, even_ref, odd_ref, *, n_blk):
    x_refs = (x0_ref, x1_ref, x2_ref, x3_ref)
    half = n_blk // 2
    tiles = NSA_KVW // LANES

    def block_row(c, tl, r):
        ref = x_refs[c * tiles + tl]
        xe = ref[pl.ds(r, half, stride=2 * CMP_BLOCK), :]
        xo = ref[pl.ds(r + CMP_BLOCK, half, stride=2 * CMP_BLOCK), :]
        return jnp.concatenate([xe, xo], axis=0) + pos_ref[c, r:r + 1, tl * LANES:(tl + 1) * LANES]

    full = _compress_rows(block_row, w1_ref, w2_ref, n_blk)
    even_ref[0] = full[:half]
    odd_ref[0] = full[half:]


def nsa_compress(kv, col_block, pos4, w1, w2, *, batch, n_blk):
    m = kv.shape[0]
    n_t = m // batch // (n_blk * CMP_BLOCK)
    half = n_blk // 2
    const = lambda shape: pl.BlockSpec(shape, lambda b, t: (0,) * len(shape))
    out = jax.ShapeDtypeStruct((batch, n_t * half, NSA_ROW), F32)
    tiles = NSA_ROW // LANES
    lane_tile = lambda i: pl.BlockSpec((n_blk * CMP_BLOCK, LANES), lambda b, t: (b * n_t + t, col_block * tiles + i))
    return pl.pallas_call(
        functools.partial(_compress_kernel, n_blk=n_blk),
        grid=(batch, n_t),
        in_specs=[lane_tile(i) for i in range(tiles)]
        + [const((2, CMP_BLOCK, NSA_KVW)), const((2, CMP_BLOCK * NSA_DH, CMP_HID)), const((2, CMP_HID, NSA_DH))],
        out_specs=[pl.BlockSpec((1, half, NSA_ROW), lambda b, t: (b, t, 0))] * 2,
        out_shape=[out, out],
        compiler_params=_params("parallel", "parallel"),
        name="nsa_compress",
    )(kv, kv, kv, kv, pos4, w1, w2)


def nsa_prepare_weights(w_in, pos, w1, w2):
    hq = NSA_HEADS * NSA_DH
    kv_end = hq + 3 * NSA_ROW
    w_qkv = w_in[:, :kv_end].astype(BF16)
    w_kvb = w_in[:, hq + NSA_ROW:kv_end].astype(BF16)
    w_g = w_in[:, kv_end:].reshape(-1, NSA_KVH, 3 * NSA_GROUP)
    w_g = jnp.pad(w_g, ((0, 0), (0, 0), (0, LANES - 3 * NSA_GROUP))).reshape(-1, NSA_KVH * LANES).astype(BF16)
    pos4 = jnp.tile(pos, (1, 1, NSA_KVH))
    w1 = w1.reshape(2, CMP_BLOCK * NSA_DH, CMP_HID)
    w_kv_t = w_in[:, hq:kv_end].T.astype(BF16)
    return w_qkv, w_kvb, w_g, w_kv_t, pos4, w1.astype(BF16), w2.astype(BF16)


def _stack_group_queries(q, par):
    half = lax.broadcasted_iota(jnp.int32, (q.shape[0], LANES), 1) // NSA_DH
    parts = []
    for g in range(NSA_GROUP):
        qg = q[:, g * NSA_DH:(g + 1) * NSA_DH]
        parts.append(jnp.where(half == par, jnp.concatenate([qg, qg], axis=1), 0.0))
    return jnp.concatenate(parts, axis=0).astype(BF16)


def _nsa_prompt_kernel(q_ref, gp_ref, ck_ref, cv_ref, ks_ref, vs_ref, kw_ref, vw_ref, o_ref):
    j = pl.program_id(2)
    qb, grp = NSA_QBLOCK, NSA_GROUP
    rows = grp * qb
    qw = grp * NSA_DH
    pair = range(LANES // NSA_DH)
    t_pos = j * qb + lax.broadcasted_iota(jnp.int32, (qb, 1), 0)

    def grouped_softmax(s, mask):
        n = s.shape[-1]
        return _masked_softmax(s.reshape(grp, qb, n), mask[None], jnp.exp2).reshape(rows, n)

    n_c = ck_ref.shape[1]
    n_s = n_c // 2
    ck = ck_ref[0].astype(BF16)
    cv = cv_ref[0].astype(BF16)
    nn = lax.broadcasted_iota(jnp.int32, (1, n_c), 1)
    c_blk = jnp.where(nn < n_s, 2 * nn, 2 * (nn - n_s) + 1)
    c_vis = (c_blk * CMP_BLOCK + CMP_BLOCK - 1) <= t_pos
    n_w = WINDOW + qb
    w0 = pl.multiple_of(jnp.maximum(j * qb - WINDOW, 0), qb)
    dist = t_pos - (w0 + lax.broadcasted_iota(jnp.int32, (1, n_w), 1))
    w_bias = jnp.where((dist >= 0) & (dist <= WINDOW), 0.0, MASK_NEG)[None]
    k_win, v_win = kw_ref[pl.ds(w0, n_w), :], vw_ref[pl.ds(w0, n_w), :]
    blk = lax.broadcasted_iota(jnp.int32, (qb, n_s), 1)
    cur = t_pos // SLC_BLOCK
    forced = (blk == 0) | (blk == cur) | (blk == cur - 1)

    n_gate = 3 * grp
    gate_spread = (lax.broadcasted_iota(jnp.int32, (LANES, n_gate * LANES), 0)
                   == lax.broadcasted_iota(jnp.int32, (LANES, n_gate * LANES), 1) // LANES).astype(BF16)

    def before_selection(par):
        qp = _stack_group_queries(q_ref[:, par * qw:(par + 1) * qw] * (NSA_DH ** -0.5 * LOG2_E), par)
        p_c = grouped_softmax(_dot_nt(qp, ck), c_vis)
        o_c = jnp.dot(p_c.astype(BF16), cv, preferred_element_type=F32)
        s_w = _dot_nt(qp, k_win).reshape(grp, qb, n_w) + w_bias
        e_w = jnp.exp2(s_w - jnp.max(s_w, axis=-1, keepdims=True)).reshape(rows, n_w)
        own_w = lax.broadcasted_iota(jnp.int32, (n_w, LANES), 1) // NSA_DH == par
        o_w = jnp.dot(e_w.astype(BF16), jnp.where(own_w, v_win, 1.0), preferred_element_type=F32)
        o_w = o_w / jnp.maximum(pltpu.roll(o_w, NSA_DH, 1), 1e-30)
        gates = _sigmoid(gp_ref[:, par * LANES:(par + 1) * LANES])
        hi = gates.astype(BF16)
        r1 = gates - hi.astype(F32)
        mid = r1.astype(BF16)
        lo = (r1 - mid.astype(F32)).astype(BF16)
        spread = sum(jnp.dot(piece, gate_spread, preferred_element_type=F32) for piece in (hi, mid, lo))
        gate = lambda br: jnp.concatenate(
            [spread[:, (3 * g + br) * LANES:(3 * g + br + 1) * LANES] for g in range(grp)], axis=0)
        imp = p_c[0:qb]
        for g in range(1, grp):
            imp = imp + p_c[g * qb:(g + 1) * qb]
        imp = imp[:, :n_s] + imp[:, n_s:]
        score = jnp.where(forced, SEL_BIG, jnp.where(blk <= cur, imp, -SEL_BIG))
        return qp, gate(0) * o_c + gate(2) * o_w, gate(1), score.T

    qps, o_cws, gate_s, scores_t = zip(*[before_selection(par) for par in pair])
    sel_t = _top_blocks(jnp.concatenate(scores_t, axis=1), SLC_TOPK, 0)
    sel_bs = [sel_t[:, par * qb:(par + 1) * qb].T.astype(BF16) for par in pair]

    per = SEL_KEYS // SLC_BLOCK
    e_row = lax.broadcasted_iota(jnp.int32, (n_s, SEL_KEYS), 0)
    e_col = lax.broadcasted_iota(jnp.int32, (n_s, SEL_KEYS), 1) // SLC_BLOCK
    k_off = lax.broadcasted_iota(jnp.int32, (1, SEL_KEYS), 1)

    n_kb = (j * qb + qb - 1) // SEL_KEYS + 1

    def key_rows(kb):
        return pl.ds(pl.multiple_of(kb * SEL_KEYS, SEL_KEYS), SEL_KEYS)

    lane_head = lax.broadcasted_iota(jnp.int32, (SEL_KEYS, LANES), 1) // NSA_DH

    def sel_step(kb, carry):
        expand = (e_row == kb * per + e_col).astype(BF16)
        causal = (kb * SEL_KEYS + k_off) <= t_pos
        k_blk, v_blk = ks_ref[key_rows(kb), :], vs_ref[key_rows(kb), :]
        out = []
        for par in pair:
            m, acc = carry[par]
            chosen = jnp.dot(sel_bs[par], expand, preferred_element_type=F32) > 0.5
            bias = jnp.where(chosen & causal, 0.0, MASK_NEG)[None]
            s = _dot_nt(qps[par], k_blk).reshape(grp, qb, SEL_KEYS) + bias
            m_new = jnp.maximum(m, jnp.max(s, axis=-1, keepdims=True))
            alpha = jnp.exp2(m - m_new)
            e = jnp.exp2(s - m_new)
            v_one = jnp.where(lane_head == par, v_blk, 1.0)
            pv = jnp.dot(e.reshape(rows, SEL_KEYS).astype(BF16), v_one, preferred_element_type=F32)
            out.append((m_new, alpha * acc + pv.reshape(grp, qb, LANES)))
        return tuple(out)

    init = tuple((jnp.full((grp, qb, 1), MASK_NEG, F32), jnp.zeros((grp, qb, LANES), F32)) for _ in pair)
    fin = lax.fori_loop(0, n_kb, sel_step, init)
    heads = []
    for par in pair:
        acc = fin[par][1].reshape(rows, LANES)
        o_s = acc / jnp.maximum(pltpu.roll(acc, NSA_DH, 1), 1e-30)
        o = (o_cws[par] + gate_s[par] * o_s)[:, par * NSA_DH:(par + 1) * NSA_DH]
        heads += [o[g * qb:(g + 1) * qb] for g in range(grp)]
    o_ref[...] = jnp.concatenate(heads, axis=1).astype(o_ref.dtype)


def nsa_prompt_attention(qkv, gates_pre, ckv, kvb, *, batch):
    m = qkv.shape[0]
    l_seq = m // batch
    nq = l_seq // NSA_QBLOCK
    n_c = ckv.shape[1]
    per_tile = LANES // NSA_DH
    n_pairs = NSA_KVH // per_tile
    qw = per_tile * NSA_GROUP * NSA_DH
    kv_spec = lambda part: pl.BlockSpec((l_seq, LANES), lambda b, p, j: (b, part * n_pairs + p))
    return pl.pallas_call(
        _nsa_prompt_kernel,
        grid=(batch, n_pairs, nq),
        in_specs=[pl.BlockSpec((NSA_QBLOCK, qw), lambda b, p, j: (b * nq + j, p)),
                  pl.BlockSpec((NSA_QBLOCK, per_tile * LANES), lambda b, p, j: (b * nq + j, p)),
                  pl.BlockSpec((1, n_c, LANES), lambda b, p, j: (b, 0, p)),
                  pl.BlockSpec((1, n_c, LANES), lambda b, p, j: (b, 0, n_pairs + p)),
                  kv_spec(0), kv_spec(1), kv_spec(2), kv_spec(3)],
        out_specs=pl.BlockSpec((NSA_QBLOCK, qw), lambda b, p, j: (b * nq + j, p)),
        out_shape=jax.ShapeDtypeStruct((m, NSA_HEADS * NSA_DH), BF16),
        compiler_params=_params("parallel", "parallel", "arbitrary"),
        name="nsa_prompt_attention",
    )(qkv, gates_pre, ckv, ckv, kvb, kvb, kvb, kvb)


def _paged_compress_kernel(pt_ref, *refs, n_pages, page):
    del pt_ref
    pages = refs[:n_pages]
    pos_ref, w1_ref, w2_ref, out_ref = refs[n_pages:n_pages + 4]
    rows = refs[n_pages + 4:]
    tiles = NSA_KVW // LANES
    per_page = page // CMP_BLOCK
    group = SUBLANES // per_page
    width = group * page
    dst = lax.broadcasted_iota(jnp.int32, (width, width), 0)
    slot = dst % SUBLANES
    src = (slot // per_page) * page + (slot % per_page) * CMP_BLOCK + dst // SUBLANES
    perm = (lax.broadcasted_iota(jnp.int32, (width, width), 1) == src).astype(BF16)
    for q in range(n_pages // group):
        for i, ref in enumerate(rows):
            c, tl = divmod(i, tiles)
            lanes = slice(tl * LANES, (tl + 1) * LANES)
            xt = jnp.concatenate([pages[q * group + g][0, c, lanes, :] for g in range(group)], axis=1)
            moved = _dot_nt(perm, (xt + pos_ref[c, lanes, :]).astype(BF16))
            ref[:, q * SUBLANES:(q + 1) * SUBLANES, :] = moved.reshape(CMP_BLOCK, SUBLANES, LANES)
    n_blk = n_pages * per_page
    out_ref[0] = _compress_rows(lambda c, tl, r: rows[c * tiles + tl][r], w1_ref, w2_ref, n_blk)


def _page_specs(n_pages, page):
    return [pl.BlockSpec((1, 2, NSA_KVW, page), lambda b, pt, p=p: (pt[b, p], 0, 0, 0)) for p in range(n_pages)]


def nsa_compress_paged(cache_t, page_table, pos, w1, w2):
    n_seq, n_pages = page_table.shape
    page = cache_t.shape[3]
    n_blk = n_pages * page // CMP_BLOCK
    width = SUBLANES // (page // CMP_BLOCK) * page
    pos_t = jnp.tile(pos.transpose(0, 2, 1), (1, NSA_KVH, width // CMP_BLOCK))
    const = lambda shape: pl.BlockSpec(shape, lambda b, pt: (0,) * len(shape))
    return pl.pallas_call(
        functools.partial(_paged_compress_kernel, n_pages=n_pages, page=page),
        grid_spec=pltpu.PrefetchScalarGridSpec(
            num_scalar_prefetch=1,
            grid=(n_seq,),
            in_specs=_page_specs(n_pages, page)
            + [const((2, NSA_KVW, width)), const((2, CMP_BLOCK * NSA_DH, CMP_HID)), const((2, CMP_HID, NSA_DH))],
            out_specs=pl.BlockSpec((1, n_blk, NSA_ROW), lambda b, pt: (b, 0, 0)),
            scratch_shapes=[pltpu.VMEM((CMP_BLOCK, n_blk, LANES), F32)] * (NSA_ROW // LANES)),
        out_shape=jax.ShapeDtypeStruct((n_seq, n_blk, NSA_ROW), F32),
        compiler_params=_params("arbitrary"),
        name="nsa_compress_paged",
    )(page_table, *([cache_t] * n_pages), pos_t, w1, w2)


def _softmax_with_new_key(s, ok, s_new, new_ok):
    s = jnp.where(ok, s, MASK_NEG)
    s_new = jnp.where(new_ok, s_new, MASK_NEG)
    m = jnp.maximum(jnp.max(s, axis=-1, keepdims=True), s_new)
    e = jnp.where(ok, jnp.exp(s - m), 0.0)
    e_new = jnp.where(new_ok, jnp.exp(s_new - m), 0.0)
    total = jnp.sum(e, axis=-1, keepdims=True) + e_new
    return e, e_new, 1.0 / jnp.maximum(total, 1e-30)


def _nsa_step_kernel(pt_ref, *refs, n_pages, page, past):
    del pt_ref
    pages = refs[:n_pages]
    (q_ref, gp_ref, cke_ref, cko_ref, kvs_new_ref, swa_ref, kvw_new_ref,
     o_ref, swa_out_ref, kt_ref, vt_ref) = refs[n_pages:]
    nh, kw = NSA_HEADS, NSA_KVW
    t = past
    own = (lax.broadcasted_iota(jnp.int32, (nh, kw), 1) // NSA_DH
           == lax.broadcasted_iota(jnp.int32, (nh, kw), 0) // NSA_GROUP)
    q = q_ref[0] * (NSA_DH ** -0.5)
    qp = jnp.where(own, jnp.concatenate([q] * NSA_KVH, axis=1), 0.0)
    qpb = qp.astype(BF16)

    half = cke_ref.shape[1]
    n_c = 2 * half
    ce, co = cke_ref[0], cko_ref[0]
    ck = jnp.concatenate([ce[:, :kw], co[:, :kw]], axis=0).astype(BF16)
    cv = jnp.concatenate([ce[:, kw:], co[:, kw:]], axis=0).astype(BF16)
    nn = lax.broadcasted_iota(jnp.int32, (1, n_c), 1)
    c_blk = jnp.where(nn < half, 2 * nn, 2 * (nn - half) + 1)
    p_c = _masked_softmax(_dot_nt(qpb, ck), (c_blk * CMP_BLOCK + CMP_BLOCK - 1) <= t)
    o_c = jnp.dot(p_c.astype(BF16), cv, preferred_element_type=F32)
    same_group = (lax.broadcasted_iota(jnp.int32, (nh, nh), 0) // NSA_GROUP
                  == lax.broadcasted_iota(jnp.int32, (nh, nh), 1) // NSA_GROUP).astype(F32)
    imp = lax.dot_general(p_c, same_group, (((0,), (0,)), ((), ())), preferred_element_type=F32, precision=HIGHEST)
    imp = imp[:half] + imp[half:]
    imp = jnp.concatenate([imp, jnp.zeros((LANES - half, nh), F32)], axis=0)
    blk = lax.broadcasted_iota(jnp.int32, (LANES, nh), 0)
    cur = t // SLC_BLOCK
    forced = (blk == 0) | (blk == cur) | (blk == cur - 1)
    sel_t = _top_blocks(jnp.where(forced, SEL_BIG, jnp.where(blk <= cur, imp, -SEL_BIG)), SLC_TOPK, 0)

    for p in range(n_pages):
        kt_ref[:, p * page:(p + 1) * page] = pages[p][0, 0].astype(BF16)
        vt_ref[:, p * page:(p + 1) * page] = pages[p][0, 1].astype(BF16)
    n_keys = n_pages * page
    pos = lax.broadcasted_iota(jnp.int32, (LANES, n_keys + LANES), 1)
    expand = (lax.broadcasted_iota(jnp.int32, (LANES, n_keys + LANES), 0)
              == jnp.minimum(pos, t) // SLC_BLOCK).astype(BF16)
    chosen = _dot_tn(sel_t.astype(BF16), expand) > 0.5
    ok = chosen[:, :n_keys] & (lax.broadcasted_iota(jnp.int32, (1, n_keys), 1) <= t)
    kv_new = kvs_new_ref[0]
    s_new = jnp.sum(qp * kv_new[:, :kw], axis=-1, keepdims=True)
    scores = jnp.dot(qpb, kt_ref[...], preferred_element_type=F32)
    e, e_new, inv = _softmax_with_new_key(scores, ok, s_new, chosen[:, n_keys:n_keys + 1])
    o_s = (_dot_nt(e.astype(BF16), vt_ref[...]) + e_new * kv_new[:, kw:]) * inv

    n_w = swa_ref.shape[3]
    dist = t - (t - n_w + lax.broadcasted_iota(jnp.int32, (1, n_w), 1))
    w_ok = (dist >= 0) & (dist <= WINDOW) & (t - dist >= 0)
    kvw_new = kvw_new_ref[0]
    s_wn = jnp.sum(qp * kvw_new[:, :kw], axis=-1, keepdims=True)
    new_ok = lax.broadcasted_iota(jnp.int32, (nh, 1), 0) >= 0
    scores = jnp.dot(qpb, swa_ref[0, 0].astype(BF16), preferred_element_type=F32)
    e, e_new, inv = _softmax_with_new_key(scores, w_ok, s_wn, new_ok)
    o_w = (_dot_nt(e.astype(BF16), swa_ref[0, 1].astype(BF16)) + e_new * kvw_new[:, kw:]) * inv

    g = _sigmoid(gp_ref[0])
    o = jnp.where(own, g[:, 0:1] * o_c + g[:, 1:2] * o_s + g[:, 2:3] * o_w, 0.0)
    o_h = o[:, :NSA_DH]
    for h in range(1, NSA_KVH):
        o_h = o_h + o[:, h * NSA_DH:(h + 1) * NSA_DH]
    o_ref[0] = o_h
    last = lax.broadcasted_iota(jnp.int32, (kw, n_w), 1) == n_w - 1
    diag = lax.broadcasted_iota(jnp.int32, (kw, kw), 0) == lax.broadcasted_iota(jnp.int32, (kw, kw), 1)
    for c in range(2):
        column = jnp.sum(jnp.where(diag, kvw_new[:, c * kw:(c + 1) * kw], 0.0), axis=-1, keepdims=True)
        swa_out_ref[0, c] = jnp.where(last, column, pltpu.roll(swa_ref[0, c], n_w - 1, 1))


def nsa_step_attention(q3, gates3, cke, cko, cache_slc_t, page_table, kvs_new, swa_t, kvw_new, *, past):
    n_seq, n_pages = page_table.shape
    page = cache_slc_t.shape[3]
    seq = lambda a: pl.BlockSpec((1,) + a.shape[1:], lambda b, pt: (b,) + (0,) * (a.ndim - 1))
    return pl.pallas_call(
        functools.partial(_nsa_step_kernel, n_pages=n_pages, page=page, past=past),
        grid_spec=pltpu.PrefetchScalarGridSpec(
            num_scalar_prefetch=1,
            grid=(n_seq,),
            in_specs=_page_specs(n_pages, page) + [seq(a) for a in (q3, gates3, cke, cko, kvs_new, swa_t, kvw_new)],
            out_specs=[seq(q3), seq(swa_t)],
            scratch_shapes=[pltpu.VMEM((NSA_KVW, n_pages * page), BF16)] * 2),
        out_shape=[jax.ShapeDtypeStruct(q3.shape, F32), jax.ShapeDtypeStruct(swa_t.shape, F32)],
        compiler_params=_params("arbitrary"),
        name="nsa_step_attention",
    )(page_table, *([cache_slc_t] * n_pages), q3, gates3, cke, cko, kvs_new, swa_t, kvw_new)


PROMPT_TM = 1024
FFN_TM = 512
HG_TL = 512
RG_TL = 256
FFN_TF = 1408
CMP_TILE_BLOCKS = 64
MOE_TB = 896
MOE_TS = 256


def _ffn_layer(x, w_gu, w_down, g, b, *, tm):
    return ffn_residual_ln(x, w_gu.astype(BF16), w_down.astype(BF16), g, b, tm=tm, tf=FFN_TF)


def _moe_layer(x, w_router, w_gu_t, w_down_t, slot, g, b):
    m = x.shape[0]
    tb = min(MOE_TB, m)
    gates, slots = moe_route(x, w_router.T, tb=tb)
    n_e = gates.shape[0]
    counts = (jnp.max(slots.reshape(n_e, -1, tb), axis=-1) + 1).T.reshape(-1)
    return moe_residual_ln(x, gates, slots, counts, w_gu_t, w_down_t, slot, g, b,
                           tb=tb, ts=min(MOE_TS, tb), tf=FFN_TF)


def kernel(x_prompt, x_sample, state_hgrn, state_rglru_conv, state_rglru_h, cache_nsa_cmp, cache_nsa_slc, cache_nsa_swa, page_table, ln_g, ln_b, hgrn_w_in, hgrn_lb, hgrn_norm_g, hgrn_w_out, rg_w_in, rg_conv_w, rg_conv_b, rg_w_gate, rg_b_gate, rg_lambda, rg_w_out, nsa_w_in, nsa_cmp_pos, nsa_cmp_w1, nsa_cmp_w2, nsa_w_out, ffn_w_gu, ffn_w_down, moe_router, moe_w_gu, moe_w_down):
    bp, l_seq, d = x_prompt.shape
    n_seq = x_sample.shape[0]
    past = page_table.shape[1] * cache_nsa_cmp.shape[2]
    xp = x_prompt.reshape(bp * l_seq, d)
    xs = x_sample.reshape(n_seq, d)
    tm_p, tm_s = PROMPT_TM, n_seq
    kv_shape = (2, NSA_KVH, NSA_DH)
    moe_w_gu_t = jnp.swapaxes(moe_w_gu, 2, 3).astype(BF16)
    moe_w_down_t = jnp.swapaxes(moe_w_down, 2, 3).astype(BF16)
    hg_p, hg_s, rc_p, rc_s, rh_p, rh_s = [], [], [], [], [], []
    cm_p, cm_s, sl_p, sl_s, sw_p, sw_s = [], [], [], [], [], []
    for layer in range(DEPTH):
        kind, slot = layer % 3, layer // 3
        g0, b0 = ln_g[layer, 0], ln_b[layer, 0]
        if kind == 0:
            w_in, w_out = hgrn_w_in[slot].astype(BF16), hgrn_w_out[slot].astype(BF16)
            op, st_p = hgrn_prompt(matmul(xp, w_in, tm=tm_p), hgrn_lb, hgrn_norm_g[slot],
                                   layer=layer, batch=bp, tl=HG_TL)
            os_, st_s = hgrn_step(matmul(xs, w_in, tm=tm_s), state_hgrn, slot, hgrn_lb, hgrn_norm_g[slot],
                                  layer=layer, sb=8)
            hg_p.append(st_p)
            hg_s.append(st_s)
        elif kind == 1:
            w_in, w_out = rg_w_in[slot].astype(BF16), rg_w_out[slot].astype(BF16)
            rg_w = (rg_conv_w[slot], rg_conv_b[slot], rg_w_gate[slot].astype(BF16), rg_b_gate[slot], rg_lambda[slot])
            op, cb_p, h_p = rglru_prompt(matmul(xp, w_in, tm=tm_p), *rg_w, batch=bp, tl=RG_TL)
            os_, cb_s, h_s = rglru_step(matmul(xs, w_in, tm=tm_s), state_rglru_conv[slot], state_rglru_h[slot], *rg_w)
            rc_p.append(cb_p)
            rc_s.append(cb_s)
            rh_p.append(h_p[:, 0])
            rh_s.append(h_s)
        else:
            w_qkv, w_kvb, w_g, w_kv_t, pos4, w1, w2 = nsa_prepare_weights(
                nsa_w_in[slot], nsa_cmp_pos[slot], nsa_cmp_w1[slot], nsa_cmp_w2[slot])
            w_out = nsa_w_out[slot].astype(BF16)
            hq = NSA_HEADS * NSA_DH
            qkv = matmul(xp, w_qkv[:, :hq + NSA_ROW], tm=tm_p)
            kvb = matmul(xp, w_kvb, tm=tm_p, out_dtype=BF16)
            gates_pre = matmul(xp, w_g, tm=tm_p)
            cke, cko = nsa_compress(qkv, hq // NSA_ROW, pos4, w1, w2, batch=bp, n_blk=CMP_TILE_BLOCKS)
            op = nsa_prompt_attention(qkv, gates_pre, jnp.concatenate([cke, cko], axis=1), kvb, batch=bp)
            as_rows = lambda a: a.reshape(bp, *kv_shape, -1).transpose(0, 4, 1, 2, 3)
            kv_c, kv_s, kv_w = matmul_t(xp, w_kv_t, batch=bp, n_out=3, tm=tm_p)
            cm_p.append(as_rows(kv_c))
            sl_p.append(as_rows(kv_s))
            sw_p.append(as_rows(kv_w[:, :, l_seq - min(WINDOW, l_seq):]))
            qkv = matmul(xs, w_qkv, tm=tm_s)
            gates_pre = matmul(xs, w_g, tm=tm_s)
            rows_last = lambda c: c.transpose(0, 2, 3, 4, 1).reshape(c.shape[0], 2, NSA_KVW, c.shape[1])
            ckv = nsa_compress_paged(rows_last(cache_nsa_cmp[slot]), page_table, nsa_cmp_pos[slot], w1, w2)
            cke, cko = ckv[:, 0::2], ckv[:, 1::2]
            gates3 = gates_pre.reshape(n_seq, NSA_KVH, LANES)[:, :, :3 * NSA_GROUP].reshape(n_seq, NSA_HEADS, 3)
            kv = qkv[:, hq:].reshape(n_seq, 3, 1, NSA_ROW)
            o3, swa_new = nsa_step_attention(qkv[:, :hq].reshape(n_seq, NSA_HEADS, NSA_DH), gates3, cke, cko,
                                             rows_last(cache_nsa_slc[slot]), page_table, kv[:, 1],
                                             rows_last(cache_nsa_swa[slot]), kv[:, 2], past=past)
            os_ = o3.reshape(n_seq, hq)
            cm_s.append(kv[:, 0].reshape(n_seq, 1, *kv_shape))
            sl_s.append(kv[:, 1].reshape(n_seq, 1, *kv_shape))
            sw_s.append(swa_new.reshape(n_seq, *kv_shape, -1).transpose(0, 4, 1, 2, 3))
        xp = matmul_residual_ln(op, w_out, xp, g0, b0, tm=tm_p)
        xs = matmul_residual_ln(os_, w_out, xs, g0, b0, tm=tm_s)
        g1, b1 = ln_g[layer, 1], ln_b[layer, 1]
        if layer % 2 == 0:
            fw = (ffn_w_gu[layer // 2], ffn_w_down[layer // 2])
            xp = _ffn_layer(xp, *fw, g1, b1, tm=FFN_TM)
            xs = _ffn_layer(xs, *fw, g1, b1, tm=tm_s)
        else:
            fw = (moe_router[layer // 2], moe_w_gu_t, moe_w_down_t, layer // 2)
            xp = _moe_layer(xp, *fw, g1, b1)
            xs = _moe_layer(xs, *fw, g1, b1)
    return (xp.reshape(bp, l_seq, d), xs.reshape(n_seq, 1, d),
            jnp.stack(hg_p), jnp.stack(hg_s), jnp.stack(rc_p), jnp.stack(rc_s), jnp.stack(rh_p), jnp.stack(rh_s),
            jnp.stack(cm_p), jnp.stack(cm_s), jnp.stack(sl_p), jnp.stack(sl_s), jnp.stack(sw_p), jnp.stack(sw_s))
```
